```python
import math
import jax, jax.numpy as jnp
from jax import lax
import numpy as np

D_MODEL = 1024
BATCH = 16
SEQ = 256
DEPTH = 2
DEC_BATCH = 8
DEC_SEQ = 1024
PAST_LEN = 256

GRID_W = 64
POOL_WINDOWS = (2, 4, 8, 16)
N_POOL_GROUPS = len(POOL_WINDOWS)
POOL_WIDTH = D_MODEL // 4
POOL_GROUP = POOL_WIDTH // N_POOL_GROUPS
ATTN_WIDTH = D_MODEL // 2
N_HEADS = 4
V_DIM = ATTN_WIDTH // N_HEADS
QK_DIM = V_DIM // 2
Q_BLOCK = 128
ROPE_BASE = 10000.0
ROPE_AXIS_DIM = QK_DIM // 2
CHUNK = 128
SGU_WIDTH = D_MODEL // 4
SGU_GROUPS = 4
SGU_GROUP_DIM = SGU_WIDTH // SGU_GROUPS
QK_WIDTH = N_HEADS * 2 * QK_DIM
IN_WIDTH = POOL_WIDTH + 2 * QK_WIDTH + ATTN_WIDTH + 2 * SGU_WIDTH
MIX_WIDTH = POOL_WIDTH + ATTN_WIDTH + SGU_WIDTH
SPLITS = (POOL_WIDTH, POOL_WIDTH + QK_WIDTH, POOL_WIDTH + 2 * QK_WIDTH,
          POOL_WIDTH + 2 * QK_WIDTH + ATTN_WIDTH)
D_FF = -(-8 * D_MODEL // (3 * 256)) * 256
N_MOD = 6
EPS = 1e-6

kernel_name = "hybrid_pool_diffattn_sgu_prefix_dit_step"


def rms_norm(x, g):
    xf = x.astype(jnp.float32)
    y = xf * lax.rsqrt(jnp.mean(xf * xf, axis=-1, keepdims=True) + EPS)
    return (y * g.astype(jnp.float32)).astype(x.dtype)


def layer_norm(x, g):
    xf = x.astype(jnp.float32)
    mu = jnp.mean(xf, axis=-1, keepdims=True)
    xc = xf - mu
    y = xc * lax.rsqrt(jnp.mean(xc * xc, axis=-1, keepdims=True) + EPS)
    return (y * g.astype(jnp.float32)).astype(x.dtype)


def ada_modulation(cond, w, b):
    m = jax.nn.silu(cond) @ w + b
    return jnp.split(m[:, None, :], N_MOD, axis=-1)


def multiscale_pool(x):
    L = x.shape[1]
    xf = x.astype(jnp.float32)
    cs = jnp.concatenate([jnp.zeros_like(xf[:, :1]), jnp.cumsum(xf, axis=1)], axis=1)
    t = np.arange(L)
    outs = []
    for g, w in enumerate(POOL_WINDOWS):
        lo = np.clip(t - w // 2, 0, L)
        hi = np.clip(t + w - w // 2, 0, L)
        sl = slice(g * POOL_GROUP, (g + 1) * POOL_GROUP)
        seg = cs[..., sl]
        cnt = jnp.asarray((hi - lo).astype(np.float32))[None, :, None]
        outs.append((seg[:, hi] - seg[:, lo]) / cnt - xf[..., sl])
    return jnp.concatenate(outs, axis=-1).astype(x.dtype)


def axial_rope_tables(L):
    n_rows = L // GRID_W
    rows = np.repeat(np.arange(n_rows), GRID_W).astype(np.float32)
    cols = np.tile(np.arange(GRID_W), n_rows).astype(np.float32)
    inv = 1.0 / (ROPE_BASE ** (np.arange(0, ROPE_AXIS_DIM, 2, dtype=np.float32) / ROPE_AXIS_DIM))
    ar, ac = rows[:, None] * inv[None], cols[:, None] * inv[None]
    return (jnp.asarray(np.cos(ar)), jnp.asarray(np.sin(ar)),
            jnp.asarray(np.cos(ac)), jnp.asarray(np.sin(ac)))


def _rotate(x, cos, sin):
    n = x.shape[-1] // 2
    x1, x2 = x[..., :n], x[..., n:]
    c, s = cos[None, :, None, :], sin[None, :, None, :]
    return jnp.concatenate([x1 * c - x2 * s, x2 * c + x1 * s], axis=-1)


def apply_axial_rope(x, cos_r, sin_r, cos_c, sin_c):
    xf = x.astype(jnp.float32)
    xr = _rotate(xf[..., :ROPE_AXIS_DIM], cos_r, sin_r)
    xc = _rotate(xf[..., ROPE_AXIS_DIM:], cos_c, sin_c)
    return jnp.concatenate([xr, xc], axis=-1).astype(x.dtype)


def diff_attention(q1, q2, k1, k2, v, lam):
    B, Lq = q1.shape[:2]
    nb = Lq // Q_BLOCK
    scale = QK_DIM ** -0.5

    def to_blocks(q):
        return q.reshape(B, nb, Q_BLOCK, N_HEADS, QK_DIM).transpose(1, 0, 2, 3, 4)

    def block(args):
        a1, a2 = args
        s1 = jnp.einsum('bqhd,bkhd->bhqk', a1, k1, preferred_element_type=jnp.float32) * scale
        s2 = jnp.einsum('bqhd,bkhd->bhqk', a2, k2, preferred_element_type=jnp.float32) * scale
        p = jax.nn.softmax(s1, axis=-1) - lam * jax.nn.softmax(s2, axis=-1)
        return jnp.einsum('bhqk,bkhe->bqhe', p.astype(v.dtype), v)

    out = lax.map(block, (to_blocks(q1), to_blocks(q2)))
    return out.transpose(1, 0, 2, 3, 4).reshape(B, Lq, N_HEADS, V_DIM)


def chunk_spatial_gating(uv, g_n, w_s, b_s):
    B, L, _ = uv.shape
    u, v = jnp.split(jax.nn.gelu(uv), 2, axis=-1)
    v = layer_norm(v, g_n).reshape(B, L // CHUNK, CHUNK, SGU_GROUPS, SGU_GROUP_DIM)
    v = jnp.einsum('gpq,bnqgc->bnpgc', w_s, v) + b_s.T[None, None, :, :, None]
    return u * v.reshape(B, L, SGU_WIDTH)


def setup_inputs(seed: int = 0) -> dict:
    key = jax.random.key(seed)
    ks = jax.random.split(key, 32)
    f32 = jnp.float32
    nrm = lambda k, shape, s: (jax.random.normal(k, shape, f32) * s)
    gain = lambda k, shape: 1.0 + 0.02 * jax.random.normal(k, shape, f32)
    return {
        "x_prompt": nrm(ks[0], (BATCH, SEQ, D_MODEL), 1.0),
        "x_sample": nrm(ks[1], (DEC_BATCH, DEC_SEQ, D_MODEL), 1.0),
        "cache_k": nrm(ks[2], (DEC_BATCH, DEPTH, PAST_LEN, N_HEADS, 2 * QK_DIM), 1.0),
        "cache_v": nrm(ks[3], (DEC_BATCH, DEPTH, PAST_LEN, N_HEADS, V_DIM), 1.0),
        "c": nrm(ks[4], (DEC_BATCH, D_MODEL), 1.0),
        "c_ctx": nrm(ks[5], (D_MODEL,), 1.0),
        "norm1_g": gain(ks[6], (DEPTH, D_MODEL)),
        "w_ada": nrm(ks[7], (DEPTH, D_MODEL, N_MOD * D_MODEL), 0.5 * D_MODEL ** -0.5),
        "b_ada": nrm(ks[8], (DEPTH, N_MOD * D_MODEL), 0.02),
        "w_in": nrm(ks[9], (DEPTH, D_MODEL, IN_WIDTH), D_MODEL ** -0.5),
        "w_pool": nrm(ks[10], (DEPTH, N_POOL_GROUPS, POOL_GROUP, POOL_GROUP), POOL_GROUP ** -0.5),
        "pool_scale": gain(ks[11], (DEPTH, POOL_WIDTH)),
        "lam_q1": nrm(ks[12], (DEPTH, QK_DIM), 0.1),
        "lam_k1": nrm(ks[13], (DEPTH, QK_DIM), 0.1),
        "lam_q2": nrm(ks[14], (DEPTH, QK_DIM), 0.1),
        "lam_k2": nrm(ks[15], (DEPTH, QK_DIM), 0.1),
        "subln_g": gain(ks[16], (DEPTH, V_DIM)),
        "sgu_norm_g": gain(ks[17], (DEPTH, SGU_WIDTH)),
        "w_sgu": nrm(ks[18], (DEPTH, SGU_GROUPS, CHUNK, CHUNK), CHUNK ** -0.5),
        "b_sgu": nrm(ks[19], (DEPTH, SGU_GROUPS, CHUNK), 0.02),
        "w_out": nrm(ks[20], (DEPTH, MIX_WIDTH, D_MODEL), MIX_WIDTH ** -0.5),
        "norm2_g": gain(ks[21], (DEPTH, D_MODEL)),
        "w_ffn_in": nrm(ks[22], (DEPTH, D_MODEL, 2 * D_FF), D_MODEL ** -0.5),
        "w_ffn_out": nrm(ks[23], (DEPTH, D_FF, D_MODEL), D_FF ** -0.5),
        "final_g": gain(ks[24], (D_MODEL,)),
    }


def reference(x_prompt, x_sample, cache_k, cache_v, c, c_ctx, norm1_g, w_ada, b_ada, w_in,
              w_pool, pool_scale, lam_q1, lam_k1, lam_q2, lam_k2, subln_g, sgu_norm_g,
              w_sgu, b_sgu, w_out, norm2_g, w_ffn_in, w_ffn_out, final_g):

    def layer(x, l, cond, rope, ctx_k, ctx_v):
        B, L, _ = x.shape
        sh1, sc1, g1, sh2, sc2, g2 = ada_modulation(cond, w_ada[l], b_ada[l])
        h = rms_norm(x, norm1_g[l]) * (1.0 + sc1) + sh1
        p_pool, p_q, p_k, p_v, p_uv = jnp.split(h @ w_in[l], SPLITS, axis=-1)

        pooled = multiscale_pool(p_pool).reshape(B, L, N_POOL_GROUPS, POOL_GROUP)
        y_a = jnp.einsum('blgc,gcd->blgd', pooled, w_pool[l]).reshape(B, L, POOL_WIDTH) * pool_scale[l]

        q = p_q.reshape(B, L, N_HEADS, 2, QK_DIM)
        k = p_k.reshape(B, L, N_HEADS, 2, QK_DIM)
        q1, q2, k1, k2 = q[..., 0, :], q[..., 1, :], k[..., 0, :], k[..., 1, :]
        if rope is not None:
            q1, q2, k1, k2 = (apply_axial_rope(t, *rope) for t in (q1, q2, k1, k2))
        k_cat = jnp.concatenate([k1, k2], axis=-1)
        v = p_v.reshape(B, L, N_HEADS, V_DIM)
        if ctx_k is not None:
            k_all = jnp.concatenate([ctx_k, k_cat], axis=1)
            v_all = jnp.concatenate([ctx_v, v], axis=1)
        else:
            k_all, v_all = k_cat, v
        lam_init = 0.8 - 0.6 * math.exp(-0.3 * l)
        lam = (jnp.exp(jnp.sum(lam_q1[l].astype(jnp.float32) * lam_k1[l].astype(jnp.float32)))
               - jnp.exp(jnp.sum(lam_q2[l].astype(jnp.float32) * lam_k2[l].astype(jnp.float32)))
               + lam_init)
        y_b = diff_attention(q1, q2, k_all[..., :QK_DIM], k_all[..., QK_DIM:], v_all, lam)
        y_b = (rms_norm(y_b, subln_g[l]) * (1.0 - lam_init)).reshape(B, L, ATTN_WIDTH)

        y_c = chunk_spatial_gating(p_uv, sgu_norm_g[l], w_sgu[l], b_sgu[l])

        y = jnp.concatenate([y_a, y_b, y_c], axis=-1) @ w_out[l]
        x = x + g1 * y
        h2 = rms_norm(x, norm2_g[l]) * (1.0 + sc2) + sh2
        gate, up = jnp.split(h2 @ w_ffn_in[l], 2, axis=-1)
        x = x + g2 * ((jax.nn.silu(gate) * up) @ w_ffn_out[l])
        return x, k_cat, v

    cond_ctx = c_ctx[None, :]
    xp = x_prompt
    ks_new, vs_new = [], []
    for l in range(DEPTH):
        xp, k_l, v_l = layer(xp, l, cond_ctx, None, None, None)
        ks_new.append(k_l)
        vs_new.append(v_l)
    y_prompt = rms_norm(xp, final_g)
    new_cache_k = jnp.stack(ks_new, axis=1)
    new_cache_v = jnp.stack(vs_new, axis=1)

    rope = axial_rope_tables(x_sample.shape[1])
    xs = x_sample
    for l in range(DEPTH):
        xs, _, _ = layer(xs, l, c, rope, cache_k[:, l], cache_v[:, l])
    y_sample = rms_norm(xs, final_g)

    return (y_prompt, y_sample, new_cache_k, new_cache_v)
```

```python
import functools
import math

import numpy as np
import jax
import jax.numpy as jnp
from jax import lax
from jax.experimental import pallas as pl
from jax.experimental.pallas import tpu as pltpu

D_MODEL = 1024
DEPTH = 2
GRID_W = 64
POOL_WINDOWS = (2, 4, 8, 16)
POOL_WIDTH = 256
POOL_GROUP = 64
POOL_HALO = 8
ATTN_WIDTH = 512
N_HEADS = 4
V_DIM = 128
QK_DIM = 64
HEAD_COLS = 2 * QK_DIM
ROPE_BASE = 10000.0
ROPE_AXIS_DIM = 32
ROPE_HALF = ROPE_AXIS_DIM // 2
CHUNK = 128
SGU_WIDTH = 256
QK_WIDTH = 512
IN_WIDTH = 2304
MIX_WIDTH = 1024
D_FF = 2816
N_MOD = 6
EPS = 1e-6
PAST_LEN = 256

C_POOL, C_Q, C_K, C_V, C_UV = 0, 256, 768, 1280, 1792
Y_POOL, Y_ATTN, Y_SGU = 0, 256, 768

ROWS = 1024
ROW_CHUNK = 256
FFN_ROWS = 512
COND_ROWS = 16
ADA_COLS = 1024
VMEM_LIMIT = 56 * 1024 * 1024

BF16 = jnp.bfloat16
F32 = jnp.float32


def _dot(a, b):
    return jnp.dot(a, b, preferred_element_type=F32)


def _dot_nt(a, b):
    return lax.dot_general(a, b, (((1,), (1,)), ((), ())), preferred_element_type=F32)


def _sigmoid(x):
    return 1.0 / (1.0 + jnp.exp(-x))


def _rms_mod(x, g, scale, shift):
    ms = jnp.mean(x * x, axis=-1, keepdims=True)
    y = x * lax.rsqrt(ms + EPS) * g
    return y * (1.0 + scale) + shift


def _gelu_tanh(x):
    c = math.sqrt(2.0 / math.pi)
    return x * (0.5 * (1.0 + jnp.tanh(c * (x + 0.044715 * (x * x * x)))))


def _ada_kernel(cond_ref, w_ref, b_ref, out_ref):
    cond = cond_ref[...]
    s = (cond * _sigmoid(cond)).astype(BF16)
    out_ref[...] = _dot(s, w_ref[...].astype(BF16)) + b_ref[...]


def _ada_modulation(cond, w_ada, b_ada):
    n_cols = N_MOD * D_MODEL
    return pl.pallas_call(
        _ada_kernel,
        out_shape=jax.ShapeDtypeStruct((DEPTH, COND_ROWS, n_cols), F32),
        grid=(DEPTH, n_cols // ADA_COLS),
        in_specs=[
            pl.BlockSpec((COND_ROWS, D_MODEL), lambda l, j: (0, 0)),
            pl.BlockSpec((None, D_MODEL, ADA_COLS), lambda l, j: (l, 0, j)),
            pl.BlockSpec((None, 1, ADA_COLS), lambda l, j: (l, 0, j)),
        ],
        out_specs=pl.BlockSpec((None, COND_ROWS, ADA_COLS), lambda l, j: (l, 0, j)),
        compiler_params=pltpu.CompilerParams(
            dimension_semantics=("arbitrary", "arbitrary"), vmem_limit_bytes=VMEM_LIMIT),
        name="ada_modulation",
    )(cond, w_ada, b_ada.reshape(DEPTH, 1, n_cols))


def _mixer_kernel(*refs, nb, seq, has_ctx, lam_init):
    it = iter(refs)
    x_ref, mod_ref, g1_ref, w_in_ref = next(it), next(it), next(it), next(it)
    if has_ctx:
        rope_c_ref, rope_s_ref, ck_ref, cv_ref = next(it), next(it), next(it), next(it)
    lam_ref, subln_ref = next(it), next(it)
    cnt_ref, wpool_ref, pscale_ref = next(it), next(it), next(it)
    sgu_g_ref, wsgu_ref, bsgu_ref = next(it), next(it), next(it)
    w_out_ref = next(it)
    out_ref = next(it)
    if not has_ctx:
        kc_out_ref, vc_out_ref = next(it), next(it)
    pool_scr, q_scr, k_scr, v_scr, u_scr, vs_scr, y_scr = (next(it) for _ in range(7))

    kv_base = PAST_LEN if has_ctx else 0
    kv_len = seq + kv_base
    shift1, scale1, gate1 = mod_ref[0:1, :], mod_ref[1:2, :], mod_ref[2:3, :]

    lane_head = lax.broadcasted_iota(jnp.int32, (ROW_CHUNK, HEAD_COLS), 1)
    first_half = lane_head < QK_DIM

    for b in range(nb):
        pool_scr[b, 0:POOL_HALO, :] = jnp.zeros((POOL_HALO, POOL_WIDTH), F32)
        pool_scr[b, POOL_HALO + seq:POOL_HALO + seq + POOL_HALO, :] = jnp.zeros((POOL_HALO, POOL_WIDTH), F32)
    if has_ctx:
        k_scr[0, 0:PAST_LEN, :] = ck_ref[...].astype(BF16)
        v_scr[0, 0:PAST_LEN, :] = cv_ref[...].astype(BF16)
        lane_q = lax.broadcasted_iota(jnp.int32, (ROW_CHUNK, QK_WIDTH), 1)
        rope_low = (lane_q % ROPE_AXIS_DIM) < ROPE_HALF

        def rope(t, pos0):
            partner = jnp.where(rope_low,
                                pltpu.roll(t, QK_WIDTH - ROPE_HALF, 1),
                                pltpu.roll(t, ROPE_HALF, 1))
            cos = rope_c_ref[pos0:pos0 + ROW_CHUNK, :]
            sin = rope_s_ref[pos0:pos0 + ROW_CHUNK, :]
            cos = jnp.concatenate([cos] * N_HEADS, axis=1)
            sin = jnp.concatenate([sin] * N_HEADS, axis=1)
            return t * cos + partner * sin

    for c in range(ROWS // ROW_CHUNK):
        r0 = c * ROW_CHUNK
        b, pos0 = r0 // seq, r0 % seq
        rows = slice(r0, r0 + ROW_CHUNK)
        h = _rms_mod(x_ref[rows, :], g1_ref[...], scale1, shift1).astype(BF16)

        pool_scr[b, POOL_HALO + pos0:POOL_HALO + pos0 + ROW_CHUNK, :] = _dot(h, w_in_ref[:, C_POOL:C_Q])

        q = _dot(h, w_in_ref[:, C_Q:C_K])
        k = _dot(h, w_in_ref[:, C_K:C_V])
        if has_ctx:
            q = rope(q, pos0)
            k = rope(k, pos0)
        else:
            kc_out_ref[b, pos0:pos0 + ROW_CHUNK, :] = k
        q = q * (QK_DIM ** -0.5)
        for hd in range(N_HEADS):
            cols = slice(hd * HEAD_COLS, (hd + 1) * HEAD_COLS)
            qh = q[:, cols]
            q_scr[0, rows, cols] = jnp.where(first_half, qh, 0.0).astype(BF16)
            q_scr[1, rows, cols] = jnp.where(first_half, 0.0, qh).astype(BF16)
        k_scr[b, kv_base + pos0:kv_base + pos0 + ROW_CHUNK, :] = k.astype(BF16)

        v = _dot(h, w_in_ref[:, C_V:C_UV])
        if not has_ctx:
            vc_out_ref[b, pos0:pos0 + ROW_CHUNK, :] = v
        v_scr[b, kv_base + pos0:kv_base + pos0 + ROW_CHUNK, :] = v.astype(BF16)

        uv = _gelu_tanh(_dot(h, w_in_ref[:, C_UV:IN_WIDTH]))
        u_scr[rows, :] = uv[:, :SGU_WIDTH]
        vv = uv[:, SGU_WIDTH:]
        mu = jnp.mean(vv, axis=-1, keepdims=True)
        vc = vv - mu
        vn = vc * lax.rsqrt(jnp.mean(vc * vc, axis=-1, keepdims=True) + EPS) * sgu_g_ref[...]
        vs_scr[rows, :] = vn.astype(BF16)

    lane_pool = lax.broadcasted_iota(jnp.int32, (ROW_CHUNK, 2 * POOL_GROUP), 1)
    narrow = lane_pool < POOL_GROUP
    for c in range(ROWS // ROW_CHUNK):
        r0 = c * ROW_CHUNK
        b, pos0 = r0 // seq, r0 % seq
        base = POOL_HALO + pos0
        pooled = []
        for j, (w_small, w_big) in enumerate(((POOL_WINDOWS[0], POOL_WINDOWS[1]),
                                              (POOL_WINDOWS[2], POOL_WINDOWS[3]))):
            cols = slice(j * 2 * POOL_GROUP, (j + 1) * 2 * POOL_GROUP)
            s_small = jnp.zeros((ROW_CHUNK, 2 * POOL_GROUP), F32)
            s_rest = jnp.zeros((ROW_CHUNK, 2 * POOL_GROUP), F32)
            for d in range(-(w_big // 2), w_big - w_big // 2):
                t = pool_scr[b, base + d:base + d + ROW_CHUNK, cols]
                if -(w_small // 2) <= d < w_small - w_small // 2:
                    s_small = s_small + t
                else:
                    s_rest = s_rest + t
            win_sum = jnp.where(narrow, s_small, s_small + s_rest)
            centre = pool_scr[b, base:base + ROW_CHUNK, cols]
            pooled.append(win_sum / cnt_ref[pos0:pos0 + ROW_CHUNK, cols] - centre)
        pooled = jnp.concatenate(pooled, axis=1).astype(BF16)
        y_a = _dot(pooled, wpool_ref[...]) * pscale_ref[...]
        y_scr[r0:r0 + ROW_CHUNK, Y_POOL:Y_ATTN] = y_a.astype(BF16)

    lq1, lk1, lq2, lk2 = lam_ref[0:1, :], lam_ref[1:2, :], lam_ref[2:3, :], lam_ref[3:4, :]
    lam = (jnp.exp(jnp.sum(lq1 * lk1, axis=-1, keepdims=True))
           - jnp.exp(jnp.sum(lq2 * lk2, axis=-1, keepdims=True)) + lam_init)
    n_qb = seq // ROW_CHUNK

    def attn_block(idx, carry):
        b = idx // n_qb
        r0 = pl.multiple_of(idx * ROW_CHUNK, ROW_CHUNK)
        for hd in range(N_HEADS):
            cols = slice(hd * HEAD_COLS, (hd + 1) * HEAD_COLS)
            kh = k_scr[b, :, cols]
            vh = v_scr[b, :, cols]
            s1 = _dot_nt(q_scr[0, pl.ds(r0, ROW_CHUNK), cols], kh)
            s2 = _dot_nt(q_scr[1, pl.ds(r0, ROW_CHUNK), cols], kh)
            e1 = jnp.exp(s1 - jnp.max(s1, axis=-1, keepdims=True))
            e2 = jnp.exp(s2 - jnp.max(s2, axis=-1, keepdims=True))
            r1 = 1.0 / jnp.sum(e1, axis=-1, keepdims=True)
            r2 = lam / jnp.sum(e2, axis=-1, keepdims=True)
            p = (e1 * r1 - e2 * r2).astype(BF16)
            o = _dot(p, vh)
            o = o * lax.rsqrt(jnp.mean(o * o, axis=-1, keepdims=True) + EPS) * subln_ref[...]
            o = o * (1.0 - lam_init)
            y_scr[pl.ds(r0, ROW_CHUNK), Y_ATTN + hd * V_DIM:Y_ATTN + (hd + 1) * V_DIM] = o.astype(BF16)
        return carry

    lax.fori_loop(0, ROWS // ROW_CHUNK, attn_block, 0)

    lane_sgu = lax.broadcasted_iota(jnp.int32, (CHUNK, 2 * POOL_GROUP), 1)
    sgu_first = lane_sgu < (SGU_WIDTH // 4)

    def sgu_block(idx, carry):
        r0 = pl.multiple_of(idx * CHUNK, CHUNK)
        for j in range(2):
            cols = slice(j * 128, (j + 1) * 128)
            t = _dot(wsgu_ref[j], vs_scr[pl.ds(r0, CHUNK), cols])
            mixed = jnp.where(sgu_first, t[:CHUNK, :], t[CHUNK:, :]) + bsgu_ref[:, cols]
            y_c = u_scr[pl.ds(r0, CHUNK), cols] * mixed
            y_scr[pl.ds(r0, CHUNK), Y_SGU + j * 128:Y_SGU + (j + 1) * 128] = y_c.astype(BF16)
        return carry

    lax.fori_loop(0, ROWS // CHUNK, sgu_block, 0)

    for c in range(ROWS // ROW_CHUNK):
        rows = slice(c * ROW_CHUNK, (c + 1) * ROW_CHUNK)
        y = _dot(y_scr[rows, :], w_out_ref[...])
        out_ref[rows, :] = x_ref[rows, :] + gate1 * y


def _const_spec(shape):
    zeros = (0,) * len(shape)
    return pl.BlockSpec(shape, lambda i: zeros, pipeline_mode=pl.Buffered(1))


def _mixer_call(x2d, mod4, layer, seq, has_ctx, lam_init, weights, ctx=None):
    n_rows = x2d.shape[0]
    nb = ROWS // seq
    n_steps = n_rows // ROWS
    n_seq = n_rows // seq
    kv_len = seq + (PAST_LEN if has_ctx else 0)

    if has_ctx:
        mod_map = lambda i: (layer, 1 + i, 0, 0)
    else:
        mod_map = lambda i: (layer, 0, 0, 0)

    operands = [x2d, mod4, weights["norm1_g"], weights["w_in"]]
    in_specs = [
        pl.BlockSpec((ROWS, D_MODEL), lambda i: (i, 0)),
        pl.BlockSpec((None, None, N_MOD, D_MODEL), mod_map),
        _const_spec((1, D_MODEL)),
        _const_spec((D_MODEL, IN_WIDTH)),
    ]
    if has_ctx:
        rope_c, rope_s, cache_k, cache_v = ctx
        operands += [rope_c, rope_s, cache_k, cache_v]
        in_specs += [
            _const_spec((seq, HEAD_COLS)),
            _const_spec((seq, HEAD_COLS)),
            pl.BlockSpec((None, None, PAST_LEN, QK_WIDTH), lambda i: (i, layer, 0, 0)),
            pl.BlockSpec((None, None, PAST_LEN, ATTN_WIDTH), lambda i: (i, layer, 0, 0)),
        ]
    operands += [weights["lam"], weights["subln_g"], weights["pool_cnt"], weights["w_pool"],
                 weights["pool_scale"], weights["sgu_norm_g"], weights["w_sgu"], weights["b_sgu"],
                 weights["w_out"]]
    in_specs += [
        _const_spec((8, 128)),
        _const_spec((1, V_DIM)),
        _const_spec((seq, POOL_WIDTH)),
        _const_spec((POOL_WIDTH, POOL_WIDTH)),
        _const_spec((1, POOL_WIDTH)),
        _const_spec((1, SGU_WIDTH)),
        _const_spec((2, 2 * CHUNK, CHUNK)),
        _const_spec((CHUNK, SGU_WIDTH)),
        _const_spec((MIX_WIDTH, D_MODEL)),
    ]

    out_shape = [jax.ShapeDtypeStruct((n_rows, D_MODEL), F32)]
    out_specs = [pl.BlockSpec((ROWS, D_MODEL), lambda i: (i, 0))]
    if not has_ctx:
        out_shape += [jax.ShapeDtypeStruct((n_seq, seq, QK_WIDTH), F32),
                      jax.ShapeDtypeStruct((n_seq, seq, ATTN_WIDTH), F32)]
        out_specs += [pl.BlockSpec((nb, seq, QK_WIDTH), lambda i: (i, 0, 0)),
                      pl.BlockSpec((nb, seq, ATTN_WIDTH), lambda i: (i, 0, 0))]

    scratch = [
        pltpu.VMEM((nb, seq + 2 * POOL_HALO, POOL_WIDTH), F32),
        pltpu.VMEM((2, ROWS, QK_WIDTH), BF16),
        pltpu.VMEM((nb, kv_len, QK_WIDTH), BF16),
        pltpu.VMEM((nb, kv_len, ATTN_WIDTH), BF16),
        pltpu.VMEM((ROWS, SGU_WIDTH), F32),
        pltpu.VMEM((ROWS, SGU_WIDTH), BF16),
        pltpu.VMEM((ROWS, MIX_WIDTH), BF16),
    ]
    kernel = functools.partial(_mixer_kernel, nb=nb, seq=seq, has_ctx=has_ctx, lam_init=lam_init)
    return pl.pallas_call(
        kernel,
        out_shape=out_shape,
        grid=(n_steps,),
        in_specs=in_specs,
        out_specs=out_specs,
        scratch_shapes=scratch,
        compiler_params=pltpu.CompilerParams(
            dimension_semantics=("arbitrary",), vmem_limit_bytes=VMEM_LIMIT),
        name="mixer_ctx" if has_ctx else "mixer_prompt",
    )(*operands)


def _ffn_kernel(x_ref, mod_ref, g2_ref, w1_ref, w2_ref, gf_ref, out_ref, act_scr, *, final_norm):
    shift2, scale2, gate2 = mod_ref[3:4, :], mod_ref[4:5, :], mod_ref[5:6, :]
    x = x_ref[...]
    h = _rms_mod(x, g2_ref[...], scale2, shift2).astype(BF16)
    half = D_FF // 2
    for c in range(2):
        gate = _dot(h, w1_ref[:, c * half:(c + 1) * half])
        up = _dot(h, w1_ref[:, D_FF + c * half:D_FF + (c + 1) * half])
        act_scr[:, c * half:(c + 1) * half] = (gate * _sigmoid(gate) * up).astype(BF16)
    y = x + gate2 * _dot(act_scr[...], w2_ref[...])
    if final_norm:
        ms = jnp.mean(y * y, axis=-1, keepdims=True)
        y = y * lax.rsqrt(ms + EPS) * gf_ref[...]
    out_ref[...] = y


def _ffn_call(x2d, mod4, layer, seq, has_ctx, weights, final_norm):
    n_rows = x2d.shape[0]
    steps_per_seq = max(seq // FFN_ROWS, 1)
    if has_ctx:
        mod_map = lambda i: (layer, 1 + i // steps_per_seq, 0, 0)
    else:
        mod_map = lambda i: (layer, 0, 0, 0)
    kernel = functools.partial(_ffn_kernel, final_norm=final_norm)
    return pl.pallas_call(
        kernel,
        out_shape=jax.ShapeDtypeStruct((n_rows, D_MODEL), F32),
        grid=(n_rows // FFN_ROWS,),
        in_specs=[
            pl.BlockSpec((FFN_ROWS, D_MODEL), lambda i: (i, 0)),
            pl.BlockSpec((None, None, N_MOD, D_MODEL), mod_map),
            _const_spec((1, D_MODEL)),
            _const_spec((D_MODEL, 2 * D_FF)),
            _const_spec((D_FF, D_MODEL)),
            _const_spec((1, D_MODEL)),
        ],
        out_specs=pl.BlockSpec((FFN_ROWS, D_MODEL), lambda i: (i, 0)),
        scratch_shapes=[pltpu.VMEM((FFN_ROWS, D_FF), BF16)],
        compiler_params=pltpu.CompilerParams(
            dimension_semantics=("arbitrary",), vmem_limit_bytes=VMEM_LIMIT),
        name="ffn",
    )(x2d, mod4, weights["norm2_g"], weights["w_ffn_in"], weights["w_ffn_out"], weights["final_g"])


def _rope_tables(seq):
    n_rows = seq // GRID_W
    rows = np.repeat(np.arange(n_rows), GRID_W).astype(np.float32)
    cols = np.tile(np.arange(GRID_W), n_rows).astype(np.float32)
    inv = 1.0 / (ROPE_BASE ** (np.arange(0, ROPE_AXIS_DIM, 2, dtype=np.float32) / ROPE_AXIS_DIM))
    ar, ac = rows[:, None] * inv[None], cols[:, None] * inv[None]
    cos_parts, sin_parts = [], []
    for ang in (ar, ac):
        cos_parts += [np.cos(ang), np.cos(ang)]
        sin_parts += [-np.sin(ang), np.sin(ang)]
    cos64 = np.concatenate(cos_parts, axis=1)
    sin64 = np.concatenate(sin_parts, axis=1)
    cos = np.concatenate([cos64, cos64], axis=1).astype(np.float32)
    sin = np.concatenate([sin64, sin64], axis=1).astype(np.float32)
    return jnp.asarray(cos), jnp.asarray(sin)


def _pool_counts(seq):
    t = np.arange(seq)
    cols = []
    for w in POOL_WINDOWS:
        lo = np.clip(t - w // 2, 0, seq)
        hi = np.clip(t + w - w // 2, 0, seq)
        cols.append(np.repeat((hi - lo).astype(np.float32)[:, None], POOL_GROUP, axis=1))
    return jnp.asarray(np.concatenate(cols, axis=1))


def _layer_weights(l, seq, norm1_g, w_in, w_pool, pool_scale, lam_q1, lam_k1, lam_q2, lam_k2,
                   subln_g, sgu_norm_g, w_sgu, b_sgu, w_out, norm2_g, w_ffn_in, w_ffn_out, final_g):
    n_groups = len(POOL_WINDOWS)
    eye = jnp.eye(n_groups, dtype=F32)
    w_pool_bd = (eye[:, None, :, None] * w_pool[l][:, :, None, :]).reshape(POOL_WIDTH, POOL_WIDTH)
    lam = jnp.zeros((8, 128), F32)
    lam = lam.at[0, :QK_DIM].set(lam_q1[l]).at[1, :QK_DIM].set(lam_k1[l])
    lam = lam.at[2, :QK_DIM].set(lam_q2[l]).at[3, :QK_DIM].set(lam_k2[l])
    return {
        "norm1_g": norm1_g[l][None, :],
        "w_in": w_in[l].astype(BF16),
        "lam": lam,
        "subln_g": subln_g[l][None, :],
        "pool_cnt": _pool_counts(seq),
        "w_pool": w_pool_bd.astype(BF16),
        "pool_scale": pool_scale[l][None, :],
        "sgu_norm_g": sgu_norm_g[l][None, :],
        "w_sgu": w_sgu[l].reshape(2, 2 * CHUNK, CHUNK).astype(BF16),
        "b_sgu": jnp.repeat(b_sgu[l].T, SGU_WIDTH // 4, axis=1),
        "w_out": w_out[l].astype(BF16),
        "norm2_g": norm2_g[l][None, :],
        "w_ffn_in": w_ffn_in[l].astype(BF16),
        "w_ffn_out": w_ffn_out[l].astype(BF16),
        "final_g": final_g[None, :],
    }


def kernel(x_prompt, x_sample, cache_k, cache_v, c, c_ctx, norm1_g, w_ada, b_ada, w_in, w_pool, pool_scale, lam_q1, lam_k1, lam_q2, lam_k2, subln_g, sgu_norm_g, w_sgu, b_sgu, w_out, norm2_g, w_ffn_in, w_ffn_out, final_g):
    batch, seq, _ = x_prompt.shape
    dec_batch, dec_seq, _ = x_sample.shape
    assert ROWS % seq == 0 and dec_seq == ROWS and 1 + dec_batch <= COND_ROWS

    cond = jnp.zeros((COND_ROWS, D_MODEL), F32).at[0].set(c_ctx).at[1:1 + dec_batch].set(c)
    mod4 = _ada_modulation(cond, w_ada, b_ada).reshape(DEPTH, COND_ROWS, N_MOD, D_MODEL)

    rope_c, rope_s = _rope_tables(dec_seq)
    ck = cache_k.reshape(dec_batch, DEPTH, PAST_LEN, QK_WIDTH)
    cv = cache_v.reshape(dec_batch, DEPTH, PAST_LEN, ATTN_WIDTH)

    xp = x_prompt.reshape(batch * seq, D_MODEL)
    xs = x_sample.reshape(dec_batch * dec_seq, D_MODEL)
    ks_new, vs_new = [], []
    for l in range(DEPTH):
        lam_init = 0.8 - 0.6 * math.exp(-0.3 * l)
        args = (norm1_g, w_in, w_pool, pool_scale, lam_q1, lam_k1, lam_q2, lam_k2, subln_g,
                sgu_norm_g, w_sgu, b_sgu, w_out, norm2_g, w_ffn_in, w_ffn_out, final_g)
        wp = _layer_weights(l, seq, *args)
        ws = dict(wp, pool_cnt=_pool_counts(dec_seq))
        last = l == DEPTH - 1

        xp, k_l, v_l = _mixer_call(xp, mod4, l, seq, False, lam_init, wp)
        ks_new.append(k_l)
        vs_new.append(v_l)
        xp = _ffn_call(xp, mod4, l, seq, False, wp, last)

        xs = _mixer_call(xs, mod4, l, dec_seq, True, lam_init, ws, ctx=(rope_c, rope_s, ck, cv))[0]
        xs = _ffn_call(xs, mod4, l, dec_seq, True, ws, last)

    y_prompt = xp.reshape(batch, seq, D_MODEL)
    y_sample = xs.reshape(dec_batch, dec_seq, D_MODEL)
    new_cache_k = jnp.stack(ks_new, axis=1).reshape(batch, DEPTH, seq, N_HEADS, 2 * QK_DIM)
    new_cache_v = jnp.stack(vs_new, axis=1).reshape(batch, DEPTH, seq, N_HEADS, V_DIM)
    return (y_prompt, y_sample, new_cache_k, new_cache_v)
```

```python
import functools
import math

import numpy as np
import jax
import jax.numpy as jnp
from jax import lax
from jax.experimental import pallas as pl
from jax.experimental.pallas import tpu as pltpu

D_MODEL = 1024
DEPTH = 2
GRID_W = 64
POOL_WINDOWS = (2, 4, 8, 16)
POOL_WIDTH = 256
POOL_GROUP = 64
POOL_HALO = 8
ATTN_WIDTH = 512
N_HEADS = 4
V_DIM = 128
QK_DIM = 64
HEAD_COLS = 2 * QK_DIM
ROPE_BASE = 10000.0
ROPE_AXIS_DIM = 32
ROPE_HALF = ROPE_AXIS_DIM // 2
CHUNK = 128
SGU_WIDTH = 256
QK_WIDTH = 512
IN_WIDTH = 2304
MIX_WIDTH = 1024
D_FF = 2816
N_MOD = 6
EPS = 1e-6
PAST_LEN = 256

C_POOL, C_Q, C_K, C_V, C_UV = 0, 256, 768, 1280, 1792
Y_POOL, Y_ATTN, Y_SGU = 0, 256, 768

ROWS = 1024
ROW_CHUNK = 256
FFN_ROWS = 512
COND_ROWS = 16
ADA_COLS = 1024
VMEM_LIMIT = 56 * 1024 * 1024

BF16 = jnp.bfloat16
F32 = jnp.float32


def _dot(a, b):
    return jnp.dot(a, b, preferred_element_type=F32)


def _dot_nt(a, b):
    return lax.dot_general(a, b, (((1,), (1,)), ((), ())), preferred_element_type=F32)


def _sigmoid(x):
    return 1.0 / (1.0 + jnp.exp(-x))


def _rms_mod(x, g, scale, shift):
    ms = jnp.mean(x * x, axis=-1, keepdims=True)
    y = x * lax.rsqrt(ms + EPS) * g
    return y * (1.0 + scale) + shift


def _gelu_tanh(x):
    c = math.sqrt(2.0 / math.pi)
    return x * (0.5 * (1.0 + jnp.tanh(c * (x + 0.044715 * (x * x * x)))))


def _ada_kernel(cond_ref, w_ref, b_ref, out_ref):
    cond = cond_ref[...]
    s = (cond * _sigmoid(cond)).astype(BF16)
    out_ref[...] = _dot(s, w_ref[...].astype(BF16)) + b_ref[...]


def _ada_modulation(cond, w_ada, b_ada):
    n_cols = N_MOD * D_MODEL
    return pl.pallas_call(
        _ada_kernel,
        out_shape=jax.ShapeDtypeStruct((DEPTH, COND_ROWS, n_cols), F32),
        grid=(DEPTH, n_cols // ADA_COLS),
        in_specs=[
            pl.BlockSpec((COND_ROWS, D_MODEL), lambda l, j: (0, 0)),
            pl.BlockSpec((None, D_MODEL, ADA_COLS), lambda l, j: (l, 0, j)),
            pl.BlockSpec((None, 1, ADA_COLS), lambda l, j: (l, 0, j)),
        ],
        out_specs=pl.BlockSpec((None, COND_ROWS, ADA_COLS), lambda l, j: (l, 0, j)),
        compiler_params=pltpu.CompilerParams(
            dimension_semantics=("arbitrary", "arbitrary"), vmem_limit_bytes=VMEM_LIMIT),
        name="ada_modulation",
    )(cond, w_ada, b_ada.reshape(DEPTH, 1, n_cols))


def _mixer_kernel(*refs, nb, seq, has_ctx, lam_init, n_passthrough):
    it = iter(refs)
    x_ref, mod_ref, g1_ref, w_in_ref = next(it), next(it), next(it), next(it)
    if has_ctx:
        rope_c_ref, rope_s_ref, ck_ref, cv_ref = next(it), next(it), next(it), next(it)
    lam_ref, subln_ref = next(it), next(it)
    cnt_ref, wpool_ref, pscale_ref = next(it), next(it), next(it)
    sgu_g_ref, wsgu_ref, bsgu_ref = next(it), next(it), next(it)
    w_out_ref = next(it)
    for _ in range(n_passthrough):
        next(it)
    out_ref = next(it)
    if not has_ctx:
        kc_out_ref, vc_out_ref = next(it), next(it)
    pool_scr, q_scr, k_scr, v_scr, u_scr, vs_scr, y_scr = (next(it) for _ in range(7))

    kv_base = PAST_LEN if has_ctx else 0
    kv_len = seq + kv_base
    shift1, scale1, gate1 = mod_ref[0:1, :], mod_ref[1:2, :], mod_ref[2:3, :]

    lane_head = lax.broadcasted_iota(jnp.int32, (ROW_CHUNK, HEAD_COLS), 1)
    first_half = lane_head < QK_DIM

    for b in range(nb):
        pool_scr[b, 0:POOL_HALO, :] = jnp.zeros((POOL_HALO, POOL_WIDTH), F32)
        pool_scr[b, POOL_HALO + seq:POOL_HALO + seq + POOL_HALO, :] = jnp.zeros((POOL_HALO, POOL_WIDTH), F32)
    if has_ctx:
        k_scr[0, 0:PAST_LEN, :] = ck_ref[...].astype(BF16)
        v_scr[0, 0:PAST_LEN, :] = cv_ref[...].astype(BF16)
        lane_q = lax.broadcasted_iota(jnp.int32, (ROW_CHUNK, QK_WIDTH), 1)
        rope_low = (lane_q % ROPE_AXIS_DIM) < ROPE_HALF

        def rope(t, pos0):
            partner = jnp.where(rope_low,
                                pltpu.roll(t, QK_WIDTH - ROPE_HALF, 1),
                                pltpu.roll(t, ROPE_HALF, 1))
            cos = rope_c_ref[pos0:pos0 + ROW_CHUNK, :]
            sin = rope_s_ref[pos0:pos0 + ROW_CHUNK, :]
            cos = jnp.concatenate([cos] * N_HEADS, axis=1)
            sin = jnp.concatenate([sin] * N_HEADS, axis=1)
            return t * cos + partner * sin

    for c in range(ROWS // ROW_CHUNK):
        r0 = c * ROW_CHUNK
        b, pos0 = r0 // seq, r0 % seq
        rows = slice(r0, r0 + ROW_CHUNK)
        h = _rms_mod(x_ref[rows, :], g1_ref[...], scale1, shift1).astype(BF16)

        pool_scr[b, POOL_HALO + pos0:POOL_HALO + pos0 + ROW_CHUNK, :] = _dot(h, w_in_ref[:, C_POOL:C_Q])

        q = _dot(h, w_in_ref[:, C_Q:C_K])
        k = _dot(h, w_in_ref[:, C_K:C_V])
        if has_ctx:
            q = rope(q, pos0)
            k = rope(k, pos0)
        else:
            kc_out_ref[b, pos0:pos0 + ROW_CHUNK, :] = k
        q = q * (QK_DIM ** -0.5)
        for hd in range(N_HEADS):
            cols = slice(hd * HEAD_COLS, (hd + 1) * HEAD_COLS)
            qh = q[:, cols]
            q_scr[0, rows, cols] = jnp.where(first_half, qh, 0.0).astype(BF16)
            q_scr[1, rows, cols] = jnp.where(first_half, 0.0, qh).astype(BF16)
        k_scr[b, kv_base + pos0:kv_base + pos0 + ROW_CHUNK, :] = k.astype(BF16)

        v = _dot(h, w_in_ref[:, C_V:C_UV])
        if not has_ctx:
            vc_out_ref[b, pos0:pos0 + ROW_CHUNK, :] = v
        v_scr[b, kv_base + pos0:kv_base + pos0 + ROW_CHUNK, :] = v.astype(BF16)

        uv = _gelu_tanh(_dot(h, w_in_ref[:, C_UV:IN_WIDTH]))
        u_scr[rows, :] = uv[:, :SGU_WIDTH]
        vv = uv[:, SGU_WIDTH:]
        mu = jnp.mean(vv, axis=-1, keepdims=True)
        vc = vv - mu
        vn = vc * lax.rsqrt(jnp.mean(vc * vc, axis=-1, keepdims=True) + EPS) * sgu_g_ref[...]
        vs_scr[rows, :] = vn.astype(BF16)

    lane_pool = lax.broadcasted_iota(jnp.int32, (ROW_CHUNK, 2 * POOL_GROUP), 1)
    narrow = lane_pool < POOL_GROUP
    for c in range(ROWS // ROW_CHUNK):
        r0 = c * ROW_CHUNK
        b, pos0 = r0 // seq, r0 % seq
        base = POOL_HALO + pos0
        pooled = []
        for j, (w_small, w_big) in enumerate(((POOL_WINDOWS[0], POOL_WINDOWS[1]),
                                              (POOL_WINDOWS[2], POOL_WINDOWS[3]))):
            cols = slice(j * 2 * POOL_GROUP, (j + 1) * 2 * POOL_GROUP)
            s_small = jnp.zeros((ROW_CHUNK, 2 * POOL_GROUP), F32)
            s_rest = jnp.zeros((ROW_CHUNK, 2 * POOL_GROUP), F32)
            for d in range(-(w_big // 2), w_big - w_big // 2):
                t = pool_scr[b, base + d:base + d + ROW_CHUNK, cols]
                if -(w_small // 2) <= d < w_small - w_small // 2:
                    s_small = s_small + t
                else:
                    s_rest = s_rest + t
            win_sum = jnp.where(narrow, s_small, s_small + s_rest)
            centre = pool_scr[b, base:base + ROW_CHUNK, cols]
            pooled.append(win_sum / cnt_ref[pos0:pos0 + ROW_CHUNK, cols] - centre)
        pooled = jnp.concatenate(pooled, axis=1).astype(BF16)
        y_a = _dot(pooled, wpool_ref[...]) * pscale_ref[...]
        y_scr[r0:r0 + ROW_CHUNK, Y_POOL:Y_ATTN] = y_a.astype(BF16)

    lq1, lk1, lq2, lk2 = lam_ref[0:1, :], lam_ref[1:2, :], lam_ref[2:3, :], lam_ref[3:4, :]
    lam = (jnp.exp(jnp.sum(lq1 * lk1, axis=-1, keepdims=True))
           - jnp.exp(jnp.sum(lq2 * lk2, axis=-1, keepdims=True)) + lam_init)
    n_qb = seq // ROW_CHUNK

    def attn_block(idx, carry):
        b = idx // n_qb
        r0 = pl.multiple_of(idx * ROW_CHUNK, ROW_CHUNK)
        for hd in range(N_HEADS):
            cols = slice(hd * HEAD_COLS, (hd + 1) * HEAD_COLS)
            kh = k_scr[b, :, cols]
            vh = v_scr[b, :, cols]
            s1 = _dot_nt(q_scr[0, pl.ds(r0, ROW_CHUNK), cols], kh)
            s2 = _dot_nt(q_scr[1, pl.ds(r0, ROW_CHUNK), cols], kh)
            e1 = jnp.exp(s1 - jnp.max(s1, axis=-1, keepdims=True))
            e2 = jnp.exp(s2 - jnp.max(s2, axis=-1, keepdims=True))
            l1 = jnp.sum(e1, axis=-1, keepdims=True)
            l2 = jnp.sum(e2, axis=-1, keepdims=True)
            p = (e1 - e2 * (lam * l1 / l2)).astype(BF16)
            o = _dot(p, vh) * (1.0 / l1)
            o = o * lax.rsqrt(jnp.mean(o * o, axis=-1, keepdims=True) + EPS) * subln_ref[...]
            o = o * (1.0 - lam_init)
            y_scr[pl.ds(r0, ROW_CHUNK), Y_ATTN + hd * V_DIM:Y_ATTN + (hd + 1) * V_DIM] = o.astype(BF16)
        return carry

    lax.fori_loop(0, ROWS // ROW_CHUNK, attn_block, 0)

    lane_sgu = lax.broadcasted_iota(jnp.int32, (CHUNK, 2 * POOL_GROUP), 1)
    sgu_first = lane_sgu < (SGU_WIDTH // 4)

    def sgu_block(idx, carry):
        r0 = pl.multiple_of(idx * CHUNK, CHUNK)
        for j in range(2):
            cols = slice(j * 128, (j + 1) * 128)
            t = _dot(wsgu_ref[j], vs_scr[pl.ds(r0, CHUNK), cols])
            mixed = jnp.where(sgu_first, t[:CHUNK, :], t[CHUNK:, :]) + bsgu_ref[:, cols]
            y_c = u_scr[pl.ds(r0, CHUNK), cols] * mixed
            y_scr[pl.ds(r0, CHUNK), Y_SGU + j * 128:Y_SGU + (j + 1) * 128] = y_c.astype(BF16)
        return carry

    lax.fori_loop(0, ROWS // CHUNK, sgu_block, 0)

    for c in range(ROWS // ROW_CHUNK):
        rows = slice(c * ROW_CHUNK, (c + 1) * ROW_CHUNK)
        y = _dot(y_scr[rows, :], w_out_ref[...])
        out_ref[rows, :] = x_ref[rows, :] + gate1 * y


def _const_spec(shape):
    zeros = (0,) * len(shape)
    return pl.BlockSpec(shape, lambda i: zeros, pipeline_mode=pl.Buffered(1))


def _layer_spec(shape, layer):
    index = (layer,) + (0,) * len(shape)
    return pl.BlockSpec((None,) + tuple(shape), lambda i: index, pipeline_mode=pl.Buffered(1))


def _mixer_call(x2d, mod4, layer, seq, has_ctx, lam_init, weights, ctx=None, new_cache=None):
    n_rows = x2d.shape[0]
    nb = ROWS // seq
    n_steps = n_rows // ROWS
    n_seq = n_rows // seq
    kv_len = seq + (PAST_LEN if has_ctx else 0)

    if has_ctx:
        mod_map = lambda i: (layer, 1 + i, 0, 0)
    else:
        mod_map = lambda i: (layer, 0, 0, 0)

    operands = [x2d, mod4, weights["norm1_g"], weights["w_in"]]
    in_specs = [
        pl.BlockSpec((ROWS, D_MODEL), lambda i: (i, 0)),
        pl.BlockSpec((None, None, N_MOD, D_MODEL), mod_map),
        _layer_spec((1, D_MODEL), layer),
        _layer_spec((D_MODEL, IN_WIDTH), layer),
    ]
    if has_ctx:
        rope_c, rope_s, cache_k, cache_v = ctx
        operands += [rope_c, rope_s, cache_k, cache_v]
        in_specs += [
            _const_spec((seq, HEAD_COLS)),
            _const_spec((seq, HEAD_COLS)),
            pl.BlockSpec((None, None, PAST_LEN, QK_WIDTH), lambda i: (i, layer, 0, 0)),
            pl.BlockSpec((None, None, PAST_LEN, ATTN_WIDTH), lambda i: (i, layer, 0, 0)),
        ]
    operands += [weights["lam"], weights["subln_g"], weights["pool_cnt"], weights["w_pool"],
                 weights["pool_scale"], weights["sgu_norm_g"], weights["w_sgu"], weights["b_sgu"],
                 weights["w_out"]]
    in_specs += [
        _layer_spec((8, 128), layer),
        _layer_spec((1, V_DIM), layer),
        _const_spec((seq, POOL_WIDTH)),
        _layer_spec((POOL_WIDTH, POOL_WIDTH), layer),
        _layer_spec((1, POOL_WIDTH), layer),
        _layer_spec((1, SGU_WIDTH), layer),
        _layer_spec((2, 2 * CHUNK, CHUNK), layer),
        _layer_spec((CHUNK, SGU_WIDTH), layer),
        _layer_spec((MIX_WIDTH, D_MODEL), layer),
    ]

    out_shape = [jax.ShapeDtypeStruct((n_rows, D_MODEL), F32)]
    out_specs = [pl.BlockSpec((ROWS, D_MODEL), lambda i: (i, 0))]
    aliases = {}
    if not has_ctx:
        out_shape += [jax.ShapeDtypeStruct((n_seq, DEPTH, seq, QK_WIDTH), F32),
                      jax.ShapeDtypeStruct((n_seq, DEPTH, seq, ATTN_WIDTH), F32)]
        out_specs += [pl.BlockSpec((nb, None, seq, QK_WIDTH), lambda i: (i, layer, 0, 0)),
                      pl.BlockSpec((nb, None, seq, ATTN_WIDTH), lambda i: (i, layer, 0, 0))]
        if new_cache is not None:
            aliases = {len(operands): 1, len(operands) + 1: 2}
            operands += list(new_cache)
            in_specs += [pl.BlockSpec(memory_space=pl.ANY)] * 2

    scratch = [
        pltpu.VMEM((nb, seq + 2 * POOL_HALO, POOL_WIDTH), F32),
        pltpu.VMEM((2, ROWS, QK_WIDTH), BF16),
        pltpu.VMEM((nb, kv_len, QK_WIDTH), BF16),
        pltpu.VMEM((nb, kv_len, ATTN_WIDTH), BF16),
        pltpu.VMEM((ROWS, SGU_WIDTH), F32),
        pltpu.VMEM((ROWS, SGU_WIDTH), BF16),
        pltpu.VMEM((ROWS, MIX_WIDTH), BF16),
    ]
    kernel = functools.partial(_mixer_kernel, nb=nb, seq=seq, has_ctx=has_ctx, lam_init=lam_init,
                               n_passthrough=len(aliases))
    return pl.pallas_call(
        kernel,
        out_shape=out_shape,
        grid=(n_steps,),
        in_specs=in_specs,
        out_specs=out_specs,
        scratch_shapes=scratch,
        input_output_aliases=aliases,
        compiler_params=pltpu.CompilerParams(
            dimension_semantics=("arbitrary",), vmem_limit_bytes=VMEM_LIMIT),
        name="mixer_ctx" if has_ctx else "mixer_prompt",
    )(*operands)


def _ffn_kernel(x_ref, mod_ref, g2_ref, w1_ref, w2_ref, gf_ref, out_ref, act_scr, *, final_norm):
    shift2, scale2, gate2 = mod_ref[3:4, :], mod_ref[4:5, :], mod_ref[5:6, :]
    x = x_ref[...]
    h = _rms_mod(x, g2_ref[...], scale2, shift2).astype(BF16)
    half = D_FF // 2
    for c in range(2):
        gate = _dot(h, w1_ref[:, c * half:(c + 1) * half])
        up = _dot(h, w1_ref[:, D_FF + c * half:D_FF + (c + 1) * half])
        act_scr[:, c * half:(c + 1) * half] = (gate * _sigmoid(gate) * up).astype(BF16)
    y = x + gate2 * _dot(act_scr[...], w2_ref[...])
    if final_norm:
        ms = jnp.mean(y * y, axis=-1, keepdims=True)
        y = y * lax.rsqrt(ms + EPS) * gf_ref[...]
    out_ref[...] = y


def _ffn_call(x2d, mod4, layer, seq, has_ctx, weights, final_norm):
    n_rows = x2d.shape[0]
    steps_per_seq = max(seq // FFN_ROWS, 1)
    if has_ctx:
        mod_map = lambda i: (layer, 1 + i // steps_per_seq, 0, 0)
    else:
        mod_map = lambda i: (layer, 0, 0, 0)
    kernel = functools.partial(_ffn_kernel, final_norm=final_norm)
    return pl.pallas_call(
        kernel,
        out_shape=jax.ShapeDtypeStruct((n_rows, D_MODEL), F32),
        grid=(n_rows // FFN_ROWS,),
        in_specs=[
            pl.BlockSpec((FFN_ROWS, D_MODEL), lambda i: (i, 0)),
            pl.BlockSpec((None, None, N_MOD, D_MODEL), mod_map),
            _layer_spec((1, D_MODEL), layer),
            _layer_spec((D_MODEL, 2 * D_FF), layer),
            _layer_spec((D_FF, D_MODEL), layer),
            _const_spec((1, D_MODEL)),
        ],
        out_specs=pl.BlockSpec((FFN_ROWS, D_MODEL), lambda i: (i, 0)),
        scratch_shapes=[pltpu.VMEM((FFN_ROWS, D_FF), BF16)],
        compiler_params=pltpu.CompilerParams(
            dimension_semantics=("arbitrary",), vmem_limit_bytes=VMEM_LIMIT),
        name="ffn",
    )(x2d, mod4, weights["norm2_g"], weights["w_ffn_in"], weights["w_ffn_out"], weights["final_g"])


def _rope_tables(seq):
    n_rows = seq // GRID_W
    rows = np.repeat(np.arange(n_rows), GRID_W).astype(np.float32)
    cols = np.tile(np.arange(GRID_W), n_rows).astype(np.float32)
    inv = 1.0 / (ROPE_BASE ** (np.arange(0, ROPE_AXIS_DIM, 2, dtype=np.float32) / ROPE_AXIS_DIM))
    ar, ac = rows[:, None] * inv[None], cols[:, None] * inv[None]
    cos_parts, sin_parts = [], []
    for ang in (ar, ac):
        cos_parts += [np.cos(ang), np.cos(ang)]
        sin_parts += [-np.sin(ang), np.sin(ang)]
    cos64 = np.concatenate(cos_parts, axis=1)
    sin64 = np.concatenate(sin_parts, axis=1)
    cos = np.concatenate([cos64, cos64], axis=1).astype(np.float32)
    sin = np.concatenate([sin64, sin64], axis=1).astype(np.float32)
    return jnp.asarray(cos), jnp.asarray(sin)


def _pool_counts(seq):
    t = np.arange(seq)
    cols = []
    for w in POOL_WINDOWS:
        lo = np.clip(t - w // 2, 0, seq)
        hi = np.clip(t + w - w // 2, 0, seq)
        cols.append(np.repeat((hi - lo).astype(np.float32)[:, None], POOL_GROUP, axis=1))
    return jnp.asarray(np.concatenate(cols, axis=1))


def _stacked_weights(norm1_g, w_in, w_pool, pool_scale, lam_q1, lam_k1, lam_q2, lam_k2, subln_g,
                     sgu_norm_g, w_sgu, b_sgu, w_out, norm2_g, w_ffn_in, w_ffn_out, final_g):
    n_groups = len(POOL_WINDOWS)
    eye = jnp.eye(n_groups, dtype=F32)
    w_pool_bd = (eye[None, :, None, :, None] * w_pool[:, :, :, None, :]).reshape(DEPTH, POOL_WIDTH, POOL_WIDTH)
    lam = jnp.zeros((DEPTH, 8, 128), F32)
    lam = lam.at[:, 0, :QK_DIM].set(lam_q1).at[:, 1, :QK_DIM].set(lam_k1)
    lam = lam.at[:, 2, :QK_DIM].set(lam_q2).at[:, 3, :QK_DIM].set(lam_k2)
    return {
        "norm1_g": norm1_g[:, None, :],
        "w_in": w_in.astype(BF16),
        "lam": lam,
        "subln_g": subln_g[:, None, :],
        "w_pool": w_pool_bd.astype(BF16),
        "pool_scale": pool_scale[:, None, :],
        "sgu_norm_g": sgu_norm_g[:, None, :],
        "w_sgu": w_sgu.reshape(DEPTH, 2, 2 * CHUNK, CHUNK).astype(BF16),
        "b_sgu": jnp.repeat(jnp.swapaxes(b_sgu, 1, 2), SGU_WIDTH // 4, axis=2),
        "w_out": w_out.astype(BF16),
        "norm2_g": norm2_g[:, None, :],
        "w_ffn_in": w_ffn_in.astype(BF16),
        "w_ffn_out": w_ffn_out.astype(BF16),
        "final_g": final_g[None, :],
    }


def kernel(x_prompt, x_sample, cache_k, cache_v, c, c_ctx, norm1_g, w_ada, b_ada, w_in, w_pool, pool_scale, lam_q1, lam_k1, lam_q2, lam_k2, subln_g, sgu_norm_g, w_sgu, b_sgu, w_out, norm2_g, w_ffn_in, w_ffn_out, final_g):
    batch, seq, _ = x_prompt.shape
    dec_batch, dec_seq, _ = x_sample.shape
    assert ROWS % seq == 0 and dec_seq == ROWS and 1 + dec_batch <= COND_ROWS

    cond = jnp.zeros((COND_ROWS, D_MODEL), F32).at[0].set(c_ctx).at[1:1 + dec_batch].set(c)
    mod4 = _ada_modulation(cond, w_ada, b_ada).reshape(DEPTH, COND_ROWS, N_MOD, D_MODEL)

    rope_c, rope_s = _rope_tables(dec_seq)
    ck = cache_k.reshape(dec_batch, DEPTH, PAST_LEN, QK_WIDTH)
    cv = cache_v.reshape(dec_batch, DEPTH, PAST_LEN, ATTN_WIDTH)

    weights = _stacked_weights(norm1_g, w_in, w_pool, pool_scale, lam_q1, lam_k1, lam_q2, lam_k2, subln_g,
                               sgu_norm_g, w_sgu, b_sgu, w_out, norm2_g, w_ffn_in, w_ffn_out, final_g)
    wp = dict(weights, pool_cnt=_pool_counts(seq))
    ws = dict(weights, pool_cnt=_pool_counts(dec_seq))

    xp = x_prompt.reshape(batch * seq, D_MODEL)
    xs = x_sample.reshape(dec_batch * dec_seq, D_MODEL)
    new_cache = None
    for l in range(DEPTH):
        lam_init = 0.8 - 0.6 * math.exp(-0.3 * l)
        last = l == DEPTH - 1

        xp, new_k, new_v = _mixer_call(xp, mod4, l, seq, False, lam_init, wp, new_cache=new_cache)
        new_cache = (new_k, new_v)
        xp = _ffn_call(xp, mod4, l, seq, False, wp, last)

        xs = _mixer_call(xs, mod4, l, dec_seq, True, lam_init, ws, ctx=(rope_c, rope_s, ck, cv))[0]
        xs = _ffn_call(xs, mod4, l, dec_seq, True, ws, last)

    y_prompt = xp.reshape(batch, seq, D_MODEL)
    y_sample = xs.reshape(dec_batch, dec_seq, D_MODEL)
    new_cache_k = new_cache[0].reshape(batch, DEPTH, seq, N_HEADS, 2 * QK_DIM)
    new_cache_v = new_cache[1].reshape(batch, DEPTH, seq, N_HEADS, V_DIM)
    return (y_prompt, y_sample, new_cache_k, new_cache_v)
```

```python
import functools
import math

import numpy as np
import jax
import jax.numpy as jnp
from jax import lax
from jax.experimental import pallas as pl
from jax.experimental.pallas import tpu as pltpu

D_MODEL = 1024
DEPTH = 2
GRID_W = 64
POOL_WINDOWS = (2, 4, 8, 16)
POOL_WIDTH = 256
POOL_GROUP = 64
POOL_HALO = 8
ATTN_WIDTH = 512
N_HEADS = 4
V_DIM = 128
QK_DIM = 64
HEAD_COLS = 2 * QK_DIM
ROPE_BASE = 10000.0
ROPE_AXIS_DIM = 32
ROPE_HALF = ROPE_AXIS_DIM // 2
CHUNK = 128
SGU_WIDTH = 256
QK_WIDTH = 512
IN_WIDTH = 2304
MIX_WIDTH = 1024
D_FF = 2816
N_MOD = 6
EPS = 1e-6
PAST_LEN = 256

C_POOL, C_Q, C_K, C_V, C_UV = 0, 256, 768, 1280, 1792
Y_POOL, Y_ATTN, Y_SGU = 0, 256, 768

ROWS = 1024
ROW_CHUNK = 256
FFN_ROWS = 512
COND_ROWS = 16
ADA_COLS = 1024
VMEM_LIMIT = 56 * 1024 * 1024

BF16 = jnp.bfloat16
F32 = jnp.float32


def _dot(a, b):
    return jnp.dot(a, b, preferred_element_type=F32)


def _dot_nt(a, b):
    return lax.dot_general(a, b, (((1,), (1,)), ((), ())), preferred_element_type=F32)


def _sigmoid(x):
    return 1.0 / (1.0 + jnp.exp(-x))


def _rms_mod(x, g, scale, shift):
    ms = jnp.mean(x * x, axis=-1, keepdims=True)
    y = x * lax.rsqrt(ms + EPS) * g
    return y * (1.0 + scale) + shift


def _gelu_tanh(x):
    c = math.sqrt(2.0 / math.pi)
    return x * (0.5 * (1.0 + jnp.tanh(c * (x + 0.044715 * (x * x * x)))))


def _ada_kernel(cond_ref, w_ref, b_ref, out_ref):
    cond = cond_ref[...]
    s = (cond * _sigmoid(cond)).astype(BF16)
    out_ref[...] = _dot(s, w_ref[...].astype(BF16)) + b_ref[...]


def _ada_modulation(cond, w_ada, b_ada):
    n_cols = N_MOD * D_MODEL
    return pl.pallas_call(
        _ada_kernel,
        out_shape=jax.ShapeDtypeStruct((DEPTH, COND_ROWS, n_cols), F32),
        grid=(DEPTH, n_cols // ADA_COLS),
        in_specs=[
            pl.BlockSpec((COND_ROWS, D_MODEL), lambda l, j: (0, 0)),
            pl.BlockSpec((None, D_MODEL, ADA_COLS), lambda l, j: (l, 0, j)),
            pl.BlockSpec((None, 1, ADA_COLS), lambda l, j: (l, 0, j)),
        ],
        out_specs=pl.BlockSpec((None, COND_ROWS, ADA_COLS), lambda l, j: (l, 0, j)),
        compiler_params=pltpu.CompilerParams(
            dimension_semantics=("arbitrary", "arbitrary"), vmem_limit_bytes=VMEM_LIMIT),
        name="ada_modulation",
    )(cond, w_ada, b_ada.reshape(DEPTH, 1, n_cols))


def _mixer_kernel(*refs, nb, seq, has_ctx, lam_init, n_passthrough):
    it = iter(refs)
    x_ref, mod_ref, g1_ref, w_in_ref = next(it), next(it), next(it), next(it)
    if has_ctx:
        rope_c_ref, rope_s_ref, ck_ref, cv_ref = next(it), next(it), next(it), next(it)
    lam_ref, subln_ref = next(it), next(it)
    cnt_ref, wpool_ref, pscale_ref = next(it), next(it), next(it)
    sgu_g_ref, wsgu_ref, bsgu_ref = next(it), next(it), next(it)
    w_out_ref = next(it)
    for _ in range(n_passthrough):
        next(it)
    out_ref = next(it)
    if not has_ctx:
        kc_out_ref, vc_out_ref = next(it), next(it)
    pool_scr, q_scr, k_scr, v_scr, u_scr, vs_scr, y_scr = (next(it) for _ in range(7))

    kv_base = PAST_LEN if has_ctx else 0
    kv_len = seq + kv_base
    shift1, scale1, gate1 = mod_ref[0:1, :], mod_ref[1:2, :], mod_ref[2:3, :]

    lane_head = lax.broadcasted_iota(jnp.int32, (ROW_CHUNK, HEAD_COLS), 1)
    first_half = lane_head < QK_DIM

    for b in range(nb):
        pool_scr[b, 0:POOL_HALO, :] = jnp.zeros((POOL_HALO, POOL_WIDTH), F32)
        pool_scr[b, POOL_HALO + seq:POOL_HALO + seq + POOL_HALO, :] = jnp.zeros((POOL_HALO, POOL_WIDTH), F32)
    if has_ctx:
        k_scr[0, :, 0:PAST_LEN] = ck_ref[...].T.astype(BF16)
        v_scr[0, 0:PAST_LEN, :] = cv_ref[...].astype(BF16)
        lane_q = lax.broadcasted_iota(jnp.int32, (ROW_CHUNK, QK_WIDTH), 1)
        rope_low = (lane_q % ROPE_AXIS_DIM) < ROPE_HALF

        def rope(t, pos0):
            partner = jnp.where(rope_low,
                                pltpu.roll(t, QK_WIDTH - ROPE_HALF, 1),
                                pltpu.roll(t, ROPE_HALF, 1))
            cos = rope_c_ref[pos0:pos0 + ROW_CHUNK, :]
            sin = rope_s_ref[pos0:pos0 + ROW_CHUNK, :]
            cos = jnp.concatenate([cos] * N_HEADS, axis=1)
            sin = jnp.concatenate([sin] * N_HEADS, axis=1)
            return t * cos + partner * sin

    for c in range(ROWS // ROW_CHUNK):
        r0 = c * ROW_CHUNK
        b, pos0 = r0 // seq, r0 % seq
        rows = slice(r0, r0 + ROW_CHUNK)
        h = _rms_mod(x_ref[rows, :], g1_ref[...], scale1, shift1).astype(BF16)

        pool_scr[b, POOL_HALO + pos0:POOL_HALO + pos0 + ROW_CHUNK, :] = _dot(h, w_in_ref[:, C_POOL:C_Q])

        q = _dot(h, w_in_ref[:, C_Q:C_K])
        k = _dot(h, w_in_ref[:, C_K:C_V])
        if has_ctx:
            q = rope(q, pos0)
            k = rope(k, pos0)
        else:
            kc_out_ref[b, pos0:pos0 + ROW_CHUNK, :] = k
        q = q * (QK_DIM ** -0.5 * math.log2(math.e))
        for hd in range(N_HEADS):
            cols = slice(hd * HEAD_COLS, (hd + 1) * HEAD_COLS)
            qh = q[:, cols]
            q_scr[c, 0:ROW_CHUNK, cols] = jnp.where(first_half, qh, 0.0).astype(BF16)
            q_scr[c, ROW_CHUNK:2 * ROW_CHUNK, cols] = jnp.where(first_half, 0.0, qh).astype(BF16)
        k_scr[b, :, kv_base + pos0:kv_base + pos0 + ROW_CHUNK] = k.T.astype(BF16)

        v = _dot(h, w_in_ref[:, C_V:C_UV])
        if not has_ctx:
            vc_out_ref[b, pos0:pos0 + ROW_CHUNK, :] = v
        v_scr[b, kv_base + pos0:kv_base + pos0 + ROW_CHUNK, :] = v.astype(BF16)

        uv = _gelu_tanh(_dot(h, w_in_ref[:, C_UV:IN_WIDTH]))
        u_scr[rows, :] = uv[:, :SGU_WIDTH]
        vv = uv[:, SGU_WIDTH:]
        mu = jnp.mean(vv, axis=-1, keepdims=True)
        vc = vv - mu
        vn = vc * lax.rsqrt(jnp.mean(vc * vc, axis=-1, keepdims=True) + EPS) * sgu_g_ref[...]
        vs_scr[rows, :] = vn.astype(BF16)

    lane_pool = lax.broadcasted_iota(jnp.int32, (ROW_CHUNK, 2 * POOL_GROUP), 1)
    narrow = lane_pool < POOL_GROUP
    for c in range(ROWS // ROW_CHUNK):
        r0 = c * ROW_CHUNK
        b, pos0 = r0 // seq, r0 % seq
        base = POOL_HALO + pos0
        pooled = []
        for j, (w_small, w_big) in enumerate(((POOL_WINDOWS[0], POOL_WINDOWS[1]),
                                              (POOL_WINDOWS[2], POOL_WINDOWS[3]))):
            cols = slice(j * 2 * POOL_GROUP, (j + 1) * 2 * POOL_GROUP)
            s_small = jnp.zeros((ROW_CHUNK, 2 * POOL_GROUP), F32)
            s_rest = jnp.zeros((ROW_CHUNK, 2 * POOL_GROUP), F32)
            for d in range(-(w_big // 2), w_big - w_big // 2):
                t = pool_scr[b, base + d:base + d + ROW_CHUNK, cols]
                if -(w_small // 2) <= d < w_small - w_small // 2:
                    s_small = s_small + t
                else:
                    s_rest = s_rest + t
            win_sum = jnp.where(narrow, s_small, s_small + s_rest)
            centre = pool_scr[b, base:base + ROW_CHUNK, cols]
            pooled.append(win_sum / cnt_ref[pos0:pos0 + ROW_CHUNK, cols] - centre)
        pooled = jnp.concatenate(pooled, axis=1).astype(BF16)
        y_a = _dot(pooled, wpool_ref[...]) * pscale_ref[...]
        y_scr[r0:r0 + ROW_CHUNK, Y_POOL:Y_ATTN] = y_a.astype(BF16)

    lq1, lk1, lq2, lk2 = lam_ref[0:1, :], lam_ref[1:2, :], lam_ref[2:3, :], lam_ref[3:4, :]
    lam = (jnp.exp(jnp.sum(lq1 * lk1, axis=-1, keepdims=True))
           - jnp.exp(jnp.sum(lq2 * lk2, axis=-1, keepdims=True)) + lam_init)
    n_qb = seq // ROW_CHUNK

    def attn_block(idx, carry):
        b = idx // n_qb
        r0 = pl.multiple_of(idx * ROW_CHUNK, ROW_CHUNK)
        def scores(hd):
            cols = slice(hd * HEAD_COLS, (hd + 1) * HEAD_COLS)
            s = _dot(q_scr[idx, :, cols], k_scr[b, cols, :])
            return s[:ROW_CHUNK], s[ROW_CHUNK:]

        nxt = scores(0)
        for hd in range(N_HEADS):
            cols = slice(hd * HEAD_COLS, (hd + 1) * HEAD_COLS)
            vh = v_scr[b, :, cols]
            s1, s2 = nxt
            if hd + 1 < N_HEADS:
                nxt = scores(hd + 1)
            e1 = jnp.exp2(s1 - jnp.max(s1, axis=-1, keepdims=True))
            e2 = jnp.exp2(s2 - jnp.max(s2, axis=-1, keepdims=True))
            l1 = jnp.sum(e1, axis=-1, keepdims=True)
            l2 = jnp.sum(e2, axis=-1, keepdims=True)
            p = (e1 - e2 * (lam * l1 / l2)).astype(BF16)
            o = _dot(p, vh) * (1.0 / l1)
            o = o * lax.rsqrt(jnp.mean(o * o, axis=-1, keepdims=True) + EPS) * subln_ref[...]
            o = o * (1.0 - lam_init)
            y_scr[pl.ds(r0, ROW_CHUNK), Y_ATTN + hd * V_DIM:Y_ATTN + (hd + 1) * V_DIM] = o.astype(BF16)
        return carry

    lax.fori_loop(0, ROWS // ROW_CHUNK, attn_block, 0)

    lane_sgu = lax.broadcasted_iota(jnp.int32, (CHUNK, 2 * POOL_GROUP), 1)
    sgu_first = lane_sgu < (SGU_WIDTH // 4)

    def sgu_block(idx, carry):
        r0 = pl.multiple_of(idx * CHUNK, CHUNK)
        for j in range(2):
            cols = slice(j * 128, (j + 1) * 128)
            t = _dot(wsgu_ref[j], vs_scr[pl.ds(r0, CHUNK), cols])
            mixed = jnp.where(sgu_first, t[:CHUNK, :], t[CHUNK:, :]) + bsgu_ref[:, cols]
            y_c = u_scr[pl.ds(r0, CHUNK), cols] * mixed
            y_scr[pl.ds(r0, CHUNK), Y_SGU + j * 128:Y_SGU + (j + 1) * 128] = y_c.astype(BF16)
        return carry

    lax.fori_loop(0, ROWS // CHUNK, sgu_block, 0)

    for c in range(ROWS // ROW_CHUNK):
        rows = slice(c * ROW_CHUNK, (c + 1) * ROW_CHUNK)
        y = _dot(y_scr[rows, :], w_out_ref[...])
        out_ref[rows, :] = x_ref[rows, :] + gate1 * y


def _const_spec(shape):
    zeros = (0,) * len(shape)
    return pl.BlockSpec(shape, lambda i: zeros, pipeline_mode=pl.Buffered(1))


def _layer_spec(shape, layer):
    index = (layer,) + (0,) * len(shape)
    return pl.BlockSpec((None,) + tuple(shape), lambda i: index, pipeline_mode=pl.Buffered(1))


def _mixer_call(x2d, mod4, layer, seq, has_ctx, lam_init, weights, ctx=None, new_cache=None):
    n_rows = x2d.shape[0]
    nb = ROWS // seq
    n_steps = n_rows // ROWS
    n_seq = n_rows // seq
    kv_len = seq + (PAST_LEN if has_ctx else 0)

    if has_ctx:
        mod_map = lambda i: (layer, 1 + i, 0, 0)
    else:
        mod_map = lambda i: (layer, 0, 0, 0)

    operands = [x2d, mod4, weights["norm1_g"], weights["w_in"]]
    in_specs = [
        pl.BlockSpec((ROWS, D_MODEL), lambda i: (i, 0)),
        pl.BlockSpec((None, None, N_MOD, D_MODEL), mod_map),
        _layer_spec((1, D_MODEL), layer),
        _layer_spec((D_MODEL, IN_WIDTH), layer),
    ]
    if has_ctx:
        rope_c, rope_s, cache_k, cache_v = ctx
        operands += [rope_c, rope_s, cache_k, cache_v]
        in_specs += [
            _const_spec((seq, HEAD_COLS)),
            _const_spec((seq, HEAD_COLS)),
            pl.BlockSpec((None, None, PAST_LEN, QK_WIDTH), lambda i: (i, layer, 0, 0)),
            pl.BlockSpec((None, None, PAST_LEN, ATTN_WIDTH), lambda i: (i, layer, 0, 0)),
        ]
    operands += [weights["lam"], weights["subln_g"], weights["pool_cnt"], weights["w_pool"],
                 weights["pool_scale"], weights["sgu_norm_g"], weights["w_sgu"], weights["b_sgu"],
                 weights["w_out"]]
    in_specs += [
        _layer_spec((8, 128), layer),
        _layer_spec((1, V_DIM), layer),
        _const_spec((seq, POOL_WIDTH)),
        _layer_spec((POOL_WIDTH, POOL_WIDTH), layer),
        _layer_spec((1, POOL_WIDTH), layer),
        _layer_spec((1, SGU_WIDTH), layer),
        _layer_spec((2, 2 * CHUNK, CHUNK), layer),
        _layer_spec((CHUNK, SGU_WIDTH), layer),
        _layer_spec((MIX_WIDTH, D_MODEL), layer),
    ]

    out_shape = [jax.ShapeDtypeStruct((n_rows, D_MODEL), F32)]
    out_specs = [pl.BlockSpec((ROWS, D_MODEL), lambda i: (i, 0))]
    aliases = {}
    if not has_ctx:
        out_shape += [jax.ShapeDtypeStruct((n_seq, DEPTH, seq, QK_WIDTH), F32),
                      jax.ShapeDtypeStruct((n_seq, DEPTH, seq, ATTN_WIDTH), F32)]
        out_specs += [pl.BlockSpec((nb, None, seq, QK_WIDTH), lambda i: (i, layer, 0, 0)),
                      pl.BlockSpec((nb, None, seq, ATTN_WIDTH), lambda i: (i, layer, 0, 0))]
        if new_cache is not None:
            aliases = {len(operands): 1, len(operands) + 1: 2}
            operands += list(new_cache)
            in_specs += [pl.BlockSpec(memory_space=pl.ANY)] * 2

    scratch = [
        pltpu.VMEM((nb, seq + 2 * POOL_HALO, POOL_WIDTH), F32),
        pltpu.VMEM((ROWS // ROW_CHUNK, 2 * ROW_CHUNK, QK_WIDTH), BF16),
        pltpu.VMEM((nb, QK_WIDTH, kv_len), BF16),
        pltpu.VMEM((nb, kv_len, ATTN_WIDTH), BF16),
        pltpu.VMEM((ROWS, SGU_WIDTH), F32),
        pltpu.VMEM((ROWS, SGU_WIDTH), BF16),
        pltpu.VMEM((ROWS, MIX_WIDTH), BF16),
    ]
    kernel = functools.partial(_mixer_kernel, nb=nb, seq=seq, has_ctx=has_ctx, lam_init=lam_init,
                               n_passthrough=len(aliases))
    return pl.pallas_call(
        kernel,
        out_shape=out_shape,
        grid=(n_steps,),
        in_specs=in_specs,
        out_specs=out_specs,
        scratch_shapes=scratch,
        input_output_aliases=aliases,
        compiler_params=pltpu.CompilerParams(
            dimension_semantics=("arbitrary",), vmem_limit_bytes=VMEM_LIMIT),
        name="mixer_ctx" if has_ctx else "mixer_prompt",
    )(*operands)


def _ffn_kernel(x_ref, mod_ref, g2_ref, w1_ref, w2_ref, gf_ref, out_ref, act_scr, *, final_norm):
    shift2, scale2, gate2 = mod_ref[3:4, :], mod_ref[4:5, :], mod_ref[5:6, :]
    x = x_ref[...]
    h = _rms_mod(x, g2_ref[...], scale2, shift2).astype(BF16)
    half = D_FF // 2
    for c in range(2):
        gate = _dot(h, w1_ref[:, c * half:(c + 1) * half])
        up = _dot(h, w1_ref[:, D_FF + c * half:D_FF + (c + 1) * half])
        act_scr[:, c * half:(c + 1) * half] = (gate * _sigmoid(gate) * up).astype(BF16)
    y = x + gate2 * _dot(act_scr[...], w2_ref[...])
    if final_norm:
        ms = jnp.mean(y * y, axis=-1, keepdims=True)
        y = y * lax.rsqrt(ms + EPS) * gf_ref[...]
    out_ref[...] = y


def _ffn_call(x2d, mod4, layer, seq, has_ctx, weights, final_norm):
    n_rows = x2d.shape[0]
    steps_per_seq = max(seq // FFN_ROWS, 1)
    if has_ctx:
        mod_map = lambda i: (layer, 1 + i // steps_per_seq, 0, 0)
    else:
        mod_map = lambda i: (layer, 0, 0, 0)
    kernel = functools.partial(_ffn_kernel, final_norm=final_norm)
    return pl.pallas_call(
        kernel,
        out_shape=jax.ShapeDtypeStruct((n_rows, D_MODEL), F32),
        grid=(n_rows // FFN_ROWS,),
        in_specs=[
            pl.BlockSpec((FFN_ROWS, D_MODEL), lambda i: (i, 0)),
            pl.BlockSpec((None, None, N_MOD, D_MODEL), mod_map),
            _layer_spec((1, D_MODEL), layer),
            _layer_spec((D_MODEL, 2 * D_FF), layer),
            _layer_spec((D_FF, D_MODEL), layer),
            _const_spec((1, D_MODEL)),
        ],
        out_specs=pl.BlockSpec((FFN_ROWS, D_MODEL), lambda i: (i, 0)),
        scratch_shapes=[pltpu.VMEM((FFN_ROWS, D_FF), BF16)],
        compiler_params=pltpu.CompilerParams(
            dimension_semantics=("arbitrary",), vmem_limit_bytes=VMEM_LIMIT),
        name="ffn",
    )(x2d, mod4, weights["norm2_g"], weights["w_ffn_in"], weights["w_ffn_out"], weights["final_g"])


def _rope_tables(seq):
    n_rows = seq // GRID_W
    rows = np.repeat(np.arange(n_rows), GRID_W).astype(np.float32)
    cols = np.tile(np.arange(GRID_W), n_rows).astype(np.float32)
    inv = 1.0 / (ROPE_BASE ** (np.arange(0, ROPE_AXIS_DIM, 2, dtype=np.float32) / ROPE_AXIS_DIM))
    ar, ac = rows[:, None] * inv[None], cols[:, None] * inv[None]
    cos_parts, sin_parts = [], []
    for ang in (ar, ac):
        cos_parts += [np.cos(ang), np.cos(ang)]
        sin_parts += [-np.sin(ang), np.sin(ang)]
    cos64 = np.concatenate(cos_parts, axis=1)
    sin64 = np.concatenate(sin_parts, axis=1)
    cos = np.concatenate([cos64, cos64], axis=1).astype(np.float32)
    sin = np.concatenate([sin64, sin64], axis=1).astype(np.float32)
    return jnp.asarray(cos), jnp.asarray(sin)


def _pool_counts(seq):
    t = np.arange(seq)
    cols = []
    for w in POOL_WINDOWS:
        lo = np.clip(t - w // 2, 0, seq)
        hi = np.clip(t + w - w // 2, 0, seq)
        cols.append(np.repeat((hi - lo).astype(np.float32)[:, None], POOL_GROUP, axis=1))
    return jnp.asarray(np.concatenate(cols, axis=1))


def _stacked_weights(norm1_g, w_in, w_pool, pool_scale, lam_q1, lam_k1, lam_q2, lam_k2, subln_g,
                     sgu_norm_g, w_sgu, b_sgu, w_out, norm2_g, w_ffn_in, w_ffn_out, final_g):
    n_groups = len(POOL_WINDOWS)
    eye = jnp.eye(n_groups, dtype=F32)
    w_pool_bd = (eye[None, :, None, :, None] * w_pool[:, :, :, None, :]).reshape(DEPTH, POOL_WIDTH, POOL_WIDTH)
    lam = jnp.zeros((DEPTH, 8, 128), F32)
    lam = lam.at[:, 0, :QK_DIM].set(lam_q1).at[:, 1, :QK_DIM].set(lam_k1)
    lam = lam.at[:, 2, :QK_DIM].set(lam_q2).at[:, 3, :QK_DIM].set(lam_k2)
    return {
        "norm1_g": norm1_g[:, None, :],
        "w_in": w_in.astype(BF16),
        "lam": lam,
        "subln_g": subln_g[:, None, :],
        "w_pool": w_pool_bd.astype(BF16),
        "pool_scale": pool_scale[:, None, :],
        "sgu_norm_g": sgu_norm_g[:, None, :],
        "w_sgu": w_sgu.reshape(DEPTH, 2, 2 * CHUNK, CHUNK).astype(BF16),
        "b_sgu": jnp.repeat(jnp.swapaxes(b_sgu, 1, 2), SGU_WIDTH // 4, axis=2),
        "w_out": w_out.astype(BF16),
        "norm2_g": norm2_g[:, None, :],
        "w_ffn_in": w_ffn_in.astype(BF16),
        "w_ffn_out": w_ffn_out.astype(BF16),
        "final_g": final_g[None, :],
    }


def kernel(x_prompt, x_sample, cache_k, cache_v, c, c_ctx, norm1_g, w_ada, b_ada, w_in, w_pool, pool_scale, lam_q1, lam_k1, lam_q2, lam_k2, subln_g, sgu_norm_g, w_sgu, b_sgu, w_out, norm2_g, w_ffn_in, w_ffn_out, final_g):
    batch, seq, _ = x_prompt.shape
    dec_batch, dec_seq, _ = x_sample.shape
    assert ROWS % seq == 0 and dec_seq == ROWS and 1 + dec_batch <= COND_ROWS

    cond = jnp.zeros((COND_ROWS, D_MODEL), F32).at[0].set(c_ctx).at[1:1 + dec_batch].set(c)
    mod4 = _ada_modulation(cond, w_ada, b_ada).reshape(DEPTH, COND_ROWS, N_MOD, D_MODEL)

    rope_c, rope_s = _rope_tables(dec_seq)
    ck = cache_k.reshape(dec_batch, DEPTH, PAST_LEN, QK_WIDTH)
    cv = cache_v.reshape(dec_batch, DEPTH, PAST_LEN, ATTN_WIDTH)

    weights = _stacked_weights(norm1_g, w_in, w_pool, pool_scale, lam_q1, lam_k1, lam_q2, lam_k2, subln_g,
                               sgu_norm_g, w_sgu, b_sgu, w_out, norm2_g, w_ffn_in, w_ffn_out, final_g)
    wp = dict(weights, pool_cnt=_pool_counts(seq))
    ws = dict(weights, pool_cnt=_pool_counts(dec_seq))

    xp = x_prompt.reshape(batch * seq, D_MODEL)
    xs = x_sample.reshape(dec_batch * dec_seq, D_MODEL)
    new_cache = None
    for l in range(DEPTH):
        lam_init = 0.8 - 0.6 * math.exp(-0.3 * l)
        last = l == DEPTH - 1

        xp, new_k, new_v = _mixer_call(xp, mod4, l, seq, False, lam_init, wp, new_cache=new_cache)
        new_cache = (new_k, new_v)
        xp = _ffn_call(xp, mod4, l, seq, False, wp, last)

        xs = _mixer_call(xs, mod4, l, dec_seq, True, lam_init, ws, ctx=(rope_c, rope_s, ck, cv))[0]
        xs = _ffn_call(xs, mod4, l, dec_seq, True, ws, last)

    y_prompt = xp.reshape(batch, seq, D_MODEL)
    y_sample = xs.reshape(dec_batch, dec_seq, D_MODEL)
    new_cache_k = new_cache[0].reshape(batch, DEPTH, seq, N_HEADS, 2 * QK_DIM)
    new_cache_v = new_cache[1].reshape(batch, DEPTH, seq, N_HEADS, V_DIM)
    return (y_prompt, y_sample, new_cache_k, new_cache_v)
```

```python
import functools
import math
import types

import numpy as np
import jax
import jax.numpy as jnp
from jax import lax
from jax.experimental import pallas as pl
from jax.experimental.pallas import tpu as pltpu

D_MODEL = 1024
DEPTH = 2
GRID_W = 64
POOL_WINDOWS = (2, 4, 8, 16)
POOL_WIDTH = 256
POOL_GROUP = 64
POOL_HALO = 8
ATTN_WIDTH = 512
N_HEADS = 4
V_DIM = 128
QK_DIM = 64
HEAD_COLS = 2 * QK_DIM
ROPE_BASE = 10000.0
ROPE_AXIS_DIM = 32
ROPE_HALF = ROPE_AXIS_DIM // 2
CHUNK = 128
SGU_WIDTH = 256
QK_WIDTH = 512
IN_WIDTH = 2304
MIX_WIDTH = 1024
D_FF = 2816
N_MOD = 6
EPS = 1e-6
PAST_LEN = 256

C_POOL, C_Q, C_K, C_V, C_UV = 0, 256, 768, 1280, 1792
Y_POOL, Y_ATTN, Y_SGU = 0, 256, 768

ROWS = 1024
ROW_CHUNK = 256
ATTN_LOOKAHEAD = 2
FFN_ROWS = 512
COND_ROWS = 16
ADA_COLS = 1024
VMEM_LIMIT = 58 * 1024 * 1024

BF16 = jnp.bfloat16
F32 = jnp.float32


def _dot(a, b):
    return jnp.dot(a, b, preferred_element_type=F32)


def _sigmoid(x):
    return 1.0 / (1.0 + jnp.exp(-x))


def _rms_mod(x, g, scale, shift):
    ms = jnp.mean(x * x, axis=-1, keepdims=True)
    y = x * lax.rsqrt(ms + EPS) * g
    return y * (1.0 + scale) + shift


def _gelu_tanh(x):
    c = math.sqrt(2.0 / math.pi)
    return x * (0.5 * (1.0 + jnp.tanh(c * (x + 0.044715 * (x * x * x)))))


def _const_spec(shape):
    zeros = (0,) * len(shape)
    return pl.BlockSpec(shape, lambda *_: zeros, pipeline_mode=pl.Buffered(1))


def _layer_spec(shape, layer):
    index = (layer,) + (0,) * len(shape)
    return pl.BlockSpec((None,) + tuple(shape), lambda *_: index, pipeline_mode=pl.Buffered(1))


def _ada_kernel(cond_ref, w_ref, b_ref, out_ref):
    cond = cond_ref[...]
    s = (cond * _sigmoid(cond)).astype(BF16)
    out_ref[...] = _dot(s, w_ref[...].astype(BF16)) + b_ref[...]


def _ada_modulation(cond, w_ada, b_ada):
    n_cols = N_MOD * D_MODEL
    return pl.pallas_call(
        _ada_kernel,
        out_shape=jax.ShapeDtypeStruct((DEPTH, COND_ROWS, n_cols), F32),
        grid=(DEPTH, n_cols // ADA_COLS),
        in_specs=[
            pl.BlockSpec((COND_ROWS, D_MODEL), lambda l, j: (0, 0)),
            pl.BlockSpec((None, D_MODEL, ADA_COLS), lambda l, j: (l, 0, j)),
            pl.BlockSpec((None, 1, ADA_COLS), lambda l, j: (l, 0, j)),
        ],
        out_specs=pl.BlockSpec((None, COND_ROWS, ADA_COLS), lambda l, j: (l, 0, j)),
        compiler_params=pltpu.CompilerParams(
            dimension_semantics=("arbitrary", "arbitrary"), vmem_limit_bytes=VMEM_LIMIT),
        name="ada_modulation",
    )(cond, w_ada, b_ada.reshape(DEPTH, 1, n_cols))


def _mixer_body(r, x_ref, cnt_ref, pool_scr, k_scr, v_scr, *, nb, seq, has_ctx, lam_init):
    kv_base = PAST_LEN if has_ctx else 0
    shift1, scale1, gate1 = r.mod[0:1, :], r.mod[1:2, :], r.mod[2:3, :]

    lane_head = lax.broadcasted_iota(jnp.int32, (ROW_CHUNK, HEAD_COLS), 1)
    first_half = lane_head < QK_DIM

    for b in range(nb):
        pool_scr[b, 0:POOL_HALO, :] = jnp.zeros((POOL_HALO, POOL_WIDTH), F32)
        pool_scr[b, POOL_HALO + seq:POOL_HALO + seq + POOL_HALO, :] = jnp.zeros((POOL_HALO, POOL_WIDTH), F32)
    if has_ctx:
        for hd in range(N_HEADS):
            cols = slice(hd * HEAD_COLS, (hd + 1) * HEAD_COLS)
            k_scr[0, cols, 0:PAST_LEN] = r.ck[:, hd, :].T.astype(BF16)
            v_scr[0, 0:PAST_LEN, cols] = r.cv[:, hd, :].astype(BF16)
        lane_q = lax.broadcasted_iota(jnp.int32, (ROW_CHUNK, QK_WIDTH), 1)
        rope_low = (lane_q % ROPE_AXIS_DIM) < ROPE_HALF

        def rope(t, pos0):
            partner = jnp.where(rope_low,
                                pltpu.roll(t, QK_WIDTH - ROPE_HALF, 1),
                                pltpu.roll(t, ROPE_HALF, 1))
            cos = r.rope_c[pos0:pos0 + ROW_CHUNK, :]
            sin = r.rope_s[pos0:pos0 + ROW_CHUNK, :]
            cos = jnp.concatenate([cos] * N_HEADS, axis=1)
            sin = jnp.concatenate([sin] * N_HEADS, axis=1)
            return t * cos + partner * sin

    for c in range(ROWS // ROW_CHUNK):
        r0 = c * ROW_CHUNK
        b, pos0 = r0 // seq, r0 % seq
        rows = slice(r0, r0 + ROW_CHUNK)
        h = _rms_mod(x_ref[rows, :], r.g1[...], scale1, shift1).astype(BF16)

        pool_scr[b, POOL_HALO + pos0:POOL_HALO + pos0 + ROW_CHUNK, :] = _dot(h, r.w_in[:, C_POOL:C_Q])

        q = _dot(h, r.w_in[:, C_Q:C_K])
        k = _dot(h, r.w_in[:, C_K:C_V])
        if has_ctx:
            q = rope(q, pos0)
            k = rope(k, pos0)
        else:
            r.kc_out[b, pos0:pos0 + ROW_CHUNK, :] = k
        q = q * (QK_DIM ** -0.5 * math.log2(math.e))
        for hd in range(N_HEADS):
            cols = slice(hd * HEAD_COLS, (hd + 1) * HEAD_COLS)
            qh = q[:, cols]
            r.q_scr[c, 0:ROW_CHUNK, cols] = jnp.where(first_half, qh, 0.0).astype(BF16)
            r.q_scr[c, ROW_CHUNK:2 * ROW_CHUNK, cols] = jnp.where(first_half, 0.0, qh).astype(BF16)
        k_scr[b, :, kv_base + pos0:kv_base + pos0 + ROW_CHUNK] = k.T.astype(BF16)

        v = _dot(h, r.w_in[:, C_V:C_UV])
        if not has_ctx:
            r.vc_out[b, pos0:pos0 + ROW_CHUNK, :] = v
        v_scr[b, kv_base + pos0:kv_base + pos0 + ROW_CHUNK, :] = v.astype(BF16)

        uv = _gelu_tanh(_dot(h, r.w_in[:, C_UV:IN_WIDTH]))
        r.u_scr[rows, :] = uv[:, :SGU_WIDTH]
        vv = uv[:, SGU_WIDTH:]
        mu = jnp.mean(vv, axis=-1, keepdims=True)
        vc = vv - mu
        vn = vc * lax.rsqrt(jnp.mean(vc * vc, axis=-1, keepdims=True) + EPS) * r.sgu_g[...]
        r.vs_scr[rows, :] = vn.astype(BF16)

    lane_pool = lax.broadcasted_iota(jnp.int32, (ROW_CHUNK, 2 * POOL_GROUP), 1)
    narrow = lane_pool < POOL_GROUP
    for c in range(ROWS // ROW_CHUNK):
        r0 = c * ROW_CHUNK
        b, pos0 = r0 // seq, r0 % seq
        base = POOL_HALO + pos0
        pooled = []
        for j, (w_small, w_big) in enumerate(((POOL_WINDOWS[0], POOL_WINDOWS[1]),
                                              (POOL_WINDOWS[2], POOL_WINDOWS[3]))):
            cols = slice(j * 2 * POOL_GROUP, (j + 1) * 2 * POOL_GROUP)
            s_small = jnp.zeros((ROW_CHUNK, 2 * POOL_GROUP), F32)
            s_rest = jnp.zeros((ROW_CHUNK, 2 * POOL_GROUP), F32)
            for d in range(-(w_big // 2), w_big - w_big // 2):
                t = pool_scr[b, base + d:base + d + ROW_CHUNK, cols]
                if -(w_small // 2) <= d < w_small - w_small // 2:
                    s_small = s_small + t
                else:
                    s_rest = s_rest + t
            win_sum = jnp.where(narrow, s_small, s_small + s_rest)
            centre = pool_scr[b, base:base + ROW_CHUNK, cols]
            pooled.append(win_sum / cnt_ref[pos0:pos0 + ROW_CHUNK, cols] - centre)
        pooled = jnp.concatenate(pooled, axis=1).astype(BF16)
        y_a = _dot(pooled, r.wpool[...]) * r.pscale[...]
        r.y_scr[r0:r0 + ROW_CHUNK, Y_POOL:Y_ATTN] = y_a.astype(BF16)

    lq1, lk1, lq2, lk2 = r.lam[0:1, :], r.lam[1:2, :], r.lam[2:3, :], r.lam[3:4, :]
    lam = (jnp.exp(jnp.sum(lq1 * lk1, axis=-1, keepdims=True))
           - jnp.exp(jnp.sum(lq2 * lk2, axis=-1, keepdims=True)) + lam_init)
    n_qb = seq // ROW_CHUNK

    def attn_block(idx, carry):
        b = idx // n_qb
        r0 = pl.multiple_of(idx * ROW_CHUNK, ROW_CHUNK)

        def scores(hd):
            cols = slice(hd * HEAD_COLS, (hd + 1) * HEAD_COLS)
            s = _dot(r.q_scr[idx, :, cols], k_scr[b, cols, :])
            return s[:ROW_CHUNK], s[ROW_CHUNK:]

        pending = [scores(hd) for hd in range(ATTN_LOOKAHEAD)]
        for hd in range(N_HEADS):
            cols = slice(hd * HEAD_COLS, (hd + 1) * HEAD_COLS)
            s1, s2 = pending.pop(0)
            if hd + ATTN_LOOKAHEAD < N_HEADS:
                pending.append(scores(hd + ATTN_LOOKAHEAD))
            e1 = jnp.exp2(s1 - jnp.max(s1, axis=-1, keepdims=True))
            e2 = jnp.exp2(s2 - jnp.max(s2, axis=-1, keepdims=True))
            l1 = jnp.sum(e1, axis=-1, keepdims=True)
            l2 = jnp.sum(e2, axis=-1, keepdims=True)
            p = (e1 - e2 * (lam * l1 / l2)).astype(BF16)
            o = _dot(p, v_scr[b, :, cols]) * (1.0 / l1)
            o = o * lax.rsqrt(jnp.mean(o * o, axis=-1, keepdims=True) + EPS) * r.subln[...]
            o = o * (1.0 - lam_init)
            r.y_scr[pl.ds(r0, ROW_CHUNK), Y_ATTN + hd * V_DIM:Y_ATTN + (hd + 1) * V_DIM] = o.astype(BF16)
        return carry

    lax.fori_loop(0, ROWS // ROW_CHUNK, attn_block, 0)

    lane_sgu = lax.broadcasted_iota(jnp.int32, (CHUNK, 2 * POOL_GROUP), 1)
    sgu_first = lane_sgu < (SGU_WIDTH // 4)

    def sgu_block(idx, carry):
        r0 = pl.multiple_of(idx * CHUNK, CHUNK)
        for j in range(2):
            cols = slice(j * 128, (j + 1) * 128)
            t = _dot(r.wsgu[j], r.vs_scr[pl.ds(r0, CHUNK), cols])
            mixed = jnp.where(sgu_first, t[:CHUNK, :], t[CHUNK:, :]) + r.bsgu[:, cols]
            y_c = r.u_scr[pl.ds(r0, CHUNK), cols] * mixed
            r.y_scr[pl.ds(r0, CHUNK), Y_SGU + j * 128:Y_SGU + (j + 1) * 128] = y_c.astype(BF16)
        return carry

    lax.fori_loop(0, ROWS // CHUNK, sgu_block, 0)

    for c in range(ROWS // ROW_CHUNK):
        rows = slice(c * ROW_CHUNK, (c + 1) * ROW_CHUNK)
        y = _dot(r.y_scr[rows, :], r.w_out[...])
        r.out[rows, :] = x_ref[rows, :] + gate1 * y


def _mixer_kernel(*refs, nb, seq, has_ctx, lam_init):
    it = iter(refs)
    r = types.SimpleNamespace()
    x_ref, r.mod, r.g1, r.w_in = next(it), next(it), next(it), next(it)
    if has_ctx:
        r.rope_c, r.rope_s, r.ck, r.cv = next(it), next(it), next(it), next(it)
    (r.lam, r.subln, cnt_ref, r.wpool, r.pscale, r.sgu_g, r.wsgu, r.bsgu, r.w_out) = (next(it) for _ in range(9))
    if not has_ctx:
        next(it), next(it)
    r.out = next(it)
    if not has_ctx:
        r.kc_out, r.vc_out = next(it), next(it)
    pool_scr, r.q_scr, k_scr, v_scr, r.u_scr, r.vs_scr, r.y_scr = (next(it) for _ in range(7))
    _mixer_body(r, x_ref, cnt_ref, pool_scr, k_scr, v_scr, nb=nb, seq=seq, has_ctx=has_ctx, lam_init=lam_init)


def _mixer_call(x2d, mod4, layer, seq, has_ctx, lam_init, params, big, ctx=None, new_cache=None):
    n_rows = x2d.shape[0]
    nb = ROWS // seq
    n_seq = n_rows // seq
    kv_len = seq + (PAST_LEN if has_ctx else 0)

    if has_ctx:
        mod_map = lambda i: (layer, 1 + i, 0, 0)
    else:
        mod_map = lambda i: (layer, 0, 0, 0)

    operands = [x2d, mod4, params["norm1_g"], big["w_in"]]
    in_specs = [
        pl.BlockSpec((ROWS, D_MODEL), lambda i: (i, 0)),
        pl.BlockSpec((None, None, N_MOD, D_MODEL), mod_map),
        _layer_spec((1, D_MODEL), layer),
        _const_spec((D_MODEL, IN_WIDTH)),
    ]
    if has_ctx:
        operands += list(ctx)
        in_specs += [
            _const_spec((seq, HEAD_COLS)),
            _const_spec((seq, HEAD_COLS)),
            pl.BlockSpec((None, None, PAST_LEN, N_HEADS, HEAD_COLS), lambda i: (i, layer, 0, 0, 0)),
            pl.BlockSpec((None, None, PAST_LEN, N_HEADS, V_DIM), lambda i: (i, layer, 0, 0, 0)),
        ]
    operands += [params["lam"], params["subln_g"], _pool_counts(seq), params["w_pool"], params["pool_scale"],
                 params["sgu_norm_g"], params["w_sgu"], params["b_sgu"], big["w_out"]]
    in_specs += [
        _layer_spec((8, 128), layer),
        _layer_spec((1, V_DIM), layer),
        _const_spec((seq, POOL_WIDTH)),
        _layer_spec((POOL_WIDTH, POOL_WIDTH), layer),
        _layer_spec((1, POOL_WIDTH), layer),
        _layer_spec((1, SGU_WIDTH), layer),
        _layer_spec((2, 2 * CHUNK, CHUNK), layer),
        _layer_spec((CHUNK, SGU_WIDTH), layer),
        _const_spec((MIX_WIDTH, D_MODEL)),
    ]

    out_shape = [jax.ShapeDtypeStruct((n_rows, D_MODEL), F32)]
    out_specs = [pl.BlockSpec((ROWS, D_MODEL), lambda i: (i, 0))]
    aliases = {}
    if not has_ctx:
        out_shape += [jax.ShapeDtypeStruct((n_seq, DEPTH, seq, QK_WIDTH), F32),
                      jax.ShapeDtypeStruct((n_seq, DEPTH, seq, ATTN_WIDTH), F32)]
        out_specs += [pl.BlockSpec((nb, None, seq, QK_WIDTH), lambda i: (i, layer, 0, 0)),
                      pl.BlockSpec((nb, None, seq, ATTN_WIDTH), lambda i: (i, layer, 0, 0))]
        aliases = {len(operands): 1, len(operands) + 1: 2}
        operands += list(new_cache)
        in_specs += [pl.BlockSpec(memory_space=pl.ANY)] * 2

    scratch = [
        pltpu.VMEM((nb, seq + 2 * POOL_HALO, POOL_WIDTH), F32),
        pltpu.VMEM((ROWS // ROW_CHUNK, 2 * ROW_CHUNK, QK_WIDTH), BF16),
        pltpu.VMEM((nb, QK_WIDTH, kv_len), BF16),
        pltpu.VMEM((nb, kv_len, ATTN_WIDTH), BF16),
        pltpu.VMEM((ROWS, SGU_WIDTH), F32),
        pltpu.VMEM((ROWS, SGU_WIDTH), BF16),
        pltpu.VMEM((ROWS, MIX_WIDTH), BF16),
    ]
    kernel = functools.partial(_mixer_kernel, nb=nb, seq=seq, has_ctx=has_ctx, lam_init=lam_init)
    return pl.pallas_call(
        kernel,
        out_shape=out_shape,
        grid=(n_rows // ROWS,),
        in_specs=in_specs,
        out_specs=out_specs,
        scratch_shapes=scratch,
        input_output_aliases=aliases,
        compiler_params=pltpu.CompilerParams(
            dimension_semantics=("arbitrary",), vmem_limit_bytes=VMEM_LIMIT),
        name="mixer_ctx" if has_ctx else "mixer_prompt",
    )(*operands)


def _ffn_kernel(*refs, final_norm, n_cast):
    x_ref, mod_ref, g2_ref, w1_ref, w2_ref, gf_ref = refs[:6]
    cast_in = refs[6:6 + n_cast]
    out_ref = refs[6 + n_cast]
    cast_out = refs[7 + n_cast:7 + 2 * n_cast]
    act_scr = refs[7 + 2 * n_cast]

    for src, dst in zip(cast_in, cast_out):
        dst[...] = src[...].astype(BF16)

    shift2, scale2, gate2 = mod_ref[3:4, :], mod_ref[4:5, :], mod_ref[5:6, :]
    x = x_ref[...]
    h = _rms_mod(x, g2_ref[...], scale2, shift2).astype(BF16)
    half = D_FF // 2
    for c in range(2):
        gate = _dot(h, w1_ref[:, c * half:(c + 1) * half])
        up = _dot(h, w1_ref[:, D_FF + c * half:D_FF + (c + 1) * half])
        act_scr[:, c * half:(c + 1) * half] = (gate * _sigmoid(gate) * up).astype(BF16)
    y = x + gate2 * _dot(act_scr[...], w2_ref[...])
    if final_norm:
        ms = jnp.mean(y * y, axis=-1, keepdims=True)
        y = y * lax.rsqrt(ms + EPS) * gf_ref[...]
    out_ref[...] = y


def _ffn_call(x2d, mod4, layer, seq, has_ctx, params, big, final_norm, next_f32=()):
    n_steps = x2d.shape[0] // FFN_ROWS
    steps_per_seq = max(seq // FFN_ROWS, 1)
    if has_ctx:
        mod_map = lambda i: (layer, 1 + i // steps_per_seq, 0, 0)
    else:
        mod_map = lambda i: (layer, 0, 0, 0)
    in_specs = [
        pl.BlockSpec((FFN_ROWS, D_MODEL), lambda i: (i, 0)),
        pl.BlockSpec((None, None, N_MOD, D_MODEL), mod_map),
        _layer_spec((1, D_MODEL), layer),
        _const_spec((D_MODEL, 2 * D_FF)),
        _const_spec((D_FF, D_MODEL)),
        _const_spec((1, D_MODEL)),
    ]
    out_shape = [jax.ShapeDtypeStruct(x2d.shape, F32)]
    out_specs = [pl.BlockSpec((FFN_ROWS, D_MODEL), lambda i: (i, 0))]
    for w in next_f32:
        _, rows, cols = w.shape
        slab = rows // n_steps
        assert slab * n_steps == rows and slab % 16 == 0
        in_specs.append(pl.BlockSpec((None, slab, cols), lambda i: (layer + 1, i, 0)))
        out_shape.append(jax.ShapeDtypeStruct((rows, cols), BF16))
        out_specs.append(pl.BlockSpec((slab, cols), lambda i: (i, 0)))

    kernel = functools.partial(_ffn_kernel, final_norm=final_norm, n_cast=len(next_f32))
    return pl.pallas_call(
        kernel,
        out_shape=out_shape,
        grid=(n_steps,),
        in_specs=in_specs,
        out_specs=out_specs,
        scratch_shapes=[pltpu.VMEM((FFN_ROWS, D_FF), BF16)],
        compiler_params=pltpu.CompilerParams(
            dimension_semantics=("arbitrary",), vmem_limit_bytes=VMEM_LIMIT),
        name="ffn",
    )(x2d, mod4, params["norm2_g"], big["w_ffn_in"], big["w_ffn_out"], params["final_g"], *next_f32)


def _rope_tables(seq):
    n_rows = seq // GRID_W
    rows = np.repeat(np.arange(n_rows), GRID_W).astype(np.float32)
    cols = np.tile(np.arange(GRID_W), n_rows).astype(np.float32)
    inv = 1.0 / (ROPE_BASE ** (np.arange(0, ROPE_AXIS_DIM, 2, dtype=np.float32) / ROPE_AXIS_DIM))
    ar, ac = rows[:, None] * inv[None], cols[:, None] * inv[None]
    cos_parts, sin_parts = [], []
    for ang in (ar, ac):
        cos_parts += [np.cos(ang), np.cos(ang)]
        sin_parts += [-np.sin(ang), np.sin(ang)]
    cos64 = np.concatenate(cos_parts, axis=1)
    sin64 = np.concatenate(sin_parts, axis=1)
    cos = np.concatenate([cos64, cos64], axis=1).astype(np.float32)
    sin = np.concatenate([sin64, sin64], axis=1).astype(np.float32)
    return jnp.asarray(cos), jnp.asarray(sin)


def _pool_counts(seq):
    t = np.arange(seq)
    cols = []
    for w in POOL_WINDOWS:
        lo = np.clip(t - w // 2, 0, seq)
        hi = np.clip(t + w - w // 2, 0, seq)
        cols.append(np.repeat((hi - lo).astype(np.float32)[:, None], POOL_GROUP, axis=1))
    return jnp.asarray(np.concatenate(cols, axis=1))


def _small_params(norm1_g, w_pool, pool_scale, lam_q1, lam_k1, lam_q2, lam_k2, subln_g,
                  sgu_norm_g, w_sgu, b_sgu, norm2_g, final_g):
    n_groups = len(POOL_WINDOWS)
    eye = jnp.eye(n_groups, dtype=F32)
    w_pool_bd = (eye[None, :, None, :, None] * w_pool[:, :, :, None, :]).reshape(DEPTH, POOL_WIDTH, POOL_WIDTH)
    lam = jnp.zeros((DEPTH, 8, 128), F32)
    lam = lam.at[:, 0, :QK_DIM].set(lam_q1).at[:, 1, :QK_DIM].set(lam_k1)
    lam = lam.at[:, 2, :QK_DIM].set(lam_q2).at[:, 3, :QK_DIM].set(lam_k2)
    return {
        "norm1_g": norm1_g[:, None, :],
        "lam": lam,
        "subln_g": subln_g[:, None, :],
        "w_pool": w_pool_bd.astype(BF16),
        "pool_scale": pool_scale[:, None, :],
        "sgu_norm_g": sgu_norm_g[:, None, :],
        "w_sgu": w_sgu.reshape(DEPTH, 2, 2 * CHUNK, CHUNK).astype(BF16),
        "b_sgu": jnp.repeat(jnp.swapaxes(b_sgu, 1, 2), SGU_WIDTH // 4, axis=2),
        "norm2_g": norm2_g[:, None, :],
        "final_g": final_g[None, :],
    }


BIG_WEIGHTS = ("w_in", "w_out", "w_ffn_in", "w_ffn_out")


def kernel(x_prompt, x_sample, cache_k, cache_v, c, c_ctx, norm1_g, w_ada, b_ada, w_in, w_pool, pool_scale, lam_q1, lam_k1, lam_q2, lam_k2, subln_g, sgu_norm_g, w_sgu, b_sgu, w_out, norm2_g, w_ffn_in, w_ffn_out, final_g):
    batch, seq, _ = x_prompt.shape
    dec_batch, dec_seq, _ = x_sample.shape
    assert ROWS % seq == 0 and dec_seq == ROWS and 1 + dec_batch <= COND_ROWS

    cond = jnp.zeros((COND_ROWS, D_MODEL), F32).at[0].set(c_ctx).at[1:1 + dec_batch].set(c)
    mod4 = _ada_modulation(cond, w_ada, b_ada).reshape(DEPTH, COND_ROWS, N_MOD, D_MODEL)

    ctx = _rope_tables(dec_seq) + (cache_k, cache_v)
    params = _small_params(norm1_g, w_pool, pool_scale, lam_q1, lam_k1, lam_q2, lam_k2, subln_g,
                           sgu_norm_g, w_sgu, b_sgu, norm2_g, final_g)

    big_f32 = (w_in, w_out, w_ffn_in, w_ffn_out)
    big = dict(zip(BIG_WEIGHTS, (w[0].astype(BF16) for w in big_f32)))

    xp = x_prompt.reshape(batch * seq, D_MODEL)
    xs = x_sample.reshape(dec_batch * dec_seq, D_MODEL)
    new_cache = (jnp.zeros((batch, DEPTH, seq, QK_WIDTH), F32), jnp.zeros((batch, DEPTH, seq, ATTN_WIDTH), F32))
    for l in range(DEPTH):
        lam_init = 0.8 - 0.6 * math.exp(-0.3 * l)
        last = l == DEPTH - 1

        xp, *new_cache = _mixer_call(xp, mod4, l, seq, False, lam_init, params, big, new_cache=new_cache)
        xp, = _ffn_call(xp, mod4, l, seq, False, params, big, last)

        xs, = _mixer_call(xs, mod4, l, dec_seq, True, lam_init, params, big, ctx=ctx)
        xs, *next_big = _ffn_call(xs, mod4, l, dec_seq, True, params, big, last,
                                  next_f32=() if last else big_f32)
        big = dict(zip(BIG_WEIGHTS, next_big))

    y_prompt = xp.reshape(batch, seq, D_MODEL)
    y_sample = xs.reshape(dec_batch, dec_seq, D_MODEL)
    new_cache_k = new_cache[0].reshape(batch, DEPTH, seq, N_HEADS, 2 * QK_DIM)
    new_cache_v = new_cache[1].reshape(batch, DEPTH, seq, N_HEADS, V_DIM)
    return (y_prompt, y_sample, new_cache_k, new_cache_v)
```

```python
import functools
import math
import types

import numpy as np
import jax
import jax.numpy as jnp
from jax import lax
from jax.experimental import pallas as pl
from jax.experimental.pallas import tpu as pltpu

D_MODEL = 1024
DEPTH = 2
GRID_W = 64
POOL_WINDOWS = (2, 4, 8, 16)
POOL_WIDTH = 256
POOL_GROUP = 64
POOL_HALO = 8
ATTN_WIDTH = 512
N_HEADS = 4
V_DIM = 128
QK_DIM = 64
HEAD_COLS = 2 * QK_DIM
ROPE_BASE = 10000.0
ROPE_AXIS_DIM = 32
ROPE_HALF = ROPE_AXIS_DIM // 2
CHUNK = 128
SGU_WIDTH = 256
QK_WIDTH = 512
IN_WIDTH = 2304
MIX_WIDTH = 1024
D_FF = 2816
N_MOD = 6
EPS = 1e-6
PAST_LEN = 256

C_POOL, C_Q, C_K, C_V, C_UV = 0, 256, 768, 1280, 1792
Y_POOL, Y_ATTN, Y_SGU = 0, 256, 768

ROWS = 1024
ROW_CHUNK = 256
ATTN_LOOKAHEAD = 2
FFN_ROWS = 512
COND_ROWS = 16
ADA_COLS = 1024
VMEM_LIMIT = 58 * 1024 * 1024

BF16 = jnp.bfloat16
F32 = jnp.float32


def _dot(a, b):
    return jnp.dot(a, b, preferred_element_type=F32)


def _sigmoid(x):
    return 1.0 / (1.0 + jnp.exp(-x))


def _rms_mod(x, g, scale, shift):
    ms = jnp.mean(x * x, axis=-1, keepdims=True)
    y = x * lax.rsqrt(ms + EPS) * g
    return y * (1.0 + scale) + shift


def _gelu_tanh(x):
    c = math.sqrt(2.0 / math.pi)
    return x * (0.5 * (1.0 + jnp.tanh(c * (x + 0.044715 * (x * x * x)))))


def _const_spec(shape):
    zeros = (0,) * len(shape)
    return pl.BlockSpec(shape, lambda *_: zeros, pipeline_mode=pl.Buffered(1))


def _layer_spec(shape, layer):
    index = (layer,) + (0,) * len(shape)
    return pl.BlockSpec((None,) + tuple(shape), lambda *_: index, pipeline_mode=pl.Buffered(1))


def _ada_kernel(cond_ref, w_ref, b_ref, out_ref):
    cond = cond_ref[...]
    s = (cond * _sigmoid(cond)).astype(BF16)
    out_ref[...] = _dot(s, w_ref[...].astype(BF16)) + b_ref[...]


def _ada_modulation(cond, w_ada, b_ada):
    n_cols = N_MOD * D_MODEL
    return pl.pallas_call(
        _ada_kernel,
        out_shape=jax.ShapeDtypeStruct((DEPTH, COND_ROWS, n_cols), F32),
        grid=(DEPTH, n_cols // ADA_COLS),
        in_specs=[
            pl.BlockSpec((COND_ROWS, D_MODEL), lambda l, j: (0, 0)),
            pl.BlockSpec((None, D_MODEL, ADA_COLS), lambda l, j: (l, 0, j)),
            pl.BlockSpec((None, 1, ADA_COLS), lambda l, j: (l, 0, j)),
        ],
        out_specs=pl.BlockSpec((None, COND_ROWS, ADA_COLS), lambda l, j: (l, 0, j)),
        compiler_params=pltpu.CompilerParams(
            dimension_semantics=("arbitrary", "arbitrary"), vmem_limit_bytes=VMEM_LIMIT),
        name="ada_modulation",
    )(cond, w_ada, b_ada.reshape(DEPTH, 1, n_cols))


def _mixer_body(r, x_ref, cnt_ref, pool_scr, k_scr, v_scr, *, nb, seq, has_ctx, lam_init):
    kv_base = PAST_LEN if has_ctx else 0
    shift1, scale1, gate1 = r.mod[0:1, :], r.mod[1:2, :], r.mod[2:3, :]

    lane_head = lax.broadcasted_iota(jnp.int32, (ROW_CHUNK, HEAD_COLS), 1)
    first_half = lane_head < QK_DIM

    for b in range(nb):
        pool_scr[b, 0:POOL_HALO, :] = jnp.zeros((POOL_HALO, POOL_WIDTH), F32)
        pool_scr[b, POOL_HALO + seq:POOL_HALO + seq + POOL_HALO, :] = jnp.zeros((POOL_HALO, POOL_WIDTH), F32)
    if not has_ctx:
        for slot in range(r.kc_out.shape[1]):
            if slot != r.cache_slot:
                r.kc_out[:, slot] = jnp.zeros((nb,) + r.kc_out.shape[2:], F32)
                r.vc_out[:, slot] = jnp.zeros((nb,) + r.vc_out.shape[2:], F32)
    if has_ctx:
        for hd in range(N_HEADS):
            cols = slice(hd * HEAD_COLS, (hd + 1) * HEAD_COLS)
            k_scr[0, cols, 0:PAST_LEN] = r.ck[pl.ds(hd, PAST_LEN, stride=N_HEADS), :].T.astype(BF16)
            v_scr[0, 0:PAST_LEN, cols] = r.cv[pl.ds(hd, PAST_LEN, stride=N_HEADS), :].astype(BF16)
        lane_q = lax.broadcasted_iota(jnp.int32, (ROW_CHUNK, QK_WIDTH), 1)
        rope_low = (lane_q % ROPE_AXIS_DIM) < ROPE_HALF

        def rope(t, pos0):
            partner = jnp.where(rope_low,
                                pltpu.roll(t, QK_WIDTH - ROPE_HALF, 1),
                                pltpu.roll(t, ROPE_HALF, 1))
            cos = r.rope_c[pos0:pos0 + ROW_CHUNK, :]
            sin = r.rope_s[pos0:pos0 + ROW_CHUNK, :]
            cos = jnp.concatenate([cos] * N_HEADS, axis=1)
            sin = jnp.concatenate([sin] * N_HEADS, axis=1)
            return t * cos + partner * sin

    for c in range(ROWS // ROW_CHUNK):
        r0 = c * ROW_CHUNK
        b, pos0 = r0 // seq, r0 % seq
        rows = slice(r0, r0 + ROW_CHUNK)
        h = _rms_mod(x_ref[rows, :], r.g1[...], scale1, shift1).astype(BF16)

        pool_scr[b, POOL_HALO + pos0:POOL_HALO + pos0 + ROW_CHUNK, :] = _dot(h, r.w_in[:, C_POOL:C_Q])

        q = _dot(h, r.w_in[:, C_Q:C_K])
        k = _dot(h, r.w_in[:, C_K:C_V])
        if has_ctx:
            q = rope(q, pos0)
            k = rope(k, pos0)
        else:
            for hd in range(N_HEADS):
                r.kc_out[b, r.cache_slot, pl.ds(N_HEADS * pos0 + hd, ROW_CHUNK, stride=N_HEADS), :] = (
                    k[:, hd * HEAD_COLS:(hd + 1) * HEAD_COLS])
        q = q * (QK_DIM ** -0.5 * math.log2(math.e))
        for hd in range(N_HEADS):
            cols = slice(hd * HEAD_COLS, (hd + 1) * HEAD_COLS)
            qh = q[:, cols]
            r.q_scr[c, 0:ROW_CHUNK, cols] = jnp.where(first_half, qh, 0.0).astype(BF16)
            r.q_scr[c, ROW_CHUNK:2 * ROW_CHUNK, cols] = jnp.where(first_half, 0.0, qh).astype(BF16)
        k_scr[b, :, kv_base + pos0:kv_base + pos0 + ROW_CHUNK] = k.T.astype(BF16)

        v = _dot(h, r.w_in[:, C_V:C_UV])
        if not has_ctx:
            for hd in range(N_HEADS):
                r.vc_out[b, r.cache_slot, pl.ds(N_HEADS * pos0 + hd, ROW_CHUNK, stride=N_HEADS), :] = (
                    v[:, hd * V_DIM:(hd + 1) * V_DIM])
        v_scr[b, kv_base + pos0:kv_base + pos0 + ROW_CHUNK, :] = v.astype(BF16)

        uv = _gelu_tanh(_dot(h, r.w_in[:, C_UV:IN_WIDTH]))
        r.u_scr[rows, :] = uv[:, :SGU_WIDTH]
        vv = uv[:, SGU_WIDTH:]
        mu = jnp.mean(vv, axis=-1, keepdims=True)
        vc = vv - mu
        vn = vc * lax.rsqrt(jnp.mean(vc * vc, axis=-1, keepdims=True) + EPS) * r.sgu_g[...]
        r.vs_scr[rows, :] = vn.astype(BF16)

    lane_pool = lax.broadcasted_iota(jnp.int32, (ROW_CHUNK, 2 * POOL_GROUP), 1)
    narrow = lane_pool < POOL_GROUP
    for c in range(ROWS // ROW_CHUNK):
        r0 = c * ROW_CHUNK
        b, pos0 = r0 // seq, r0 % seq
        base = POOL_HALO + pos0
        pooled = []
        for j, (w_small, w_big) in enumerate(((POOL_WINDOWS[0], POOL_WINDOWS[1]),
                                              (POOL_WINDOWS[2], POOL_WINDOWS[3]))):
            cols = slice(j * 2 * POOL_GROUP, (j + 1) * 2 * POOL_GROUP)
            s_small = jnp.zeros((ROW_CHUNK, 2 * POOL_GROUP), F32)
            s_rest = jnp.zeros((ROW_CHUNK, 2 * POOL_GROUP), F32)
            for d in range(-(w_big // 2), w_big - w_big // 2):
                t = pool_scr[b, base + d:base + d + ROW_CHUNK, cols]
                if -(w_small // 2) <= d < w_small - w_small // 2:
                    s_small = s_small + t
                else:
                    s_rest = s_rest + t
            win_sum = jnp.where(narrow, s_small, s_small + s_rest)
            centre = pool_scr[b, base:base + ROW_CHUNK, cols]
            pooled.append(win_sum / cnt_ref[pos0:pos0 + ROW_CHUNK, cols] - centre)
        pooled = jnp.concatenate(pooled, axis=1).astype(BF16)
        y_a = _dot(pooled, r.wpool[...]) * r.pscale[...]
        r.y_scr[r0:r0 + ROW_CHUNK, Y_POOL:Y_ATTN] = y_a.astype(BF16)

    lq1, lk1, lq2, lk2 = r.lam[0:1, :], r.lam[1:2, :], r.lam[2:3, :], r.lam[3:4, :]
    lam = (jnp.exp(jnp.sum(lq1 * lk1, axis=-1, keepdims=True))
           - jnp.exp(jnp.sum(lq2 * lk2, axis=-1, keepdims=True)) + lam_init)
    n_qb = seq // ROW_CHUNK

    def attn_block(idx, carry):
        b = idx // n_qb
        r0 = pl.multiple_of(idx * ROW_CHUNK, ROW_CHUNK)

        def scores(hd):
            cols = slice(hd * HEAD_COLS, (hd + 1) * HEAD_COLS)
            s = _dot(r.q_scr[idx, :, cols], k_scr[b, cols, :])
            return s[:ROW_CHUNK], s[ROW_CHUNK:]

        pending = [scores(hd) for hd in range(ATTN_LOOKAHEAD)]
        for hd in range(N_HEADS):
            cols = slice(hd * HEAD_COLS, (hd + 1) * HEAD_COLS)
            s1, s2 = pending.pop(0)
            if hd + ATTN_LOOKAHEAD < N_HEADS:
                pending.append(scores(hd + ATTN_LOOKAHEAD))
            e1 = jnp.exp2(s1 - jnp.max(s1, axis=-1, keepdims=True))
            e2 = jnp.exp2(s2 - jnp.max(s2, axis=-1, keepdims=True))
            l1 = jnp.sum(e1, axis=-1, keepdims=True)
            l2 = jnp.sum(e2, axis=-1, keepdims=True)
            p = (e1 - e2 * (lam * l1 / l2)).astype(BF16)
            o = _dot(p, v_scr[b, :, cols]) * (1.0 / l1)
            o = o * lax.rsqrt(jnp.mean(o * o, axis=-1, keepdims=True) + EPS) * r.subln[...]
            o = o * (1.0 - lam_init)
            r.y_scr[pl.ds(r0, ROW_CHUNK), Y_ATTN + hd * V_DIM:Y_ATTN + (hd + 1) * V_DIM] = o.astype(BF16)
        return carry

    lax.fori_loop(0, ROWS // ROW_CHUNK, attn_block, 0)

    lane_sgu = lax.broadcasted_iota(jnp.int32, (CHUNK, 2 * POOL_GROUP), 1)
    sgu_first = lane_sgu < (SGU_WIDTH // 4)

    def sgu_block(idx, carry):
        r0 = pl.multiple_of(idx * CHUNK, CHUNK)
        for j in range(2):
            cols = slice(j * 128, (j + 1) * 128)
            t = _dot(r.wsgu[j], r.vs_scr[pl.ds(r0, CHUNK), cols])
            mixed = jnp.where(sgu_first, t[:CHUNK, :], t[CHUNK:, :]) + r.bsgu[:, cols]
            y_c = r.u_scr[pl.ds(r0, CHUNK), cols] * mixed
            r.y_scr[pl.ds(r0, CHUNK), Y_SGU + j * 128:Y_SGU + (j + 1) * 128] = y_c.astype(BF16)
        return carry

    lax.fori_loop(0, ROWS // CHUNK, sgu_block, 0)

    for c in range(ROWS // ROW_CHUNK):
        rows = slice(c * ROW_CHUNK, (c + 1) * ROW_CHUNK)
        y = _dot(r.y_scr[rows, :], r.w_out[...])
        r.out[rows, :] = x_ref[rows, :] + gate1 * y


def _mixer_kernel(*refs, nb, seq, has_ctx, lam_init, cache_slot, n_passthrough):
    it = iter(refs)
    r = types.SimpleNamespace(cache_slot=cache_slot)
    x_ref, r.mod, r.g1, r.w_in = next(it), next(it), next(it), next(it)
    if has_ctx:
        r.rope_c, r.rope_s, r.ck, r.cv = next(it), next(it), next(it), next(it)
    (r.lam, r.subln, cnt_ref, r.wpool, r.pscale, r.sgu_g, r.wsgu, r.bsgu, r.w_out) = (next(it) for _ in range(9))
    for _ in range(n_passthrough):
        next(it)
    r.out = next(it)
    if not has_ctx:
        r.kc_out, r.vc_out = next(it), next(it)
    pool_scr, r.q_scr, k_scr, v_scr, r.u_scr, r.vs_scr, r.y_scr = (next(it) for _ in range(7))
    _mixer_body(r, x_ref, cnt_ref, pool_scr, k_scr, v_scr, nb=nb, seq=seq, has_ctx=has_ctx, lam_init=lam_init)


def _mixer_call(x2d, mod4, layer, seq, has_ctx, lam_init, params, big, ctx=None, new_cache=None):
    n_rows = x2d.shape[0]
    nb = ROWS // seq
    n_seq = n_rows // seq
    kv_len = seq + (PAST_LEN if has_ctx else 0)

    if has_ctx:
        mod_map = lambda i: (layer, 1 + i, 0, 0)
    else:
        mod_map = lambda i: (layer, 0, 0, 0)

    operands = [x2d, mod4, params["norm1_g"], big["w_in"]]
    in_specs = [
        pl.BlockSpec((ROWS, D_MODEL), lambda i: (i, 0)),
        pl.BlockSpec((None, None, N_MOD, D_MODEL), mod_map),
        _layer_spec((1, D_MODEL), layer),
        _const_spec((D_MODEL, IN_WIDTH)),
    ]
    if has_ctx:
        operands += list(ctx)
        in_specs += [
            _const_spec((seq, HEAD_COLS)),
            _const_spec((seq, HEAD_COLS)),
            pl.BlockSpec((None, None, PAST_LEN * N_HEADS, HEAD_COLS), lambda i: (i, layer, 0, 0)),
            pl.BlockSpec((None, None, PAST_LEN * N_HEADS, V_DIM), lambda i: (i, layer, 0, 0)),
        ]
    operands += [params["lam"], params["subln_g"], _pool_counts(seq), params["w_pool"], params["pool_scale"],
                 params["sgu_norm_g"], params["w_sgu"], params["b_sgu"], big["w_out"]]
    in_specs += [
        _layer_spec((8, 128), layer),
        _layer_spec((1, V_DIM), layer),
        _const_spec((seq, POOL_WIDTH)),
        _layer_spec((POOL_WIDTH, POOL_WIDTH), layer),
        _layer_spec((1, POOL_WIDTH), layer),
        _layer_spec((1, SGU_WIDTH), layer),
        _layer_spec((2, 2 * CHUNK, CHUNK), layer),
        _layer_spec((CHUNK, SGU_WIDTH), layer),
        _const_spec((MIX_WIDTH, D_MODEL)),
    ]

    out_shape = [jax.ShapeDtypeStruct((n_rows, D_MODEL), F32)]
    out_specs = [pl.BlockSpec((ROWS, D_MODEL), lambda i: (i, 0))]
    aliases = {}
    cache_slot = 0
    if not has_ctx:
        out_shape += [jax.ShapeDtypeStruct((n_seq, DEPTH, seq * N_HEADS, HEAD_COLS), F32),
                      jax.ShapeDtypeStruct((n_seq, DEPTH, seq * N_HEADS, V_DIM), F32)]
        if new_cache is None:
            cache_slot = layer
            out_specs += [pl.BlockSpec((nb, DEPTH, seq * N_HEADS, HEAD_COLS), lambda i: (i, 0, 0, 0)),
                          pl.BlockSpec((nb, DEPTH, seq * N_HEADS, V_DIM), lambda i: (i, 0, 0, 0))]
        else:
            out_specs += [pl.BlockSpec((nb, 1, seq * N_HEADS, HEAD_COLS), lambda i: (i, layer, 0, 0)),
                          pl.BlockSpec((nb, 1, seq * N_HEADS, V_DIM), lambda i: (i, layer, 0, 0))]
            aliases = {len(operands): 1, len(operands) + 1: 2}
            operands += list(new_cache)
            in_specs += [pl.BlockSpec(memory_space=pl.ANY)] * 2

    scratch = [
        pltpu.VMEM((nb, seq + 2 * POOL_HALO, POOL_WIDTH), F32),
        pltpu.VMEM((ROWS // ROW_CHUNK, 2 * ROW_CHUNK, QK_WIDTH), BF16),
        pltpu.VMEM((nb, QK_WIDTH, kv_len), BF16),
        pltpu.VMEM((nb, kv_len, ATTN_WIDTH), BF16),
        pltpu.VMEM((ROWS, SGU_WIDTH), F32),
        pltpu.VMEM((ROWS, SGU_WIDTH), BF16),
        pltpu.VMEM((ROWS, MIX_WIDTH), BF16),
    ]
    kernel = functools.partial(_mixer_kernel, nb=nb, seq=seq, has_ctx=has_ctx, lam_init=lam_init,
                               cache_slot=cache_slot, n_passthrough=len(aliases))
    return pl.pallas_call(
        kernel,
        out_shape=out_shape,
        grid=(n_rows // ROWS,),
        in_specs=in_specs,
        out_specs=out_specs,
        scratch_shapes=scratch,
        input_output_aliases=aliases,
        compiler_params=pltpu.CompilerParams(
            dimension_semantics=("arbitrary",), vmem_limit_bytes=VMEM_LIMIT),
        name="mixer_ctx" if has_ctx else "mixer_prompt",
    )(*operands)


def _ffn_kernel(*refs, final_norm, n_cast):
    x_ref, mod_ref, g2_ref, w1_ref, w2_ref, gf_ref = refs[:6]
    cast_in = refs[6:6 + n_cast]
    out_ref = refs[6 + n_cast]
    cast_out = refs[7 + n_cast:7 + 2 * n_cast]
    act_scr = refs[7 + 2 * n_cast]

    for src, dst in zip(cast_in, cast_out):
        dst[...] = src[...].astype(BF16)

    shift2, scale2, gate2 = mod_ref[3:4, :], mod_ref[4:5, :], mod_ref[5:6, :]
    x = x_ref[...]
    h = _rms_mod(x, g2_ref[...], scale2, shift2).astype(BF16)
    half = D_FF // 2
    for c in range(2):
        gate = _dot(h, w1_ref[:, c * half:(c + 1) * half])
        up = _dot(h, w1_ref[:, D_FF + c * half:D_FF + (c + 1) * half])
        act_scr[:, c * half:(c + 1) * half] = (gate * _sigmoid(gate) * up).astype(BF16)
    y = x + gate2 * _dot(act_scr[...], w2_ref[...])
    if final_norm:
        ms = jnp.mean(y * y, axis=-1, keepdims=True)
        y = y * lax.rsqrt(ms + EPS) * gf_ref[...]
    out_ref[...] = y


def _ffn_call(x2d, mod4, layer, seq, has_ctx, params, big, final_norm, next_f32=()):
    n_steps = x2d.shape[0] // FFN_ROWS
    steps_per_seq = max(seq // FFN_ROWS, 1)
    if has_ctx:
        mod_map = lambda i: (layer, 1 + i // steps_per_seq, 0, 0)
    else:
        mod_map = lambda i: (layer, 0, 0, 0)
    in_specs = [
        pl.BlockSpec((FFN_ROWS, D_MODEL), lambda i: (i, 0)),
        pl.BlockSpec((None, None, N_MOD, D_MODEL), mod_map),
        _layer_spec((1, D_MODEL), layer),
        _const_spec((D_MODEL, 2 * D_FF)),
        _const_spec((D_FF, D_MODEL)),
        _const_spec((1, D_MODEL)),
    ]
    out_shape = [jax.ShapeDtypeStruct(x2d.shape, F32)]
    out_specs = [pl.BlockSpec((FFN_ROWS, D_MODEL), lambda i: (i, 0))]
    for w in next_f32:
        _, rows, cols = w.shape
        slab = rows // n_steps
        assert slab * n_steps == rows and slab % 16 == 0
        in_specs.append(pl.BlockSpec((None, slab, cols), lambda i: (layer + 1, i, 0)))
        out_shape.append(jax.ShapeDtypeStruct((rows, cols), BF16))
        out_specs.append(pl.BlockSpec((slab, cols), lambda i: (i, 0)))

    kernel = functools.partial(_ffn_kernel, final_norm=final_norm, n_cast=len(next_f32))
    return pl.pallas_call(
        kernel,
        out_shape=out_shape,
        grid=(n_steps,),
        in_specs=in_specs,
        out_specs=out_specs,
        scratch_shapes=[pltpu.VMEM((FFN_ROWS, D_FF), BF16)],
        compiler_params=pltpu.CompilerParams(
            dimension_semantics=("arbitrary",), vmem_limit_bytes=VMEM_LIMIT),
        name="ffn",
    )(x2d, mod4, params["norm2_g"], big["w_ffn_in"], big["w_ffn_out"], params["final_g"], *next_f32)


def _rope_tables(seq):
    n_rows = seq // GRID_W
    rows = np.repeat(np.arange(n_rows), GRID_W).astype(np.float32)
    cols = np.tile(np.arange(GRID_W), n_rows).astype(np.float32)
    inv = 1.0 / (ROPE_BASE ** (np.arange(0, ROPE_AXIS_DIM, 2, dtype=np.float32) / ROPE_AXIS_DIM))
    ar, ac = rows[:, None] * inv[None], cols[:, None] * inv[None]
    cos_parts, sin_parts = [], []
    for ang in (ar, ac):
        cos_parts += [np.cos(ang), np.cos(ang)]
        sin_parts += [-np.sin(ang), np.sin(ang)]
    cos64 = np.concatenate(cos_parts, axis=1)
    sin64 = np.concatenate(sin_parts, axis=1)
    cos = np.concatenate([cos64, cos64], axis=1).astype(np.float32)
    sin = np.concatenate([sin64, sin64], axis=1).astype(np.float32)
    return jnp.asarray(cos), jnp.asarray(sin)


def _pool_counts(seq):
    t = np.arange(seq)
    cols = []
    for w in POOL_WINDOWS:
        lo = np.clip(t - w // 2, 0, seq)
        hi = np.clip(t + w - w // 2, 0, seq)
        cols.append(np.repeat((hi - lo).astype(np.float32)[:, None], POOL_GROUP, axis=1))
    return jnp.asarray(np.concatenate(cols, axis=1))


def _small_params(norm1_g, w_pool, pool_scale, lam_q1, lam_k1, lam_q2, lam_k2, subln_g,
                  sgu_norm_g, w_sgu, b_sgu, norm2_g, final_g):
    n_groups = len(POOL_WINDOWS)
    eye = jnp.eye(n_groups, dtype=F32)
    w_pool_bd = (eye[None, :, None, :, None] * w_pool[:, :, :, None, :]).reshape(DEPTH, POOL_WIDTH, POOL_WIDTH)
    lam = jnp.zeros((DEPTH, 8, 128), F32)
    lam = lam.at[:, 0, :QK_DIM].set(lam_q1).at[:, 1, :QK_DIM].set(lam_k1)
    lam = lam.at[:, 2, :QK_DIM].set(lam_q2).at[:, 3, :QK_DIM].set(lam_k2)
    return {
        "norm1_g": norm1_g[:, None, :],
        "lam": lam,
        "subln_g": subln_g[:, None, :],
        "w_pool": w_pool_bd.astype(BF16),
        "pool_scale": pool_scale[:, None, :],
        "sgu_norm_g": sgu_norm_g[:, None, :],
        "w_sgu": w_sgu.reshape(DEPTH, 2, 2 * CHUNK, CHUNK).astype(BF16),
        "b_sgu": jnp.repeat(jnp.swapaxes(b_sgu, 1, 2), SGU_WIDTH // 4, axis=2),
        "norm2_g": norm2_g[:, None, :],
        "final_g": final_g[None, :],
    }


BIG_WEIGHTS = ("w_in", "w_out", "w_ffn_in", "w_ffn_out")


def kernel(x_prompt, x_sample, cache_k, cache_v, c, c_ctx, norm1_g, w_ada, b_ada, w_in, w_pool, pool_scale, lam_q1, lam_k1, lam_q2, lam_k2, subln_g, sgu_norm_g, w_sgu, b_sgu, w_out, norm2_g, w_ffn_in, w_ffn_out, final_g):
    batch, seq, _ = x_prompt.shape
    dec_batch, dec_seq, _ = x_sample.shape
    assert ROWS % seq == 0 and dec_seq == ROWS and 1 + dec_batch <= COND_ROWS

    cond = jnp.zeros((COND_ROWS, D_MODEL), F32).at[0].set(c_ctx).at[1:1 + dec_batch].set(c)
    mod4 = _ada_modulation(cond, w_ada, b_ada).reshape(DEPTH, COND_ROWS, N_MOD, D_MODEL)

    ctx = _rope_tables(dec_seq) + (cache_k.reshape(dec_batch, DEPTH, PAST_LEN * N_HEADS, HEAD_COLS),
                                   cache_v.reshape(dec_batch, DEPTH, PAST_LEN * N_HEADS, V_DIM))
    params = _small_params(norm1_g, w_pool, pool_scale, lam_q1, lam_k1, lam_q2, lam_k2, subln_g,
                           sgu_norm_g, w_sgu, b_sgu, norm2_g, final_g)

    big_f32 = (w_in, w_out, w_ffn_in, w_ffn_out)
    big = dict(zip(BIG_WEIGHTS, (w[0].astype(BF16) for w in big_f32)))

    xp = x_prompt.reshape(batch * seq, D_MODEL)
    xs = x_sample.reshape(dec_batch * dec_seq, D_MODEL)
    new_cache = None
    for l in range(DEPTH):
        lam_init = 0.8 - 0.6 * math.exp(-0.3 * l)
        last = l == DEPTH - 1

        xp, *new_cache = _mixer_call(xp, mod4, l, seq, False, lam_init, params, big, new_cache=new_cache)
        xp, = _ffn_call(xp, mod4, l, seq, False, params, big, last)

        xs, = _mixer_call(xs, mod4, l, dec_seq, True, lam_init, params, big, ctx=ctx)
        xs, *next_big = _ffn_call(xs, mod4, l, dec_seq, True, params, big, last,
                                  next_f32=() if last else big_f32)
        big = dict(zip(BIG_WEIGHTS, next_big))

    y_prompt = xp.reshape(batch, seq, D_MODEL)
    y_sample = xs.reshape(dec_batch, dec_seq, D_MODEL)
    new_cache_k = new_cache[0].reshape(batch, DEPTH, seq, N_HEADS, 2 * QK_DIM)
    new_cache_v = new_cache[1].reshape(batch, DEPTH, seq, N_HEADS, V_DIM)
    return (y_prompt, y_sample, new_cache_k, new_cache_v)
```

```python
import functools
import math
import types

import numpy as np
import jax
import jax.numpy as jnp
from jax import lax
from jax.experimental import pallas as pl
from jax.experimental.pallas import tpu as pltpu

D_MODEL = 1024
DEPTH = 2
GRID_W = 64
POOL_WINDOWS = (2, 4, 8, 16)
POOL_WIDTH = 256
POOL_GROUP = 64
POOL_HALO = 8
ATTN_WIDTH = 512
N_HEADS = 4
V_DIM = 128
QK_DIM = 64
HEAD_COLS = 2 * QK_DIM
ROPE_BASE = 10000.0
ROPE_AXIS_DIM = 32
ROPE_HALF = ROPE_AXIS_DIM // 2
CHUNK = 128
SGU_WIDTH = 256
QK_WIDTH = 512
IN_WIDTH = 2304
MIX_WIDTH = 1024
D_FF = 2816
N_MOD = 6
EPS = 1e-6
PAST_LEN = 256

C_POOL, C_Q, C_K, C_V, C_UV = 0, 256, 768, 1280, 1792
Y_POOL, Y_ATTN, Y_SGU = 0, 256, 768

ROWS = 1024
ROW_CHUNK = 256
ATTN_LOOKAHEAD = 2
FFN_ROWS = 512
COND_ROWS = 16
ADA_COLS = 1024
VMEM_LIMIT = 58 * 1024 * 1024

BF16 = jnp.bfloat16
F32 = jnp.float32


def _dot(a, b):
    return jnp.dot(a, b, preferred_element_type=F32)


def _sigmoid(x):
    return 1.0 / (1.0 + jnp.exp(-x))


def _rms_mod(x, g, scale, shift):
    ms = jnp.mean(x * x, axis=-1, keepdims=True)
    y = x * lax.rsqrt(ms + EPS) * g
    return y * (1.0 + scale) + shift


def _gelu_tanh(x):
    c = math.sqrt(2.0 / math.pi)
    return x * (0.5 * (1.0 + jnp.tanh(c * (x + 0.044715 * (x * x * x)))))


def _const_spec(shape):
    zeros = (0,) * len(shape)
    return pl.BlockSpec(shape, lambda *_: zeros, pipeline_mode=pl.Buffered(1))


def _layer_spec(shape, layer):
    index = (layer,) + (0,) * len(shape)
    return pl.BlockSpec((None,) + tuple(shape), lambda *_: index, pipeline_mode=pl.Buffered(1))


def _ada_kernel(cond_ref, w_ref, b_ref, out_ref):
    cond = cond_ref[...]
    s = (cond * _sigmoid(cond)).astype(BF16)
    out_ref[...] = _dot(s, w_ref[...].astype(BF16)) + b_ref[...]


def _ada_modulation(cond, w_ada, b_ada):
    n_cols = N_MOD * D_MODEL
    return pl.pallas_call(
        _ada_kernel,
        out_shape=jax.ShapeDtypeStruct((DEPTH, COND_ROWS, n_cols), F32),
        grid=(DEPTH, n_cols // ADA_COLS),
        in_specs=[
            pl.BlockSpec((COND_ROWS, D_MODEL), lambda l, j: (0, 0)),
            pl.BlockSpec((None, D_MODEL, ADA_COLS), lambda l, j: (l, 0, j)),
            pl.BlockSpec((None, 1, ADA_COLS), lambda l, j: (l, 0, j)),
        ],
        out_specs=pl.BlockSpec((None, COND_ROWS, ADA_COLS), lambda l, j: (l, 0, j)),
        compiler_params=pltpu.CompilerParams(
            dimension_semantics=("arbitrary", "arbitrary"), vmem_limit_bytes=VMEM_LIMIT),
        name="ada_modulation",
    )(cond, w_ada, b_ada.reshape(DEPTH, 1, n_cols))


def _mixer_body(r, x_ref, cnt_ref, pool_scr, k_scr, v_scr, *, nb, seq, has_ctx, lam_init):
    kv_base = PAST_LEN if has_ctx else 0
    shift1, scale1, gate1 = r.mod[0:1, :], r.mod[1:2, :], r.mod[2:3, :]

    lane_head = lax.broadcasted_iota(jnp.int32, (ROW_CHUNK, HEAD_COLS), 1)
    first_half = lane_head < QK_DIM

    for b in range(nb):
        pool_scr[b, 0:POOL_HALO, :] = jnp.zeros((POOL_HALO, POOL_WIDTH), F32)
        pool_scr[b, POOL_HALO + seq:POOL_HALO + seq + POOL_HALO, :] = jnp.zeros((POOL_HALO, POOL_WIDTH), F32)
    if not has_ctx:
        for slot in range(r.kc_out.shape[1]):
            if slot != r.cache_slot:
                r.kc_out[:, slot] = jnp.zeros((nb,) + r.kc_out.shape[2:], F32)
                r.vc_out[:, slot] = jnp.zeros((nb,) + r.vc_out.shape[2:], F32)
    if has_ctx:
        for hd in range(N_HEADS):
            cols = slice(hd * HEAD_COLS, (hd + 1) * HEAD_COLS)
            k_scr[0, cols, 0:PAST_LEN] = r.ck[pl.ds(hd, PAST_LEN, stride=N_HEADS), :].T.astype(BF16)
            v_scr[0, 0:PAST_LEN, cols] = r.cv[pl.ds(hd, PAST_LEN, stride=N_HEADS), :].astype(BF16)
        lane_q = lax.broadcasted_iota(jnp.int32, (ROW_CHUNK, QK_WIDTH), 1)
        rope_low = (lane_q % ROPE_AXIS_DIM) < ROPE_HALF

        def rope(t, pos0):
            partner = jnp.where(rope_low,
                                pltpu.roll(t, QK_WIDTH - ROPE_HALF, 1),
                                pltpu.roll(t, ROPE_HALF, 1))
            cos = r.rope_c[pos0:pos0 + ROW_CHUNK, :]
            sin = r.rope_s[pos0:pos0 + ROW_CHUNK, :]
            cos = jnp.concatenate([cos] * N_HEADS, axis=1)
            sin = jnp.concatenate([sin] * N_HEADS, axis=1)
            return t * cos + partner * sin

    for c in range(ROWS // ROW_CHUNK):
        r0 = c * ROW_CHUNK
        b, pos0 = r0 // seq, r0 % seq
        rows = slice(r0, r0 + ROW_CHUNK)
        h = _rms_mod(x_ref[rows, :], r.g1[...], scale1, shift1).astype(BF16)

        pool_scr[b, POOL_HALO + pos0:POOL_HALO + pos0 + ROW_CHUNK, :] = _dot(h, r.w_in[:, C_POOL:C_Q])

        q = _dot(h, r.w_in[:, C_Q:C_K])
        k = _dot(h, r.w_in[:, C_K:C_V])
        if has_ctx:
            q = rope(q, pos0)
            k = rope(k, pos0)
        else:
            for hd in range(N_HEADS):
                r.kc_out[b, r.cache_slot, pl.ds(N_HEADS * pos0 + hd, ROW_CHUNK, stride=N_HEADS), :] = (
                    k[:, hd * HEAD_COLS:(hd + 1) * HEAD_COLS])
        q = q * (QK_DIM ** -0.5 * math.log2(math.e))
        for hd in range(N_HEADS):
            cols = slice(hd * HEAD_COLS, (hd + 1) * HEAD_COLS)
            qh = q[:, cols]
            r.q_scr[c, 0:ROW_CHUNK, cols] = jnp.where(first_half, qh, 0.0).astype(BF16)
            r.q_scr[c, ROW_CHUNK:2 * ROW_CHUNK, cols] = jnp.where(first_half, 0.0, qh).astype(BF16)
        k_scr[b, :, kv_base + pos0:kv_base + pos0 + ROW_CHUNK] = k.T.astype(BF16)

        v = _dot(h, r.w_in[:, C_V:C_UV])
        if not has_ctx:
            for hd in range(N_HEADS):
                r.vc_out[b, r.cache_slot, pl.ds(N_HEADS * pos0 + hd, ROW_CHUNK, stride=N_HEADS), :] = (
                    v[:, hd * V_DIM:(hd + 1) * V_DIM])
        v_scr[b, kv_base + pos0:kv_base + pos0 + ROW_CHUNK, :] = v.astype(BF16)

        uv = _gelu_tanh(_dot(h, r.w_in[:, C_UV:IN_WIDTH]))
        r.u_scr[rows, :] = uv[:, :SGU_WIDTH]
        vv = uv[:, SGU_WIDTH:]
        mu = jnp.mean(vv, axis=-1, keepdims=True)
        vc = vv - mu
        vn = vc * lax.rsqrt(jnp.mean(vc * vc, axis=-1, keepdims=True) + EPS) * r.sgu_g[...]
        r.vs_scr[rows, :] = vn.astype(BF16)

    lane_pool = lax.broadcasted_iota(jnp.int32, (ROW_CHUNK, 2 * POOL_GROUP), 1)
    narrow = lane_pool < POOL_GROUP
    for c in range(ROWS // ROW_CHUNK):
        r0 = c * ROW_CHUNK
        b, pos0 = r0 // seq, r0 % seq
        base = POOL_HALO + pos0
        pooled = []
        for j, (w_small, w_big) in enumerate(((POOL_WINDOWS[0], POOL_WINDOWS[1]),
                                              (POOL_WINDOWS[2], POOL_WINDOWS[3]))):
            cols = slice(j * 2 * POOL_GROUP, (j + 1) * 2 * POOL_GROUP)
            s_small = jnp.zeros((ROW_CHUNK, 2 * POOL_GROUP), F32)
            s_rest = jnp.zeros((ROW_CHUNK, 2 * POOL_GROUP), F32)
            for d in range(-(w_big // 2), w_big - w_big // 2):
                t = pool_scr[b, base + d:base + d + ROW_CHUNK, cols]
                if -(w_small // 2) <= d < w_small - w_small // 2:
                    s_small = s_small + t
                else:
                    s_rest = s_rest + t
            win_sum = jnp.where(narrow, s_small, s_small + s_rest)
            centre = pool_scr[b, base:base + ROW_CHUNK, cols]
            pooled.append(win_sum / cnt_ref[pos0:pos0 + ROW_CHUNK, cols] - centre)
        pooled = jnp.concatenate(pooled, axis=1).astype(BF16)
        y_a = _dot(pooled, r.wpool[...]) * r.pscale[...]
        r.y_scr[r0:r0 + ROW_CHUNK, Y_POOL:Y_ATTN] = y_a.astype(BF16)

    lq1, lk1, lq2, lk2 = r.lam[0:1, :], r.lam[1:2, :], r.lam[2:3, :], r.lam[3:4, :]
    lam = (jnp.exp(jnp.sum(lq1 * lk1, axis=-1, keepdims=True))
           - jnp.exp(jnp.sum(lq2 * lk2, axis=-1, keepdims=True)) + lam_init)
    n_qb = seq // ROW_CHUNK

    lane_sgu = lax.broadcasted_iota(jnp.int32, (CHUNK, 2 * POOL_GROUP), 1)
    sgu_first = lane_sgu < (SGU_WIDTH // 4)
    out_cols = MIX_WIDTH // N_HEADS

    def out_proj(rows, part):
        cols = slice(part * out_cols, (part + 1) * out_cols)
        y = _dot(r.y_scr[rows, :], r.w_out[:, cols])
        r.out[rows, cols] = x_ref[rows, cols] + gate1[:, cols] * y

    def chunk_block(idx, prev_rows):
        b = idx // n_qb
        r0 = idx * ROW_CHUNK
        if not isinstance(idx, int):
            r0 = pl.multiple_of(r0, ROW_CHUNK)
        rows = pl.ds(r0, ROW_CHUNK)

        for half in range(ROW_CHUNK // CHUNK):
            sub = pl.ds(r0 + half * CHUNK, CHUNK)
            for j in range(2):
                cols = slice(j * 128, (j + 1) * 128)
                t = _dot(r.wsgu[j], r.vs_scr[sub, cols])
                mixed = jnp.where(sgu_first, t[:CHUNK, :], t[CHUNK:, :]) + r.bsgu[:, cols]
                y_c = r.u_scr[sub, cols] * mixed
                r.y_scr[sub, Y_SGU + j * 128:Y_SGU + (j + 1) * 128] = y_c.astype(BF16)

        def scores(hd):
            cols = slice(hd * HEAD_COLS, (hd + 1) * HEAD_COLS)
            s = _dot(r.q_scr[idx, :, cols], k_scr[b, cols, :])
            return s[:ROW_CHUNK], s[ROW_CHUNK:]

        pending = [scores(hd) for hd in range(ATTN_LOOKAHEAD)]
        for hd in range(N_HEADS):
            cols = slice(hd * HEAD_COLS, (hd + 1) * HEAD_COLS)
            s1, s2 = pending.pop(0)
            if hd + ATTN_LOOKAHEAD < N_HEADS:
                pending.append(scores(hd + ATTN_LOOKAHEAD))
            e1 = jnp.exp2(s1 - jnp.max(s1, axis=-1, keepdims=True))
            e2 = jnp.exp2(s2 - jnp.max(s2, axis=-1, keepdims=True))
            l1 = jnp.sum(e1, axis=-1, keepdims=True)
            l2 = jnp.sum(e2, axis=-1, keepdims=True)
            p = (e1 - e2 * (lam * l1 / l2)).astype(BF16)
            o = _dot(p, v_scr[b, :, cols]) * (1.0 / l1)
            if prev_rows is not None:
                out_proj(prev_rows, hd)
            o = o * lax.rsqrt(jnp.mean(o * o, axis=-1, keepdims=True) + EPS) * r.subln[...]
            o = o * (1.0 - lam_init)
            r.y_scr[rows, Y_ATTN + hd * V_DIM:Y_ATTN + (hd + 1) * V_DIM] = o.astype(BF16)

    def loop_block(idx, carry):
        chunk_block(idx, pl.ds(pl.multiple_of((idx - 1) * ROW_CHUNK, ROW_CHUNK), ROW_CHUNK))
        return carry

    n_chunks = ROWS // ROW_CHUNK
    chunk_block(0, None)
    lax.fori_loop(1, n_chunks, loop_block, 0)
    for part in range(N_HEADS):
        out_proj(pl.ds((n_chunks - 1) * ROW_CHUNK, ROW_CHUNK), part)


def _mixer_kernel(*refs, nb, seq, has_ctx, lam_init, cache_slot, n_passthrough):
    it = iter(refs)
    r = types.SimpleNamespace(cache_slot=cache_slot)
    x_ref, r.mod, r.g1, r.w_in = next(it), next(it), next(it), next(it)
    if has_ctx:
        r.rope_c, r.rope_s, r.ck, r.cv = next(it), next(it), next(it), next(it)
    (r.lam, r.subln, cnt_ref, r.wpool, r.pscale, r.sgu_g, r.wsgu, r.bsgu, r.w_out) = (next(it) for _ in range(9))
    for _ in range(n_passthrough):
        next(it)
    r.out = next(it)
    if not has_ctx:
        r.kc_out, r.vc_out = next(it), next(it)
    pool_scr, r.q_scr, k_scr, v_scr, r.u_scr, r.vs_scr, r.y_scr = (next(it) for _ in range(7))
    _mixer_body(r, x_ref, cnt_ref, pool_scr, k_scr, v_scr, nb=nb, seq=seq, has_ctx=has_ctx, lam_init=lam_init)


def _mixer_call(x2d, mod4, layer, seq, has_ctx, lam_init, params, big, ctx=None, new_cache=None):
    n_rows = x2d.shape[0]
    nb = ROWS // seq
    n_seq = n_rows // seq
    kv_len = seq + (PAST_LEN if has_ctx else 0)

    if has_ctx:
        mod_map = lambda i: (layer, 1 + i, 0, 0)
    else:
        mod_map = lambda i: (layer, 0, 0, 0)

    operands = [x2d, mod4, params["norm1_g"], big["w_in"]]
    in_specs = [
        pl.BlockSpec((ROWS, D_MODEL), lambda i: (i, 0)),
        pl.BlockSpec((None, None, N_MOD, D_MODEL), mod_map),
        _layer_spec((1, D_MODEL), layer),
        _const_spec((D_MODEL, IN_WIDTH)),
    ]
    if has_ctx:
        operands += list(ctx)
        in_specs += [
            _const_spec((seq, HEAD_COLS)),
            _const_spec((seq, HEAD_COLS)),
            pl.BlockSpec((None, None, PAST_LEN * N_HEADS, HEAD_COLS), lambda i: (i, layer, 0, 0)),
            pl.BlockSpec((None, None, PAST_LEN * N_HEADS, V_DIM), lambda i: (i, layer, 0, 0)),
        ]
    operands += [params["lam"], params["subln_g"], _pool_counts(seq), params["w_pool"], params["pool_scale"],
                 params["sgu_norm_g"], params["w_sgu"], params["b_sgu"], big["w_out"]]
    in_specs += [
        _layer_spec((8, 128), layer),
        _layer_spec((1, V_DIM), layer),
        _const_spec((seq, POOL_WIDTH)),
        _layer_spec((POOL_WIDTH, POOL_WIDTH), layer),
        _layer_spec((1, POOL_WIDTH), layer),
        _layer_spec((1, SGU_WIDTH), layer),
        _layer_spec((2, 2 * CHUNK, CHUNK), layer),
        _layer_spec((CHUNK, SGU_WIDTH), layer),
        _const_spec((MIX_WIDTH, D_MODEL)),
    ]

    out_shape = [jax.ShapeDtypeStruct((n_rows, D_MODEL), F32)]
    out_specs = [pl.BlockSpec((ROWS, D_MODEL), lambda i: (i, 0))]
    aliases = {}
    cache_slot = 0
    if not has_ctx:
        out_shape += [jax.ShapeDtypeStruct((n_seq, DEPTH, seq * N_HEADS, HEAD_COLS), F32),
                      jax.ShapeDtypeStruct((n_seq, DEPTH, seq * N_HEADS, V_DIM), F32)]
        if new_cache is None:
            cache_slot = layer
            out_specs += [pl.BlockSpec((nb, DEPTH, seq * N_HEADS, HEAD_COLS), lambda i: (i, 0, 0, 0)),
                          pl.BlockSpec((nb, DEPTH, seq * N_HEADS, V_DIM), lambda i: (i, 0, 0, 0))]
        else:
            out_specs += [pl.BlockSpec((nb, 1, seq * N_HEADS, HEAD_COLS), lambda i: (i, layer, 0, 0)),
                          pl.BlockSpec((nb, 1, seq * N_HEADS, V_DIM), lambda i: (i, layer, 0, 0))]
            aliases = {len(operands): 1, len(operands) + 1: 2}
            operands += list(new_cache)
            in_specs += [pl.BlockSpec(memory_space=pl.ANY)] * 2

    scratch = [
        pltpu.VMEM((nb, seq + 2 * POOL_HALO, POOL_WIDTH), F32),
        pltpu.VMEM((ROWS // ROW_CHUNK, 2 * ROW_CHUNK, QK_WIDTH), BF16),
        pltpu.VMEM((nb, QK_WIDTH, kv_len), BF16),
        pltpu.VMEM((nb, kv_len, ATTN_WIDTH), BF16),
        pltpu.VMEM((ROWS, SGU_WIDTH), F32),
        pltpu.VMEM((ROWS, SGU_WIDTH), BF16),
        pltpu.VMEM((ROWS, MIX_WIDTH), BF16),
    ]
    kernel = functools.partial(_mixer_kernel, nb=nb, seq=seq, has_ctx=has_ctx, lam_init=lam_init,
                               cache_slot=cache_slot, n_passthrough=len(aliases))
    return pl.pallas_call(
        kernel,
        out_shape=out_shape,
        grid=(n_rows // ROWS,),
        in_specs=in_specs,
        out_specs=out_specs,
        scratch_shapes=scratch,
        input_output_aliases=aliases,
        compiler_params=pltpu.CompilerParams(
            dimension_semantics=("arbitrary",), vmem_limit_bytes=VMEM_LIMIT),
        name="mixer_ctx" if has_ctx else "mixer_prompt",
    )(*operands)


def _ffn_kernel(*refs, final_norm, n_cast):
    x_ref, mod_ref, g2_ref, w1_ref, w2_ref, gf_ref = refs[:6]
    cast_in = refs[6:6 + n_cast]
    out_ref = refs[6 + n_cast]
    cast_out = refs[7 + n_cast:7 + 2 * n_cast]
    act_scr = refs[7 + 2 * n_cast]

    for src, dst in zip(cast_in, cast_out):
        dst[...] = src[...].astype(BF16)

    shift2, scale2, gate2 = mod_ref[3:4, :], mod_ref[4:5, :], mod_ref[5:6, :]
    x = x_ref[...]
    h = _rms_mod(x, g2_ref[...], scale2, shift2).astype(BF16)
    half = D_FF // 2
    for c in range(2):
        gate = _dot(h, w1_ref[:, c * half:(c + 1) * half])
        up = _dot(h, w1_ref[:, D_FF + c * half:D_FF + (c + 1) * half])
        act_scr[:, c * half:(c + 1) * half] = (gate * _sigmoid(gate) * up).astype(BF16)
    y = x + gate2 * _dot(act_scr[...], w2_ref[...])
    if final_norm:
        ms = jnp.mean(y * y, axis=-1, keepdims=True)
        y = y * lax.rsqrt(ms + EPS) * gf_ref[...]
    out_ref[...] = y


def _ffn_call(x2d, mod4, layer, seq, has_ctx, params, big, final_norm, next_f32=()):
    n_steps = x2d.shape[0] // FFN_ROWS
    steps_per_seq = max(seq // FFN_ROWS, 1)
    if has_ctx:
        mod_map = lambda i: (layer, 1 + i // steps_per_seq, 0, 0)
    else:
        mod_map = lambda i: (layer, 0, 0, 0)
    in_specs = [
        pl.BlockSpec((FFN_ROWS, D_MODEL), lambda i: (i, 0)),
        pl.BlockSpec((None, None, N_MOD, D_MODEL), mod_map),
        _layer_spec((1, D_MODEL), layer),
        _const_spec((D_MODEL, 2 * D_FF)),
        _const_spec((D_FF, D_MODEL)),
        _const_spec((1, D_MODEL)),
    ]
    out_shape = [jax.ShapeDtypeStruct(x2d.shape, F32)]
    out_specs = [pl.BlockSpec((FFN_ROWS, D_MODEL), lambda i: (i, 0))]
    for w in next_f32:
        _, rows, cols = w.shape
        slab = rows // n_steps
        assert slab * n_steps == rows and slab % 16 == 0
        in_specs.append(pl.BlockSpec((None, slab, cols), lambda i: (layer + 1, i, 0)))
        out_shape.append(jax.ShapeDtypeStruct((rows, cols), BF16))
        out_specs.append(pl.BlockSpec((slab, cols), lambda i: (i, 0)))

    kernel = functools.partial(_ffn_kernel, final_norm=final_norm, n_cast=len(next_f32))
    return pl.pallas_call(
        kernel,
        out_shape=out_shape,
        grid=(n_steps,),
        in_specs=in_specs,
        out_specs=out_specs,
        scratch_shapes=[pltpu.VMEM((FFN_ROWS, D_FF), BF16)],
        compiler_params=pltpu.CompilerParams(
            dimension_semantics=("arbitrary",), vmem_limit_bytes=VMEM_LIMIT),
        name="ffn",
    )(x2d, mod4, params["norm2_g"], big["w_ffn_in"], big["w_ffn_out"], params["final_g"], *next_f32)


def _rope_tables(seq):
    n_rows = seq // GRID_W
    rows = np.repeat(np.arange(n_rows), GRID_W).astype(np.float32)
    cols = np.tile(np.arange(GRID_W), n_rows).astype(np.float32)
    inv = 1.0 / (ROPE_BASE ** (np.arange(0, ROPE_AXIS_DIM, 2, dtype=np.float32) / ROPE_AXIS_DIM))
    ar, ac = rows[:, None] * inv[None], cols[:, None] * inv[None]
    cos_parts, sin_parts = [], []
    for ang in (ar, ac):
        cos_parts += [np.cos(ang), np.cos(ang)]
        sin_parts += [-np.sin(ang), np.sin(ang)]
    cos64 = np.concatenate(cos_parts, axis=1)
    sin64 = np.concatenate(sin_parts, axis=1)
    cos = np.concatenate([cos64, cos64], axis=1).astype(np.float32)
    sin = np.concatenate([sin64, sin64], axis=1).astype(np.float32)
    return jnp.asarray(cos), jnp.asarray(sin)


def _pool_counts(seq):
    t = np.arange(seq)
    cols = []
    for w in POOL_WINDOWS:
        lo = np.clip(t - w // 2, 0, seq)
        hi = np.clip(t + w - w // 2, 0, seq)
        cols.append(np.repeat((hi - lo).astype(np.float32)[:, None], POOL_GROUP, axis=1))
    return jnp.asarray(np.concatenate(cols, axis=1))


def _small_params(norm1_g, w_pool, pool_scale, lam_q1, lam_k1, lam_q2, lam_k2, subln_g,
                  sgu_norm_g, w_sgu, b_sgu, norm2_g, final_g):
    n_groups = len(POOL_WINDOWS)
    eye = jnp.eye(n_groups, dtype=F32)
    w_pool_bd = (eye[None, :, None, :, None] * w_pool[:, :, :, None, :]).reshape(DEPTH, POOL_WIDTH, POOL_WIDTH)
    lam = jnp.zeros((DEPTH, 8, 128), F32)
    lam = lam.at[:, 0, :QK_DIM].set(lam_q1).at[:, 1, :QK_DIM].set(lam_k1)
    lam = lam.at[:, 2, :QK_DIM].set(lam_q2).at[:, 3, :QK_DIM].set(lam_k2)
    return {
        "norm1_g": norm1_g[:, None, :],
        "lam": lam,
        "subln_g": subln_g[:, None, :],
        "w_pool": w_pool_bd.astype(BF16),
        "pool_scale": pool_scale[:, None, :],
        "sgu_norm_g": sgu_norm_g[:, None, :],
        "w_sgu": w_sgu.reshape(DEPTH, 2, 2 * CHUNK, CHUNK).astype(BF16),
        "b_sgu": jnp.repeat(jnp.swapaxes(b_sgu, 1, 2), SGU_WIDTH // 4, axis=2),
        "norm2_g": norm2_g[:, None, :],
        "final_g": final_g[None, :],
    }


BIG_WEIGHTS = ("w_in", "w_out", "w_ffn_in", "w_ffn_out")


def kernel(x_prompt, x_sample, cache_k, cache_v, c, c_ctx, norm1_g, w_ada, b_ada, w_in, w_pool, pool_scale, lam_q1, lam_k1, lam_q2, lam_k2, subln_g, sgu_norm_g, w_sgu, b_sgu, w_out, norm2_g, w_ffn_in, w_ffn_out, final_g):
    batch, seq, _ = x_prompt.shape
    dec_batch, dec_seq, _ = x_sample.shape
    assert ROWS % seq == 0 and dec_seq == ROWS and 1 + dec_batch <= COND_ROWS

    cond = jnp.zeros((COND_ROWS, D_MODEL), F32).at[0].set(c_ctx).at[1:1 + dec_batch].set(c)
    mod4 = _ada_modulation(cond, w_ada, b_ada).reshape(DEPTH, COND_ROWS, N_MOD, D_MODEL)

    ctx = _rope_tables(dec_seq) + (cache_k.reshape(dec_batch, DEPTH, PAST_LEN * N_HEADS, HEAD_COLS),
                                   cache_v.reshape(dec_batch, DEPTH, PAST_LEN * N_HEADS, V_DIM))
    params = _small_params(norm1_g, w_pool, pool_scale, lam_q1, lam_k1, lam_q2, lam_k2, subln_g,
                           sgu_norm_g, w_sgu, b_sgu, norm2_g, final_g)

    big_f32 = (w_in, w_out, w_ffn_in, w_ffn_out)
    big = dict(zip(BIG_WEIGHTS, (w[0].astype(BF16) for w in big_f32)))

    xp = x_prompt.reshape(batch * seq, D_MODEL)
    xs = x_sample.reshape(dec_batch * dec_seq, D_MODEL)
    new_cache = None
    for l in range(DEPTH):
        lam_init = 0.8 - 0.6 * math.exp(-0.3 * l)
        last = l == DEPTH - 1

        xp, *new_cache = _mixer_call(xp, mod4, l, seq, False, lam_init, params, big, new_cache=new_cache)
        xp, = _ffn_call(xp, mod4, l, seq, False, params, big, last)

        xs, = _mixer_call(xs, mod4, l, dec_seq, True, lam_init, params, big, ctx=ctx)
        xs, *next_big = _ffn_call(xs, mod4, l, dec_seq, True, params, big, last,
                                  next_f32=() if last else big_f32)
        big = dict(zip(BIG_WEIGHTS, next_big))

    y_prompt = xp.reshape(batch, seq, D_MODEL)
    y_sample = xs.reshape(dec_batch, dec_seq, D_MODEL)
    new_cache_k = new_cache[0].reshape(batch, DEPTH, seq, N_HEADS, 2 * QK_DIM)
    new_cache_v = new_cache[1].reshape(batch, DEPTH, seq, N_HEADS, V_DIM)
    return (y_prompt, y_sample, new_cache_k, new_cache_v)
```

```python
import functools
import math
import types

import numpy as np
import jax
import jax.numpy as jnp
from jax import lax
from jax.experimental import pallas as pl
from jax.experimental.pallas import tpu as pltpu

D_MODEL = 1024
DEPTH = 2
GRID_W = 64
POOL_WINDOWS = (2, 4, 8, 16)
POOL_WIDTH = 256
POOL_GROUP = 64
POOL_HALO = 8
ATTN_WIDTH = 512
N_HEADS = 4
V_DIM = 128
QK_DIM = 64
HEAD_COLS = 2 * QK_DIM
ROPE_BASE = 10000.0
ROPE_AXIS_DIM = 32
ROPE_HALF = ROPE_AXIS_DIM // 2
CHUNK = 128
SGU_WIDTH = 256
QK_WIDTH = 512
IN_WIDTH = 2304
MIX_WIDTH = 1024
D_FF = 2816
N_MOD = 6
EPS = 1e-6
PAST_LEN = 256

C_POOL, C_Q, C_K, C_V, C_UV = 0, 256, 768, 1280, 1792
Y_POOL, Y_ATTN, Y_SGU = 0, 256, 768

ROWS = 1024
ROW_CHUNK = 256
ATTN_LOOKAHEAD = 2
FFN_ROWS = 512
MXU_COLS = 256
FF_SPLIT = (D_FF // MXU_COLS + 1) // 2 * MXU_COLS
COND_ROWS = 16
ADA_COLS = 1024
VMEM_LIMIT = 58 * 1024 * 1024

BF16 = jnp.bfloat16
F32 = jnp.float32


def _dot(a, b):
    return jnp.dot(a, b, preferred_element_type=F32)


def _sigmoid(x):
    return 1.0 / (1.0 + jnp.exp(-x))


def _rms_mod(x, g, scale, shift):
    ms = jnp.mean(x * x, axis=-1, keepdims=True)
    y = x * lax.rsqrt(ms + EPS) * g
    return y * (1.0 + scale) + shift


def _gelu_tanh(x):
    c = math.sqrt(2.0 / math.pi)
    return x * (0.5 * (1.0 + jnp.tanh(c * (x + 0.044715 * (x * x * x)))))


def _const_spec(shape):
    zeros = (0,) * len(shape)
    return pl.BlockSpec(shape, lambda *_: zeros, pipeline_mode=pl.Buffered(1))


def _layer_spec(shape, layer):
    index = (layer,) + (0,) * len(shape)
    return pl.BlockSpec((None,) + tuple(shape), lambda *_: index, pipeline_mode=pl.Buffered(1))


def _ada_kernel(cond_ref, w_ref, b_ref, out_ref):
    cond = cond_ref[...]
    s = (cond * _sigmoid(cond)).astype(BF16)
    out_ref[...] = _dot(s, w_ref[...].astype(BF16)) + b_ref[...]


def _ada_modulation(cond, w_ada, b_ada):
    n_cols = N_MOD * D_MODEL
    return pl.pallas_call(
        _ada_kernel,
        out_shape=jax.ShapeDtypeStruct((DEPTH, COND_ROWS, n_cols), F32),
        grid=(DEPTH, n_cols // ADA_COLS),
        in_specs=[
            pl.BlockSpec((COND_ROWS, D_MODEL), lambda l, j: (0, 0)),
            pl.BlockSpec((None, D_MODEL, ADA_COLS), lambda l, j: (l, 0, j)),
            pl.BlockSpec((None, 1, ADA_COLS), lambda l, j: (l, 0, j)),
        ],
        out_specs=pl.BlockSpec((None, COND_ROWS, ADA_COLS), lambda l, j: (l, 0, j)),
        compiler_params=pltpu.CompilerParams(
            dimension_semantics=("arbitrary", "arbitrary"), vmem_limit_bytes=VMEM_LIMIT),
        name="ada_modulation",
    )(cond, w_ada, b_ada.reshape(DEPTH, 1, n_cols))


def _mixer_body(r, x_ref, cnt_ref, pool_scr, k_scr, v_scr, *, nb, seq, has_ctx, lam_init):
    kv_base = PAST_LEN if has_ctx else 0
    shift1, scale1, gate1 = r.mod[0:1, :], r.mod[1:2, :], r.mod[2:3, :]

    lane_head = lax.broadcasted_iota(jnp.int32, (ROW_CHUNK, HEAD_COLS), 1)
    first_half = lane_head < QK_DIM

    for b in range(nb):
        pool_scr[b, 0:POOL_HALO, :] = jnp.zeros((POOL_HALO, POOL_WIDTH), F32)
        pool_scr[b, POOL_HALO + seq:POOL_HALO + seq + POOL_HALO, :] = jnp.zeros((POOL_HALO, POOL_WIDTH), F32)
    if not has_ctx:
        for slot in range(r.kc_out.shape[1]):
            if slot != r.cache_slot:
                r.kc_out[:, slot] = jnp.zeros((nb,) + r.kc_out.shape[2:], F32)
                r.vc_out[:, slot] = jnp.zeros((nb,) + r.vc_out.shape[2:], F32)
    if has_ctx:
        for hd in range(N_HEADS):
            cols = slice(hd * HEAD_COLS, (hd + 1) * HEAD_COLS)
            k_scr[0, cols, 0:PAST_LEN] = r.ck[pl.ds(hd, PAST_LEN, stride=N_HEADS), :].T.astype(BF16)
            v_scr[0, 0:PAST_LEN, cols] = r.cv[pl.ds(hd, PAST_LEN, stride=N_HEADS), :].astype(BF16)
        lane_q = lax.broadcasted_iota(jnp.int32, (ROW_CHUNK, QK_WIDTH), 1)
        rope_low = (lane_q % ROPE_AXIS_DIM) < ROPE_HALF

        def rope(t, pos0):
            partner = jnp.where(rope_low,
                                pltpu.roll(t, QK_WIDTH - ROPE_HALF, 1),
                                pltpu.roll(t, ROPE_HALF, 1))
            cos = r.rope_c[pos0:pos0 + ROW_CHUNK, :]
            sin = r.rope_s[pos0:pos0 + ROW_CHUNK, :]
            cos = jnp.concatenate([cos] * N_HEADS, axis=1)
            sin = jnp.concatenate([sin] * N_HEADS, axis=1)
            return t * cos + partner * sin

    for c in range(ROWS // ROW_CHUNK):
        r0 = c * ROW_CHUNK
        b, pos0 = r0 // seq, r0 % seq
        rows = slice(r0, r0 + ROW_CHUNK)
        h = _rms_mod(x_ref[rows, :], r.g1[...], scale1, shift1).astype(BF16)

        pool_scr[b, POOL_HALO + pos0:POOL_HALO + pos0 + ROW_CHUNK, :] = _dot(h, r.w_in[:, C_POOL:C_Q])

        q = _dot(h, r.w_in[:, C_Q:C_K])
        k = _dot(h, r.w_in[:, C_K:C_V])
        if has_ctx:
            q = rope(q, pos0)
            k = rope(k, pos0)
        else:
            for hd in range(N_HEADS):
                r.kc_out[b, r.cache_slot, pl.ds(N_HEADS * pos0 + hd, ROW_CHUNK, stride=N_HEADS), :] = (
                    k[:, hd * HEAD_COLS:(hd + 1) * HEAD_COLS])
        q = q * (QK_DIM ** -0.5 * math.log2(math.e))
        for hd in range(N_HEADS):
            cols = slice(hd * HEAD_COLS, (hd + 1) * HEAD_COLS)
            qh = q[:, cols]
            r.q_scr[c, 0:ROW_CHUNK, cols] = jnp.where(first_half, qh, 0.0).astype(BF16)
            r.q_scr[c, ROW_CHUNK:2 * ROW_CHUNK, cols] = jnp.where(first_half, 0.0, qh).astype(BF16)
        k_scr[b, :, kv_base + pos0:kv_base + pos0 + ROW_CHUNK] = k.T.astype(BF16)

        v = _dot(h, r.w_in[:, C_V:C_UV])
        if not has_ctx:
            for hd in range(N_HEADS):
                r.vc_out[b, r.cache_slot, pl.ds(N_HEADS * pos0 + hd, ROW_CHUNK, stride=N_HEADS), :] = (
                    v[:, hd * V_DIM:(hd + 1) * V_DIM])
        v_scr[b, kv_base + pos0:kv_base + pos0 + ROW_CHUNK, :] = v.astype(BF16)

        uv = _gelu_tanh(_dot(h, r.w_in[:, C_UV:IN_WIDTH]))
        r.u_scr[rows, :] = uv[:, :SGU_WIDTH]
        vv = uv[:, SGU_WIDTH:]
        mu = jnp.mean(vv, axis=-1, keepdims=True)
        vc = vv - mu
        vn = vc * lax.rsqrt(jnp.mean(vc * vc, axis=-1, keepdims=True) + EPS) * r.sgu_g[...]
        r.vs_scr[rows, :] = vn.astype(BF16)

    lane_pool = lax.broadcasted_iota(jnp.int32, (ROW_CHUNK, 2 * POOL_GROUP), 1)
    narrow = lane_pool < POOL_GROUP
    for c in range(ROWS // ROW_CHUNK):
        r0 = c * ROW_CHUNK
        b, pos0 = r0 // seq, r0 % seq
        base = POOL_HALO + pos0
        pooled = []
        for j, (w_small, w_big) in enumerate(((POOL_WINDOWS[0], POOL_WINDOWS[1]),
                                              (POOL_WINDOWS[2], POOL_WINDOWS[3]))):
            cols = slice(j * 2 * POOL_GROUP, (j + 1) * 2 * POOL_GROUP)
            s_small = jnp.zeros((ROW_CHUNK, 2 * POOL_GROUP), F32)
            s_rest = jnp.zeros((ROW_CHUNK, 2 * POOL_GROUP), F32)
            for d in range(-(w_big // 2), w_big - w_big // 2):
                t = pool_scr[b, base + d:base + d + ROW_CHUNK, cols]
                if -(w_small // 2) <= d < w_small - w_small // 2:
                    s_small = s_small + t
                else:
                    s_rest = s_rest + t
            win_sum = jnp.where(narrow, s_small, s_small + s_rest)
            centre = pool_scr[b, base:base + ROW_CHUNK, cols]
            pooled.append(win_sum / cnt_ref[pos0:pos0 + ROW_CHUNK, cols] - centre)
        pooled = jnp.concatenate(pooled, axis=1).astype(BF16)
        y_a = _dot(pooled, r.wpool[...]) * r.pscale[...]
        r.y_scr[r0:r0 + ROW_CHUNK, Y_POOL:Y_ATTN] = y_a.astype(BF16)

    lq1, lk1, lq2, lk2 = r.lam[0:1, :], r.lam[1:2, :], r.lam[2:3, :], r.lam[3:4, :]
    lam = (jnp.exp(jnp.sum(lq1 * lk1, axis=-1, keepdims=True))
           - jnp.exp(jnp.sum(lq2 * lk2, axis=-1, keepdims=True)) + lam_init)
    n_qb = seq // ROW_CHUNK

    lane_sgu = lax.broadcasted_iota(jnp.int32, (CHUNK, 2 * POOL_GROUP), 1)
    sgu_first = lane_sgu < (SGU_WIDTH // 4)
    out_cols = MIX_WIDTH // N_HEADS

    def out_proj(rows, part):
        cols = slice(part * out_cols, (part + 1) * out_cols)
        y = _dot(r.y_scr[rows, :], r.w_out[:, cols])
        r.out[rows, cols] = x_ref[rows, cols] + gate1[:, cols] * y

    def chunk_block(idx, prev_rows):
        b = idx // n_qb
        r0 = idx * ROW_CHUNK
        if not isinstance(idx, int):
            r0 = pl.multiple_of(r0, ROW_CHUNK)
        rows = pl.ds(r0, ROW_CHUNK)

        for half in range(ROW_CHUNK // CHUNK):
            sub = pl.ds(r0 + half * CHUNK, CHUNK)
            for j in range(2):
                cols = slice(j * 128, (j + 1) * 128)
                t = _dot(r.wsgu[j], r.vs_scr[sub, cols])
                mixed = jnp.where(sgu_first, t[:CHUNK, :], t[CHUNK:, :]) + r.bsgu[:, cols]
                y_c = r.u_scr[sub, cols] * mixed
                r.y_scr[sub, Y_SGU + j * 128:Y_SGU + (j + 1) * 128] = y_c.astype(BF16)

        def scores(hd):
            cols = slice(hd * HEAD_COLS, (hd + 1) * HEAD_COLS)
            s = _dot(r.q_scr[idx, :, cols], k_scr[b, cols, :])
            return s[:ROW_CHUNK], s[ROW_CHUNK:]

        pending = [scores(hd) for hd in range(ATTN_LOOKAHEAD)]
        for hd in range(N_HEADS):
            cols = slice(hd * HEAD_COLS, (hd + 1) * HEAD_COLS)
            s1, s2 = pending.pop(0)
            if hd + ATTN_LOOKAHEAD < N_HEADS:
                pending.append(scores(hd + ATTN_LOOKAHEAD))
            e1 = jnp.exp2(s1 - jnp.max(s1, axis=-1, keepdims=True))
            e2 = jnp.exp2(s2 - jnp.max(s2, axis=-1, keepdims=True))
            l1 = jnp.sum(e1, axis=-1, keepdims=True)
            l2 = jnp.sum(e2, axis=-1, keepdims=True)
            p = (e1 - e2 * (lam * l1 / l2)).astype(BF16)
            o = _dot(p, v_scr[b, :, cols]) * (1.0 / l1)
            if prev_rows is not None:
                out_proj(prev_rows, hd)
            o = o * lax.rsqrt(jnp.mean(o * o, axis=-1, keepdims=True) + EPS) * r.subln[...]
            o = o * (1.0 - lam_init)
            r.y_scr[rows, Y_ATTN + hd * V_DIM:Y_ATTN + (hd + 1) * V_DIM] = o.astype(BF16)

    def loop_block(idx, carry):
        chunk_block(idx, pl.ds(pl.multiple_of((idx - 1) * ROW_CHUNK, ROW_CHUNK), ROW_CHUNK))
        return carry

    n_chunks = ROWS // ROW_CHUNK
    chunk_block(0, None)
    lax.fori_loop(1, n_chunks, loop_block, 0)
    for part in range(N_HEADS):
        out_proj(pl.ds((n_chunks - 1) * ROW_CHUNK, ROW_CHUNK), part)


def _mixer_kernel(*refs, nb, seq, has_ctx, lam_init, cache_slot, n_passthrough):
    it = iter(refs)
    r = types.SimpleNamespace(cache_slot=cache_slot)
    x_ref, r.mod, r.g1, r.w_in = next(it), next(it), next(it), next(it)
    if has_ctx:
        r.rope_c, r.rope_s, r.ck, r.cv = next(it), next(it), next(it), next(it)
    (r.lam, r.subln, cnt_ref, r.wpool, r.pscale, r.sgu_g, r.wsgu, r.bsgu, r.w_out) = (next(it) for _ in range(9))
    for _ in range(n_passthrough):
        next(it)
    r.out = next(it)
    if not has_ctx:
        r.kc_out, r.vc_out = next(it), next(it)
    pool_scr, r.q_scr, k_scr, v_scr, r.u_scr, r.vs_scr, r.y_scr = (next(it) for _ in range(7))
    _mixer_body(r, x_ref, cnt_ref, pool_scr, k_scr, v_scr, nb=nb, seq=seq, has_ctx=has_ctx, lam_init=lam_init)


def _mixer_call(x2d, mod4, layer, seq, has_ctx, lam_init, params, big, ctx=None, new_cache=None):
    n_rows = x2d.shape[0]
    nb = ROWS // seq
    n_seq = n_rows // seq
    kv_len = seq + (PAST_LEN if has_ctx else 0)

    if has_ctx:
        mod_map = lambda i: (layer, 1 + i, 0, 0)
    else:
        mod_map = lambda i: (layer, 0, 0, 0)

    operands = [x2d, mod4, params["norm1_g"], big["w_in"]]
    in_specs = [
        pl.BlockSpec((ROWS, D_MODEL), lambda i: (i, 0)),
        pl.BlockSpec((None, None, N_MOD, D_MODEL), mod_map),
        _layer_spec((1, D_MODEL), layer),
        _const_spec((D_MODEL, IN_WIDTH)),
    ]
    if has_ctx:
        operands += list(ctx)
        in_specs += [
            _const_spec((seq, HEAD_COLS)),
            _const_spec((seq, HEAD_COLS)),
            pl.BlockSpec((None, None, PAST_LEN * N_HEADS, HEAD_COLS), lambda i: (i, layer, 0, 0)),
            pl.BlockSpec((None, None, PAST_LEN * N_HEADS, V_DIM), lambda i: (i, layer, 0, 0)),
        ]
    operands += [params["lam"], params["subln_g"], _pool_counts(seq), params["w_pool"], params["pool_scale"],
                 params["sgu_norm_g"], params["w_sgu"], params["b_sgu"], big["w_out"]]
    in_specs += [
        _layer_spec((8, 128), layer),
        _layer_spec((1, V_DIM), layer),
        _const_spec((seq, POOL_WIDTH)),
        _layer_spec((POOL_WIDTH, POOL_WIDTH), layer),
        _layer_spec((1, POOL_WIDTH), layer),
        _layer_spec((1, SGU_WIDTH), layer),
        _layer_spec((2, 2 * CHUNK, CHUNK), layer),
        _layer_spec((CHUNK, SGU_WIDTH), layer),
        _const_spec((MIX_WIDTH, D_MODEL)),
    ]

    out_shape = [jax.ShapeDtypeStruct((n_rows, D_MODEL), F32)]
    out_specs = [pl.BlockSpec((ROWS, D_MODEL), lambda i: (i, 0))]
    aliases = {}
    cache_slot = 0
    if not has_ctx:
        out_shape += [jax.ShapeDtypeStruct((n_seq, DEPTH, seq * N_HEADS, HEAD_COLS), F32),
                      jax.ShapeDtypeStruct((n_seq, DEPTH, seq * N_HEADS, V_DIM), F32)]
        if new_cache is None:
            cache_slot = layer
            out_specs += [pl.BlockSpec((nb, DEPTH, seq * N_HEADS, HEAD_COLS), lambda i: (i, 0, 0, 0)),
                          pl.BlockSpec((nb, DEPTH, seq * N_HEADS, V_DIM), lambda i: (i, 0, 0, 0))]
        else:
            out_specs += [pl.BlockSpec((nb, 1, seq * N_HEADS, HEAD_COLS), lambda i: (i, layer, 0, 0)),
                          pl.BlockSpec((nb, 1, seq * N_HEADS, V_DIM), lambda i: (i, layer, 0, 0))]
            aliases = {len(operands): 1, len(operands) + 1: 2}
            operands += list(new_cache)
            in_specs += [pl.BlockSpec(memory_space=pl.ANY)] * 2

    scratch = [
        pltpu.VMEM((nb, seq + 2 * POOL_HALO, POOL_WIDTH), F32),
        pltpu.VMEM((ROWS // ROW_CHUNK, 2 * ROW_CHUNK, QK_WIDTH), BF16),
        pltpu.VMEM((nb, QK_WIDTH, kv_len), BF16),
        pltpu.VMEM((nb, kv_len, ATTN_WIDTH), BF16),
        pltpu.VMEM((ROWS, SGU_WIDTH), F32),
        pltpu.VMEM((ROWS, SGU_WIDTH), BF16),
        pltpu.VMEM((ROWS, MIX_WIDTH), BF16),
    ]
    kernel = functools.partial(_mixer_kernel, nb=nb, seq=seq, has_ctx=has_ctx, lam_init=lam_init,
                               cache_slot=cache_slot, n_passthrough=len(aliases))
    return pl.pallas_call(
        kernel,
        out_shape=out_shape,
        grid=(n_rows // ROWS,),
        in_specs=in_specs,
        out_specs=out_specs,
        scratch_shapes=scratch,
        input_output_aliases=aliases,
        compiler_params=pltpu.CompilerParams(
            dimension_semantics=("arbitrary",), vmem_limit_bytes=VMEM_LIMIT),
        name="mixer_ctx" if has_ctx else "mixer_prompt",
    )(*operands)


def _ffn_kernel(*refs, final_norm, n_cast):
    x_ref, mod_ref, g2_ref, w1_ref, w2_ref, gf_ref = refs[:6]
    cast_in = refs[6:6 + n_cast]
    out_ref = refs[6 + n_cast]
    cast_out = refs[7 + n_cast:7 + 2 * n_cast]
    act_scr = refs[7 + 2 * n_cast]

    for src, dst in zip(cast_in, cast_out):
        dst[...] = src[...].astype(BF16)

    shift2, scale2, gate2 = mod_ref[3:4, :], mod_ref[4:5, :], mod_ref[5:6, :]
    x = x_ref[...]
    h = _rms_mod(x, g2_ref[...], scale2, shift2).astype(BF16)
    for lo, hi in ((0, FF_SPLIT), (FF_SPLIT, D_FF)):
        gate = _dot(h, w1_ref[:, lo:hi])
        up = _dot(h, w1_ref[:, D_FF + lo:D_FF + hi])
        act_scr[:, lo:hi] = (gate * _sigmoid(gate) * up).astype(BF16)
    y = x + gate2 * _dot(act_scr[...], w2_ref[...])
    if final_norm:
        ms = jnp.mean(y * y, axis=-1, keepdims=True)
        y = y * lax.rsqrt(ms + EPS) * gf_ref[...]
    out_ref[...] = y


def _ffn_call(x2d, mod4, layer, seq, has_ctx, params, big, final_norm, next_f32=()):
    n_steps = x2d.shape[0] // FFN_ROWS
    steps_per_seq = max(seq // FFN_ROWS, 1)
    if has_ctx:
        mod_map = lambda i: (layer, 1 + i // steps_per_seq, 0, 0)
    else:
        mod_map = lambda i: (layer, 0, 0, 0)
    in_specs = [
        pl.BlockSpec((FFN_ROWS, D_MODEL), lambda i: (i, 0)),
        pl.BlockSpec((None, None, N_MOD, D_MODEL), mod_map),
        _layer_spec((1, D_MODEL), layer),
        _const_spec((D_MODEL, 2 * D_FF)),
        _const_spec((D_FF, D_MODEL)),
        _const_spec((1, D_MODEL)),
    ]
    out_shape = [jax.ShapeDtypeStruct(x2d.shape, F32)]
    out_specs = [pl.BlockSpec((FFN_ROWS, D_MODEL), lambda i: (i, 0))]
    for w in next_f32:
        _, rows, cols = w.shape
        slab = rows // n_steps
        assert slab * n_steps == rows and slab % 16 == 0
        in_specs.append(pl.BlockSpec((None, slab, cols), lambda i: (layer + 1, i, 0)))
        out_shape.append(jax.ShapeDtypeStruct((rows, cols), BF16))
        out_specs.append(pl.BlockSpec((slab, cols), lambda i: (i, 0)))

    kernel = functools.partial(_ffn_kernel, final_norm=final_norm, n_cast=len(next_f32))
    return pl.pallas_call(
        kernel,
        out_shape=out_shape,
        grid=(n_steps,),
        in_specs=in_specs,
        out_specs=out_specs,
        scratch_shapes=[pltpu.VMEM((FFN_ROWS, D_FF), BF16)],
        compiler_params=pltpu.CompilerParams(
            dimension_semantics=("arbitrary",), vmem_limit_bytes=VMEM_LIMIT),
        name="ffn",
    )(x2d, mod4, params["norm2_g"], big["w_ffn_in"], big["w_ffn_out"], params["final_g"], *next_f32)


def _rope_tables(seq):
    n_rows = seq // GRID_W
    rows = np.repeat(np.arange(n_rows), GRID_W).astype(np.float32)
    cols = np.tile(np.arange(GRID_W), n_rows).astype(np.float32)
    inv = 1.0 / (ROPE_BASE ** (np.arange(0, ROPE_AXIS_DIM, 2, dtype=np.float32) / ROPE_AXIS_DIM))
    ar, ac = rows[:, None] * inv[None], cols[:, None] * inv[None]
    cos_parts, sin_parts = [], []
    for ang in (ar, ac):
        cos_parts += [np.cos(ang), np.cos(ang)]
        sin_parts += [-np.sin(ang), np.sin(ang)]
    cos64 = np.concatenate(cos_parts, axis=1)
    sin64 = np.concatenate(sin_parts, axis=1)
    cos = np.concatenate([cos64, cos64], axis=1).astype(np.float32)
    sin = np.concatenate([sin64, sin64], axis=1).astype(np.float32)
    return jnp.asarray(cos), jnp.asarray(sin)


def _pool_counts(seq):
    t = np.arange(seq)
    cols = []
    for w in POOL_WINDOWS:
        lo = np.clip(t - w // 2, 0, seq)
        hi = np.clip(t + w - w // 2, 0, seq)
        cols.append(np.repeat((hi - lo).astype(np.float32)[:, None], POOL_GROUP, axis=1))
    return jnp.asarray(np.concatenate(cols, axis=1))


def _small_params(norm1_g, w_pool, pool_scale, lam_q1, lam_k1, lam_q2, lam_k2, subln_g,
                  sgu_norm_g, w_sgu, b_sgu, norm2_g, final_g):
    n_groups = len(POOL_WINDOWS)
    eye = jnp.eye(n_groups, dtype=F32)
    w_pool_bd = (eye[None, :, None, :, None] * w_pool[:, :, :, None, :]).reshape(DEPTH, POOL_WIDTH, POOL_WIDTH)
    lam = jnp.zeros((DEPTH, 8, 128), F32)
    lam = lam.at[:, 0, :QK_DIM].set(lam_q1).at[:, 1, :QK_DIM].set(lam_k1)
    lam = lam.at[:, 2, :QK_DIM].set(lam_q2).at[:, 3, :QK_DIM].set(lam_k2)
    return {
        "norm1_g": norm1_g[:, None, :],
        "lam": lam,
        "subln_g": subln_g[:, None, :],
        "w_pool": w_pool_bd.astype(BF16),
        "pool_scale": pool_scale[:, None, :],
        "sgu_norm_g": sgu_norm_g[:, None, :],
        "w_sgu": w_sgu.reshape(DEPTH, 2, 2 * CHUNK, CHUNK).astype(BF16),
        "b_sgu": jnp.repeat(jnp.swapaxes(b_sgu, 1, 2), SGU_WIDTH // 4, axis=2),
        "norm2_g": norm2_g[:, None, :],
        "final_g": final_g[None, :],
    }


BIG_WEIGHTS = ("w_in", "w_out", "w_ffn_in", "w_ffn_out")


def kernel(x_prompt, x_sample, cache_k, cache_v, c, c_ctx, norm1_g, w_ada, b_ada, w_in, w_pool, pool_scale, lam_q1, lam_k1, lam_q2, lam_k2, subln_g, sgu_norm_g, w_sgu, b_sgu, w_out, norm2_g, w_ffn_in, w_ffn_out, final_g):
    batch, seq, _ = x_prompt.shape
    dec_batch, dec_seq, _ = x_sample.shape
    assert ROWS % seq == 0 and dec_seq == ROWS and 1 + dec_batch <= COND_ROWS

    cond = jnp.zeros((COND_ROWS, D_MODEL), F32).at[0].set(c_ctx).at[1:1 + dec_batch].set(c)
    mod4 = _ada_modulation(cond, w_ada, b_ada).reshape(DEPTH, COND_ROWS, N_MOD, D_MODEL)

    ctx = _rope_tables(dec_seq) + (cache_k.reshape(dec_batch, DEPTH, PAST_LEN * N_HEADS, HEAD_COLS),
                                   cache_v.reshape(dec_batch, DEPTH, PAST_LEN * N_HEADS, V_DIM))
    params = _small_params(norm1_g, w_pool, pool_scale, lam_q1, lam_k1, lam_q2, lam_k2, subln_g,
                           sgu_norm_g, w_sgu, b_sgu, norm2_g, final_g)

    big_f32 = (w_in, w_out, w_ffn_in, w_ffn_out)
    big = dict(zip(BIG_WEIGHTS, (w[0].astype(BF16) for w in big_f32)))

    xp = x_prompt.reshape(batch * seq, D_MODEL)
    xs = x_sample.reshape(dec_batch * dec_seq, D_MODEL)
    new_cache = None
    for l in range(DEPTH):
        lam_init = 0.8 - 0.6 * math.exp(-0.3 * l)
        last = l == DEPTH - 1

        xp, *new_cache = _mixer_call(xp, mod4, l, seq, False, lam_init, params, big, new_cache=new_cache)
        xp, = _ffn_call(xp, mod4, l, seq, False, params, big, last)

        xs, = _mixer_call(xs, mod4, l, dec_seq, True, lam_init, params, big, ctx=ctx)
        xs, *next_big = _ffn_call(xs, mod4, l, dec_seq, True, params, big, last,
                                  next_f32=() if last else big_f32)
        big = dict(zip(BIG_WEIGHTS, next_big))

    y_prompt = xp.reshape(batch, seq, D_MODEL)
    y_sample = xs.reshape(dec_batch, dec_seq, D_MODEL)
    new_cache_k = new_cache[0].reshape(batch, DEPTH, seq, N_HEADS, 2 * QK_DIM)
    new_cache_v = new_cache[1].reshape(batch, DEPTH, seq, N_HEADS, V_DIM)
    return (y_prompt, y_sample, new_cache_k, new_cache_v)
```

```python
import functools
import math
import types

import numpy as np
import jax
import jax.numpy as jnp
from jax import lax
from jax.experimental import pallas as pl
from jax.experimental.pallas import tpu as pltpu

D_MODEL = 1024
DEPTH = 2
GRID_W = 64
POOL_WINDOWS = (2, 4, 8, 16)
POOL_WIDTH = 256
POOL_GROUP = 64
POOL_HALO = 8
ATTN_WIDTH = 512
N_HEADS = 4
V_DIM = 128
QK_DIM = 64
HEAD_COLS = 2 * QK_DIM
ROPE_BASE = 10000.0
ROPE_AXIS_DIM = 32
ROPE_HALF = ROPE_AXIS_DIM // 2
CHUNK = 128
SGU_WIDTH = 256
QK_WIDTH = 512
IN_WIDTH = 2304
MIX_WIDTH = 1024
D_FF = 2816
N_MOD = 6
EPS = 1e-6
PAST_LEN = 256

C_POOL, C_Q, C_K, C_V, C_UV = 0, 256, 768, 1280, 1792
Y_POOL, Y_ATTN, Y_SGU = 0, 256, 768

ROWS = 1024
ROW_CHUNK = 256
ATTN_LOOKAHEAD = 2
FFN_ROWS = 512
MXU_COLS = 256
FF_SPLIT = (D_FF // MXU_COLS + 1) // 2 * MXU_COLS
COND_ROWS = 16
ADA_COLS = 1024
VMEM_LIMIT = 58 * 1024 * 1024

BF16 = jnp.bfloat16
F32 = jnp.float32


def _dot(a, b):
    return jnp.dot(a, b, preferred_element_type=F32)


def _sigmoid(x):
    return 1.0 / (1.0 + jnp.exp(-x))


def _rms_mod(x, g, scale, shift):
    ms = jnp.mean(x * x, axis=-1, keepdims=True)
    y = x * lax.rsqrt(ms + EPS) * g
    return y * (1.0 + scale) + shift


def _gelu_tanh(x):
    c = math.sqrt(2.0 / math.pi)
    return x * (0.5 * (1.0 + jnp.tanh(c * (x + 0.044715 * (x * x * x)))))


def _const_spec(shape):
    zeros = (0,) * len(shape)
    return pl.BlockSpec(shape, lambda *_: zeros, pipeline_mode=pl.Buffered(1))


def _layer_spec(shape, layer):
    index = (layer,) + (0,) * len(shape)
    return pl.BlockSpec((None,) + tuple(shape), lambda *_: index, pipeline_mode=pl.Buffered(1))


def _ada_kernel(cond_ref, w_ref, b_ref, out_ref):
    cond = cond_ref[...]
    s = (cond * _sigmoid(cond)).astype(BF16)
    out_ref[...] = _dot(s, w_ref[...].astype(BF16)) + b_ref[...]


def _ada_modulation(cond, w_ada, b_ada):
    n_cols = N_MOD * D_MODEL
    return pl.pallas_call(
        _ada_kernel,
        out_shape=jax.ShapeDtypeStruct((DEPTH, COND_ROWS, n_cols), F32),
        grid=(DEPTH, n_cols // ADA_COLS),
        in_specs=[
            pl.BlockSpec((COND_ROWS, D_MODEL), lambda l, j: (0, 0)),
            pl.BlockSpec((None, D_MODEL, ADA_COLS), lambda l, j: (l, 0, j)),
            pl.BlockSpec((None, 1, ADA_COLS), lambda l, j: (l, 0, j)),
        ],
        out_specs=pl.BlockSpec((None, COND_ROWS, ADA_COLS), lambda l, j: (l, 0, j)),
        compiler_params=pltpu.CompilerParams(
            dimension_semantics=("arbitrary", "arbitrary"), vmem_limit_bytes=VMEM_LIMIT),
        name="ada_modulation",
    )(cond, w_ada, b_ada.reshape(DEPTH, 1, n_cols))


def _mixer_body(r, x_ref, cnt_ref, pool_scr, k_scr, v_scr, *, nb, seq, has_ctx, lam_init):
    kv_base = PAST_LEN if has_ctx else 0
    shift1, scale1, gate1 = r.mod[0:1, :], r.mod[1:2, :], r.mod[2:3, :]

    lane_head = lax.broadcasted_iota(jnp.int32, (ROW_CHUNK, HEAD_COLS), 1)
    first_half = lane_head < QK_DIM

    for b in range(nb):
        pool_scr[b, 0:POOL_HALO, :] = jnp.zeros((POOL_HALO, POOL_WIDTH), F32)
        pool_scr[b, POOL_HALO + seq:POOL_HALO + seq + POOL_HALO, :] = jnp.zeros((POOL_HALO, POOL_WIDTH), F32)
    if not has_ctx:
        for slot in range(r.kc_out.shape[1]):
            if slot != r.cache_slot:
                r.kc_out[:, slot] = jnp.zeros((nb,) + r.kc_out.shape[2:], F32)
                r.vc_out[:, slot] = jnp.zeros((nb,) + r.vc_out.shape[2:], F32)
    if has_ctx:
        for hd in range(N_HEADS):
            cols = slice(hd * HEAD_COLS, (hd + 1) * HEAD_COLS)
            k_scr[0, cols, 0:PAST_LEN] = r.ck[pl.ds(hd, PAST_LEN, stride=N_HEADS), :].T.astype(BF16)
            v_scr[0, 0:PAST_LEN, cols] = r.cv[pl.ds(hd, PAST_LEN, stride=N_HEADS), :].astype(BF16)
        lane_q = lax.broadcasted_iota(jnp.int32, (ROW_CHUNK, QK_WIDTH), 1)
        rope_low = (lane_q % ROPE_AXIS_DIM) < ROPE_HALF

        def rope(t, pos0):
            partner = jnp.where(rope_low,
                                pltpu.roll(t, QK_WIDTH - ROPE_HALF, 1),
                                pltpu.roll(t, ROPE_HALF, 1))
            cos = r.rope_c[pos0:pos0 + ROW_CHUNK, :]
            sin = r.rope_s[pos0:pos0 + ROW_CHUNK, :]
            cos = jnp.concatenate([cos] * N_HEADS, axis=1)
            sin = jnp.concatenate([sin] * N_HEADS, axis=1)
            return t * cos + partner * sin

    for c in range(ROWS // ROW_CHUNK):
        r0 = c * ROW_CHUNK
        b, pos0 = r0 // seq, r0 % seq
        rows = slice(r0, r0 + ROW_CHUNK)
        h = _rms_mod(x_ref[rows, :], r.g1[...], scale1, shift1).astype(BF16)

        pool_scr[b, POOL_HALO + pos0:POOL_HALO + pos0 + ROW_CHUNK, :] = _dot(h, r.w_in[:, C_POOL:C_Q])

        q = _dot(h, r.w_in[:, C_Q:C_K])
        k = _dot(h, r.w_in[:, C_K:C_V])
        if has_ctx:
            q = rope(q, pos0)
            k = rope(k, pos0)
        else:
            for hd in range(N_HEADS):
                r.kc_out[b, r.cache_slot, pl.ds(N_HEADS * pos0 + hd, ROW_CHUNK, stride=N_HEADS), :] = (
                    k[:, hd * HEAD_COLS:(hd + 1) * HEAD_COLS])
        q = q * (QK_DIM ** -0.5 * math.log2(math.e))
        for hd in range(N_HEADS):
            cols = slice(hd * HEAD_COLS, (hd + 1) * HEAD_COLS)
            qh = q[:, cols]
            r.q_scr[c, 0:ROW_CHUNK, cols] = jnp.where(first_half, qh, 0.0).astype(BF16)
            r.q_scr[c, ROW_CHUNK:2 * ROW_CHUNK, cols] = jnp.where(first_half, 0.0, qh).astype(BF16)
        k_scr[b, :, kv_base + pos0:kv_base + pos0 + ROW_CHUNK] = k.T.astype(BF16)

        v = _dot(h, r.w_in[:, C_V:C_UV])
        if not has_ctx:
            for hd in range(N_HEADS):
                r.vc_out[b, r.cache_slot, pl.ds(N_HEADS * pos0 + hd, ROW_CHUNK, stride=N_HEADS), :] = (
                    v[:, hd * V_DIM:(hd + 1) * V_DIM])
        v_scr[b, kv_base + pos0:kv_base + pos0 + ROW_CHUNK, :] = v.astype(BF16)

        uv = _gelu_tanh(_dot(h, r.w_in[:, C_UV:IN_WIDTH]))
        r.u_scr[rows, :] = uv[:, :SGU_WIDTH]
        vv = uv[:, SGU_WIDTH:]
        mu = jnp.mean(vv, axis=-1, keepdims=True)
        vc = vv - mu
        vn = vc * lax.rsqrt(jnp.mean(vc * vc, axis=-1, keepdims=True) + EPS) * r.sgu_g[...]
        r.vs_scr[rows, :] = vn.astype(BF16)

    lane_pool = lax.broadcasted_iota(jnp.int32, (ROW_CHUNK, 2 * POOL_GROUP), 1)
    narrow = lane_pool < POOL_GROUP
    for c in range(ROWS // ROW_CHUNK):
        r0 = c * ROW_CHUNK
        b, pos0 = r0 // seq, r0 % seq
        base = POOL_HALO + pos0
        pooled = []
        for j, (w_small, w_big) in enumerate(((POOL_WINDOWS[0], POOL_WINDOWS[1]),
                                              (POOL_WINDOWS[2], POOL_WINDOWS[3]))):
            cols = slice(j * 2 * POOL_GROUP, (j + 1) * 2 * POOL_GROUP)
            s_small = jnp.zeros((ROW_CHUNK, 2 * POOL_GROUP), F32)
            s_rest = jnp.zeros((ROW_CHUNK, 2 * POOL_GROUP), F32)
            for d in range(-(w_big // 2), w_big - w_big // 2):
                t = pool_scr[b, base + d:base + d + ROW_CHUNK, cols]
                if -(w_small // 2) <= d < w_small - w_small // 2:
                    s_small = s_small + t
                else:
                    s_rest = s_rest + t
            win_sum = jnp.where(narrow, s_small, s_small + s_rest)
            centre = pool_scr[b, base:base + ROW_CHUNK, cols]
            pooled.append(win_sum / cnt_ref[pos0:pos0 + ROW_CHUNK, cols] - centre)
        pooled = jnp.concatenate(pooled, axis=1).astype(BF16)
        y_a = _dot(pooled, r.wpool[...]) * r.pscale[...]
        r.y_scr[r0:r0 + ROW_CHUNK, Y_POOL:Y_ATTN] = y_a.astype(BF16)

    lq1, lk1, lq2, lk2 = r.lam[0:1, :], r.lam[1:2, :], r.lam[2:3, :], r.lam[3:4, :]
    lam = (jnp.exp(jnp.sum(lq1 * lk1, axis=-1, keepdims=True))
           - jnp.exp(jnp.sum(lq2 * lk2, axis=-1, keepdims=True)) + lam_init)
    n_qb = seq // ROW_CHUNK

    lane_sgu = lax.broadcasted_iota(jnp.int32, (CHUNK, 2 * POOL_GROUP), 1)
    sgu_first = lane_sgu < (SGU_WIDTH // 4)
    out_cols = MIX_WIDTH // N_HEADS

    def out_proj(rows, part):
        cols = slice(part * out_cols, (part + 1) * out_cols)
        y = _dot(r.y_scr[rows, :], r.w_out[:, cols])
        r.out[rows, cols] = x_ref[rows, cols] + gate1[:, cols] * y

    def chunk_block(idx, prev_rows):
        b = idx // n_qb
        r0 = idx * ROW_CHUNK
        if not isinstance(idx, int):
            r0 = pl.multiple_of(r0, ROW_CHUNK)
        rows = pl.ds(r0, ROW_CHUNK)

        for half in range(ROW_CHUNK // CHUNK):
            sub = pl.ds(r0 + half * CHUNK, CHUNK)
            for j in range(2):
                cols = slice(j * 128, (j + 1) * 128)
                t = _dot(r.wsgu[j], r.vs_scr[sub, cols])
                mixed = jnp.where(sgu_first, t[:CHUNK, :], t[CHUNK:, :]) + r.bsgu[:, cols]
                y_c = r.u_scr[sub, cols] * mixed
                r.y_scr[sub, Y_SGU + j * 128:Y_SGU + (j + 1) * 128] = y_c.astype(BF16)

        def scores(hd):
            cols = slice(hd * HEAD_COLS, (hd + 1) * HEAD_COLS)
            s = _dot(r.q_scr[idx, :, cols], k_scr[b, cols, :])
            return s[:ROW_CHUNK], s[ROW_CHUNK:]

        pending = [scores(hd) for hd in range(ATTN_LOOKAHEAD)]
        for hd in range(N_HEADS):
            cols = slice(hd * HEAD_COLS, (hd + 1) * HEAD_COLS)
            s1, s2 = pending.pop(0)
            if hd + ATTN_LOOKAHEAD < N_HEADS:
                pending.append(scores(hd + ATTN_LOOKAHEAD))
            e1 = jnp.exp2(s1 - jnp.max(s1, axis=-1, keepdims=True))
            e2 = jnp.exp2(s2 - jnp.max(s2, axis=-1, keepdims=True))
            l1 = jnp.sum(e1, axis=-1, keepdims=True)
            l2 = jnp.sum(e2, axis=-1, keepdims=True)
            p = (e1 - e2 * (lam * l1 / l2)).astype(BF16)
            o = _dot(p, v_scr[b, :, cols]) * (1.0 / l1)
            if prev_rows is not None:
                out_proj(prev_rows, hd)
            o = o * lax.rsqrt(jnp.mean(o * o, axis=-1, keepdims=True) + EPS) * r.subln[...]
            o = o * (1.0 - lam_init)
            r.y_scr[rows, Y_ATTN + hd * V_DIM:Y_ATTN + (hd + 1) * V_DIM] = o.astype(BF16)

    def loop_block(idx, carry):
        chunk_block(idx, pl.ds(pl.multiple_of((idx - 1) * ROW_CHUNK, ROW_CHUNK), ROW_CHUNK))
        return carry

    n_chunks = ROWS // ROW_CHUNK
    chunk_block(0, None)
    lax.fori_loop(1, n_chunks, loop_block, 0)
    for part in range(N_HEADS):
        out_proj(pl.ds((n_chunks - 1) * ROW_CHUNK, ROW_CHUNK), part)


def _mixer_kernel(*refs, nb, seq, has_ctx, lam_init, cache_slot, n_passthrough):
    it = iter(refs)
    r = types.SimpleNamespace(cache_slot=cache_slot)
    x_ref, r.mod, r.g1, r.w_in = next(it), next(it), next(it), next(it)
    if has_ctx:
        r.rope_c, r.rope_s, r.ck, r.cv = next(it), next(it), next(it), next(it)
    (r.lam, r.subln, cnt_ref, r.wpool, r.pscale, r.sgu_g, r.wsgu, r.bsgu, r.w_out) = (next(it) for _ in range(9))
    for _ in range(n_passthrough):
        next(it)
    r.out = next(it)
    if not has_ctx:
        r.kc_out, r.vc_out = next(it), next(it)
    pool_scr, r.q_scr, k_scr, v_scr, r.u_scr, r.vs_scr, r.y_scr = (next(it) for _ in range(7))
    _mixer_body(r, x_ref, cnt_ref, pool_scr, k_scr, v_scr, nb=nb, seq=seq, has_ctx=has_ctx, lam_init=lam_init)


def _mixer_call(x2d, mod4, layer, seq, has_ctx, lam_init, params, big, ctx=None, new_cache=None):
    n_rows = x2d.shape[0]
    nb = ROWS // seq
    n_seq = n_rows // seq
    kv_len = seq + (PAST_LEN if has_ctx else 0)

    if has_ctx:
        mod_map = lambda i: (layer, 1 + i, 0, 0)
    else:
        mod_map = lambda i: (layer, 0, 0, 0)

    operands = [x2d, mod4, params["norm1_g"], big["w_in"]]
    in_specs = [
        pl.BlockSpec((ROWS, D_MODEL), lambda i: (i, 0)),
        pl.BlockSpec((None, None, N_MOD, D_MODEL), mod_map),
        _layer_spec((1, D_MODEL), layer),
        _const_spec((D_MODEL, IN_WIDTH)),
    ]
    if has_ctx:
        operands += list(ctx)
        in_specs += [
            _const_spec((seq, HEAD_COLS)),
            _const_spec((seq, HEAD_COLS)),
            pl.BlockSpec((None, None, PAST_LEN * N_HEADS, HEAD_COLS), lambda i: (i, layer, 0, 0)),
            pl.BlockSpec((None, None, PAST_LEN * N_HEADS, V_DIM), lambda i: (i, layer, 0, 0)),
        ]
    operands += [params["lam"], params["subln_g"], _pool_counts(seq), params["w_pool"], params["pool_scale"],
                 params["sgu_norm_g"], params["w_sgu"], params["b_sgu"], big["w_out"]]
    in_specs += [
        _layer_spec((8, 128), layer),
        _layer_spec((1, V_DIM), layer),
        _const_spec((seq, POOL_WIDTH)),
        _layer_spec((POOL_WIDTH, POOL_WIDTH), layer),
        _layer_spec((1, POOL_WIDTH), layer),
        _layer_spec((1, SGU_WIDTH), layer),
        _layer_spec((2, 2 * CHUNK, CHUNK), layer),
        _layer_spec((CHUNK, SGU_WIDTH), layer),
        _const_spec((MIX_WIDTH, D_MODEL)),
    ]

    out_shape = [jax.ShapeDtypeStruct((n_rows, D_MODEL), F32)]
    out_specs = [pl.BlockSpec((ROWS, D_MODEL), lambda i: (i, 0))]
    aliases = {}
    cache_slot = 0
    if not has_ctx:
        out_shape += [jax.ShapeDtypeStruct((n_seq, DEPTH, seq * N_HEADS, HEAD_COLS), F32),
                      jax.ShapeDtypeStruct((n_seq, DEPTH, seq * N_HEADS, V_DIM), F32)]
        if new_cache is None:
            cache_slot = layer
            out_specs += [pl.BlockSpec((nb, DEPTH, seq * N_HEADS, HEAD_COLS), lambda i: (i, 0, 0, 0)),
                          pl.BlockSpec((nb, DEPTH, seq * N_HEADS, V_DIM), lambda i: (i, 0, 0, 0))]
        else:
            out_specs += [pl.BlockSpec((nb, 1, seq * N_HEADS, HEAD_COLS), lambda i: (i, layer, 0, 0)),
                          pl.BlockSpec((nb, 1, seq * N_HEADS, V_DIM), lambda i: (i, layer, 0, 0))]
            aliases = {len(operands): 1, len(operands) + 1: 2}
            operands += list(new_cache)
            in_specs += [pl.BlockSpec(memory_space=pl.ANY)] * 2

    scratch = [
        pltpu.VMEM((nb, seq + 2 * POOL_HALO, POOL_WIDTH), F32),
        pltpu.VMEM((ROWS // ROW_CHUNK, 2 * ROW_CHUNK, QK_WIDTH), BF16),
        pltpu.VMEM((nb, QK_WIDTH, kv_len), BF16),
        pltpu.VMEM((nb, kv_len, ATTN_WIDTH), BF16),
        pltpu.VMEM((ROWS, SGU_WIDTH), F32),
        pltpu.VMEM((ROWS, SGU_WIDTH), BF16),
        pltpu.VMEM((ROWS, MIX_WIDTH), BF16),
    ]
    kernel = functools.partial(_mixer_kernel, nb=nb, seq=seq, has_ctx=has_ctx, lam_init=lam_init,
                               cache_slot=cache_slot, n_passthrough=len(aliases))
    return pl.pallas_call(
        kernel,
        out_shape=out_shape,
        grid=(n_rows // ROWS,),
        in_specs=in_specs,
        out_specs=out_specs,
        scratch_shapes=scratch,
        input_output_aliases=aliases,
        compiler_params=pltpu.CompilerParams(
            dimension_semantics=("arbitrary",), vmem_limit_bytes=VMEM_LIMIT),
        name="mixer_ctx" if has_ctx else "mixer_prompt",
    )(*operands)


def _ffn_block(x_ref, out_ref, mod_ref, g2_ref, w1_ref, w2_ref, gf_ref, act_scr, final_norm):
    shift2, scale2, gate2 = mod_ref[3:4, :], mod_ref[4:5, :], mod_ref[5:6, :]
    x = x_ref[...]
    h = _rms_mod(x, g2_ref[...], scale2, shift2).astype(BF16)
    for lo, hi in ((0, FF_SPLIT), (FF_SPLIT, D_FF)):
        gate = _dot(h, w1_ref[:, lo:hi])
        up = _dot(h, w1_ref[:, D_FF + lo:D_FF + hi])
        act_scr[:, lo:hi] = (gate * _sigmoid(gate) * up).astype(BF16)
    y = x + gate2 * _dot(act_scr[...], w2_ref[...])
    if final_norm:
        ms = jnp.mean(y * y, axis=-1, keepdims=True)
        y = y * lax.rsqrt(ms + EPS) * gf_ref[...]
    out_ref[...] = y


def _ffn_kernel(*refs, n_prompt_steps, final_norm, n_cast):
    xp_ref, xs_ref, mod_ref, g2_ref, w1_ref, w2_ref, gf_ref = refs[:7]
    cast_in = refs[7:7 + n_cast]
    yp_ref, ys_ref = refs[7 + n_cast:9 + n_cast]
    cast_out = refs[9 + n_cast:9 + 2 * n_cast]
    act_scr = refs[9 + 2 * n_cast]
    shared = (mod_ref, g2_ref, w1_ref, w2_ref, gf_ref, act_scr, final_norm)
    is_prompt = pl.program_id(0) < n_prompt_steps

    @pl.when(is_prompt)
    def _():
        _ffn_block(xp_ref, yp_ref, *shared)

    @pl.when(jnp.logical_not(is_prompt))
    def _():
        for src, dst in zip(cast_in, cast_out):
            dst[...] = src[...].astype(BF16)
        _ffn_block(xs_ref, ys_ref, *shared)


def _ffn_call(xp, xs, mod4, layer, sample_seq, params, big, final_norm, next_f32=()):
    n_p, n_s = xp.shape[0] // FFN_ROWS, xs.shape[0] // FFN_ROWS
    steps_per_seq = sample_seq // FFN_ROWS

    def prompt_step(i):
        return jnp.minimum(i, n_p - 1)

    def sample_step(i):
        return jnp.maximum(i - n_p, 0)

    def mod_map(i):
        return (layer, jnp.where(i < n_p, 0, 1 + sample_step(i) // steps_per_seq), 0, 0)

    in_specs = [
        pl.BlockSpec((FFN_ROWS, D_MODEL), lambda i: (prompt_step(i), 0)),
        pl.BlockSpec((FFN_ROWS, D_MODEL), lambda i: (sample_step(i), 0)),
        pl.BlockSpec((None, None, N_MOD, D_MODEL), mod_map),
        _layer_spec((1, D_MODEL), layer),
        _const_spec((D_MODEL, 2 * D_FF)),
        _const_spec((D_FF, D_MODEL)),
        _const_spec((1, D_MODEL)),
    ]
    out_shape = [jax.ShapeDtypeStruct(xp.shape, F32), jax.ShapeDtypeStruct(xs.shape, F32)]
    out_specs = [pl.BlockSpec((FFN_ROWS, D_MODEL), lambda i: (prompt_step(i), 0)),
                 pl.BlockSpec((FFN_ROWS, D_MODEL), lambda i: (sample_step(i), 0))]
    for w in next_f32:
        _, rows, cols = w.shape
        slab = rows // n_s
        assert slab * n_s == rows and slab % 16 == 0
        in_specs.append(pl.BlockSpec((None, slab, cols), lambda i: (layer + 1, sample_step(i), 0)))
        out_shape.append(jax.ShapeDtypeStruct((rows, cols), BF16))
        out_specs.append(pl.BlockSpec((slab, cols), lambda i: (sample_step(i), 0)))

    kernel = functools.partial(_ffn_kernel, n_prompt_steps=n_p, final_norm=final_norm, n_cast=len(next_f32))
    return pl.pallas_call(
        kernel,
        out_shape=out_shape,
        grid=(n_p + n_s,),
        in_specs=in_specs,
        out_specs=out_specs,
        scratch_shapes=[pltpu.VMEM((FFN_ROWS, D_FF), BF16)],
        compiler_params=pltpu.CompilerParams(
            dimension_semantics=("arbitrary",), vmem_limit_bytes=VMEM_LIMIT),
        name="ffn",
    )(xp, xs, mod4, params["norm2_g"], big["w_ffn_in"], big["w_ffn_out"], params["final_g"], *next_f32)


def _rope_tables(seq):
    n_rows = seq // GRID_W
    rows = np.repeat(np.arange(n_rows), GRID_W).astype(np.float32)
    cols = np.tile(np.arange(GRID_W), n_rows).astype(np.float32)
    inv = 1.0 / (ROPE_BASE ** (np.arange(0, ROPE_AXIS_DIM, 2, dtype=np.float32) / ROPE_AXIS_DIM))
    ar, ac = rows[:, None] * inv[None], cols[:, None] * inv[None]
    cos_parts, sin_parts = [], []
    for ang in (ar, ac):
        cos_parts += [np.cos(ang), np.cos(ang)]
        sin_parts += [-np.sin(ang), np.sin(ang)]
    cos64 = np.concatenate(cos_parts, axis=1)
    sin64 = np.concatenate(sin_parts, axis=1)
    cos = np.concatenate([cos64, cos64], axis=1).astype(np.float32)
    sin = np.concatenate([sin64, sin64], axis=1).astype(np.float32)
    return jnp.asarray(cos), jnp.asarray(sin)


def _pool_counts(seq):
    t = np.arange(seq)
    cols = []
    for w in POOL_WINDOWS:
        lo = np.clip(t - w // 2, 0, seq)
        hi = np.clip(t + w - w // 2, 0, seq)
        cols.append(np.repeat((hi - lo).astype(np.float32)[:, None], POOL_GROUP, axis=1))
    return jnp.asarray(np.concatenate(cols, axis=1))


def _small_params(norm1_g, w_pool, pool_scale, lam_q1, lam_k1, lam_q2, lam_k2, subln_g,
                  sgu_norm_g, w_sgu, b_sgu, norm2_g, final_g):
    n_groups = len(POOL_WINDOWS)
    eye = jnp.eye(n_groups, dtype=F32)
    w_pool_bd = (eye[None, :, None, :, None] * w_pool[:, :, :, None, :]).reshape(DEPTH, POOL_WIDTH, POOL_WIDTH)
    lam = jnp.zeros((DEPTH, 8, 128), F32)
    lam = lam.at[:, 0, :QK_DIM].set(lam_q1).at[:, 1, :QK_DIM].set(lam_k1)
    lam = lam.at[:, 2, :QK_DIM].set(lam_q2).at[:, 3, :QK_DIM].set(lam_k2)
    return {
        "norm1_g": norm1_g[:, None, :],
        "lam": lam,
        "subln_g": subln_g[:, None, :],
        "w_pool": w_pool_bd.astype(BF16),
        "pool_scale": pool_scale[:, None, :],
        "sgu_norm_g": sgu_norm_g[:, None, :],
        "w_sgu": w_sgu.reshape(DEPTH, 2, 2 * CHUNK, CHUNK).astype(BF16),
        "b_sgu": jnp.repeat(jnp.swapaxes(b_sgu, 1, 2), SGU_WIDTH // 4, axis=2),
        "norm2_g": norm2_g[:, None, :],
        "final_g": final_g[None, :],
    }


BIG_WEIGHTS = ("w_in", "w_out", "w_ffn_in", "w_ffn_out")


def kernel(x_prompt, x_sample, cache_k, cache_v, c, c_ctx, norm1_g, w_ada, b_ada, w_in, w_pool, pool_scale, lam_q1, lam_k1, lam_q2, lam_k2, subln_g, sgu_norm_g, w_sgu, b_sgu, w_out, norm2_g, w_ffn_in, w_ffn_out, final_g):
    batch, seq, _ = x_prompt.shape
    dec_batch, dec_seq, _ = x_sample.shape
    assert ROWS % seq == 0 and dec_seq == ROWS and 1 + dec_batch <= COND_ROWS

    cond = jnp.zeros((COND_ROWS, D_MODEL), F32).at[0].set(c_ctx).at[1:1 + dec_batch].set(c)
    mod4 = _ada_modulation(cond, w_ada, b_ada).reshape(DEPTH, COND_ROWS, N_MOD, D_MODEL)

    ctx = _rope_tables(dec_seq) + (cache_k.reshape(dec_batch, DEPTH, PAST_LEN * N_HEADS, HEAD_COLS),
                                   cache_v.reshape(dec_batch, DEPTH, PAST_LEN * N_HEADS, V_DIM))
    params = _small_params(norm1_g, w_pool, pool_scale, lam_q1, lam_k1, lam_q2, lam_k2, subln_g,
                           sgu_norm_g, w_sgu, b_sgu, norm2_g, final_g)

    big_f32 = (w_in, w_out, w_ffn_in, w_ffn_out)
    big = dict(zip(BIG_WEIGHTS, (w[0].astype(BF16) for w in big_f32)))

    xp = x_prompt.reshape(batch * seq, D_MODEL)
    xs = x_sample.reshape(dec_batch * dec_seq, D_MODEL)
    new_cache = None
    for l in range(DEPTH):
        lam_init = 0.8 - 0.6 * math.exp(-0.3 * l)
        last = l == DEPTH - 1

        xp, *new_cache = _mixer_call(xp, mod4, l, seq, False, lam_init, params, big, new_cache=new_cache)
        xs, = _mixer_call(xs, mod4, l, dec_seq, True, lam_init, params, big, ctx=ctx)
        xp, xs, *next_big = _ffn_call(xp, xs, mod4, l, dec_seq, params, big, last,
                                      next_f32=() if last else big_f32)
        big = dict(zip(BIG_WEIGHTS, next_big))

    y_prompt = xp.reshape(batch, seq, D_MODEL)
    y_sample = xs.reshape(dec_batch, dec_seq, D_MODEL)
    new_cache_k = new_cache[0].reshape(batch, DEPTH, seq, N_HEADS, 2 * QK_DIM)
    new_cache_v = new_cache[1].reshape(batch, DEPTH, seq, N_HEADS, V_DIM)
    return (y_prompt, y_sample, new_cache_k, new_cache_v)
```

```python
import functools
import math
import types

import numpy as np
import jax
import jax.numpy as jnp
from jax import lax
from jax.experimental import pallas as pl
from jax.experimental.pallas import tpu as pltpu

D_MODEL = 1024
DEPTH = 2
GRID_W = 64
POOL_WINDOWS = (2, 4, 8, 16)
POOL_WIDTH = 256
POOL_GROUP = 64
POOL_HALO = 8
ATTN_WIDTH = 512
N_HEADS = 4
V_DIM = 128
QK_DIM = 64
HEAD_COLS = 2 * QK_DIM
ROPE_BASE = 10000.0
ROPE_AXIS_DIM = 32
ROPE_HALF = ROPE_AXIS_DIM // 2
CHUNK = 128
SGU_WIDTH = 256
QK_WIDTH = 512
IN_WIDTH = 2304
MIX_WIDTH = 1024
D_FF = 2816
N_MOD = 6
EPS = 1e-6
PAST_LEN = 256

C_POOL, C_Q, C_K, C_V, C_UV = 0, 256, 768, 1280, 1792
Y_POOL, Y_ATTN, Y_SGU = 0, 256, 768

ROWS = 1024
ROW_CHUNK = 256
ATTN_LOOKAHEAD = 2
SMALL_SEQ_CHUNKS_PER_BLOCK = 2
FFN_ROWS = 512
MXU_COLS = 256
FF_SPLIT = (D_FF // MXU_COLS + 1) // 2 * MXU_COLS
COND_ROWS = 16
ADA_COLS = 1024
VMEM_LIMIT = 58 * 1024 * 1024

BF16 = jnp.bfloat16
F32 = jnp.float32


def _dot(a, b):
    return jnp.dot(a, b, preferred_element_type=F32)


def _sigmoid(x):
    return 1.0 / (1.0 + jnp.exp(-x))


def _rms_mod(x, g, scale, shift):
    ms = jnp.mean(x * x, axis=-1, keepdims=True)
    y = x * lax.rsqrt(ms + EPS) * g
    return y * (1.0 + scale) + shift


def _gelu_tanh(x):
    c = math.sqrt(2.0 / math.pi)
    return x * (0.5 * (1.0 + jnp.tanh(c * (x + 0.044715 * (x * x * x)))))


def _const_spec(shape):
    zeros = (0,) * len(shape)
    return pl.BlockSpec(shape, lambda *_: zeros, pipeline_mode=pl.Buffered(1))


def _layer_spec(shape, layer):
    index = (layer,) + (0,) * len(shape)
    return pl.BlockSpec((None,) + tuple(shape), lambda *_: index, pipeline_mode=pl.Buffered(1))


def _ada_kernel(cond_ref, w_ref, b_ref, out_ref):
    cond = cond_ref[...]
    s = (cond * _sigmoid(cond)).astype(BF16)
    out_ref[...] = _dot(s, w_ref[...].astype(BF16)) + b_ref[...]


def _ada_modulation(cond, w_ada, b_ada):
    n_cols = N_MOD * D_MODEL
    return pl.pallas_call(
        _ada_kernel,
        out_shape=jax.ShapeDtypeStruct((DEPTH, COND_ROWS, n_cols), F32),
        grid=(DEPTH, n_cols // ADA_COLS),
        in_specs=[
            pl.BlockSpec((COND_ROWS, D_MODEL), lambda l, j: (0, 0)),
            pl.BlockSpec((None, D_MODEL, ADA_COLS), lambda l, j: (l, 0, j)),
            pl.BlockSpec((None, 1, ADA_COLS), lambda l, j: (l, 0, j)),
        ],
        out_specs=pl.BlockSpec((None, COND_ROWS, ADA_COLS), lambda l, j: (l, 0, j)),
        compiler_params=pltpu.CompilerParams(
            dimension_semantics=("arbitrary", "arbitrary"), vmem_limit_bytes=VMEM_LIMIT),
        name="ada_modulation",
    )(cond, w_ada, b_ada.reshape(DEPTH, 1, n_cols))


def _mixer_body(r, x_ref, cnt_ref, pool_scr, k_scr, v_scr, *, nb, seq, has_ctx, lam_init, cpb):
    kv_base = PAST_LEN if has_ctx else 0
    shift1, scale1, gate1 = r.mod[0:1, :], r.mod[1:2, :], r.mod[2:3, :]

    lane_head = lax.broadcasted_iota(jnp.int32, (ROW_CHUNK, HEAD_COLS), 1)
    first_half = lane_head < QK_DIM

    for b in range(nb):
        pool_scr[b, 0:POOL_HALO, :] = jnp.zeros((POOL_HALO, POOL_WIDTH), F32)
        pool_scr[b, POOL_HALO + seq:POOL_HALO + seq + POOL_HALO, :] = jnp.zeros((POOL_HALO, POOL_WIDTH), F32)
    if not has_ctx:
        for slot in range(r.kc_out.shape[1]):
            if slot != r.cache_slot:
                r.kc_out[:, slot] = jnp.zeros((nb,) + r.kc_out.shape[2:], F32)
                r.vc_out[:, slot] = jnp.zeros((nb,) + r.vc_out.shape[2:], F32)
    if has_ctx:
        for hd in range(N_HEADS):
            cols = slice(hd * HEAD_COLS, (hd + 1) * HEAD_COLS)
            k_scr[0, cols, 0:PAST_LEN] = r.ck[pl.ds(hd, PAST_LEN, stride=N_HEADS), :].T.astype(BF16)
            v_scr[0, 0:PAST_LEN, cols] = r.cv[pl.ds(hd, PAST_LEN, stride=N_HEADS), :].astype(BF16)
        lane_q = lax.broadcasted_iota(jnp.int32, (ROW_CHUNK, QK_WIDTH), 1)
        rope_low = (lane_q % ROPE_AXIS_DIM) < ROPE_HALF

        def rope(t, pos0):
            partner = jnp.where(rope_low,
                                pltpu.roll(t, QK_WIDTH - ROPE_HALF, 1),
                                pltpu.roll(t, ROPE_HALF, 1))
            cos = r.rope_c[pos0:pos0 + ROW_CHUNK, :]
            sin = r.rope_s[pos0:pos0 + ROW_CHUNK, :]
            cos = jnp.concatenate([cos] * N_HEADS, axis=1)
            sin = jnp.concatenate([sin] * N_HEADS, axis=1)
            return t * cos + partner * sin

    for c in range(ROWS // ROW_CHUNK):
        r0 = c * ROW_CHUNK
        b, pos0 = r0 // seq, r0 % seq
        rows = slice(r0, r0 + ROW_CHUNK)
        h = _rms_mod(x_ref[rows, :], r.g1[...], scale1, shift1).astype(BF16)

        pool_scr[b, POOL_HALO + pos0:POOL_HALO + pos0 + ROW_CHUNK, :] = _dot(h, r.w_in[:, C_POOL:C_Q])

        q = _dot(h, r.w_in[:, C_Q:C_K])
        k = _dot(h, r.w_in[:, C_K:C_V])
        if has_ctx:
            q = rope(q, pos0)
            k = rope(k, pos0)
        else:
            for hd in range(N_HEADS):
                r.kc_out[b, r.cache_slot, pl.ds(N_HEADS * pos0 + hd, ROW_CHUNK, stride=N_HEADS), :] = (
                    k[:, hd * HEAD_COLS:(hd + 1) * HEAD_COLS])
        q = q * (QK_DIM ** -0.5 * math.log2(math.e))
        for hd in range(N_HEADS):
            cols = slice(hd * HEAD_COLS, (hd + 1) * HEAD_COLS)
            qh = q[:, cols]
            r.q_scr[c, 0:ROW_CHUNK, cols] = jnp.where(first_half, qh, 0.0).astype(BF16)
            r.q_scr[c, ROW_CHUNK:2 * ROW_CHUNK, cols] = jnp.where(first_half, 0.0, qh).astype(BF16)
        k_scr[b, :, kv_base + pos0:kv_base + pos0 + ROW_CHUNK] = k.T.astype(BF16)

        v = _dot(h, r.w_in[:, C_V:C_UV])
        if not has_ctx:
            for hd in range(N_HEADS):
                r.vc_out[b, r.cache_slot, pl.ds(N_HEADS * pos0 + hd, ROW_CHUNK, stride=N_HEADS), :] = (
                    v[:, hd * V_DIM:(hd + 1) * V_DIM])
        v_scr[b, kv_base + pos0:kv_base + pos0 + ROW_CHUNK, :] = v.astype(BF16)

        uv = _gelu_tanh(_dot(h, r.w_in[:, C_UV:IN_WIDTH]))
        r.u_scr[rows, :] = uv[:, :SGU_WIDTH]
        vv = uv[:, SGU_WIDTH:]
        mu = jnp.mean(vv, axis=-1, keepdims=True)
        vc = vv - mu
        vn = vc * lax.rsqrt(jnp.mean(vc * vc, axis=-1, keepdims=True) + EPS) * r.sgu_g[...]
        r.vs_scr[rows, :] = vn.astype(BF16)

    lane_pool = lax.broadcasted_iota(jnp.int32, (ROW_CHUNK, 2 * POOL_GROUP), 1)
    narrow = lane_pool < POOL_GROUP
    for c in range(ROWS // ROW_CHUNK):
        r0 = c * ROW_CHUNK
        b, pos0 = r0 // seq, r0 % seq
        base = POOL_HALO + pos0
        pooled = []
        for j, (w_small, w_big) in enumerate(((POOL_WINDOWS[0], POOL_WINDOWS[1]),
                                              (POOL_WINDOWS[2], POOL_WINDOWS[3]))):
            cols = slice(j * 2 * POOL_GROUP, (j + 1) * 2 * POOL_GROUP)
            s_small = jnp.zeros((ROW_CHUNK, 2 * POOL_GROUP), F32)
            s_rest = jnp.zeros((ROW_CHUNK, 2 * POOL_GROUP), F32)
            for d in range(-(w_big // 2), w_big - w_big // 2):
                t = pool_scr[b, base + d:base + d + ROW_CHUNK, cols]
                if -(w_small // 2) <= d < w_small - w_small // 2:
                    s_small = s_small + t
                else:
                    s_rest = s_rest + t
            win_sum = jnp.where(narrow, s_small, s_small + s_rest)
            centre = pool_scr[b, base:base + ROW_CHUNK, cols]
            pooled.append(win_sum / cnt_ref[pos0:pos0 + ROW_CHUNK, cols] - centre)
        pooled = jnp.concatenate(pooled, axis=1).astype(BF16)
        y_a = _dot(pooled, r.wpool[...]) * r.pscale[...]
        r.y_scr[r0:r0 + ROW_CHUNK, Y_POOL:Y_ATTN] = y_a.astype(BF16)

    lq1, lk1, lq2, lk2 = r.lam[0:1, :], r.lam[1:2, :], r.lam[2:3, :], r.lam[3:4, :]
    lam = (jnp.exp(jnp.sum(lq1 * lk1, axis=-1, keepdims=True))
           - jnp.exp(jnp.sum(lq2 * lk2, axis=-1, keepdims=True)) + lam_init)
    n_qb = seq // ROW_CHUNK

    lane_sgu = lax.broadcasted_iota(jnp.int32, (CHUNK, 2 * POOL_GROUP), 1)
    sgu_first = lane_sgu < (SGU_WIDTH // 4)
    out_cols = MIX_WIDTH // N_HEADS

    def out_proj(rows, part):
        cols = slice(part * out_cols, (part + 1) * out_cols)
        y = _dot(r.y_scr[rows, :], r.w_out[:, cols])
        r.out[rows, cols] = x_ref[rows, cols] + gate1[:, cols] * y

    def block(blk, with_prev):
        static = isinstance(blk, int)

        def row0(chunk):
            return chunk * ROW_CHUNK if static else pl.multiple_of(chunk * ROW_CHUNK, ROW_CHUNK)

        chunks = [blk * cpb + j for j in range(cpb)]

        for idx in chunks:
            for half in range(ROW_CHUNK // CHUNK):
                sub = pl.ds(row0(idx) + half * CHUNK, CHUNK)
                for j in range(2):
                    cols = slice(j * 128, (j + 1) * 128)
                    t = _dot(r.wsgu[j], r.vs_scr[sub, cols])
                    mixed = jnp.where(sgu_first, t[:CHUNK, :], t[CHUNK:, :]) + r.bsgu[:, cols]
                    y_c = r.u_scr[sub, cols] * mixed
                    r.y_scr[sub, Y_SGU + j * 128:Y_SGU + (j + 1) * 128] = y_c.astype(BF16)

        units = [(idx, hd) for idx in chunks for hd in range(N_HEADS)]
        prev = [(pl.ds(row0(idx - cpb), ROW_CHUNK), part) for idx in chunks for part in range(N_HEADS)]

        def scores(idx, hd):
            cols = slice(hd * HEAD_COLS, (hd + 1) * HEAD_COLS)
            s = _dot(r.q_scr[idx, :, cols], k_scr[idx // n_qb, cols, :])
            return s[:ROW_CHUNK], s[ROW_CHUNK:]

        pending = [scores(*u) for u in units[:ATTN_LOOKAHEAD]]
        for n, (idx, hd) in enumerate(units):
            cols = slice(hd * HEAD_COLS, (hd + 1) * HEAD_COLS)
            s1, s2 = pending.pop(0)
            if n + ATTN_LOOKAHEAD < len(units):
                pending.append(scores(*units[n + ATTN_LOOKAHEAD]))
            e1 = jnp.exp2(s1 - jnp.max(s1, axis=-1, keepdims=True))
            e2 = jnp.exp2(s2 - jnp.max(s2, axis=-1, keepdims=True))
            l1 = jnp.sum(e1, axis=-1, keepdims=True)
            l2 = jnp.sum(e2, axis=-1, keepdims=True)
            p = (e1 - e2 * (lam * l1 / l2)).astype(BF16)
            o = _dot(p, v_scr[idx // n_qb, :, cols]) * (1.0 / l1)
            if with_prev:
                out_proj(*prev[n])
            o = o * lax.rsqrt(jnp.mean(o * o, axis=-1, keepdims=True) + EPS) * r.subln[...]
            o = o * (1.0 - lam_init)
            r.y_scr[pl.ds(row0(idx), ROW_CHUNK), Y_ATTN + hd * V_DIM:Y_ATTN + (hd + 1) * V_DIM] = o.astype(BF16)

    n_blocks = ROWS // (ROW_CHUNK * cpb)
    block(0, False)
    if n_blocks == 2:
        block(1, True)
    else:
        def loop_block(blk, carry):
            block(blk, True)
            return carry

        lax.fori_loop(1, n_blocks, loop_block, 0)
    for idx in range((n_blocks - 1) * cpb, n_blocks * cpb):
        for part in range(N_HEADS):
            out_proj(pl.ds(idx * ROW_CHUNK, ROW_CHUNK), part)


def _mixer_kernel(*refs, nb, seq, has_ctx, lam_init, cache_slot, n_passthrough):
    it = iter(refs)
    r = types.SimpleNamespace(cache_slot=cache_slot)
    x_ref, r.mod, r.g1, r.w_in = next(it), next(it), next(it), next(it)
    if has_ctx:
        r.rope_c, r.rope_s, r.ck, r.cv = next(it), next(it), next(it), next(it)
    (r.lam, r.subln, cnt_ref, r.wpool, r.pscale, r.sgu_g, r.wsgu, r.bsgu, r.w_out) = (next(it) for _ in range(9))
    for _ in range(n_passthrough):
        next(it)
    r.out = next(it)
    if not has_ctx:
        r.kc_out, r.vc_out = next(it), next(it)
    pool_scr, r.q_scr, k_scr, v_scr, r.u_scr, r.vs_scr, r.y_scr = (next(it) for _ in range(7))
    _mixer_body(r, x_ref, cnt_ref, pool_scr, k_scr, v_scr, nb=nb, seq=seq, has_ctx=has_ctx, lam_init=lam_init,
                cpb=SMALL_SEQ_CHUNKS_PER_BLOCK if seq <= ROW_CHUNK else 1)


def _mixer_call(x2d, mod4, layer, seq, has_ctx, lam_init, params, big, ctx=None, new_cache=None):
    n_rows = x2d.shape[0]
    nb = ROWS // seq
    n_seq = n_rows // seq
    kv_len = seq + (PAST_LEN if has_ctx else 0)

    if has_ctx:
        mod_map = lambda i: (layer, 1 + i, 0, 0)
    else:
        mod_map = lambda i: (layer, 0, 0, 0)

    operands = [x2d, mod4, params["norm1_g"], big["w_in"]]
    in_specs = [
        pl.BlockSpec((ROWS, D_MODEL), lambda i: (i, 0)),
        pl.BlockSpec((None, None, N_MOD, D_MODEL), mod_map),
        _layer_spec((1, D_MODEL), layer),
        _const_spec((D_MODEL, IN_WIDTH)),
    ]
    if has_ctx:
        operands += list(ctx)
        in_specs += [
            _const_spec((seq, HEAD_COLS)),
            _const_spec((seq, HEAD_COLS)),
            pl.BlockSpec((None, None, PAST_LEN * N_HEADS, HEAD_COLS), lambda i: (i, layer, 0, 0)),
            pl.BlockSpec((None, None, PAST_LEN * N_HEADS, V_DIM), lambda i: (i, layer, 0, 0)),
        ]
    operands += [params["lam"], params["subln_g"], _pool_counts(seq), params["w_pool"], params["pool_scale"],
                 params["sgu_norm_g"], params["w_sgu"], params["b_sgu"], big["w_out"]]
    in_specs += [
        _layer_spec((8, 128), layer),
        _layer_spec((1, V_DIM), layer),
        _const_spec((seq, POOL_WIDTH)),
        _layer_spec((POOL_WIDTH, POOL_WIDTH), layer),
        _layer_spec((1, POOL_WIDTH), layer),
        _layer_spec((1, SGU_WIDTH), layer),
        _layer_spec((2, 2 * CHUNK, CHUNK), layer),
        _layer_spec((CHUNK, SGU_WIDTH), layer),
        _const_spec((MIX_WIDTH, D_MODEL)),
    ]

    out_shape = [jax.ShapeDtypeStruct((n_rows, D_MODEL), F32)]
    out_specs = [pl.BlockSpec((ROWS, D_MODEL), lambda i: (i, 0))]
    aliases = {}
    cache_slot = 0
    if not has_ctx:
        out_shape += [jax.ShapeDtypeStruct((n_seq, DEPTH, seq * N_HEADS, HEAD_COLS), F32),
                      jax.ShapeDtypeStruct((n_seq, DEPTH, seq * N_HEADS, V_DIM), F32)]
        if new_cache is None:
            cache_slot = layer
            out_specs += [pl.BlockSpec((nb, DEPTH, seq * N_HEADS, HEAD_COLS), lambda i: (i, 0, 0, 0)),
                          pl.BlockSpec((nb, DEPTH, seq * N_HEADS, V_DIM), lambda i: (i, 0, 0, 0))]
        else:
            out_specs += [pl.BlockSpec((nb, 1, seq * N_HEADS, HEAD_COLS), lambda i: (i, layer, 0, 0)),
                          pl.BlockSpec((nb, 1, seq * N_HEADS, V_DIM), lambda i: (i, layer, 0, 0))]
            aliases = {len(operands): 1, len(operands) + 1: 2}
            operands += list(new_cache)
            in_specs += [pl.BlockSpec(memory_space=pl.ANY)] * 2

    scratch = [
        pltpu.VMEM((nb, seq + 2 * POOL_HALO, POOL_WIDTH), F32),
        pltpu.VMEM((ROWS // ROW_CHUNK, 2 * ROW_CHUNK, QK_WIDTH), BF16),
        pltpu.VMEM((nb, QK_WIDTH, kv_len), BF16),
        pltpu.VMEM((nb, kv_len, ATTN_WIDTH), BF16),
        pltpu.VMEM((ROWS, SGU_WIDTH), F32),
        pltpu.VMEM((ROWS, SGU_WIDTH), BF16),
        pltpu.VMEM((ROWS, MIX_WIDTH), BF16),
    ]
    kernel = functools.partial(_mixer_kernel, nb=nb, seq=seq, has_ctx=has_ctx, lam_init=lam_init,
                               cache_slot=cache_slot, n_passthrough=len(aliases))
    return pl.pallas_call(
        kernel,
        out_shape=out_shape,
        grid=(n_rows // ROWS,),
        in_specs=in_specs,
        out_specs=out_specs,
        scratch_shapes=scratch,
        input_output_aliases=aliases,
        compiler_params=pltpu.CompilerParams(
            dimension_semantics=("arbitrary",), vmem_limit_bytes=VMEM_LIMIT),
        name="mixer_ctx" if has_ctx else "mixer_prompt",
    )(*operands)


def _ffn_block(x_ref, out_ref, mod_ref, g2_ref, w1_ref, w2_ref, gf_ref, act_scr, final_norm):
    shift2, scale2, gate2 = mod_ref[3:4, :], mod_ref[4:5, :], mod_ref[5:6, :]
    x = x_ref[...]
    h = _rms_mod(x, g2_ref[...], scale2, shift2).astype(BF16)
    for lo, hi in ((0, FF_SPLIT), (FF_SPLIT, D_FF)):
        gate = _dot(h, w1_ref[:, lo:hi])
        up = _dot(h, w1_ref[:, D_FF + lo:D_FF + hi])
        act_scr[:, lo:hi] = (gate * _sigmoid(gate) * up).astype(BF16)
    y = x + gate2 * _dot(act_scr[...], w2_ref[...])
    if final_norm:
        ms = jnp.mean(y * y, axis=-1, keepdims=True)
        y = y * lax.rsqrt(ms + EPS) * gf_ref[...]
    out_ref[...] = y


def _ffn_kernel(*refs, n_prompt_steps, final_norm, n_cast):
    xp_ref, xs_ref, mod_ref, g2_ref, w1_ref, w2_ref, gf_ref = refs[:7]
    cast_in = refs[7:7 + n_cast]
    yp_ref, ys_ref = refs[7 + n_cast:9 + n_cast]
    cast_out = refs[9 + n_cast:9 + 2 * n_cast]
    act_scr = refs[9 + 2 * n_cast]
    shared = (mod_ref, g2_ref, w1_ref, w2_ref, gf_ref, act_scr, final_norm)
    is_prompt = pl.program_id(0) < n_prompt_steps

    @pl.when(is_prompt)
    def _():
        _ffn_block(xp_ref, yp_ref, *shared)

    @pl.when(jnp.logical_not(is_prompt))
    def _():
        for src, dst in zip(cast_in, cast_out):
            dst[...] = src[...].astype(BF16)
        _ffn_block(xs_ref, ys_ref, *shared)


def _ffn_call(xp, xs, mod4, layer, sample_seq, params, big, final_norm, next_f32=()):
    n_p, n_s = xp.shape[0] // FFN_ROWS, xs.shape[0] // FFN_ROWS
    steps_per_seq = sample_seq // FFN_ROWS

    def prompt_step(i):
        return jnp.minimum(i, n_p - 1)

    def sample_step(i):
        return jnp.maximum(i - n_p, 0)

    def mod_map(i):
        return (layer, jnp.where(i < n_p, 0, 1 + sample_step(i) // steps_per_seq), 0, 0)

    in_specs = [
        pl.BlockSpec((FFN_ROWS, D_MODEL), lambda i: (prompt_step(i), 0)),
        pl.BlockSpec((FFN_ROWS, D_MODEL), lambda i: (sample_step(i), 0)),
        pl.BlockSpec((None, None, N_MOD, D_MODEL), mod_map),
        _layer_spec((1, D_MODEL), layer),
        _const_spec((D_MODEL, 2 * D_FF)),
        _const_spec((D_FF, D_MODEL)),
        _const_spec((1, D_MODEL)),
    ]
    out_shape = [jax.ShapeDtypeStruct(xp.shape, F32), jax.ShapeDtypeStruct(xs.shape, F32)]
    out_specs = [pl.BlockSpec((FFN_ROWS, D_MODEL), lambda i: (prompt_step(i), 0)),
                 pl.BlockSpec((FFN_ROWS, D_MODEL), lambda i: (sample_step(i), 0))]
    for w in next_f32:
        _, rows, cols = w.shape
        slab = rows // n_s
        assert slab * n_s == rows and slab % 16 == 0
        in_specs.append(pl.BlockSpec((None, slab, cols), lambda i: (layer + 1, sample_step(i), 0)))
        out_shape.append(jax.ShapeDtypeStruct((rows, cols), BF16))
        out_specs.append(pl.BlockSpec((slab, cols), lambda i: (sample_step(i), 0)))

    kernel = functools.partial(_ffn_kernel, n_prompt_steps=n_p, final_norm=final_norm, n_cast=len(next_f32))
    return pl.pallas_call(
        kernel,
        out_shape=out_shape,
        grid=(n_p + n_s,),
        in_specs=in_specs,
        out_specs=out_specs,
        scratch_shapes=[pltpu.VMEM((FFN_ROWS, D_FF), BF16)],
        compiler_params=pltpu.CompilerParams(
            dimension_semantics=("arbitrary",), vmem_limit_bytes=VMEM_LIMIT),
        name="ffn",
    )(xp, xs, mod4, params["norm2_g"], big["w_ffn_in"], big["w_ffn_out"], params["final_g"], *next_f32)


def _rope_tables(seq):
    n_rows = seq // GRID_W
    rows = np.repeat(np.arange(n_rows), GRID_W).astype(np.float32)
    cols = np.tile(np.arange(GRID_W), n_rows).astype(np.float32)
    inv = 1.0 / (ROPE_BASE ** (np.arange(0, ROPE_AXIS_DIM, 2, dtype=np.float32) / ROPE_AXIS_DIM))
    ar, ac = rows[:, None] * inv[None], cols[:, None] * inv[None]
    cos_parts, sin_parts = [], []
    for ang in (ar, ac):
        cos_parts += [np.cos(ang), np.cos(ang)]
        sin_parts += [-np.sin(ang), np.sin(ang)]
    cos64 = np.concatenate(cos_parts, axis=1)
    sin64 = np.concatenate(sin_parts, axis=1)
    cos = np.concatenate([cos64, cos64], axis=1).astype(np.float32)
    sin = np.concatenate([sin64, sin64], axis=1).astype(np.float32)
    return jnp.asarray(cos), jnp.asarray(sin)


def _pool_counts(seq):
    t = np.arange(seq)
    cols = []
    for w in POOL_WINDOWS:
        lo = np.clip(t - w // 2, 0, seq)
        hi = np.clip(t + w - w // 2, 0, seq)
        cols.append(np.repeat((hi - lo).astype(np.float32)[:, None], POOL_GROUP, axis=1))
    return jnp.asarray(np.concatenate(cols, axis=1))


def _small_params(norm1_g, w_pool, pool_scale, lam_q1, lam_k1, lam_q2, lam_k2, subln_g,
                  sgu_norm_g, w_sgu, b_sgu, norm2_g, final_g):
    n_groups = len(POOL_WINDOWS)
    eye = jnp.eye(n_groups, dtype=F32)
    w_pool_bd = (eye[None, :, None, :, None] * w_pool[:, :, :, None, :]).reshape(DEPTH, POOL_WIDTH, POOL_WIDTH)
    lam = jnp.zeros((DEPTH, 8, 128), F32)
    lam = lam.at[:, 0, :QK_DIM].set(lam_q1).at[:, 1, :QK_DIM].set(lam_k1)
    lam = lam.at[:, 2, :QK_DIM].set(lam_q2).at[:, 3, :QK_DIM].set(lam_k2)
    return {
        "norm1_g": norm1_g[:, None, :],
        "lam": lam,
        "subln_g": subln_g[:, None, :],
        "w_pool": w_pool_bd.astype(BF16),
        "pool_scale": pool_scale[:, None, :],
        "sgu_norm_g": sgu_norm_g[:, None, :],
        "w_sgu": w_sgu.reshape(DEPTH, 2, 2 * CHUNK, CHUNK).astype(BF16),
        "b_sgu": jnp.repeat(jnp.swapaxes(b_sgu, 1, 2), SGU_WIDTH // 4, axis=2),
        "norm2_g": norm2_g[:, None, :],
        "final_g": final_g[None, :],
    }


BIG_WEIGHTS = ("w_in", "w_out", "w_ffn_in", "w_ffn_out")


def kernel(x_prompt, x_sample, cache_k, cache_v, c, c_ctx, norm1_g, w_ada, b_ada, w_in, w_pool, pool_scale, lam_q1, lam_k1, lam_q2, lam_k2, subln_g, sgu_norm_g, w_sgu, b_sgu, w_out, norm2_g, w_ffn_in, w_ffn_out, final_g):
    batch, seq, _ = x_prompt.shape
    dec_batch, dec_seq, _ = x_sample.shape
    assert ROWS % seq == 0 and dec_seq == ROWS and 1 + dec_batch <= COND_ROWS

    cond = jnp.zeros((COND_ROWS, D_MODEL), F32).at[0].set(c_ctx).at[1:1 + dec_batch].set(c)
    mod4 = _ada_modulation(cond, w_ada, b_ada).reshape(DEPTH, COND_ROWS, N_MOD, D_MODEL)

    ctx = _rope_tables(dec_seq) + (cache_k.reshape(dec_batch, DEPTH, PAST_LEN * N_HEADS, HEAD_COLS),
                                   cache_v.reshape(dec_batch, DEPTH, PAST_LEN * N_HEADS, V_DIM))
    params = _small_params(norm1_g, w_pool, pool_scale, lam_q1, lam_k1, lam_q2, lam_k2, subln_g,
                           sgu_norm_g, w_sgu, b_sgu, norm2_g, final_g)

    big_f32 = (w_in, w_out, w_ffn_in, w_ffn_out)
    big = dict(zip(BIG_WEIGHTS, (w[0].astype(BF16) for w in big_f32)))

    xp = x_prompt.reshape(batch * seq, D_MODEL)
    xs = x_sample.reshape(dec_batch * dec_seq, D_MODEL)
    new_cache = None
    for l in range(DEPTH):
        lam_init = 0.8 - 0.6 * math.exp(-0.3 * l)
        last = l == DEPTH - 1

        xp, *new_cache = _mixer_call(xp, mod4, l, seq, False, lam_init, params, big, new_cache=new_cache)
        xs, = _mixer_call(xs, mod4, l, dec_seq, True, lam_init, params, big, ctx=ctx)
        xp, xs, *next_big = _ffn_call(xp, xs, mod4, l, dec_seq, params, big, last,
                                      next_f32=() if last else big_f32)
        big = dict(zip(BIG_WEIGHTS, next_big))

    y_prompt = xp.reshape(batch, seq, D_MODEL)
    y_sample = xs.reshape(dec_batch, dec_seq, D_MODEL)
    new_cache_k = new_cache[0].reshape(batch, DEPTH, seq, N_HEADS, 2 * QK_DIM)
    new_cache_v = new_cache[1].reshape(batch, DEPTH, seq, N_HEADS, V_DIM)
    return (y_prompt, y_sample, new_cache_k, new_cache_v)
```

```python
import functools
import math
import types

import numpy as np
import jax
import jax.numpy as jnp
from jax import lax
from jax.experimental import pallas as pl
from jax.experimental.pallas import tpu as pltpu

D_MODEL = 1024
DEPTH = 2
GRID_W = 64
POOL_WINDOWS = (2, 4, 8, 16)
POOL_WIDTH = 256
POOL_GROUP = 64
POOL_HALO = 8
ATTN_WIDTH = 512
N_HEADS = 4
V_DIM = 128
QK_DIM = 64
HEAD_COLS = 2 * QK_DIM
ROPE_BASE = 10000.0
ROPE_AXIS_DIM = 32
ROPE_HALF = ROPE_AXIS_DIM // 2
CHUNK = 128
SGU_WIDTH = 256
QK_WIDTH = 512
IN_WIDTH = 2304
MIX_WIDTH = 1024
D_FF = 2816
N_MOD = 6
EPS = 1e-6
PAST_LEN = 256

C_POOL, C_Q, C_K, C_V, C_UV = 0, 256, 768, 1280, 1792
Y_POOL, Y_ATTN, Y_SGU = 0, 256, 768

ROWS = 1024
ROW_CHUNK = 256
ATTN_LOOKAHEAD = 2
CHUNKS_PER_BLOCK = 2
FFN_ROWS = 512
MXU_COLS = 256
FF_SPLIT = (D_FF // MXU_COLS + 1) // 2 * MXU_COLS
COND_ROWS = 16
ADA_COLS = 1024
VMEM_LIMIT = 58 * 1024 * 1024

BF16 = jnp.bfloat16
F32 = jnp.float32


def _dot(a, b):
    return jnp.dot(a, b, preferred_element_type=F32)


def _sigmoid(x):
    return 1.0 / (1.0 + jnp.exp(-x))


def _rms_mod(x, g, scale, shift):
    ms = jnp.mean(x * x, axis=-1, keepdims=True)
    y = x * lax.rsqrt(ms + EPS) * g
    return y * (1.0 + scale) + shift


def _gelu_tanh(x):
    c = math.sqrt(2.0 / math.pi)
    return x * (0.5 * (1.0 + jnp.tanh(c * (x + 0.044715 * (x * x * x)))))


def _const_spec(shape):
    zeros = (0,) * len(shape)
    return pl.BlockSpec(shape, lambda *_: zeros, pipeline_mode=pl.Buffered(1))


def _layer_spec(shape, layer):
    index = (layer,) + (0,) * len(shape)
    return pl.BlockSpec((None,) + tuple(shape), lambda *_: index, pipeline_mode=pl.Buffered(1))


def _ada_kernel(cond_ref, w_ref, b_ref, out_ref):
    cond = cond_ref[...]
    s = (cond * _sigmoid(cond)).astype(BF16)
    out_ref[...] = _dot(s, w_ref[...].astype(BF16)) + b_ref[...]


def _ada_modulation(cond, w_ada, b_ada):
    n_cols = N_MOD * D_MODEL
    return pl.pallas_call(
        _ada_kernel,
        out_shape=jax.ShapeDtypeStruct((DEPTH, COND_ROWS, n_cols), F32),
        grid=(DEPTH, n_cols // ADA_COLS),
        in_specs=[
            pl.BlockSpec((COND_ROWS, D_MODEL), lambda l, j: (0, 0)),
            pl.BlockSpec((None, D_MODEL, ADA_COLS), lambda l, j: (l, 0, j)),
            pl.BlockSpec((None, 1, ADA_COLS), lambda l, j: (l, 0, j)),
        ],
        out_specs=pl.BlockSpec((None, COND_ROWS, ADA_COLS), lambda l, j: (l, 0, j)),
        compiler_params=pltpu.CompilerParams(
            dimension_semantics=("arbitrary", "arbitrary"), vmem_limit_bytes=VMEM_LIMIT),
        name="ada_modulation",
    )(cond, w_ada, b_ada.reshape(DEPTH, 1, n_cols))


def _mixer_body(r, x_ref, cnt_ref, pool_scr, k_scr, v_scr, *, nb, seq, has_ctx, lam_init, cpb):
    kv_base = PAST_LEN if has_ctx else 0
    shift1, scale1, gate1 = r.mod[0:1, :], r.mod[1:2, :], r.mod[2:3, :]

    lane_head = lax.broadcasted_iota(jnp.int32, (ROW_CHUNK, HEAD_COLS), 1)
    first_half = lane_head < QK_DIM

    for b in range(nb):
        pool_scr[b, 0:POOL_HALO, :] = jnp.zeros((POOL_HALO, POOL_WIDTH), F32)
        pool_scr[b, POOL_HALO + seq:POOL_HALO + seq + POOL_HALO, :] = jnp.zeros((POOL_HALO, POOL_WIDTH), F32)
    if not has_ctx:
        for slot in range(r.kc_out.shape[1]):
            if slot != r.cache_slot:
                r.kc_out[:, slot] = jnp.zeros((nb,) + r.kc_out.shape[2:], F32)
                r.vc_out[:, slot] = jnp.zeros((nb,) + r.vc_out.shape[2:], F32)
    if has_ctx:
        for hd in range(N_HEADS):
            cols = slice(hd * HEAD_COLS, (hd + 1) * HEAD_COLS)
            k_scr[0, cols, 0:PAST_LEN] = r.ck[pl.ds(hd, PAST_LEN, stride=N_HEADS), :].T.astype(BF16)
            v_scr[0, 0:PAST_LEN, cols] = r.cv[pl.ds(hd, PAST_LEN, stride=N_HEADS), :].astype(BF16)
        lane_q = lax.broadcasted_iota(jnp.int32, (ROW_CHUNK, QK_WIDTH), 1)
        rope_low = (lane_q % ROPE_AXIS_DIM) < ROPE_HALF

        def rope(t, pos0):
            partner = jnp.where(rope_low,
                                pltpu.roll(t, QK_WIDTH - ROPE_HALF, 1),
                                pltpu.roll(t, ROPE_HALF, 1))
            cos = r.rope_c[pos0:pos0 + ROW_CHUNK, :]
            sin = r.rope_s[pos0:pos0 + ROW_CHUNK, :]
            cos = jnp.concatenate([cos] * N_HEADS, axis=1)
            sin = jnp.concatenate([sin] * N_HEADS, axis=1)
            return t * cos + partner * sin

    for c in range(ROWS // ROW_CHUNK):
        r0 = c * ROW_CHUNK
        b, pos0 = r0 // seq, r0 % seq
        rows = slice(r0, r0 + ROW_CHUNK)
        h = _rms_mod(x_ref[rows, :], r.g1[...], scale1, shift1).astype(BF16)

        pool_scr[b, POOL_HALO + pos0:POOL_HALO + pos0 + ROW_CHUNK, :] = _dot(h, r.w_in[:, C_POOL:C_Q])

        q = _dot(h, r.w_in[:, C_Q:C_K])
        k = _dot(h, r.w_in[:, C_K:C_V])
        if has_ctx:
            q = rope(q, pos0)
            k = rope(k, pos0)
        else:
            for hd in range(N_HEADS):
                r.kc_out[b, r.cache_slot, pl.ds(N_HEADS * pos0 + hd, ROW_CHUNK, stride=N_HEADS), :] = (
                    k[:, hd * HEAD_COLS:(hd + 1) * HEAD_COLS])
        q = q * (QK_DIM ** -0.5 * math.log2(math.e))
        for hd in range(N_HEADS):
            cols = slice(hd * HEAD_COLS, (hd + 1) * HEAD_COLS)
            qh = q[:, cols]
            r.q_scr[c, 0:ROW_CHUNK, cols] = jnp.where(first_half, qh, 0.0).astype(BF16)
            r.q_scr[c, ROW_CHUNK:2 * ROW_CHUNK, cols] = jnp.where(first_half, 0.0, qh).astype(BF16)
        k_scr[b, :, kv_base + pos0:kv_base + pos0 + ROW_CHUNK] = k.T.astype(BF16)

        v = _dot(h, r.w_in[:, C_V:C_UV])
        if not has_ctx:
            for hd in range(N_HEADS):
                r.vc_out[b, r.cache_slot, pl.ds(N_HEADS * pos0 + hd, ROW_CHUNK, stride=N_HEADS), :] = (
                    v[:, hd * V_DIM:(hd + 1) * V_DIM])
        v_scr[b, kv_base + pos0:kv_base + pos0 + ROW_CHUNK, :] = v.astype(BF16)

        uv = _gelu_tanh(_dot(h, r.w_in[:, C_UV:IN_WIDTH]))
        r.u_scr[rows, :] = uv[:, :SGU_WIDTH]
        vv = uv[:, SGU_WIDTH:]
        mu = jnp.mean(vv, axis=-1, keepdims=True)
        vc = vv - mu
        vn = vc * lax.rsqrt(jnp.mean(vc * vc, axis=-1, keepdims=True) + EPS) * r.sgu_g[...]
        r.vs_scr[rows, :] = vn.astype(BF16)

    lane_pool = lax.broadcasted_iota(jnp.int32, (ROW_CHUNK, 2 * POOL_GROUP), 1)
    narrow = lane_pool < POOL_GROUP
    for c in range(ROWS // ROW_CHUNK):
        r0 = c * ROW_CHUNK
        b, pos0 = r0 // seq, r0 % seq
        base = POOL_HALO + pos0
        pooled = []
        for j, (w_small, w_big) in enumerate(((POOL_WINDOWS[0], POOL_WINDOWS[1]),
                                              (POOL_WINDOWS[2], POOL_WINDOWS[3]))):
            cols = slice(j * 2 * POOL_GROUP, (j + 1) * 2 * POOL_GROUP)
            s_small = jnp.zeros((ROW_CHUNK, 2 * POOL_GROUP), F32)
            s_rest = jnp.zeros((ROW_CHUNK, 2 * POOL_GROUP), F32)
            for d in range(-(w_big // 2), w_big - w_big // 2):
                t = pool_scr[b, base + d:base + d + ROW_CHUNK, cols]
                if -(w_small // 2) <= d < w_small - w_small // 2:
                    s_small = s_small + t
                else:
                    s_rest = s_rest + t
            win_sum = jnp.where(narrow, s_small, s_small + s_rest)
            centre = pool_scr[b, base:base + ROW_CHUNK, cols]
            pooled.append(win_sum / cnt_ref[pos0:pos0 + ROW_CHUNK, cols] - centre)
        pooled = jnp.concatenate(pooled, axis=1).astype(BF16)
        y_a = _dot(pooled, r.wpool[...]) * r.pscale[...]
        r.y_scr[r0:r0 + ROW_CHUNK, Y_POOL:Y_ATTN] = y_a.astype(BF16)

    lq1, lk1, lq2, lk2 = r.lam[0:1, :], r.lam[1:2, :], r.lam[2:3, :], r.lam[3:4, :]
    lam = (jnp.exp(jnp.sum(lq1 * lk1, axis=-1, keepdims=True))
           - jnp.exp(jnp.sum(lq2 * lk2, axis=-1, keepdims=True)) + lam_init)
    n_qb = seq // ROW_CHUNK

    lane_sgu = lax.broadcasted_iota(jnp.int32, (CHUNK, 2 * POOL_GROUP), 1)
    sgu_first = lane_sgu < (SGU_WIDTH // 4)
    out_cols = MIX_WIDTH // N_HEADS

    def out_proj(rows, part):
        cols = slice(part * out_cols, (part + 1) * out_cols)
        y = _dot(r.y_scr[rows, :], r.w_out[:, cols])
        r.out[rows, cols] = x_ref[rows, cols] + gate1[:, cols] * y

    def block(blk, with_prev):
        static = isinstance(blk, int)

        def row0(chunk):
            return chunk * ROW_CHUNK if static else pl.multiple_of(chunk * ROW_CHUNK, ROW_CHUNK)

        chunks = [blk * cpb + j for j in range(cpb)]

        for idx in chunks:
            for half in range(ROW_CHUNK // CHUNK):
                sub = pl.ds(row0(idx) + half * CHUNK, CHUNK)
                for j in range(2):
                    cols = slice(j * 128, (j + 1) * 128)
                    t = _dot(r.wsgu[j], r.vs_scr[sub, cols])
                    mixed = jnp.where(sgu_first, t[:CHUNK, :], t[CHUNK:, :]) + r.bsgu[:, cols]
                    y_c = r.u_scr[sub, cols] * mixed
                    r.y_scr[sub, Y_SGU + j * 128:Y_SGU + (j + 1) * 128] = y_c.astype(BF16)

        units = [(idx, hd) for idx in chunks for hd in range(N_HEADS)]
        prev = [(pl.ds(row0(idx - cpb), ROW_CHUNK), part) for idx in chunks for part in range(N_HEADS)]

        def scores(idx, hd):
            cols = slice(hd * HEAD_COLS, (hd + 1) * HEAD_COLS)
            s = _dot(r.q_scr[idx, :, cols], k_scr[idx // n_qb, cols, :])
            return s[:ROW_CHUNK], s[ROW_CHUNK:]

        pending = [scores(*u) for u in units[:ATTN_LOOKAHEAD]]
        for n, (idx, hd) in enumerate(units):
            cols = slice(hd * HEAD_COLS, (hd + 1) * HEAD_COLS)
            s1, s2 = pending.pop(0)
            if n + ATTN_LOOKAHEAD < len(units):
                pending.append(scores(*units[n + ATTN_LOOKAHEAD]))
            e1 = jnp.exp2(s1 - jnp.max(s1, axis=-1, keepdims=True))
            e2 = jnp.exp2(s2 - jnp.max(s2, axis=-1, keepdims=True))
            l1 = jnp.sum(e1, axis=-1, keepdims=True)
            l2 = jnp.sum(e2, axis=-1, keepdims=True)
            p = (e1 - e2 * (lam * l1 / l2)).astype(BF16)
            o = _dot(p, v_scr[idx // n_qb, :, cols]) * (1.0 / l1)
            if with_prev:
                out_proj(*prev[n])
            o = o * lax.rsqrt(jnp.mean(o * o, axis=-1, keepdims=True) + EPS) * r.subln[...]
            o = o * (1.0 - lam_init)
            r.y_scr[pl.ds(row0(idx), ROW_CHUNK), Y_ATTN + hd * V_DIM:Y_ATTN + (hd + 1) * V_DIM] = o.astype(BF16)

    n_blocks = ROWS // (ROW_CHUNK * cpb)
    block(0, False)
    if n_blocks == 2:
        block(1, True)
    else:
        def loop_block(blk, carry):
            block(blk, True)
            return carry

        lax.fori_loop(1, n_blocks, loop_block, 0)
    for idx in range((n_blocks - 1) * cpb, n_blocks * cpb):
        for part in range(N_HEADS):
            out_proj(pl.ds(idx * ROW_CHUNK, ROW_CHUNK), part)


def _mixer_kernel(*refs, nb, seq, has_ctx, lam_init, cache_slot, n_passthrough, n_cast):
    it = iter(refs)
    r = types.SimpleNamespace(cache_slot=cache_slot)
    x_ref, r.mod, r.g1, r.w_in = next(it), next(it), next(it), next(it)
    if has_ctx:
        r.rope_c, r.rope_s, r.ck, r.cv = next(it), next(it), next(it), next(it)
    (r.lam, r.subln, cnt_ref, r.wpool, r.pscale, r.sgu_g, r.wsgu, r.bsgu, r.w_out) = (next(it) for _ in range(9))
    for _ in range(n_passthrough):
        next(it)
    cast_in = [next(it) for _ in range(n_cast)]
    r.out = next(it)
    if not has_ctx:
        r.kc_out, r.vc_out = next(it), next(it)
    cast_out = [next(it) for _ in range(n_cast)]
    pool_scr, r.q_scr, k_scr, v_scr, r.u_scr, r.vs_scr, r.y_scr = (next(it) for _ in range(7))
    for src, dst in zip(cast_in, cast_out):
        dst[...] = src[...].astype(BF16)
    _mixer_body(r, x_ref, cnt_ref, pool_scr, k_scr, v_scr, nb=nb, seq=seq, has_ctx=has_ctx, lam_init=lam_init,
                cpb=CHUNKS_PER_BLOCK)


def _mixer_call(x2d, mod4, layer, seq, has_ctx, lam_init, params, big, ctx=None, new_cache=None, cast_f32=()):
    n_rows = x2d.shape[0]
    nb = ROWS // seq
    n_seq = n_rows // seq
    kv_len = seq + (PAST_LEN if has_ctx else 0)

    if has_ctx:
        mod_map = lambda i: (layer, 1 + i, 0, 0)
    else:
        mod_map = lambda i: (layer, 0, 0, 0)

    operands = [x2d, mod4, params["norm1_g"], big["w_in"]]
    in_specs = [
        pl.BlockSpec((ROWS, D_MODEL), lambda i: (i, 0)),
        pl.BlockSpec((None, None, N_MOD, D_MODEL), mod_map),
        _layer_spec((1, D_MODEL), layer),
        _const_spec((D_MODEL, IN_WIDTH)),
    ]
    if has_ctx:
        operands += list(ctx)
        in_specs += [
            _const_spec((seq, HEAD_COLS)),
            _const_spec((seq, HEAD_COLS)),
            pl.BlockSpec((None, None, PAST_LEN * N_HEADS, HEAD_COLS), lambda i: (i, layer, 0, 0)),
            pl.BlockSpec((None, None, PAST_LEN * N_HEADS, V_DIM), lambda i: (i, layer, 0, 0)),
        ]
    operands += [params["lam"], params["subln_g"], _pool_counts(seq), params["w_pool"], params["pool_scale"],
                 params["sgu_norm_g"], params["w_sgu"], params["b_sgu"], big["w_out"]]
    in_specs += [
        _layer_spec((8, 128), layer),
        _layer_spec((1, V_DIM), layer),
        _const_spec((seq, POOL_WIDTH)),
        _layer_spec((POOL_WIDTH, POOL_WIDTH), layer),
        _layer_spec((1, POOL_WIDTH), layer),
        _layer_spec((1, SGU_WIDTH), layer),
        _layer_spec((2, 2 * CHUNK, CHUNK), layer),
        _layer_spec((CHUNK, SGU_WIDTH), layer),
        _const_spec((MIX_WIDTH, D_MODEL)),
    ]

    out_shape = [jax.ShapeDtypeStruct((n_rows, D_MODEL), F32)]
    out_specs = [pl.BlockSpec((ROWS, D_MODEL), lambda i: (i, 0))]
    aliases = {}
    cache_slot = 0
    if not has_ctx:
        out_shape += [jax.ShapeDtypeStruct((n_seq, DEPTH, seq * N_HEADS, HEAD_COLS), F32),
                      jax.ShapeDtypeStruct((n_seq, DEPTH, seq * N_HEADS, V_DIM), F32)]
        if new_cache is None:
            cache_slot = layer
            out_specs += [pl.BlockSpec((nb, DEPTH, seq * N_HEADS, HEAD_COLS), lambda i: (i, 0, 0, 0)),
                          pl.BlockSpec((nb, DEPTH, seq * N_HEADS, V_DIM), lambda i: (i, 0, 0, 0))]
        else:
            out_specs += [pl.BlockSpec((nb, 1, seq * N_HEADS, HEAD_COLS), lambda i: (i, layer, 0, 0)),
                          pl.BlockSpec((nb, 1, seq * N_HEADS, V_DIM), lambda i: (i, layer, 0, 0))]
            aliases = {len(operands): 1, len(operands) + 1: 2}
            operands += list(new_cache)
            in_specs += [pl.BlockSpec(memory_space=pl.ANY)] * 2
    n_steps = n_rows // ROWS
    for w in cast_f32:
        _, rows, cols = w.shape
        slab = rows // n_steps
        assert slab * n_steps == rows and slab % 16 == 0
        operands.append(w)
        in_specs.append(pl.BlockSpec((None, slab, cols), lambda i: (layer, i, 0)))
        out_shape.append(jax.ShapeDtypeStruct((rows, cols), BF16))
        out_specs.append(pl.BlockSpec((slab, cols), lambda i: (i, 0)))

    scratch = [
        pltpu.VMEM((nb, seq + 2 * POOL_HALO, POOL_WIDTH), F32),
        pltpu.VMEM((ROWS // ROW_CHUNK, 2 * ROW_CHUNK, QK_WIDTH), BF16),
        pltpu.VMEM((nb, QK_WIDTH, kv_len), BF16),
        pltpu.VMEM((nb, kv_len, ATTN_WIDTH), BF16),
        pltpu.VMEM((ROWS, SGU_WIDTH), F32),
        pltpu.VMEM((ROWS, SGU_WIDTH), BF16),
        pltpu.VMEM((ROWS, MIX_WIDTH), BF16),
    ]
    kernel = functools.partial(_mixer_kernel, nb=nb, seq=seq, has_ctx=has_ctx, lam_init=lam_init,
                               cache_slot=cache_slot, n_passthrough=len(aliases), n_cast=len(cast_f32))
    return pl.pallas_call(
        kernel,
        out_shape=out_shape,
        grid=(n_rows // ROWS,),
        in_specs=in_specs,
        out_specs=out_specs,
        scratch_shapes=scratch,
        input_output_aliases=aliases,
        compiler_params=pltpu.CompilerParams(
            dimension_semantics=("arbitrary",), vmem_limit_bytes=VMEM_LIMIT),
        name="mixer_ctx" if has_ctx else "mixer_prompt",
    )(*operands)


def _ffn_block(x_ref, out_ref, mod_ref, g2_ref, w1_ref, w2_ref, gf_ref, act_scr, final_norm):
    shift2, scale2, gate2 = mod_ref[3:4, :], mod_ref[4:5, :], mod_ref[5:6, :]
    x = x_ref[...]
    h = _rms_mod(x, g2_ref[...], scale2, shift2).astype(BF16)
    for lo, hi in ((0, FF_SPLIT), (FF_SPLIT, D_FF)):
        gate = _dot(h, w1_ref[:, lo:hi])
        up = _dot(h, w1_ref[:, D_FF + lo:D_FF + hi])
        act_scr[:, lo:hi] = (gate * _sigmoid(gate) * up).astype(BF16)
    y = x + gate2 * _dot(act_scr[...], w2_ref[...])
    if final_norm:
        ms = jnp.mean(y * y, axis=-1, keepdims=True)
        y = y * lax.rsqrt(ms + EPS) * gf_ref[...]
    out_ref[...] = y


def _ffn_kernel(*refs, n_prompt_steps, final_norm, n_cast):
    xp_ref, xs_ref, mod_ref, g2_ref, w1_ref, w2_ref, gf_ref = refs[:7]
    cast_in = refs[7:7 + n_cast]
    yp_ref, ys_ref = refs[7 + n_cast:9 + n_cast]
    cast_out = refs[9 + n_cast:9 + 2 * n_cast]
    act_scr = refs[9 + 2 * n_cast]
    shared = (mod_ref, g2_ref, w1_ref, w2_ref, gf_ref, act_scr, final_norm)
    is_prompt = pl.program_id(0) < n_prompt_steps

    @pl.when(is_prompt)
    def _():
        _ffn_block(xp_ref, yp_ref, *shared)

    @pl.when(jnp.logical_not(is_prompt))
    def _():
        for src, dst in zip(cast_in, cast_out):
            dst[...] = src[...].astype(BF16)
        _ffn_block(xs_ref, ys_ref, *shared)


def _ffn_call(xp, xs, mod4, layer, sample_seq, params, big, final_norm, next_f32=()):
    n_p, n_s = xp.shape[0] // FFN_ROWS, xs.shape[0] // FFN_ROWS
    steps_per_seq = sample_seq // FFN_ROWS

    def prompt_step(i):
        return jnp.minimum(i, n_p - 1)

    def sample_step(i):
        return jnp.maximum(i - n_p, 0)

    def mod_map(i):
        return (layer, jnp.where(i < n_p, 0, 1 + sample_step(i) // steps_per_seq), 0, 0)

    in_specs = [
        pl.BlockSpec((FFN_ROWS, D_MODEL), lambda i: (prompt_step(i), 0)),
        pl.BlockSpec((FFN_ROWS, D_MODEL), lambda i: (sample_step(i), 0)),
        pl.BlockSpec((None, None, N_MOD, D_MODEL), mod_map),
        _layer_spec((1, D_MODEL), layer),
        _const_spec((D_MODEL, 2 * D_FF)),
        _const_spec((D_FF, D_MODEL)),
        _const_spec((1, D_MODEL)),
    ]
    out_shape = [jax.ShapeDtypeStruct(xp.shape, F32), jax.ShapeDtypeStruct(xs.shape, F32)]
    out_specs = [pl.BlockSpec((FFN_ROWS, D_MODEL), lambda i: (prompt_step(i), 0)),
                 pl.BlockSpec((FFN_ROWS, D_MODEL), lambda i: (sample_step(i), 0))]
    for w in next_f32:
        _, rows, cols = w.shape
        slab = rows // n_s
        assert slab * n_s == rows and slab % 16 == 0
        in_specs.append(pl.BlockSpec((None, slab, cols), lambda i: (layer + 1, sample_step(i), 0)))
        out_shape.append(jax.ShapeDtypeStruct((rows, cols), BF16))
        out_specs.append(pl.BlockSpec((slab, cols), lambda i: (sample_step(i), 0)))

    kernel = functools.partial(_ffn_kernel, n_prompt_steps=n_p, final_norm=final_norm, n_cast=len(next_f32))
    return pl.pallas_call(
        kernel,
        out_shape=out_shape,
        grid=(n_p + n_s,),
        in_specs=in_specs,
        out_specs=out_specs,
        scratch_shapes=[pltpu.VMEM((FFN_ROWS, D_FF), BF16)],
        compiler_params=pltpu.CompilerParams(
            dimension_semantics=("arbitrary",), vmem_limit_bytes=VMEM_LIMIT),
        name="ffn",
    )(xp, xs, mod4, params["norm2_g"], big["w_ffn_in"], big["w_ffn_out"], params["final_g"], *next_f32)


def _rope_tables(seq):
    n_rows = seq // GRID_W
    rows = np.repeat(np.arange(n_rows), GRID_W).astype(np.float32)
    cols = np.tile(np.arange(GRID_W), n_rows).astype(np.float32)
    inv = 1.0 / (ROPE_BASE ** (np.arange(0, ROPE_AXIS_DIM, 2, dtype=np.float32) / ROPE_AXIS_DIM))
    ar, ac = rows[:, None] * inv[None], cols[:, None] * inv[None]
    cos_parts, sin_parts = [], []
    for ang in (ar, ac):
        cos_parts += [np.cos(ang), np.cos(ang)]
        sin_parts += [-np.sin(ang), np.sin(ang)]
    cos64 = np.concatenate(cos_parts, axis=1)
    sin64 = np.concatenate(sin_parts, axis=1)
    cos = np.concatenate([cos64, cos64], axis=1).astype(np.float32)
    sin = np.concatenate([sin64, sin64], axis=1).astype(np.float32)
    return jnp.asarray(cos), jnp.asarray(sin)


def _pool_counts(seq):
    t = np.arange(seq)
    cols = []
    for w in POOL_WINDOWS:
        lo = np.clip(t - w // 2, 0, seq)
        hi = np.clip(t + w - w // 2, 0, seq)
        cols.append(np.repeat((hi - lo).astype(np.float32)[:, None], POOL_GROUP, axis=1))
    return jnp.asarray(np.concatenate(cols, axis=1))


def _small_params(norm1_g, w_pool, pool_scale, lam_q1, lam_k1, lam_q2, lam_k2, subln_g,
                  sgu_norm_g, w_sgu, b_sgu, norm2_g, final_g):
    n_groups = len(POOL_WINDOWS)
    eye = jnp.eye(n_groups, dtype=F32)
    w_pool_bd = (eye[None, :, None, :, None] * w_pool[:, :, :, None, :]).reshape(DEPTH, POOL_WIDTH, POOL_WIDTH)
    lam = jnp.zeros((DEPTH, 8, 128), F32)
    lam = lam.at[:, 0, :QK_DIM].set(lam_q1).at[:, 1, :QK_DIM].set(lam_k1)
    lam = lam.at[:, 2, :QK_DIM].set(lam_q2).at[:, 3, :QK_DIM].set(lam_k2)
    return {
        "norm1_g": norm1_g[:, None, :],
        "lam": lam,
        "subln_g": subln_g[:, None, :],
        "w_pool": w_pool_bd.astype(BF16),
        "pool_scale": pool_scale[:, None, :],
        "sgu_norm_g": sgu_norm_g[:, None, :],
        "w_sgu": w_sgu.reshape(DEPTH, 2, 2 * CHUNK, CHUNK).astype(BF16),
        "b_sgu": jnp.repeat(jnp.swapaxes(b_sgu, 1, 2), SGU_WIDTH // 4, axis=2),
        "norm2_g": norm2_g[:, None, :],
        "final_g": final_g[None, :],
    }


BIG_WEIGHTS = ("w_in", "w_out", "w_ffn_in", "w_ffn_out")


def kernel(x_prompt, x_sample, cache_k, cache_v, c, c_ctx, norm1_g, w_ada, b_ada, w_in, w_pool, pool_scale, lam_q1, lam_k1, lam_q2, lam_k2, subln_g, sgu_norm_g, w_sgu, b_sgu, w_out, norm2_g, w_ffn_in, w_ffn_out, final_g):
    batch, seq, _ = x_prompt.shape
    dec_batch, dec_seq, _ = x_sample.shape
    assert ROWS % seq == 0 and dec_seq == ROWS and 1 + dec_batch <= COND_ROWS

    cond = jnp.zeros((COND_ROWS, D_MODEL), F32).at[0].set(c_ctx).at[1:1 + dec_batch].set(c)
    mod4 = _ada_modulation(cond, w_ada, b_ada).reshape(DEPTH, COND_ROWS, N_MOD, D_MODEL)

    ctx = _rope_tables(dec_seq) + (cache_k.reshape(dec_batch, DEPTH, PAST_LEN * N_HEADS, HEAD_COLS),
                                   cache_v.reshape(dec_batch, DEPTH, PAST_LEN * N_HEADS, V_DIM))
    params = _small_params(norm1_g, w_pool, pool_scale, lam_q1, lam_k1, lam_q2, lam_k2, subln_g,
                           sgu_norm_g, w_sgu, b_sgu, norm2_g, final_g)

    big_f32 = (w_in, w_out, w_ffn_in, w_ffn_out)
    big = {"w_in": w_in[0].astype(BF16), "w_out": w_out[0].astype(BF16)}

    xp = x_prompt.reshape(batch * seq, D_MODEL)
    xs = x_sample.reshape(dec_batch * dec_seq, D_MODEL)
    new_cache = None
    for l in range(DEPTH):
        lam_init = 0.8 - 0.6 * math.exp(-0.3 * l)
        last = l == DEPTH - 1

        xp, *new_cache = _mixer_call(xp, mod4, l, seq, False, lam_init, params, big, new_cache=new_cache)
        if l == 0:
            xs, big["w_ffn_in"], big["w_ffn_out"] = _mixer_call(
                xs, mod4, l, dec_seq, True, lam_init, params, big, ctx=ctx, cast_f32=(w_ffn_in, w_ffn_out))
        else:
            xs, = _mixer_call(xs, mod4, l, dec_seq, True, lam_init, params, big, ctx=ctx)
        xp, xs, *next_big = _ffn_call(xp, xs, mod4, l, dec_seq, params, big, last,
                                      next_f32=() if last else big_f32)
        big = dict(zip(BIG_WEIGHTS, next_big))

    y_prompt = xp.reshape(batch, seq, D_MODEL)
    y_sample = xs.reshape(dec_batch, dec_seq, D_MODEL)
    new_cache_k = new_cache[0].reshape(batch, DEPTH, seq, N_HEADS, 2 * QK_DIM)
    new_cache_v = new_cache[1].reshape(batch, DEPTH, seq, N_HEADS, V_DIM)
    return (y_prompt, y_sample, new_cache_k, new_cache_v)
```

```python
import functools
import math
import types

import numpy as np
import jax
import jax.numpy as jnp
from jax import lax
from jax.experimental import pallas as pl
from jax.experimental.pallas import tpu as pltpu

D_MODEL = 1024
DEPTH = 2
GRID_W = 64
POOL_WINDOWS = (2, 4, 8, 16)
POOL_WIDTH = 256
POOL_GROUP = 64
POOL_HALO = 8
ATTN_WIDTH = 512
N_HEADS = 4
V_DIM = 128
QK_DIM = 64
HEAD_COLS = 2 * QK_DIM
ROPE_BASE = 10000.0
ROPE_AXIS_DIM = 32
ROPE_HALF = ROPE_AXIS_DIM // 2
CHUNK = 128
SGU_WIDTH = 256
QK_WIDTH = 512
IN_WIDTH = 2304
MIX_WIDTH = 1024
D_FF = 2816
N_MOD = 6
EPS = 1e-6
PAST_LEN = 256

C_POOL, C_Q, C_K, C_V, C_UV = 0, 256, 768, 1280, 1792
Y_POOL, Y_ATTN, Y_SGU = 0, 256, 768

ROWS = 1024
ROW_CHUNK = 256
ATTN_LOOKAHEAD = 1
CHUNKS_PER_BLOCK = 2
FFN_ROWS = 512
MXU_COLS = 256
FF_SPLIT = (D_FF // MXU_COLS + 1) // 2 * MXU_COLS
COND_ROWS = 16
ADA_COLS = 1024
VMEM_LIMIT = 58 * 1024 * 1024

BF16 = jnp.bfloat16
F32 = jnp.float32


def _dot(a, b):
    return jnp.dot(a, b, preferred_element_type=F32)


def _sigmoid(x):
    return 1.0 / (1.0 + jnp.exp(-x))


def _rms_mod(x, g, scale, shift):
    ms = jnp.mean(x * x, axis=-1, keepdims=True)
    y = x * lax.rsqrt(ms + EPS) * g
    return y * (1.0 + scale) + shift


def _gelu_tanh(x):
    c = math.sqrt(2.0 / math.pi)
    return x * (0.5 * (1.0 + jnp.tanh(c * (x + 0.044715 * (x * x * x)))))


def _const_spec(shape):
    zeros = (0,) * len(shape)
    return pl.BlockSpec(shape, lambda *_: zeros, pipeline_mode=pl.Buffered(1))


def _layer_spec(shape, layer):
    index = (layer,) + (0,) * len(shape)
    return pl.BlockSpec((None,) + tuple(shape), lambda *_: index, pipeline_mode=pl.Buffered(1))


def _ada_kernel(cond_ref, w_ref, b_ref, out_ref):
    cond = cond_ref[...]
    s = (cond * _sigmoid(cond)).astype(BF16)
    out_ref[...] = _dot(s, w_ref[...].astype(BF16)) + b_ref[...]


def _ada_modulation(cond, w_ada, b_ada):
    n_cols = N_MOD * D_MODEL
    return pl.pallas_call(
        _ada_kernel,
        out_shape=jax.ShapeDtypeStruct((DEPTH, COND_ROWS, n_cols), F32),
        grid=(DEPTH, n_cols // ADA_COLS),
        in_specs=[
            pl.BlockSpec((COND_ROWS, D_MODEL), lambda l, j: (0, 0)),
            pl.BlockSpec((None, D_MODEL, ADA_COLS), lambda l, j: (l, 0, j)),
            pl.BlockSpec((None, 1, ADA_COLS), lambda l, j: (l, 0, j)),
        ],
        out_specs=pl.BlockSpec((None, COND_ROWS, ADA_COLS), lambda l, j: (l, 0, j)),
        compiler_params=pltpu.CompilerParams(
            dimension_semantics=("arbitrary", "arbitrary"), vmem_limit_bytes=VMEM_LIMIT),
        name="ada_modulation",
    )(cond, w_ada, b_ada.reshape(DEPTH, 1, n_cols))


def _mixer_body(r, x_ref, cnt_ref, pool_scr, k_scr, v_scr, *, nb, seq, has_ctx, lam_init, cpb):
    kv_base = PAST_LEN if has_ctx else 0
    shift1, scale1, gate1 = r.mod[0:1, :], r.mod[1:2, :], r.mod[2:3, :]

    lane_head = lax.broadcasted_iota(jnp.int32, (ROW_CHUNK, HEAD_COLS), 1)
    first_half = lane_head < QK_DIM

    for b in range(nb):
        pool_scr[b, 0:POOL_HALO, :] = jnp.zeros((POOL_HALO, POOL_WIDTH), F32)
        pool_scr[b, POOL_HALO + seq:POOL_HALO + seq + POOL_HALO, :] = jnp.zeros((POOL_HALO, POOL_WIDTH), F32)
    if not has_ctx:
        for slot in range(r.kc_out.shape[1]):
            if slot != r.cache_slot:
                r.kc_out[:, slot] = jnp.zeros((nb,) + r.kc_out.shape[2:], F32)
                r.vc_out[:, slot] = jnp.zeros((nb,) + r.vc_out.shape[2:], F32)
    if has_ctx:
        for hd in range(N_HEADS):
            cols = slice(hd * HEAD_COLS, (hd + 1) * HEAD_COLS)
            k_scr[0, cols, 0:PAST_LEN] = r.ck[pl.ds(hd, PAST_LEN, stride=N_HEADS), :].T.astype(BF16)
            v_scr[0, 0:PAST_LEN, cols] = r.cv[pl.ds(hd, PAST_LEN, stride=N_HEADS), :].astype(BF16)
        lane_q = lax.broadcasted_iota(jnp.int32, (ROW_CHUNK, QK_WIDTH), 1)
        rope_low = (lane_q % ROPE_AXIS_DIM) < ROPE_HALF

        def rope(t, pos0):
            partner = jnp.where(rope_low,
                                pltpu.roll(t, QK_WIDTH - ROPE_HALF, 1),
                                pltpu.roll(t, ROPE_HALF, 1))
            cos = r.rope_c[pos0:pos0 + ROW_CHUNK, :]
            sin = r.rope_s[pos0:pos0 + ROW_CHUNK, :]
            cos = jnp.concatenate([cos] * N_HEADS, axis=1)
            sin = jnp.concatenate([sin] * N_HEADS, axis=1)
            return t * cos + partner * sin

    for c in range(ROWS // ROW_CHUNK):
        r0 = c * ROW_CHUNK
        b, pos0 = r0 // seq, r0 % seq
        rows = slice(r0, r0 + ROW_CHUNK)
        h = _rms_mod(x_ref[rows, :], r.g1[...], scale1, shift1).astype(BF16)

        pool_scr[b, POOL_HALO + pos0:POOL_HALO + pos0 + ROW_CHUNK, :] = _dot(h, r.w_in[:, C_POOL:C_Q])

        q = _dot(h, r.w_in[:, C_Q:C_K])
        k = _dot(h, r.w_in[:, C_K:C_V])
        if has_ctx:
            q = rope(q, pos0)
            k = rope(k, pos0)
        else:
            for hd in range(N_HEADS):
                r.kc_out[b, r.cache_slot, pl.ds(N_HEADS * pos0 + hd, ROW_CHUNK, stride=N_HEADS), :] = (
                    k[:, hd * HEAD_COLS:(hd + 1) * HEAD_COLS])
        q = q * (QK_DIM ** -0.5 * math.log2(math.e))
        for hd in range(N_HEADS):
            cols = slice(hd * HEAD_COLS, (hd + 1) * HEAD_COLS)
            qh = q[:, cols]
            r.q_scr[c, 0:ROW_CHUNK, cols] = jnp.where(first_half, qh, 0.0).astype(BF16)
            r.q_scr[c, ROW_CHUNK:2 * ROW_CHUNK, cols] = jnp.where(first_half, 0.0, qh).astype(BF16)
        k_scr[b, :, kv_base + pos0:kv_base + pos0 + ROW_CHUNK] = k.T.astype(BF16)

        v = _dot(h, r.w_in[:, C_V:C_UV])
        if not has_ctx:
            for hd in range(N_HEADS):
                r.vc_out[b, r.cache_slot, pl.ds(N_HEADS * pos0 + hd, ROW_CHUNK, stride=N_HEADS), :] = (
                    v[:, hd * V_DIM:(hd + 1) * V_DIM])
        v_scr[b, kv_base + pos0:kv_base + pos0 + ROW_CHUNK, :] = v.astype(BF16)

        uv = _gelu_tanh(_dot(h, r.w_in[:, C_UV:IN_WIDTH]))
        r.u_scr[rows, :] = uv[:, :SGU_WIDTH]
        vv = uv[:, SGU_WIDTH:]
        mu = jnp.mean(vv, axis=-1, keepdims=True)
        vc = vv - mu
        vn = vc * lax.rsqrt(jnp.mean(vc * vc, axis=-1, keepdims=True) + EPS) * r.sgu_g[...]
        r.vs_scr[rows, :] = vn.astype(BF16)

    lane_pool = lax.broadcasted_iota(jnp.int32, (ROW_CHUNK, 2 * POOL_GROUP), 1)
    narrow = lane_pool < POOL_GROUP
    for c in range(ROWS // ROW_CHUNK):
        r0 = c * ROW_CHUNK
        b, pos0 = r0 // seq, r0 % seq
        base = POOL_HALO + pos0
        pooled = []
        for j, (w_small, w_big) in enumerate(((POOL_WINDOWS[0], POOL_WINDOWS[1]),
                                              (POOL_WINDOWS[2], POOL_WINDOWS[3]))):
            cols = slice(j * 2 * POOL_GROUP, (j + 1) * 2 * POOL_GROUP)
            s_small = jnp.zeros((ROW_CHUNK, 2 * POOL_GROUP), F32)
            s_rest = jnp.zeros((ROW_CHUNK, 2 * POOL_GROUP), F32)
            for d in range(-(w_big // 2), w_big - w_big // 2):
                t = pool_scr[b, base + d:base + d + ROW_CHUNK, cols]
                if -(w_small // 2) <= d < w_small - w_small // 2:
                    s_small = s_small + t
                else:
                    s_rest = s_rest + t
            win_sum = jnp.where(narrow, s_small, s_small + s_rest)
            centre = pool_scr[b, base:base + ROW_CHUNK, cols]
            pooled.append(win_sum / cnt_ref[pos0:pos0 + ROW_CHUNK, cols] - centre)
        pooled = jnp.concatenate(pooled, axis=1).astype(BF16)
        y_a = _dot(pooled, r.wpool[...]) * r.pscale[...]
        r.y_scr[r0:r0 + ROW_CHUNK, Y_POOL:Y_ATTN] = y_a.astype(BF16)

    lq1, lk1, lq2, lk2 = r.lam[0:1, :], r.lam[1:2, :], r.lam[2:3, :], r.lam[3:4, :]
    lam = (jnp.exp(jnp.sum(lq1 * lk1, axis=-1, keepdims=True))
           - jnp.exp(jnp.sum(lq2 * lk2, axis=-1, keepdims=True)) + lam_init)
    n_qb = seq // ROW_CHUNK

    lane_sgu = lax.broadcasted_iota(jnp.int32, (CHUNK, 2 * POOL_GROUP), 1)
    sgu_first = lane_sgu < (SGU_WIDTH // 4)
    out_cols = MIX_WIDTH // N_HEADS

    def out_proj(rows, part):
        cols = slice(part * out_cols, (part + 1) * out_cols)
        y = _dot(r.y_scr[rows, :], r.w_out[:, cols])
        r.out[rows, cols] = x_ref[rows, cols] + gate1[:, cols] * y

    def block(blk, with_prev):
        static = isinstance(blk, int)

        def row0(chunk):
            return chunk * ROW_CHUNK if static else pl.multiple_of(chunk * ROW_CHUNK, ROW_CHUNK)

        chunks = [blk * cpb + j for j in range(cpb)]

        for idx in chunks:
            for half in range(ROW_CHUNK // CHUNK):
                sub = pl.ds(row0(idx) + half * CHUNK, CHUNK)
                for j in range(2):
                    cols = slice(j * 128, (j + 1) * 128)
                    t = _dot(r.wsgu[j], r.vs_scr[sub, cols])
                    mixed = jnp.where(sgu_first, t[:CHUNK, :], t[CHUNK:, :]) + r.bsgu[:, cols]
                    y_c = r.u_scr[sub, cols] * mixed
                    r.y_scr[sub, Y_SGU + j * 128:Y_SGU + (j + 1) * 128] = y_c.astype(BF16)

        units = [(idx, hd) for idx in chunks for hd in range(N_HEADS)]
        prev = [(pl.ds(row0(idx - cpb), ROW_CHUNK), part) for idx in chunks for part in range(N_HEADS)]

        def scores(idx, hd):
            cols = slice(hd * HEAD_COLS, (hd + 1) * HEAD_COLS)
            s = _dot(r.q_scr[idx, :, cols], k_scr[idx // n_qb, cols, :])
            return s[:ROW_CHUNK], s[ROW_CHUNK:]

        pending = [scores(*u) for u in units[:ATTN_LOOKAHEAD]]
        for n, (idx, hd) in enumerate(units):
            cols = slice(hd * HEAD_COLS, (hd + 1) * HEAD_COLS)
            s1, s2 = pending.pop(0)
            if n + ATTN_LOOKAHEAD < len(units):
                pending.append(scores(*units[n + ATTN_LOOKAHEAD]))
            e1 = jnp.exp2(s1 - jnp.max(s1, axis=-1, keepdims=True))
            e2 = jnp.exp2(s2 - jnp.max(s2, axis=-1, keepdims=True))
            l1 = jnp.sum(e1, axis=-1, keepdims=True)
            l2 = jnp.sum(e2, axis=-1, keepdims=True)
            p = (e1 - e2 * (lam * l1 / l2)).astype(BF16)
            o = _dot(p, v_scr[idx // n_qb, :, cols]) * (1.0 / l1)
            if with_prev:
                out_proj(*prev[n])
            o = o * lax.rsqrt(jnp.mean(o * o, axis=-1, keepdims=True) + EPS) * r.subln[...]
            o = o * (1.0 - lam_init)
            r.y_scr[pl.ds(row0(idx), ROW_CHUNK), Y_ATTN + hd * V_DIM:Y_ATTN + (hd + 1) * V_DIM] = o.astype(BF16)

    n_blocks = ROWS // (ROW_CHUNK * cpb)
    block(0, False)
    if n_blocks == 2:
        block(1, True)
    else:
        def loop_block(blk, carry):
            block(blk, True)
            return carry

        lax.fori_loop(1, n_blocks, loop_block, 0)
    for idx in range((n_blocks - 1) * cpb, n_blocks * cpb):
        for part in range(N_HEADS):
            out_proj(pl.ds(idx * ROW_CHUNK, ROW_CHUNK), part)


def _mixer_kernel(*refs, nb, seq, has_ctx, lam_init, cache_slot, n_passthrough, n_cast):
    it = iter(refs)
    r = types.SimpleNamespace(cache_slot=cache_slot)
    x_ref, r.mod, r.g1, r.w_in = next(it), next(it), next(it), next(it)
    if has_ctx:
        r.rope_c, r.rope_s, r.ck, r.cv = next(it), next(it), next(it), next(it)
    (r.lam, r.subln, cnt_ref, r.wpool, r.pscale, r.sgu_g, r.wsgu, r.bsgu, r.w_out) = (next(it) for _ in range(9))
    for _ in range(n_passthrough):
        next(it)
    cast_in = [next(it) for _ in range(n_cast)]
    r.out = next(it)
    if not has_ctx:
        r.kc_out, r.vc_out = next(it), next(it)
    cast_out = [next(it) for _ in range(n_cast)]
    pool_scr, r.q_scr, k_scr, v_scr, r.u_scr, r.vs_scr, r.y_scr = (next(it) for _ in range(7))
    for src, dst in zip(cast_in, cast_out):
        dst[...] = src[...].astype(BF16)
    _mixer_body(r, x_ref, cnt_ref, pool_scr, k_scr, v_scr, nb=nb, seq=seq, has_ctx=has_ctx, lam_init=lam_init,
                cpb=CHUNKS_PER_BLOCK)


def _mixer_call(x2d, mod4, layer, seq, has_ctx, lam_init, params, big, ctx=None, new_cache=None, cast_f32=()):
    n_rows = x2d.shape[0]
    nb = ROWS // seq
    n_seq = n_rows // seq
    kv_len = seq + (PAST_LEN if has_ctx else 0)

    if has_ctx:
        mod_map = lambda i: (layer, 1 + i, 0, 0)
    else:
        mod_map = lambda i: (layer, 0, 0, 0)

    operands = [x2d, mod4, params["norm1_g"], big["w_in"]]
    in_specs = [
        pl.BlockSpec((ROWS, D_MODEL), lambda i: (i, 0)),
        pl.BlockSpec((None, None, N_MOD, D_MODEL), mod_map),
        _layer_spec((1, D_MODEL), layer),
        _const_spec((D_MODEL, IN_WIDTH)),
    ]
    if has_ctx:
        operands += list(ctx)
        in_specs += [
            _const_spec((seq, HEAD_COLS)),
            _const_spec((seq, HEAD_COLS)),
            pl.BlockSpec((None, None, PAST_LEN * N_HEADS, HEAD_COLS), lambda i: (i, layer, 0, 0)),
            pl.BlockSpec((None, None, PAST_LEN * N_HEADS, V_DIM), lambda i: (i, layer, 0, 0)),
        ]
    operands += [params["lam"], params["subln_g"], _pool_counts(seq), params["w_pool"], params["pool_scale"],
                 params["sgu_norm_g"], params["w_sgu"], params["b_sgu"], big["w_out"]]
    in_specs += [
        _layer_spec((8, 128), layer),
        _layer_spec((1, V_DIM), layer),
        _const_spec((seq, POOL_WIDTH)),
        _layer_spec((POOL_WIDTH, POOL_WIDTH), layer),
        _layer_spec((1, POOL_WIDTH), layer),
        _layer_spec((1, SGU_WIDTH), layer),
        _layer_spec((2, 2 * CHUNK, CHUNK), layer),
        _layer_spec((CHUNK, SGU_WIDTH), layer),
        _const_spec((MIX_WIDTH, D_MODEL)),
    ]

    out_shape = [jax.ShapeDtypeStruct((n_rows, D_MODEL), F32)]
    out_specs = [pl.BlockSpec((ROWS, D_MODEL), lambda i: (i, 0))]
    aliases = {}
    cache_slot = 0
    if not has_ctx:
        out_shape += [jax.ShapeDtypeStruct((n_seq, DEPTH, seq * N_HEADS, HEAD_COLS), F32),
                      jax.ShapeDtypeStruct((n_seq, DEPTH, seq * N_HEADS, V_DIM), F32)]
        if new_cache is None:
            cache_slot = layer
            out_specs += [pl.BlockSpec((nb, DEPTH, seq * N_HEADS, HEAD_COLS), lambda i: (i, 0, 0, 0)),
                          pl.BlockSpec((nb, DEPTH, seq * N_HEADS, V_DIM), lambda i: (i, 0, 0, 0))]
        else:
            out_specs += [pl.BlockSpec((nb, 1, seq * N_HEADS, HEAD_COLS), lambda i: (i, layer, 0, 0)),
                          pl.BlockSpec((nb, 1, seq * N_HEADS, V_DIM), lambda i: (i, layer, 0, 0))]
            aliases = {len(operands): 1, len(operands) + 1: 2}
            operands += list(new_cache)
            in_specs += [pl.BlockSpec(memory_space=pl.ANY)] * 2
    n_steps = n_rows // ROWS
    for w in cast_f32:
        _, rows, cols = w.shape
        slab = rows // n_steps
        assert slab * n_steps == rows and slab % 16 == 0
        operands.append(w)
        in_specs.append(pl.BlockSpec((None, slab, cols), lambda i: (layer, i, 0)))
        out_shape.append(jax.ShapeDtypeStruct((rows, cols), BF16))
        out_specs.append(pl.BlockSpec((slab, cols), lambda i: (i, 0)))

    scratch = [
        pltpu.VMEM((nb, seq + 2 * POOL_HALO, POOL_WIDTH), F32),
        pltpu.VMEM((ROWS // ROW_CHUNK, 2 * ROW_CHUNK, QK_WIDTH), BF16),
        pltpu.VMEM((nb, QK_WIDTH, kv_len), BF16),
        pltpu.VMEM((nb, kv_len, ATTN_WIDTH), BF16),
        pltpu.VMEM((ROWS, SGU_WIDTH), F32),
        pltpu.VMEM((ROWS, SGU_WIDTH), BF16),
        pltpu.VMEM((ROWS, MIX_WIDTH), BF16),
    ]
    kernel = functools.partial(_mixer_kernel, nb=nb, seq=seq, has_ctx=has_ctx, lam_init=lam_init,
                               cache_slot=cache_slot, n_passthrough=len(aliases), n_cast=len(cast_f32))
    return pl.pallas_call(
        kernel,
        out_shape=out_shape,
        grid=(n_rows // ROWS,),
        in_specs=in_specs,
        out_specs=out_specs,
        scratch_shapes=scratch,
        input_output_aliases=aliases,
        compiler_params=pltpu.CompilerParams(
            dimension_semantics=("arbitrary",), vmem_limit_bytes=VMEM_LIMIT),
        name="mixer_ctx" if has_ctx else "mixer_prompt",
    )(*operands)


def _ffn_block(x_ref, out_ref, mod_ref, g2_ref, w1_ref, w2_ref, gf_ref, act_scr, final_norm):
    shift2, scale2, gate2 = mod_ref[3:4, :], mod_ref[4:5, :], mod_ref[5:6, :]
    x = x_ref[...]
    h = _rms_mod(x, g2_ref[...], scale2, shift2).astype(BF16)
    for lo, hi in ((0, FF_SPLIT), (FF_SPLIT, D_FF)):
        gate = _dot(h, w1_ref[:, lo:hi])
        up = _dot(h, w1_ref[:, D_FF + lo:D_FF + hi])
        act_scr[:, lo:hi] = (gate * _sigmoid(gate) * up).astype(BF16)
    y = x + gate2 * _dot(act_scr[...], w2_ref[...])
    if final_norm:
        ms = jnp.mean(y * y, axis=-1, keepdims=True)
        y = y * lax.rsqrt(ms + EPS) * gf_ref[...]
    out_ref[...] = y


def _ffn_kernel(*refs, n_prompt_steps, final_norm, n_cast):
    xp_ref, xs_ref, mod_ref, g2_ref, w1_ref, w2_ref, gf_ref = refs[:7]
    cast_in = refs[7:7 + n_cast]
    yp_ref, ys_ref = refs[7 + n_cast:9 + n_cast]
    cast_out = refs[9 + n_cast:9 + 2 * n_cast]
    act_scr = refs[9 + 2 * n_cast]
    shared = (mod_ref, g2_ref, w1_ref, w2_ref, gf_ref, act_scr, final_norm)
    is_prompt = pl.program_id(0) < n_prompt_steps

    @pl.when(is_prompt)
    def _():
        _ffn_block(xp_ref, yp_ref, *shared)

    @pl.when(jnp.logical_not(is_prompt))
    def _():
        for src, dst in zip(cast_in, cast_out):
            dst[...] = src[...].astype(BF16)
        _ffn_block(xs_ref, ys_ref, *shared)


def _ffn_call(xp, xs, mod4, layer, sample_seq, params, big, final_norm, next_f32=()):
    n_p, n_s = xp.shape[0] // FFN_ROWS, xs.shape[0] // FFN_ROWS
    steps_per_seq = sample_seq // FFN_ROWS

    def prompt_step(i):
        return jnp.minimum(i, n_p - 1)

    def sample_step(i):
        return jnp.maximum(i - n_p, 0)

    def mod_map(i):
        return (layer, jnp.where(i < n_p, 0, 1 + sample_step(i) // steps_per_seq), 0, 0)

    in_specs = [
        pl.BlockSpec((FFN_ROWS, D_MODEL), lambda i: (prompt_step(i), 0)),
        pl.BlockSpec((FFN_ROWS, D_MODEL), lambda i: (sample_step(i), 0)),
        pl.BlockSpec((None, None, N_MOD, D_MODEL), mod_map),
        _layer_spec((1, D_MODEL), layer),
        _const_spec((D_MODEL, 2 * D_FF)),
        _const_spec((D_FF, D_MODEL)),
        _const_spec((1, D_MODEL)),
    ]
    out_shape = [jax.ShapeDtypeStruct(xp.shape, F32), jax.ShapeDtypeStruct(xs.shape, F32)]
    out_specs = [pl.BlockSpec((FFN_ROWS, D_MODEL), lambda i: (prompt_step(i), 0)),
                 pl.BlockSpec((FFN_ROWS, D_MODEL), lambda i: (sample_step(i), 0))]
    for w in next_f32:
        _, rows, cols = w.shape
        slab = rows // n_s
        assert slab * n_s == rows and slab % 16 == 0
        in_specs.append(pl.BlockSpec((None, slab, cols), lambda i: (layer + 1, sample_step(i), 0)))
        out_shape.append(jax.ShapeDtypeStruct((rows, cols), BF16))
        out_specs.append(pl.BlockSpec((slab, cols), lambda i: (sample_step(i), 0)))

    kernel = functools.partial(_ffn_kernel, n_prompt_steps=n_p, final_norm=final_norm, n_cast=len(next_f32))
    return pl.pallas_call(
        kernel,
        out_shape=out_shape,
        grid=(n_p + n_s,),
        in_specs=in_specs,
        out_specs=out_specs,
        scratch_shapes=[pltpu.VMEM((FFN_ROWS, D_FF), BF16)],
        compiler_params=pltpu.CompilerParams(
            dimension_semantics=("arbitrary",), vmem_limit_bytes=VMEM_LIMIT),
        name="ffn",
    )(xp, xs, mod4, params["norm2_g"], big["w_ffn_in"], big["w_ffn_out"], params["final_g"], *next_f32)


def _rope_tables(seq):
    n_rows = seq // GRID_W
    rows = np.repeat(np.arange(n_rows), GRID_W).astype(np.float32)
    cols = np.tile(np.arange(GRID_W), n_rows).astype(np.float32)
    inv = 1.0 / (ROPE_BASE ** (np.arange(0, ROPE_AXIS_DIM, 2, dtype=np.float32) / ROPE_AXIS_DIM))
    ar, ac = rows[:, None] * inv[None], cols[:, None] * inv[None]
    cos_parts, sin_parts = [], []
    for ang in (ar, ac):
        cos_parts += [np.cos(ang), np.cos(ang)]
        sin_parts += [-np.sin(ang), np.sin(ang)]
    cos64 = np.concatenate(cos_parts, axis=1)
    sin64 = np.concatenate(sin_parts, axis=1)
    cos = np.concatenate([cos64, cos64], axis=1).astype(np.float32)
    sin = np.concatenate([sin64, sin64], axis=1).astype(np.float32)
    return jnp.asarray(cos), jnp.asarray(sin)


def _pool_counts(seq):
    t = np.arange(seq)
    cols = []
    for w in POOL_WINDOWS:
        lo = np.clip(t - w // 2, 0, seq)
        hi = np.clip(t + w - w // 2, 0, seq)
        cols.append(np.repeat((hi - lo).astype(np.float32)[:, None], POOL_GROUP, axis=1))
    return jnp.asarray(np.concatenate(cols, axis=1))


def _small_params(norm1_g, w_pool, pool_scale, lam_q1, lam_k1, lam_q2, lam_k2, subln_g,
                  sgu_norm_g, w_sgu, b_sgu, norm2_g, final_g):
    n_groups = len(POOL_WINDOWS)
    eye = jnp.eye(n_groups, dtype=F32)
    w_pool_bd = (eye[None, :, None, :, None] * w_pool[:, :, :, None, :]).reshape(DEPTH, POOL_WIDTH, POOL_WIDTH)
    lam = jnp.zeros((DEPTH, 8, 128), F32)
    lam = lam.at[:, 0, :QK_DIM].set(lam_q1).at[:, 1, :QK_DIM].set(lam_k1)
    lam = lam.at[:, 2, :QK_DIM].set(lam_q2).at[:, 3, :QK_DIM].set(lam_k2)
    return {
        "norm1_g": norm1_g[:, None, :],
        "lam": lam,
        "subln_g": subln_g[:, None, :],
        "w_pool": w_pool_bd.astype(BF16),
        "pool_scale": pool_scale[:, None, :],
        "sgu_norm_g": sgu_norm_g[:, None, :],
        "w_sgu": w_sgu.reshape(DEPTH, 2, 2 * CHUNK, CHUNK).astype(BF16),
        "b_sgu": jnp.repeat(jnp.swapaxes(b_sgu, 1, 2), SGU_WIDTH // 4, axis=2),
        "norm2_g": norm2_g[:, None, :],
        "final_g": final_g[None, :],
    }


BIG_WEIGHTS = ("w_in", "w_out", "w_ffn_in", "w_ffn_out")


def kernel(x_prompt, x_sample, cache_k, cache_v, c, c_ctx, norm1_g, w_ada, b_ada, w_in, w_pool, pool_scale, lam_q1, lam_k1, lam_q2, lam_k2, subln_g, sgu_norm_g, w_sgu, b_sgu, w_out, norm2_g, w_ffn_in, w_ffn_out, final_g):
    batch, seq, _ = x_prompt.shape
    dec_batch, dec_seq, _ = x_sample.shape
    assert ROWS % seq == 0 and dec_seq == ROWS and 1 + dec_batch <= COND_ROWS

    cond = jnp.zeros((COND_ROWS, D_MODEL), F32).at[0].set(c_ctx).at[1:1 + dec_batch].set(c)
    mod4 = _ada_modulation(cond, w_ada, b_ada).reshape(DEPTH, COND_ROWS, N_MOD, D_MODEL)

    ctx = _rope_tables(dec_seq) + (cache_k.reshape(dec_batch, DEPTH, PAST_LEN * N_HEADS, HEAD_COLS),
                                   cache_v.reshape(dec_batch, DEPTH, PAST_LEN * N_HEADS, V_DIM))
    params = _small_params(norm1_g, w_pool, pool_scale, lam_q1, lam_k1, lam_q2, lam_k2, subln_g,
                           sgu_norm_g, w_sgu, b_sgu, norm2_g, final_g)

    big_f32 = (w_in, w_out, w_ffn_in, w_ffn_out)
    big = {"w_in": w_in[0].astype(BF16), "w_out": w_out[0].astype(BF16)}

    xp = x_prompt.reshape(batch * seq, D_MODEL)
    xs = x_sample.reshape(dec_batch * dec_seq, D_MODEL)
    new_cache = None
    for l in range(DEPTH):
        lam_init = 0.8 - 0.6 * math.exp(-0.3 * l)
        last = l == DEPTH - 1

        xp, *new_cache = _mixer_call(xp, mod4, l, seq, False, lam_init, params, big, new_cache=new_cache)
        if l == 0:
            xs, big["w_ffn_in"], big["w_ffn_out"] = _mixer_call(
                xs, mod4, l, dec_seq, True, lam_init, params, big, ctx=ctx, cast_f32=(w_ffn_in, w_ffn_out))
        else:
            xs, = _mixer_call(xs, mod4, l, dec_seq, True, lam_init, params, big, ctx=ctx)
        xp, xs, *next_big = _ffn_call(xp, xs, mod4, l, dec_seq, params, big, last,
                                      next_f32=() if last else big_f32)
        big = dict(zip(BIG_WEIGHTS, next_big))

    y_prompt = xp.reshape(batch, seq, D_MODEL)
    y_sample = xs.reshape(dec_batch, dec_seq, D_MODEL)
    new_cache_k = new_cache[0].reshape(batch, DEPTH, seq, N_HEADS, 2 * QK_DIM)
    new_cache_v = new_cache[1].reshape(batch, DEPTH, seq, N_HEADS, V_DIM)
    return (y_prompt, y_sample, new_cache_k, new_cache_v)
```

```python
import functools
import math
import types

import numpy as np
import jax
import jax.numpy as jnp
from jax import lax
from jax.experimental import pallas as pl
from jax.experimental.pallas import tpu as pltpu

D_MODEL = 1024
DEPTH = 2
GRID_W = 64
POOL_WINDOWS = (2, 4, 8, 16)
POOL_WIDTH = 256
POOL_GROUP = 64
POOL_HALO = 8
ATTN_WIDTH = 512
N_HEADS = 4
V_DIM = 128
QK_DIM = 64
HEAD_COLS = 2 * QK_DIM
ROPE_BASE = 10000.0
ROPE_AXIS_DIM = 32
ROPE_HALF = ROPE_AXIS_DIM // 2
CHUNK = 128
SGU_WIDTH = 256
QK_WIDTH = 512
IN_WIDTH = 2304
MIX_WIDTH = 1024
D_FF = 2816
N_MOD = 6
EPS = 1e-6
PAST_LEN = 256

C_POOL, C_Q, C_K, C_V, C_UV = 0, 256, 768, 1280, 1792
Y_POOL, Y_ATTN, Y_SGU = 0, 256, 768

ROWS = 1024
ROW_CHUNK = 256
ATTN_LOOKAHEAD = 2
CHUNKS_PER_BLOCK = 2
FFN_ROWS = 512
MXU_COLS = 256
FF_SPLIT = (D_FF // MXU_COLS + 1) // 2 * MXU_COLS
COND_ROWS = 16
ADA_COLS = 1024
VMEM_LIMIT = 58 * 1024 * 1024

BF16 = jnp.bfloat16
F32 = jnp.float32


def _dot(a, b):
    return jnp.dot(a, b, preferred_element_type=F32)


def _sigmoid(x):
    return 1.0 / (1.0 + jnp.exp(-x))


def _rms_mod(x, g, scale, shift):
    ms = jnp.mean(x * x, axis=-1, keepdims=True)
    y = x * lax.rsqrt(ms + EPS) * g
    return y * (1.0 + scale) + shift


def _gelu_tanh(x):
    c = math.sqrt(2.0 / math.pi)
    return x * (0.5 * (1.0 + jnp.tanh(c * (x + 0.044715 * (x * x * x)))))


def _const_spec(shape):
    zeros = (0,) * len(shape)
    return pl.BlockSpec(shape, lambda *_: zeros, pipeline_mode=pl.Buffered(1))


def _layer_spec(shape, layer):
    index = (layer,) + (0,) * len(shape)
    return pl.BlockSpec((None,) + tuple(shape), lambda *_: index, pipeline_mode=pl.Buffered(1))


def _ada_kernel(cond_ref, w_ref, b_ref, out_ref):
    cond = cond_ref[...]
    s = (cond * _sigmoid(cond)).astype(BF16)
    out_ref[...] = _dot(s, w_ref[...].astype(BF16)) + b_ref[...]


def _ada_modulation(cond, w_ada, b_ada):
    n_cols = N_MOD * D_MODEL
    return pl.pallas_call(
        _ada_kernel,
        out_shape=jax.ShapeDtypeStruct((DEPTH, COND_ROWS, n_cols), F32),
        grid=(DEPTH, n_cols // ADA_COLS),
        in_specs=[
            pl.BlockSpec((COND_ROWS, D_MODEL), lambda l, j: (0, 0)),
            pl.BlockSpec((None, D_MODEL, ADA_COLS), lambda l, j: (l, 0, j)),
            pl.BlockSpec((None, 1, ADA_COLS), lambda l, j: (l, 0, j)),
        ],
        out_specs=pl.BlockSpec((None, COND_ROWS, ADA_COLS), lambda l, j: (l, 0, j)),
        compiler_params=pltpu.CompilerParams(
            dimension_semantics=("arbitrary", "arbitrary"), vmem_limit_bytes=VMEM_LIMIT),
        name="ada_modulation",
    )(cond, w_ada, b_ada.reshape(DEPTH, 1, n_cols))


def _mixer_body(r, x_ref, cnt_ref, pool_scr, k_scr, v_scr, *, nb, seq, has_ctx, lam_init, cpb):
    kv_base = PAST_LEN if has_ctx else 0
    shift1, scale1, gate1 = r.mod[0:1, :], r.mod[1:2, :], r.mod[2:3, :]

    lane_head = lax.broadcasted_iota(jnp.int32, (ROW_CHUNK, HEAD_COLS), 1)
    first_half = lane_head < QK_DIM

    for b in range(nb):
        pool_scr[b, 0:POOL_HALO, :] = jnp.zeros((POOL_HALO, POOL_WIDTH), F32)
        pool_scr[b, POOL_HALO + seq:POOL_HALO + seq + POOL_HALO, :] = jnp.zeros((POOL_HALO, POOL_WIDTH), F32)
    if not has_ctx:
        for slot in range(r.kc_out.shape[1]):
            if slot != r.cache_slot:
                r.kc_out[:, slot] = jnp.zeros((nb,) + r.kc_out.shape[2:], F32)
                r.vc_out[:, slot] = jnp.zeros((nb,) + r.vc_out.shape[2:], F32)
    if has_ctx:
        for hd in range(N_HEADS):
            cols = slice(hd * HEAD_COLS, (hd + 1) * HEAD_COLS)
            k_scr[0, cols, 0:PAST_LEN] = r.ck[pl.ds(hd, PAST_LEN, stride=N_HEADS), :].T.astype(BF16)
            v_scr[0, 0:PAST_LEN, cols] = r.cv[pl.ds(hd, PAST_LEN, stride=N_HEADS), :].astype(BF16)
        lane_q = lax.broadcasted_iota(jnp.int32, (ROW_CHUNK, QK_WIDTH), 1)
        rope_low = (lane_q % ROPE_AXIS_DIM) < ROPE_HALF

        def rope(t, pos0):
            partner = jnp.where(rope_low,
                                pltpu.roll(t, QK_WIDTH - ROPE_HALF, 1),
                                pltpu.roll(t, ROPE_HALF, 1))
            cos = r.rope_c[pos0:pos0 + ROW_CHUNK, :]
            sin = r.rope_s[pos0:pos0 + ROW_CHUNK, :]
            cos = jnp.concatenate([cos] * N_HEADS, axis=1)
            sin = jnp.concatenate([sin] * N_HEADS, axis=1)
            return t * cos + partner * sin

    for c in range(ROWS // ROW_CHUNK):
        r0 = c * ROW_CHUNK
        b, pos0 = r0 // seq, r0 % seq
        rows = slice(r0, r0 + ROW_CHUNK)
        h = _rms_mod(x_ref[rows, :], r.g1[...], scale1, shift1).astype(BF16)

        pool_scr[b, POOL_HALO + pos0:POOL_HALO + pos0 + ROW_CHUNK, :] = _dot(h, r.w_in[:, C_POOL:C_Q])

        q = _dot(h, r.w_in[:, C_Q:C_K])
        k = _dot(h, r.w_in[:, C_K:C_V])
        if has_ctx:
            q = rope(q, pos0)
            k = rope(k, pos0)
        else:
            for hd in range(N_HEADS):
                r.kc_out[b, r.cache_slot, pl.ds(N_HEADS * pos0 + hd, ROW_CHUNK, stride=N_HEADS), :] = (
                    k[:, hd * HEAD_COLS:(hd + 1) * HEAD_COLS])
        q = q * (QK_DIM ** -0.5 * math.log2(math.e))
        for hd in range(N_HEADS):
            cols = slice(hd * HEAD_COLS, (hd + 1) * HEAD_COLS)
            qh = q[:, cols]
            r.q_scr[c, 0:ROW_CHUNK, cols] = jnp.where(first_half, qh, 0.0).astype(BF16)
            r.q_scr[c, ROW_CHUNK:2 * ROW_CHUNK, cols] = jnp.where(first_half, 0.0, qh).astype(BF16)
        k_scr[b, :, kv_base + pos0:kv_base + pos0 + ROW_CHUNK] = k.T.astype(BF16)

        v = _dot(h, r.w_in[:, C_V:C_UV])
        if not has_ctx:
            for hd in range(N_HEADS):
                r.vc_out[b, r.cache_slot, pl.ds(N_HEADS * pos0 + hd, ROW_CHUNK, stride=N_HEADS), :] = (
                    v[:, hd * V_DIM:(hd + 1) * V_DIM])
        v_scr[b, kv_base + pos0:kv_base + pos0 + ROW_CHUNK, :] = v.astype(BF16)

        uv = _gelu_tanh(_dot(h, r.w_in[:, C_UV:IN_WIDTH]))
        r.u_scr[rows, :] = uv[:, :SGU_WIDTH]
        vv = uv[:, SGU_WIDTH:]
        mu = jnp.mean(vv, axis=-1, keepdims=True)
        vc = vv - mu
        vn = vc * lax.rsqrt(jnp.mean(vc * vc, axis=-1, keepdims=True) + EPS) * r.sgu_g[...]
        r.vs_scr[rows, :] = vn.astype(BF16)

    lane_pool = lax.broadcasted_iota(jnp.int32, (ROW_CHUNK, 2 * POOL_GROUP), 1)
    narrow = lane_pool < POOL_GROUP
    for c in range(ROWS // ROW_CHUNK):
        r0 = c * ROW_CHUNK
        b, pos0 = r0 // seq, r0 % seq
        base = POOL_HALO + pos0
        pooled = []
        for j, (w_small, w_big) in enumerate(((POOL_WINDOWS[0], POOL_WINDOWS[1]),
                                              (POOL_WINDOWS[2], POOL_WINDOWS[3]))):
            cols = slice(j * 2 * POOL_GROUP, (j + 1) * 2 * POOL_GROUP)
            s_small = jnp.zeros((ROW_CHUNK, 2 * POOL_GROUP), F32)
            s_rest = jnp.zeros((ROW_CHUNK, 2 * POOL_GROUP), F32)
            for d in range(-(w_big // 2), w_big - w_big // 2):
                t = pool_scr[b, base + d:base + d + ROW_CHUNK, cols]
                if -(w_small // 2) <= d < w_small - w_small // 2:
                    s_small = s_small + t
                else:
                    s_rest = s_rest + t
            win_sum = jnp.where(narrow, s_small, s_small + s_rest)
            centre = pool_scr[b, base:base + ROW_CHUNK, cols]
            pooled.append(win_sum / cnt_ref[pos0:pos0 + ROW_CHUNK, cols] - centre)
        pooled = jnp.concatenate(pooled, axis=1).astype(BF16)
        y_a = _dot(pooled, r.wpool[...]) * r.pscale[...]
        r.y_scr[r0:r0 + ROW_CHUNK, Y_POOL:Y_ATTN] = y_a.astype(BF16)

    lq1, lk1, lq2, lk2 = r.lam[0:1, :], r.lam[1:2, :], r.lam[2:3, :], r.lam[3:4, :]
    lam = (jnp.exp(jnp.sum(lq1 * lk1, axis=-1, keepdims=True))
           - jnp.exp(jnp.sum(lq2 * lk2, axis=-1, keepdims=True)) + lam_init)
    n_qb = seq // ROW_CHUNK

    lane_sgu = lax.broadcasted_iota(jnp.int32, (CHUNK, 2 * POOL_GROUP), 1)
    sgu_first = lane_sgu < (SGU_WIDTH // 4)
    out_cols = MIX_WIDTH // N_HEADS

    def out_proj(rows, part):
        cols = slice(part * out_cols, (part + 1) * out_cols)
        y = _dot(r.y_scr[rows, :], r.w_out[:, cols])
        r.out[rows, cols] = x_ref[rows, cols] + gate1[:, cols] * y

    def block(blk, with_prev):
        static = isinstance(blk, int)

        def row0(chunk):
            return chunk * ROW_CHUNK if static else pl.multiple_of(chunk * ROW_CHUNK, ROW_CHUNK)

        chunks = [blk * cpb + j for j in range(cpb)]

        for idx in chunks:
            for half in range(ROW_CHUNK // CHUNK):
                sub = pl.ds(row0(idx) + half * CHUNK, CHUNK)
                for j in range(2):
                    cols = slice(j * 128, (j + 1) * 128)
                    t = _dot(r.wsgu[j], r.vs_scr[sub, cols])
                    mixed = jnp.where(sgu_first, t[:CHUNK, :], t[CHUNK:, :]) + r.bsgu[:, cols]
                    y_c = r.u_scr[sub, cols] * mixed
                    r.y_scr[sub, Y_SGU + j * 128:Y_SGU + (j + 1) * 128] = y_c.astype(BF16)

        units = [(idx, hd) for idx in chunks for hd in range(N_HEADS)]
        prev = [(pl.ds(row0(idx - cpb), ROW_CHUNK), part) for idx in chunks for part in range(N_HEADS)]

        def scores(idx, hd):
            cols = slice(hd * HEAD_COLS, (hd + 1) * HEAD_COLS)
            s = _dot(r.q_scr[idx, :, cols], k_scr[idx // n_qb, cols, :])
            return s[:ROW_CHUNK], s[ROW_CHUNK:]

        pending = [scores(*u) for u in units[:ATTN_LOOKAHEAD]]
        for n, (idx, hd) in enumerate(units):
            cols = slice(hd * HEAD_COLS, (hd + 1) * HEAD_COLS)
            s1, s2 = pending.pop(0)
            if n + ATTN_LOOKAHEAD < len(units):
                pending.append(scores(*units[n + ATTN_LOOKAHEAD]))
            e1 = jnp.exp2(s1 - jnp.max(s1, axis=-1, keepdims=True))
            e2 = jnp.exp2(s2 - jnp.max(s2, axis=-1, keepdims=True))
            l1 = jnp.sum(e1, axis=-1, keepdims=True)
            l2 = jnp.sum(e2, axis=-1, keepdims=True)
            p = (e1 - e2 * (lam * l1 / l2)).astype(BF16)
            o = _dot(p, v_scr[idx // n_qb, :, cols]) * (1.0 / l1)
            if with_prev:
                out_proj(*prev[n])
            o = o * lax.rsqrt(jnp.mean(o * o, axis=-1, keepdims=True) + EPS) * r.subln[...]
            o = o * (1.0 - lam_init)
            r.y_scr[pl.ds(row0(idx), ROW_CHUNK), Y_ATTN + hd * V_DIM:Y_ATTN + (hd + 1) * V_DIM] = o.astype(BF16)

    n_blocks = ROWS // (ROW_CHUNK * cpb)
    block(0, False)
    if n_blocks == 2:
        block(1, True)
    else:
        def loop_block(blk, carry):
            block(blk, True)
            return carry

        lax.fori_loop(1, n_blocks, loop_block, 0)
    for idx in range((n_blocks - 1) * cpb, n_blocks * cpb):
        for part in range(N_HEADS):
            out_proj(pl.ds(idx * ROW_CHUNK, ROW_CHUNK), part)


def _mixer_kernel(*refs, nb, seq, has_ctx, lam_init, cache_slot, n_passthrough, n_cast):
    it = iter(refs)
    r = types.SimpleNamespace(cache_slot=cache_slot)
    x_ref, r.mod, r.g1, r.w_in = next(it), next(it), next(it), next(it)
    if has_ctx:
        r.rope_c, r.rope_s, r.ck, r.cv = next(it), next(it), next(it), next(it)
    (r.lam, r.subln, cnt_ref, r.wpool, r.pscale, r.sgu_g, r.wsgu, r.bsgu, r.w_out) = (next(it) for _ in range(9))
    for _ in range(n_passthrough):
        next(it)
    cast_in = [next(it) for _ in range(n_cast)]
    r.out = next(it)
    if not has_ctx:
        r.kc_out, r.vc_out = next(it), next(it)
    cast_out = [next(it) for _ in range(n_cast)]
    pool_scr, r.q_scr, k_scr, v_scr, r.u_scr, r.vs_scr, r.y_scr = (next(it) for _ in range(7))
    for src, dst in zip(cast_in, cast_out):
        dst[...] = src[...].astype(BF16)
    _mixer_body(r, x_ref, cnt_ref, pool_scr, k_scr, v_scr, nb=nb, seq=seq, has_ctx=has_ctx, lam_init=lam_init,
                cpb=CHUNKS_PER_BLOCK)


def _mixer_call(x2d, mod4, layer, seq, has_ctx, lam_init, params, big, ctx=None, new_cache=None, cast_f32=()):
    n_rows = x2d.shape[0]
    nb = ROWS // seq
    n_seq = n_rows // seq
    kv_len = seq + (PAST_LEN if has_ctx else 0)

    if has_ctx:
        mod_map = lambda i: (layer, 1 + i, 0, 0)
    else:
        mod_map = lambda i: (layer, 0, 0, 0)

    operands = [x2d, mod4, params["norm1_g"], big["w_in"]]
    in_specs = [
        pl.BlockSpec((ROWS, D_MODEL), lambda i: (i, 0)),
        pl.BlockSpec((None, None, N_MOD, D_MODEL), mod_map),
        _layer_spec((1, D_MODEL), layer),
        _const_spec((D_MODEL, IN_WIDTH)),
    ]
    if has_ctx:
        operands += list(ctx)
        in_specs += [
            _const_spec((seq, HEAD_COLS)),
            _const_spec((seq, HEAD_COLS)),
            pl.BlockSpec((None, None, PAST_LEN * N_HEADS, HEAD_COLS), lambda i: (i, layer, 0, 0)),
            pl.BlockSpec((None, None, PAST_LEN * N_HEADS, V_DIM), lambda i: (i, layer, 0, 0)),
        ]
    operands += [params["lam"], params["subln_g"], _pool_counts(seq), params["w_pool"], params["pool_scale"],
                 params["sgu_norm_g"], params["w_sgu"], params["b_sgu"], big["w_out"]]
    in_specs += [
        _layer_spec((8, 128), layer),
        _layer_spec((1, V_DIM), layer),
        _const_spec((seq, POOL_WIDTH)),
        _layer_spec((POOL_WIDTH, POOL_WIDTH), layer),
        _layer_spec((1, POOL_WIDTH), layer),
        _layer_spec((1, SGU_WIDTH), layer),
        _layer_spec((2, 2 * CHUNK, CHUNK), layer),
        _layer_spec((CHUNK, SGU_WIDTH), layer),
        _const_spec((MIX_WIDTH, D_MODEL)),
    ]

    out_shape = [jax.ShapeDtypeStruct((n_rows, D_MODEL), F32)]
    out_specs = [pl.BlockSpec((ROWS, D_MODEL), lambda i: (i, 0))]
    aliases = {}
    cache_slot = 0
    if not has_ctx:
        out_shape += [jax.ShapeDtypeStruct((n_seq, DEPTH, seq * N_HEADS, HEAD_COLS), F32),
                      jax.ShapeDtypeStruct((n_seq, DEPTH, seq * N_HEADS, V_DIM), F32)]
        if new_cache is None:
            cache_slot = layer
            out_specs += [pl.BlockSpec((nb, DEPTH, seq * N_HEADS, HEAD_COLS), lambda i: (i, 0, 0, 0)),
                          pl.BlockSpec((nb, DEPTH, seq * N_HEADS, V_DIM), lambda i: (i, 0, 0, 0))]
        else:
            out_specs += [pl.BlockSpec((nb, 1, seq * N_HEADS, HEAD_COLS), lambda i: (i, layer, 0, 0)),
                          pl.BlockSpec((nb, 1, seq * N_HEADS, V_DIM), lambda i: (i, layer, 0, 0))]
            aliases = {len(operands): 1, len(operands) + 1: 2}
            operands += list(new_cache)
            in_specs += [pl.BlockSpec(memory_space=pl.ANY)] * 2
    n_steps = n_rows // ROWS
    for w in cast_f32:
        _, rows, cols = w.shape
        slab = rows // n_steps
        assert slab * n_steps == rows and slab % 16 == 0
        operands.append(w)
        in_specs.append(pl.BlockSpec((None, slab, cols), lambda i: (layer, i, 0)))
        out_shape.append(jax.ShapeDtypeStruct((rows, cols), BF16))
        out_specs.append(pl.BlockSpec((slab, cols), lambda i: (i, 0)))

    scratch = [
        pltpu.VMEM((nb, seq + 2 * POOL_HALO, POOL_WIDTH), F32),
        pltpu.VMEM((ROWS // ROW_CHUNK, 2 * ROW_CHUNK, QK_WIDTH), BF16),
        pltpu.VMEM((nb, QK_WIDTH, kv_len), BF16),
        pltpu.VMEM((nb, kv_len, ATTN_WIDTH), BF16),
        pltpu.VMEM((ROWS, SGU_WIDTH), F32),
        pltpu.VMEM((ROWS, SGU_WIDTH), BF16),
        pltpu.VMEM((ROWS, MIX_WIDTH), BF16),
    ]
    kernel = functools.partial(_mixer_kernel, nb=nb, seq=seq, has_ctx=has_ctx, lam_init=lam_init,
                               cache_slot=cache_slot, n_passthrough=len(aliases), n_cast=len(cast_f32))
    return pl.pallas_call(
        kernel,
        out_shape=out_shape,
        grid=(n_rows // ROWS,),
        in_specs=in_specs,
        out_specs=out_specs,
        scratch_shapes=scratch,
        input_output_aliases=aliases,
        compiler_params=pltpu.CompilerParams(
            dimension_semantics=("arbitrary",), vmem_limit_bytes=VMEM_LIMIT),
        name="mixer_ctx" if has_ctx else "mixer_prompt",
    )(*operands)


def _ffn_block(x_ref, out_ref, mod_ref, g2_ref, w1_ref, w2_ref, gf_ref, act_scr, final_norm):
    shift2, scale2, gate2 = mod_ref[3:4, :], mod_ref[4:5, :], mod_ref[5:6, :]
    x = x_ref[...]
    h = _rms_mod(x, g2_ref[...], scale2, shift2).astype(BF16)
    for lo, hi in ((0, FF_SPLIT), (FF_SPLIT, D_FF)):
        gate = _dot(h, w1_ref[:, lo:hi])
        up = _dot(h, w1_ref[:, D_FF + lo:D_FF + hi])
        act_scr[:, lo:hi] = (gate * _sigmoid(gate) * up).astype(BF16)
    y = x + gate2 * _dot(act_scr[...], w2_ref[...])
    if final_norm:
        ms = jnp.mean(y * y, axis=-1, keepdims=True)
        y = y * lax.rsqrt(ms + EPS) * gf_ref[...]
    out_ref[...] = y


def _ffn_kernel(*refs, n_prompt_steps, final_norm, n_cast):
    xp_ref, xs_ref, mod_ref, g2_ref, w1_ref, w2_ref, gf_ref = refs[:7]
    cast_in = refs[7:7 + n_cast]
    yp_ref, ys_ref = refs[7 + n_cast:9 + n_cast]
    cast_out = refs[9 + n_cast:9 + 2 * n_cast]
    act_scr = refs[9 + 2 * n_cast]
    shared = (mod_ref, g2_ref, w1_ref, w2_ref, gf_ref, act_scr, final_norm)
    is_prompt = pl.program_id(0) < n_prompt_steps

    @pl.when(is_prompt)
    def _():
        _ffn_block(xp_ref, yp_ref, *shared)

    @pl.when(jnp.logical_not(is_prompt))
    def _():
        for src, dst in zip(cast_in, cast_out):
            dst[...] = src[...].astype(BF16)
        _ffn_block(xs_ref, ys_ref, *shared)


def _ffn_call(xp, xs, mod4, layer, sample_seq, params, big, final_norm, next_f32=()):
    n_p, n_s = xp.shape[0] // FFN_ROWS, xs.shape[0] // FFN_ROWS
    steps_per_seq = sample_seq // FFN_ROWS

    def prompt_step(i):
        return jnp.minimum(i, n_p - 1)

    def sample_step(i):
        return jnp.maximum(i - n_p, 0)

    def mod_map(i):
        return (layer, jnp.where(i < n_p, 0, 1 + sample_step(i) // steps_per_seq), 0, 0)

    in_specs = [
        pl.BlockSpec((FFN_ROWS, D_MODEL), lambda i: (prompt_step(i), 0)),
        pl.BlockSpec((FFN_ROWS, D_MODEL), lambda i: (sample_step(i), 0)),
        pl.BlockSpec((None, None, N_MOD, D_MODEL), mod_map),
        _layer_spec((1, D_MODEL), layer),
        _const_spec((D_MODEL, 2 * D_FF)),
        _const_spec((D_FF, D_MODEL)),
        _const_spec((1, D_MODEL)),
    ]
    out_shape = [jax.ShapeDtypeStruct(xp.shape, F32), jax.ShapeDtypeStruct(xs.shape, F32)]
    out_specs = [pl.BlockSpec((FFN_ROWS, D_MODEL), lambda i: (prompt_step(i), 0)),
                 pl.BlockSpec((FFN_ROWS, D_MODEL), lambda i: (sample_step(i), 0))]
    for w in next_f32:
        _, rows, cols = w.shape
        slab = rows // n_s
        assert slab * n_s == rows and slab % 16 == 0
        in_specs.append(pl.BlockSpec((None, slab, cols), lambda i: (layer + 1, sample_step(i), 0)))
        out_shape.append(jax.ShapeDtypeStruct((rows, cols), BF16))
        out_specs.append(pl.BlockSpec((slab, cols), lambda i: (sample_step(i), 0)))

    kernel = functools.partial(_ffn_kernel, n_prompt_steps=n_p, final_norm=final_norm, n_cast=len(next_f32))
    return pl.pallas_call(
        kernel,
        out_shape=out_shape,
        grid=(n_p + n_s,),
        in_specs=in_specs,
        out_specs=out_specs,
        scratch_shapes=[pltpu.VMEM((FFN_ROWS, D_FF), BF16)],
        compiler_params=pltpu.CompilerParams(
            dimension_semantics=("arbitrary",), vmem_limit_bytes=VMEM_LIMIT),
        name="ffn",
    )(xp, xs, mod4, params["norm2_g"], big["w_ffn_in"], big["w_ffn_out"], params["final_g"], *next_f32)


def _rope_tables(seq):
    n_rows = seq // GRID_W
    rows = np.repeat(np.arange(n_rows), GRID_W).astype(np.float32)
    cols = np.tile(np.arange(GRID_W), n_rows).astype(np.float32)
    inv = 1.0 / (ROPE_BASE ** (np.arange(0, ROPE_AXIS_DIM, 2, dtype=np.float32) / ROPE_AXIS_DIM))
    ar, ac = rows[:, None] * inv[None], cols[:, None] * inv[None]
    cos_parts, sin_parts = [], []
    for ang in (ar, ac):
        cos_parts += [np.cos(ang), np.cos(ang)]
        sin_parts += [-np.sin(ang), np.sin(ang)]
    cos64 = np.concatenate(cos_parts, axis=1)
    sin64 = np.concatenate(sin_parts, axis=1)
    cos = np.concatenate([cos64, cos64], axis=1).astype(np.float32)
    sin = np.concatenate([sin64, sin64], axis=1).astype(np.float32)
    return jnp.asarray(cos), jnp.asarray(sin)


def _pool_counts(seq):
    t = np.arange(seq)
    cols = []
    for w in POOL_WINDOWS:
        lo = np.clip(t - w // 2, 0, seq)
        hi = np.clip(t + w - w // 2, 0, seq)
        cols.append(np.repeat((hi - lo).astype(np.float32)[:, None], POOL_GROUP, axis=1))
    return jnp.asarray(np.concatenate(cols, axis=1))


def _small_params(norm1_g, w_pool, pool_scale, lam_q1, lam_k1, lam_q2, lam_k2, subln_g,
                  sgu_norm_g, w_sgu, b_sgu, norm2_g, final_g):
    n_groups = len(POOL_WINDOWS)
    eye = jnp.eye(n_groups, dtype=F32)
    w_pool_bd = (eye[None, :, None, :, None] * w_pool[:, :, :, None, :]).reshape(DEPTH, POOL_WIDTH, POOL_WIDTH)
    lam = jnp.zeros((DEPTH, 8, 128), F32)
    lam = lam.at[:, 0, :QK_DIM].set(lam_q1).at[:, 1, :QK_DIM].set(lam_k1)
    lam = lam.at[:, 2, :QK_DIM].set(lam_q2).at[:, 3, :QK_DIM].set(lam_k2)
    return {
        "norm1_g": norm1_g[:, None, :],
        "lam": lam,
        "subln_g": subln_g[:, None, :],
        "w_pool": w_pool_bd.astype(BF16),
        "pool_scale": pool_scale[:, None, :],
        "sgu_norm_g": sgu_norm_g[:, None, :],
        "w_sgu": w_sgu.reshape(DEPTH, 2, 2 * CHUNK, CHUNK).astype(BF16),
        "b_sgu": jnp.repeat(jnp.swapaxes(b_sgu, 1, 2), SGU_WIDTH // 4, axis=2),
        "norm2_g": norm2_g[:, None, :],
        "final_g": final_g[None, :],
    }


BIG_WEIGHTS = ("w_in", "w_out", "w_ffn_in", "w_ffn_out")


def kernel(x_prompt, x_sample, cache_k, cache_v, c, c_ctx, norm1_g, w_ada, b_ada, w_in, w_pool, pool_scale, lam_q1, lam_k1, lam_q2, lam_k2, subln_g, sgu_norm_g, w_sgu, b_sgu, w_out, norm2_g, w_ffn_in, w_ffn_out, final_g):
    batch, seq, _ = x_prompt.shape
    dec_batch, dec_seq, _ = x_sample.shape
    assert ROWS % seq == 0 and dec_seq == ROWS and 1 + dec_batch <= COND_ROWS

    cond = jnp.zeros((COND_ROWS, D_MODEL), F32).at[0].set(c_ctx).at[1:1 + dec_batch].set(c)
    mod4 = _ada_modulation(cond, w_ada, b_ada).reshape(DEPTH, COND_ROWS, N_MOD, D_MODEL)

    ctx = _rope_tables(dec_seq) + (cache_k.reshape(dec_batch, DEPTH, PAST_LEN * N_HEADS, HEAD_COLS),
                                   cache_v.reshape(dec_batch, DEPTH, PAST_LEN * N_HEADS, V_DIM))
    params = _small_params(norm1_g, w_pool, pool_scale, lam_q1, lam_k1, lam_q2, lam_k2, subln_g,
                           sgu_norm_g, w_sgu, b_sgu, norm2_g, final_g)

    big_f32 = (w_in, w_out, w_ffn_in, w_ffn_out)
    big = {"w_in": w_in[0].astype(BF16), "w_out": w_out[0].astype(BF16)}

    xp = x_prompt.reshape(batch * seq, D_MODEL)
    xs = x_sample.reshape(dec_batch * dec_seq, D_MODEL)
    new_cache = None
    for l in range(DEPTH):
        lam_init = 0.8 - 0.6 * math.exp(-0.3 * l)
        last = l == DEPTH - 1

        xp, *new_cache = _mixer_call(xp, mod4, l, seq, False, lam_init, params, big, new_cache=new_cache)
        if l == 0:
            xs, big["w_ffn_in"], big["w_ffn_out"] = _mixer_call(
                xs, mod4, l, dec_seq, True, lam_init, params, big, ctx=ctx, cast_f32=(w_ffn_in, w_ffn_out))
        else:
            xs, = _mixer_call(xs, mod4, l, dec_seq, True, lam_init, params, big, ctx=ctx)
        xp, xs, *next_big = _ffn_call(xp, xs, mod4, l, dec_seq, params, big, last,
                                      next_f32=() if last else big_f32)
        big = dict(zip(BIG_WEIGHTS, next_big))

    y_prompt = xp.reshape(batch, seq, D_MODEL)
    y_sample = xs.reshape(dec_batch, dec_seq, D_MODEL)
    new_cache_k = new_cache[0].reshape(batch, DEPTH, seq, N_HEADS, 2 * QK_DIM)
    new_cache_v = new_cache[1].reshape(batch, DEPTH, seq, N_HEADS, V_DIM)
    return (y_prompt, y_sample, new_cache_k, new_cache_v)
```

```python
import functools
import math
import types

import numpy as np
import jax
import jax.numpy as jnp
from jax import lax
from jax.experimental import pallas as pl
from jax.experimental.pallas import tpu as pltpu

D_MODEL = 1024
DEPTH = 2
GRID_W = 64
POOL_WINDOWS = (2, 4, 8, 16)
POOL_WIDTH = 256
POOL_GROUP = 64
POOL_HALO = 8
ATTN_WIDTH = 512
N_HEADS = 4
V_DIM = 128
QK_DIM = 64
HEAD_COLS = 2 * QK_DIM
ROPE_BASE = 10000.0
ROPE_AXIS_DIM = 32
ROPE_HALF = ROPE_AXIS_DIM // 2
CHUNK = 128
SGU_WIDTH = 256
QK_WIDTH = 512
IN_WIDTH = 2304
MIX_WIDTH = 1024
D_FF = 2816
N_MOD = 6
EPS = 1e-6
PAST_LEN = 256

C_POOL, C_Q, C_K, C_V, C_UV = 0, 256, 768, 1280, 1792
Y_POOL, Y_ATTN, Y_SGU = 0, 256, 768

ROWS = 1024
PROJ_ROWS = 512
ROW_CHUNK = 256
ATTN_LOOKAHEAD = 2
CHUNKS_PER_BLOCK = 2
FFN_ROWS = 512
MXU_COLS = 256
FF_SPLIT = (D_FF // MXU_COLS + 1) // 2 * MXU_COLS
COND_ROWS = 16
ADA_COLS = 1024
VMEM_LIMIT = 58 * 1024 * 1024

BF16 = jnp.bfloat16
F32 = jnp.float32


def _dot(a, b):
    return jnp.dot(a, b, preferred_element_type=F32)


def _sigmoid(x):
    return 1.0 / (1.0 + jnp.exp(-x))


def _rms_mod(x, g, scale, shift):
    ms = jnp.mean(x * x, axis=-1, keepdims=True)
    y = x * lax.rsqrt(ms + EPS) * g
    return y * (1.0 + scale) + shift


def _gelu_tanh(x):
    c = math.sqrt(2.0 / math.pi)
    return x * (0.5 * (1.0 + jnp.tanh(c * (x + 0.044715 * (x * x * x)))))


def _const_spec(shape):
    zeros = (0,) * len(shape)
    return pl.BlockSpec(shape, lambda *_: zeros, pipeline_mode=pl.Buffered(1))


def _layer_spec(shape, layer):
    index = (layer,) + (0,) * len(shape)
    return pl.BlockSpec((None,) + tuple(shape), lambda *_: index, pipeline_mode=pl.Buffered(1))


def _ada_kernel(cond_ref, w_ref, b_ref, out_ref):
    cond = cond_ref[...]
    s = (cond * _sigmoid(cond)).astype(BF16)
    out_ref[...] = _dot(s, w_ref[...].astype(BF16)) + b_ref[...]


def _ada_modulation(cond, w_ada, b_ada):
    n_cols = N_MOD * D_MODEL
    return pl.pallas_call(
        _ada_kernel,
        out_shape=jax.ShapeDtypeStruct((DEPTH, COND_ROWS, n_cols), F32),
        grid=(DEPTH, n_cols // ADA_COLS),
        in_specs=[
            pl.BlockSpec((COND_ROWS, D_MODEL), lambda l, j: (0, 0)),
            pl.BlockSpec((None, D_MODEL, ADA_COLS), lambda l, j: (l, 0, j)),
            pl.BlockSpec((None, 1, ADA_COLS), lambda l, j: (l, 0, j)),
        ],
        out_specs=pl.BlockSpec((None, COND_ROWS, ADA_COLS), lambda l, j: (l, 0, j)),
        compiler_params=pltpu.CompilerParams(
            dimension_semantics=("arbitrary", "arbitrary"), vmem_limit_bytes=VMEM_LIMIT),
        name="ada_modulation",
    )(cond, w_ada, b_ada.reshape(DEPTH, 1, n_cols))


def _mixer_body(r, x_ref, cnt_ref, pool_scr, k_scr, v_scr, *, nb, seq, has_ctx, lam_init, cpb):
    kv_base = PAST_LEN if has_ctx else 0
    shift1, scale1, gate1 = r.mod[0:1, :], r.mod[1:2, :], r.mod[2:3, :]

    lane_head = lax.broadcasted_iota(jnp.int32, (ROW_CHUNK, HEAD_COLS), 1)
    first_half = lane_head < QK_DIM

    for b in range(nb):
        pool_scr[b, 0:POOL_HALO, :] = jnp.zeros((POOL_HALO, POOL_WIDTH), F32)
        pool_scr[b, POOL_HALO + seq:POOL_HALO + seq + POOL_HALO, :] = jnp.zeros((POOL_HALO, POOL_WIDTH), F32)
    if not has_ctx:
        for slot in range(r.kc_out.shape[1]):
            if slot != r.cache_slot:
                r.kc_out[:, slot] = jnp.zeros((nb,) + r.kc_out.shape[2:], F32)
                r.vc_out[:, slot] = jnp.zeros((nb,) + r.vc_out.shape[2:], F32)
    if has_ctx:
        for hd in range(N_HEADS):
            cols = slice(hd * HEAD_COLS, (hd + 1) * HEAD_COLS)
            k_scr[0, cols, 0:PAST_LEN] = r.ck[pl.ds(hd, PAST_LEN, stride=N_HEADS), :].T.astype(BF16)
            v_scr[0, 0:PAST_LEN, cols] = r.cv[pl.ds(hd, PAST_LEN, stride=N_HEADS), :].astype(BF16)
        lane_q = lax.broadcasted_iota(jnp.int32, (ROW_CHUNK, QK_WIDTH), 1)
        rope_low = (lane_q % ROPE_AXIS_DIM) < ROPE_HALF

        def rope(t, pos0):
            partner = jnp.where(rope_low,
                                pltpu.roll(t, QK_WIDTH - ROPE_HALF, 1),
                                pltpu.roll(t, ROPE_HALF, 1))
            cos = r.rope_c[pos0:pos0 + ROW_CHUNK, :]
            sin = r.rope_s[pos0:pos0 + ROW_CHUNK, :]
            cos = jnp.concatenate([cos] * N_HEADS, axis=1)
            sin = jnp.concatenate([sin] * N_HEADS, axis=1)
            return t * cos + partner * sin

    sub_chunks = PROJ_ROWS // ROW_CHUNK
    for c in range(ROWS // ROW_CHUNK):
        r0 = c * ROW_CHUNK
        b, pos0 = r0 // seq, r0 % seq
        rows = slice(r0, r0 + ROW_CHUNK)
        if c % sub_chunks == 0:
            h = _rms_mod(x_ref[r0:r0 + PROJ_ROWS, :], r.g1[...], scale1, shift1).astype(BF16)
            pool_b = _dot(h, r.w_in[:, C_POOL:C_Q])
            q_b = _dot(h, r.w_in[:, C_Q:C_K])
            k_b = _dot(h, r.w_in[:, C_K:C_V])
            v_b = _dot(h, r.w_in[:, C_V:C_UV])
            uv_b = _gelu_tanh(_dot(h, r.w_in[:, C_UV:IN_WIDTH]))
        part = slice((c % sub_chunks) * ROW_CHUNK, (c % sub_chunks + 1) * ROW_CHUNK)

        pool_scr[b, POOL_HALO + pos0:POOL_HALO + pos0 + ROW_CHUNK, :] = pool_b[part]

        q, k = q_b[part], k_b[part]
        if has_ctx:
            q = rope(q, pos0)
            k = rope(k, pos0)
        else:
            for hd in range(N_HEADS):
                r.kc_out[b, r.cache_slot, pl.ds(N_HEADS * pos0 + hd, ROW_CHUNK, stride=N_HEADS), :] = (
                    k[:, hd * HEAD_COLS:(hd + 1) * HEAD_COLS])
        q = q * (QK_DIM ** -0.5 * math.log2(math.e))
        for hd in range(N_HEADS):
            cols = slice(hd * HEAD_COLS, (hd + 1) * HEAD_COLS)
            qh = q[:, cols]
            r.q_scr[c, 0:ROW_CHUNK, cols] = jnp.where(first_half, qh, 0.0).astype(BF16)
            r.q_scr[c, ROW_CHUNK:2 * ROW_CHUNK, cols] = jnp.where(first_half, 0.0, qh).astype(BF16)
        k_scr[b, :, kv_base + pos0:kv_base + pos0 + ROW_CHUNK] = k.T.astype(BF16)

        v = v_b[part]
        if not has_ctx:
            for hd in range(N_HEADS):
                r.vc_out[b, r.cache_slot, pl.ds(N_HEADS * pos0 + hd, ROW_CHUNK, stride=N_HEADS), :] = (
                    v[:, hd * V_DIM:(hd + 1) * V_DIM])
        v_scr[b, kv_base + pos0:kv_base + pos0 + ROW_CHUNK, :] = v.astype(BF16)

        uv = uv_b[part]
        r.u_scr[rows, :] = uv[:, :SGU_WIDTH]
        vv = uv[:, SGU_WIDTH:]
        mu = jnp.mean(vv, axis=-1, keepdims=True)
        vc = vv - mu
        vn = vc * lax.rsqrt(jnp.mean(vc * vc, axis=-1, keepdims=True) + EPS) * r.sgu_g[...]
        r.vs_scr[rows, :] = vn.astype(BF16)

    lane_pool = lax.broadcasted_iota(jnp.int32, (ROW_CHUNK, 2 * POOL_GROUP), 1)
    narrow = lane_pool < POOL_GROUP
    for c in range(ROWS // ROW_CHUNK):
        r0 = c * ROW_CHUNK
        b, pos0 = r0 // seq, r0 % seq
        base = POOL_HALO + pos0
        pooled = []
        for j, (w_small, w_big) in enumerate(((POOL_WINDOWS[0], POOL_WINDOWS[1]),
                                              (POOL_WINDOWS[2], POOL_WINDOWS[3]))):
            cols = slice(j * 2 * POOL_GROUP, (j + 1) * 2 * POOL_GROUP)
            s_small = jnp.zeros((ROW_CHUNK, 2 * POOL_GROUP), F32)
            s_rest = jnp.zeros((ROW_CHUNK, 2 * POOL_GROUP), F32)
            for d in range(-(w_big // 2), w_big - w_big // 2):
                t = pool_scr[b, base + d:base + d + ROW_CHUNK, cols]
                if -(w_small // 2) <= d < w_small - w_small // 2:
                    s_small = s_small + t
                else:
                    s_rest = s_rest + t
            win_sum = jnp.where(narrow, s_small, s_small + s_rest)
            centre = pool_scr[b, base:base + ROW_CHUNK, cols]
            pooled.append(win_sum / cnt_ref[pos0:pos0 + ROW_CHUNK, cols] - centre)
        pooled = jnp.concatenate(pooled, axis=1).astype(BF16)
        y_a = _dot(pooled, r.wpool[...]) * r.pscale[...]
        r.y_scr[r0:r0 + ROW_CHUNK, Y_POOL:Y_ATTN] = y_a.astype(BF16)

    lq1, lk1, lq2, lk2 = r.lam[0:1, :], r.lam[1:2, :], r.lam[2:3, :], r.lam[3:4, :]
    lam = (jnp.exp(jnp.sum(lq1 * lk1, axis=-1, keepdims=True))
           - jnp.exp(jnp.sum(lq2 * lk2, axis=-1, keepdims=True)) + lam_init)
    n_qb = seq // ROW_CHUNK

    lane_sgu = lax.broadcasted_iota(jnp.int32, (CHUNK, 2 * POOL_GROUP), 1)
    sgu_first = lane_sgu < (SGU_WIDTH // 4)
    out_cols = MIX_WIDTH // N_HEADS

    def out_proj(rows, part):
        cols = slice(part * out_cols, (part + 1) * out_cols)
        y = _dot(r.y_scr[rows, :], r.w_out[:, cols])
        r.out[rows, cols] = x_ref[rows, cols] + gate1[:, cols] * y

    def block(blk, with_prev):
        static = isinstance(blk, int)

        def row0(chunk):
            return chunk * ROW_CHUNK if static else pl.multiple_of(chunk * ROW_CHUNK, ROW_CHUNK)

        chunks = [blk * cpb + j for j in range(cpb)]

        for idx in chunks:
            for half in range(ROW_CHUNK // CHUNK):
                sub = pl.ds(row0(idx) + half * CHUNK, CHUNK)
                for j in range(2):
                    cols = slice(j * 128, (j + 1) * 128)
                    t = _dot(r.wsgu[j], r.vs_scr[sub, cols])
                    mixed = jnp.where(sgu_first, t[:CHUNK, :], t[CHUNK:, :]) + r.bsgu[:, cols]
                    y_c = r.u_scr[sub, cols] * mixed
                    r.y_scr[sub, Y_SGU + j * 128:Y_SGU + (j + 1) * 128] = y_c.astype(BF16)

        units = [(idx, hd) for idx in chunks for hd in range(N_HEADS)]
        prev = [(pl.ds(row0(idx - cpb), ROW_CHUNK), part) for idx in chunks for part in range(N_HEADS)]

        def scores(idx, hd):
            cols = slice(hd * HEAD_COLS, (hd + 1) * HEAD_COLS)
            s = _dot(r.q_scr[idx, :, cols], k_scr[idx // n_qb, cols, :])
            return s[:ROW_CHUNK], s[ROW_CHUNK:]

        pending = [scores(*u) for u in units[:ATTN_LOOKAHEAD]]
        for n, (idx, hd) in enumerate(units):
            cols = slice(hd * HEAD_COLS, (hd + 1) * HEAD_COLS)
            s1, s2 = pending.pop(0)
            if n + ATTN_LOOKAHEAD < len(units):
                pending.append(scores(*units[n + ATTN_LOOKAHEAD]))
            e1 = jnp.exp2(s1 - jnp.max(s1, axis=-1, keepdims=True))
            e2 = jnp.exp2(s2 - jnp.max(s2, axis=-1, keepdims=True))
            l1 = jnp.sum(e1, axis=-1, keepdims=True)
            l2 = jnp.sum(e2, axis=-1, keepdims=True)
            p = (e1 - e2 * (lam * l1 / l2)).astype(BF16)
            o = _dot(p, v_scr[idx // n_qb, :, cols]) * (1.0 / l1)
            if with_prev:
                out_proj(*prev[n])
            o = o * lax.rsqrt(jnp.mean(o * o, axis=-1, keepdims=True) + EPS) * r.subln[...]
            o = o * (1.0 - lam_init)
            r.y_scr[pl.ds(row0(idx), ROW_CHUNK), Y_ATTN + hd * V_DIM:Y_ATTN + (hd + 1) * V_DIM] = o.astype(BF16)

    n_blocks = ROWS // (ROW_CHUNK * cpb)
    block(0, False)
    if n_blocks == 2:
        block(1, True)
    else:
        def loop_block(blk, carry):
            block(blk, True)
            return carry

        lax.fori_loop(1, n_blocks, loop_block, 0)
    for idx in range((n_blocks - 1) * cpb, n_blocks * cpb):
        for part in range(N_HEADS):
            out_proj(pl.ds(idx * ROW_CHUNK, ROW_CHUNK), part)


def _mixer_kernel(*refs, nb, seq, has_ctx, lam_init, cache_slot, n_passthrough, n_cast):
    it = iter(refs)
    r = types.SimpleNamespace(cache_slot=cache_slot)
    x_ref, r.mod, r.g1, r.w_in = next(it), next(it), next(it), next(it)
    if has_ctx:
        r.rope_c, r.rope_s, r.ck, r.cv = next(it), next(it), next(it), next(it)
    (r.lam, r.subln, cnt_ref, r.wpool, r.pscale, r.sgu_g, r.wsgu, r.bsgu, r.w_out) = (next(it) for _ in range(9))
    for _ in range(n_passthrough):
        next(it)
    cast_in = [next(it) for _ in range(n_cast)]
    r.out = next(it)
    if not has_ctx:
        r.kc_out, r.vc_out = next(it), next(it)
    cast_out = [next(it) for _ in range(n_cast)]
    pool_scr, r.q_scr, k_scr, v_scr, r.u_scr, r.vs_scr, r.y_scr = (next(it) for _ in range(7))
    for src, dst in zip(cast_in, cast_out):
        dst[...] = src[...].astype(BF16)
    _mixer_body(r, x_ref, cnt_ref, pool_scr, k_scr, v_scr, nb=nb, seq=seq, has_ctx=has_ctx, lam_init=lam_init,
                cpb=CHUNKS_PER_BLOCK)


def _mixer_call(x2d, mod4, layer, seq, has_ctx, lam_init, params, big, ctx=None, new_cache=None, cast_f32=()):
    n_rows = x2d.shape[0]
    nb = ROWS // seq
    n_seq = n_rows // seq
    kv_len = seq + (PAST_LEN if has_ctx else 0)

    if has_ctx:
        mod_map = lambda i: (layer, 1 + i, 0, 0)
    else:
        mod_map = lambda i: (layer, 0, 0, 0)

    operands = [x2d, mod4, params["norm1_g"], big["w_in"]]
    in_specs = [
        pl.BlockSpec((ROWS, D_MODEL), lambda i: (i, 0)),
        pl.BlockSpec((None, None, N_MOD, D_MODEL), mod_map),
        _layer_spec((1, D_MODEL), layer),
        _const_spec((D_MODEL, IN_WIDTH)),
    ]
    if has_ctx:
        operands += list(ctx)
        in_specs += [
            _const_spec((seq, HEAD_COLS)),
            _const_spec((seq, HEAD_COLS)),
            pl.BlockSpec((None, None, PAST_LEN * N_HEADS, HEAD_COLS), lambda i: (i, layer, 0, 0)),
            pl.BlockSpec((None, None, PAST_LEN * N_HEADS, V_DIM), lambda i: (i, layer, 0, 0)),
        ]
    operands += [params["lam"], params["subln_g"], _pool_counts(seq), params["w_pool"], params["pool_scale"],
                 params["sgu_norm_g"], params["w_sgu"], params["b_sgu"], big["w_out"]]
    in_specs += [
        _layer_spec((8, 128), layer),
        _layer_spec((1, V_DIM), layer),
        _const_spec((seq, POOL_WIDTH)),
        _layer_spec((POOL_WIDTH, POOL_WIDTH), layer),
        _layer_spec((1, POOL_WIDTH), layer),
        _layer_spec((1, SGU_WIDTH), layer),
        _layer_spec((2, 2 * CHUNK, CHUNK), layer),
        _layer_spec((CHUNK, SGU_WIDTH), layer),
        _const_spec((MIX_WIDTH, D_MODEL)),
    ]

    out_shape = [jax.ShapeDtypeStruct((n_rows, D_MODEL), F32)]
    out_specs = [pl.BlockSpec((ROWS, D_MODEL), lambda i: (i, 0))]
    aliases = {}
    cache_slot = 0
    if not has_ctx:
        out_shape += [jax.ShapeDtypeStruct((n_seq, DEPTH, seq * N_HEADS, HEAD_COLS), F32),
                      jax.ShapeDtypeStruct((n_seq, DEPTH, seq * N_HEADS, V_DIM), F32)]
        if new_cache is None:
            cache_slot = layer
            out_specs += [pl.BlockSpec((nb, DEPTH, seq * N_HEADS, HEAD_COLS), lambda i: (i, 0, 0, 0)),
                          pl.BlockSpec((nb, DEPTH, seq * N_HEADS, V_DIM), lambda i: (i, 0, 0, 0))]
        else:
            out_specs += [pl.BlockSpec((nb, 1, seq * N_HEADS, HEAD_COLS), lambda i: (i, layer, 0, 0)),
                          pl.BlockSpec((nb, 1, seq * N_HEADS, V_DIM), lambda i: (i, layer, 0, 0))]
            aliases = {len(operands): 1, len(operands) + 1: 2}
            operands += list(new_cache)
            in_specs += [pl.BlockSpec(memory_space=pl.ANY)] * 2
    n_steps = n_rows // ROWS
    for w in cast_f32:
        _, rows, cols = w.shape
        slab = rows // n_steps
        assert slab * n_steps == rows and slab % 16 == 0
        operands.append(w)
        in_specs.append(pl.BlockSpec((None, slab, cols), lambda i: (layer, i, 0)))
        out_shape.append(jax.ShapeDtypeStruct((rows, cols), BF16))
        out_specs.append(pl.BlockSpec((slab, cols), lambda i: (i, 0)))

    scratch = [
        pltpu.VMEM((nb, seq + 2 * POOL_HALO, POOL_WIDTH), F32),
        pltpu.VMEM((ROWS // ROW_CHUNK, 2 * ROW_CHUNK, QK_WIDTH), BF16),
        pltpu.VMEM((nb, QK_WIDTH, kv_len), BF16),
        pltpu.VMEM((nb, kv_len, ATTN_WIDTH), BF16),
        pltpu.VMEM((ROWS, SGU_WIDTH), F32),
        pltpu.VMEM((ROWS, SGU_WIDTH), BF16),
        pltpu.VMEM((ROWS, MIX_WIDTH), BF16),
    ]
    kernel = functools.partial(_mixer_kernel, nb=nb, seq=seq, has_ctx=has_ctx, lam_init=lam_init,
                               cache_slot=cache_slot, n_passthrough=len(aliases), n_cast=len(cast_f32))
    return pl.pallas_call(
        kernel,
        out_shape=out_shape,
        grid=(n_rows // ROWS,),
        in_specs=in_specs,
        out_specs=out_specs,
        scratch_shapes=scratch,
        input_output_aliases=aliases,
        compiler_params=pltpu.CompilerParams(
            dimension_semantics=("arbitrary",), vmem_limit_bytes=VMEM_LIMIT),
        name="mixer_ctx" if has_ctx else "mixer_prompt",
    )(*operands)


def _ffn_block(x_ref, out_ref, mod_ref, g2_ref, w1_ref, w2_ref, gf_ref, act_scr, final_norm):
    shift2, scale2, gate2 = mod_ref[3:4, :], mod_ref[4:5, :], mod_ref[5:6, :]
    x = x_ref[...]
    h = _rms_mod(x, g2_ref[...], scale2, shift2).astype(BF16)
    for lo, hi in ((0, FF_SPLIT), (FF_SPLIT, D_FF)):
        gate = _dot(h, w1_ref[:, lo:hi])
        up = _dot(h, w1_ref[:, D_FF + lo:D_FF + hi])
        act_scr[:, lo:hi] = (gate * _sigmoid(gate) * up).astype(BF16)
    y = x + gate2 * _dot(act_scr[...], w2_ref[...])
    if final_norm:
        ms = jnp.mean(y * y, axis=-1, keepdims=True)
        y = y * lax.rsqrt(ms + EPS) * gf_ref[...]
    out_ref[...] = y


def _ffn_kernel(*refs, n_prompt_steps, final_norm, n_cast):
    xp_ref, xs_ref, mod_ref, g2_ref, w1_ref, w2_ref, gf_ref = refs[:7]
    cast_in = refs[7:7 + n_cast]
    yp_ref, ys_ref = refs[7 + n_cast:9 + n_cast]
    cast_out = refs[9 + n_cast:9 + 2 * n_cast]
    act_scr = refs[9 + 2 * n_cast]
    shared = (mod_ref, g2_ref, w1_ref, w2_ref, gf_ref, act_scr, final_norm)
    is_prompt = pl.program_id(0) < n_prompt_steps

    @pl.when(is_prompt)
    def _():
        _ffn_block(xp_ref, yp_ref, *shared)

    @pl.when(jnp.logical_not(is_prompt))
    def _():
        for src, dst in zip(cast_in, cast_out):
            dst[...] = src[...].astype(BF16)
        _ffn_block(xs_ref, ys_ref, *shared)


def _ffn_call(xp, xs, mod4, layer, sample_seq, params, big, final_norm, next_f32=()):
    n_p, n_s = xp.shape[0] // FFN_ROWS, xs.shape[0] // FFN_ROWS
    steps_per_seq = sample_seq // FFN_ROWS

    def prompt_step(i):
        return jnp.minimum(i, n_p - 1)

    def sample_step(i):
        return jnp.maximum(i - n_p, 0)

    def mod_map(i):
        return (layer, jnp.where(i < n_p, 0, 1 + sample_step(i) // steps_per_seq), 0, 0)

    in_specs = [
        pl.BlockSpec((FFN_ROWS, D_MODEL), lambda i: (prompt_step(i), 0)),
        pl.BlockSpec((FFN_ROWS, D_MODEL), lambda i: (sample_step(i), 0)),
        pl.BlockSpec((None, None, N_MOD, D_MODEL), mod_map),
        _layer_spec((1, D_MODEL), layer),
        _const_spec((D_MODEL, 2 * D_FF)),
        _const_spec((D_FF, D_MODEL)),
        _const_spec((1, D_MODEL)),
    ]
    out_shape = [jax.ShapeDtypeStruct(xp.shape, F32), jax.ShapeDtypeStruct(xs.shape, F32)]
    out_specs = [pl.BlockSpec((FFN_ROWS, D_MODEL), lambda i: (prompt_step(i), 0)),
                 pl.BlockSpec((FFN_ROWS, D_MODEL), lambda i: (sample_step(i), 0))]
    for w in next_f32:
        _, rows, cols = w.shape
        slab = rows // n_s
        assert slab * n_s == rows and slab % 16 == 0
        in_specs.append(pl.BlockSpec((None, slab, cols), lambda i: (layer + 1, sample_step(i), 0)))
        out_shape.append(jax.ShapeDtypeStruct((rows, cols), BF16))
        out_specs.append(pl.BlockSpec((slab, cols), lambda i: (sample_step(i), 0)))

    kernel = functools.partial(_ffn_kernel, n_prompt_steps=n_p, final_norm=final_norm, n_cast=len(next_f32))
    return pl.pallas_call(
        kernel,
        out_shape=out_shape,
        grid=(n_p + n_s,),
        in_specs=in_specs,
        out_specs=out_specs,
        scratch_shapes=[pltpu.VMEM((FFN_ROWS, D_FF), BF16)],
        compiler_params=pltpu.CompilerParams(
            dimension_semantics=("arbitrary",), vmem_limit_bytes=VMEM_LIMIT),
        name="ffn",
    )(xp, xs, mod4, params["norm2_g"], big["w_ffn_in"], big["w_ffn_out"], params["final_g"], *next_f32)


def _rope_tables(seq):
    n_rows = seq // GRID_W
    rows = np.repeat(np.arange(n_rows), GRID_W).astype(np.float32)
    cols = np.tile(np.arange(GRID_W), n_rows).astype(np.float32)
    inv = 1.0 / (ROPE_BASE ** (np.arange(0, ROPE_AXIS_DIM, 2, dtype=np.float32) / ROPE_AXIS_DIM))
    ar, ac = rows[:, None] * inv[None], cols[:, None] * inv[None]
    cos_parts, sin_parts = [], []
    for ang in (ar, ac):
        cos_parts += [np.cos(ang), np.cos(ang)]
        sin_parts += [-np.sin(ang), np.sin(ang)]
    cos64 = np.concatenate(cos_parts, axis=1)
    sin64 = np.concatenate(sin_parts, axis=1)
    cos = np.concatenate([cos64, cos64], axis=1).astype(np.float32)
    sin = np.concatenate([sin64, sin64], axis=1).astype(np.float32)
    return jnp.asarray(cos), jnp.asarray(sin)


def _pool_counts(seq):
    t = np.arange(seq)
    cols = []
    for w in POOL_WINDOWS:
        lo = np.clip(t - w // 2, 0, seq)
        hi = np.clip(t + w - w // 2, 0, seq)
        cols.append(np.repeat((hi - lo).astype(np.float32)[:, None], POOL_GROUP, axis=1))
    return jnp.asarray(np.concatenate(cols, axis=1))


def _small_params(norm1_g, w_pool, pool_scale, lam_q1, lam_k1, lam_q2, lam_k2, subln_g,
                  sgu_norm_g, w_sgu, b_sgu, norm2_g, final_g):
    n_groups = len(POOL_WINDOWS)
    eye = jnp.eye(n_groups, dtype=F32)
    w_pool_bd = (eye[None, :, None, :, None] * w_pool[:, :, :, None, :]).reshape(DEPTH, POOL_WIDTH, POOL_WIDTH)
    lam = jnp.zeros((DEPTH, 8, 128), F32)
    lam = lam.at[:, 0, :QK_DIM].set(lam_q1).at[:, 1, :QK_DIM].set(lam_k1)
    lam = lam.at[:, 2, :QK_DIM].set(lam_q2).at[:, 3, :QK_DIM].set(lam_k2)
    return {
        "norm1_g": norm1_g[:, None, :],
        "lam": lam,
        "subln_g": subln_g[:, None, :],
        "w_pool": w_pool_bd.astype(BF16),
        "pool_scale": pool_scale[:, None, :],
        "sgu_norm_g": sgu_norm_g[:, None, :],
        "w_sgu": w_sgu.reshape(DEPTH, 2, 2 * CHUNK, CHUNK).astype(BF16),
        "b_sgu": jnp.repeat(jnp.swapaxes(b_sgu, 1, 2), SGU_WIDTH // 4, axis=2),
        "norm2_g": norm2_g[:, None, :],
        "final_g": final_g[None, :],
    }


BIG_WEIGHTS = ("w_in", "w_out", "w_ffn_in", "w_ffn_out")


def kernel(x_prompt, x_sample, cache_k, cache_v, c, c_ctx, norm1_g, w_ada, b_ada, w_in, w_pool, pool_scale, lam_q1, lam_k1, lam_q2, lam_k2, subln_g, sgu_norm_g, w_sgu, b_sgu, w_out, norm2_g, w_ffn_in, w_ffn_out, final_g):
    batch, seq, _ = x_prompt.shape
    dec_batch, dec_seq, _ = x_sample.shape
    assert ROWS % seq == 0 and dec_seq == ROWS and 1 + dec_batch <= COND_ROWS

    cond = jnp.zeros((COND_ROWS, D_MODEL), F32).at[0].set(c_ctx).at[1:1 + dec_batch].set(c)
    mod4 = _ada_modulation(cond, w_ada, b_ada).reshape(DEPTH, COND_ROWS, N_MOD, D_MODEL)

    ctx = _rope_tables(dec_seq) + (cache_k.reshape(dec_batch, DEPTH, PAST_LEN * N_HEADS, HEAD_COLS),
                                   cache_v.reshape(dec_batch, DEPTH, PAST_LEN * N_HEADS, V_DIM))
    params = _small_params(norm1_g, w_pool, pool_scale, lam_q1, lam_k1, lam_q2, lam_k2, subln_g,
                           sgu_norm_g, w_sgu, b_sgu, norm2_g, final_g)

    big_f32 = (w_in, w_out, w_ffn_in, w_ffn_out)
    big = {"w_in": w_in[0].astype(BF16), "w_out": w_out[0].astype(BF16)}

    xp = x_prompt.reshape(batch * seq, D_MODEL)
    xs = x_sample.reshape(dec_batch * dec_seq, D_MODEL)
    new_cache = None
    for l in range(DEPTH):
        lam_init = 0.8 - 0.6 * math.exp(-0.3 * l)
        last = l == DEPTH - 1

        xp, *new_cache = _mixer_call(xp, mod4, l, seq, False, lam_init, params, big, new_cache=new_cache)
        if l == 0:
            xs, big["w_ffn_in"], big["w_ffn_out"] = _mixer_call(
                xs, mod4, l, dec_seq, True, lam_init, params, big, ctx=ctx, cast_f32=(w_ffn_in, w_ffn_out))
        else:
            xs, = _mixer_call(xs, mod4, l, dec_seq, True, lam_init, params, big, ctx=ctx)
        xp, xs, *next_big = _ffn_call(xp, xs, mod4, l, dec_seq, params, big, last,
                                      next_f32=() if last else big_f32)
        big = dict(zip(BIG_WEIGHTS, next_big))

    y_prompt = xp.reshape(batch, seq, D_MODEL)
    y_sample = xs.reshape(dec_batch, dec_seq, D_MODEL)
    new_cache_k = new_cache[0].reshape(batch, DEPTH, seq, N_HEADS, 2 * QK_DIM)
    new_cache_v = new_cache[1].reshape(batch, DEPTH, seq, N_HEADS, V_DIM)
    return (y_prompt, y_sample, new_cache_k, new_cache_v)
```

```python
import functools
import math
import types

import numpy as np
import jax
import jax.numpy as jnp
from jax import lax
from jax.experimental import pallas as pl
from jax.experimental.pallas import tpu as pltpu

D_MODEL = 1024
DEPTH = 2
GRID_W = 64
POOL_WINDOWS = (2, 4, 8, 16)
POOL_WIDTH = 256
POOL_GROUP = 64
POOL_HALO = 8
ATTN_WIDTH = 512
N_HEADS = 4
V_DIM = 128
QK_DIM = 64
HEAD_COLS = 2 * QK_DIM
ROPE_BASE = 10000.0
ROPE_AXIS_DIM = 32
ROPE_HALF = ROPE_AXIS_DIM // 2
CHUNK = 128
SGU_WIDTH = 256
QK_WIDTH = 512
IN_WIDTH = 2304
MIX_WIDTH = 1024
D_FF = 2816
N_MOD = 6
EPS = 1e-6
PAST_LEN = 256

C_POOL, C_Q, C_K, C_V, C_UV = 0, 256, 768, 1280, 1792
Y_POOL, Y_ATTN, Y_SGU = 0, 256, 768

ROWS = 1024
ROW_CHUNK = 256
ATTN_LOOKAHEAD = 2
CHUNKS_PER_BLOCK = 2
FFN_ROWS = 512
MXU_COLS = 256
FF_SPLIT = (D_FF // MXU_COLS + 1) // 2 * MXU_COLS
COND_ROWS = 16
ADA_COLS = 3072
VMEM_LIMIT = 58 * 1024 * 1024

BF16 = jnp.bfloat16
F32 = jnp.float32


def _dot(a, b):
    return jnp.dot(a, b, preferred_element_type=F32)


def _sigmoid(x):
    return 1.0 / (1.0 + jnp.exp(-x))


def _rms_mod(x, g, scale, shift):
    ms = jnp.mean(x * x, axis=-1, keepdims=True)
    return x * lax.rsqrt(ms + EPS) * (g * (1.0 + scale)) + shift


def _gelu_tanh(x):
    c = math.sqrt(2.0 / math.pi)
    return x * (0.5 * (1.0 + jnp.tanh(c * (x + 0.044715 * (x * x * x)))))


def _const_spec(shape):
    zeros = (0,) * len(shape)
    return pl.BlockSpec(shape, lambda *_: zeros, pipeline_mode=pl.Buffered(1))


def _layer_spec(shape, layer):
    index = (layer,) + (0,) * len(shape)
    return pl.BlockSpec((None,) + tuple(shape), lambda *_: index, pipeline_mode=pl.Buffered(1))


def _ada_kernel(cond_ref, w_ref, b_ref, out_ref):
    cond = cond_ref[...]
    s = (cond * _sigmoid(cond)).astype(BF16)
    out_ref[...] = _dot(s, w_ref[...].astype(BF16)) + b_ref[...]


def _ada_modulation(cond, w_ada, b_ada):
    n_cols = N_MOD * D_MODEL
    return pl.pallas_call(
        _ada_kernel,
        out_shape=jax.ShapeDtypeStruct((DEPTH, COND_ROWS, n_cols), F32),
        grid=(DEPTH, n_cols // ADA_COLS),
        in_specs=[
            pl.BlockSpec((COND_ROWS, D_MODEL), lambda l, j: (0, 0)),
            pl.BlockSpec((None, D_MODEL, ADA_COLS), lambda l, j: (l, 0, j)),
            pl.BlockSpec((None, 1, ADA_COLS), lambda l, j: (l, 0, j)),
        ],
        out_specs=pl.BlockSpec((None, COND_ROWS, ADA_COLS), lambda l, j: (l, 0, j)),
        compiler_params=pltpu.CompilerParams(
            dimension_semantics=("arbitrary", "arbitrary"), vmem_limit_bytes=VMEM_LIMIT),
        name="ada_modulation",
    )(cond, w_ada, b_ada.reshape(DEPTH, 1, n_cols))


def _mixer_body(r, x_ref, cnt_ref, pool_scr, k_scr, v_scr, *, nb, seq, has_ctx, lam_init, cpb):
    kv_base = PAST_LEN if has_ctx else 0
    shift1, scale1, gate1 = r.mod[0:1, :], r.mod[1:2, :], r.mod[2:3, :]

    lane_head = lax.broadcasted_iota(jnp.int32, (ROW_CHUNK, HEAD_COLS), 1)
    first_half = lane_head < QK_DIM

    for b in range(nb):
        pool_scr[b, 0:POOL_HALO, :] = jnp.zeros((POOL_HALO, POOL_WIDTH), F32)
        pool_scr[b, POOL_HALO + seq:POOL_HALO + seq + POOL_HALO, :] = jnp.zeros((POOL_HALO, POOL_WIDTH), F32)
    if not has_ctx:
        for slot in range(r.kc_out.shape[1]):
            if slot != r.cache_slot:
                r.kc_out[:, slot] = jnp.zeros((nb,) + r.kc_out.shape[2:], F32)
                r.vc_out[:, slot] = jnp.zeros((nb,) + r.vc_out.shape[2:], F32)
    if has_ctx:
        for hd in range(N_HEADS):
            cols = slice(hd * HEAD_COLS, (hd + 1) * HEAD_COLS)
            k_scr[0, cols, 0:PAST_LEN] = r.ck[pl.ds(hd, PAST_LEN, stride=N_HEADS), :].T.astype(BF16)
            v_scr[0, 0:PAST_LEN, cols] = r.cv[pl.ds(hd, PAST_LEN, stride=N_HEADS), :].astype(BF16)
        lane_q = lax.broadcasted_iota(jnp.int32, (ROW_CHUNK, QK_WIDTH), 1)
        rope_low = (lane_q % ROPE_AXIS_DIM) < ROPE_HALF

        def rope(t, pos0):
            partner = jnp.where(rope_low,
                                pltpu.roll(t, QK_WIDTH - ROPE_HALF, 1),
                                pltpu.roll(t, ROPE_HALF, 1))
            cos = r.rope_c[pos0:pos0 + ROW_CHUNK, :]
            sin = r.rope_s[pos0:pos0 + ROW_CHUNK, :]
            cos = jnp.concatenate([cos] * N_HEADS, axis=1)
            sin = jnp.concatenate([sin] * N_HEADS, axis=1)
            return t * cos + partner * sin

    for c in range(ROWS // ROW_CHUNK):
        r0 = c * ROW_CHUNK
        b, pos0 = r0 // seq, r0 % seq
        rows = slice(r0, r0 + ROW_CHUNK)
        h = _rms_mod(x_ref[rows, :], r.g1[...], scale1, shift1).astype(BF16)

        pool_scr[b, POOL_HALO + pos0:POOL_HALO + pos0 + ROW_CHUNK, :] = _dot(h, r.w_in[:, C_POOL:C_Q])

        q = _dot(h, r.w_in[:, C_Q:C_K])
        k = _dot(h, r.w_in[:, C_K:C_V])
        if has_ctx:
            q = rope(q, pos0)
            k = rope(k, pos0)
        else:
            for hd in range(N_HEADS):
                r.kc_out[b, r.cache_slot, pl.ds(N_HEADS * pos0 + hd, ROW_CHUNK, stride=N_HEADS), :] = (
                    k[:, hd * HEAD_COLS:(hd + 1) * HEAD_COLS])
        q = q * (QK_DIM ** -0.5 * math.log2(math.e))
        for hd in range(N_HEADS):
            cols = slice(hd * HEAD_COLS, (hd + 1) * HEAD_COLS)
            qh = q[:, cols]
            r.q_scr[c, 0:ROW_CHUNK, cols] = jnp.where(first_half, qh, 0.0).astype(BF16)
            r.q_scr[c, ROW_CHUNK:2 * ROW_CHUNK, cols] = jnp.where(first_half, 0.0, qh).astype(BF16)
        k_scr[b, :, kv_base + pos0:kv_base + pos0 + ROW_CHUNK] = k.T.astype(BF16)

        v = _dot(h, r.w_in[:, C_V:C_UV])
        if not has_ctx:
            for hd in range(N_HEADS):
                r.vc_out[b, r.cache_slot, pl.ds(N_HEADS * pos0 + hd, ROW_CHUNK, stride=N_HEADS), :] = (
                    v[:, hd * V_DIM:(hd + 1) * V_DIM])
        v_scr[b, kv_base + pos0:kv_base + pos0 + ROW_CHUNK, :] = v.astype(BF16)

        uv = _gelu_tanh(_dot(h, r.w_in[:, C_UV:IN_WIDTH]))
        r.u_scr[rows, :] = uv[:, :SGU_WIDTH]
        vv = uv[:, SGU_WIDTH:]
        mu = jnp.mean(vv, axis=-1, keepdims=True)
        vc = vv - mu
        vn = vc * lax.rsqrt(jnp.mean(vc * vc, axis=-1, keepdims=True) + EPS) * r.sgu_g[...]
        r.vs_scr[rows, :] = vn.astype(BF16)

    lane_pool = lax.broadcasted_iota(jnp.int32, (ROW_CHUNK, 2 * POOL_GROUP), 1)
    narrow = lane_pool < POOL_GROUP
    for c in range(ROWS // ROW_CHUNK):
        r0 = c * ROW_CHUNK
        b, pos0 = r0 // seq, r0 % seq
        base = POOL_HALO + pos0
        pooled = []
        for j, (w_small, w_big) in enumerate(((POOL_WINDOWS[0], POOL_WINDOWS[1]),
                                              (POOL_WINDOWS[2], POOL_WINDOWS[3]))):
            cols = slice(j * 2 * POOL_GROUP, (j + 1) * 2 * POOL_GROUP)
            s_small = jnp.zeros((ROW_CHUNK, 2 * POOL_GROUP), F32)
            s_rest = jnp.zeros((ROW_CHUNK, 2 * POOL_GROUP), F32)
            for d in range(-(w_big // 2), w_big - w_big // 2):
                t = pool_scr[b, base + d:base + d + ROW_CHUNK, cols]
                if -(w_small // 2) <= d < w_small - w_small // 2:
                    s_small = s_small + t
                else:
                    s_rest = s_rest + t
            win_sum = jnp.where(narrow, s_small, s_small + s_rest)
            centre = pool_scr[b, base:base + ROW_CHUNK, cols]
            pooled.append(win_sum / cnt_ref[pos0:pos0 + ROW_CHUNK, cols] - centre)
        pooled = jnp.concatenate(pooled, axis=1).astype(BF16)
        y_a = _dot(pooled, r.wpool[...]) * r.pscale[...]
        r.y_scr[r0:r0 + ROW_CHUNK, Y_POOL:Y_ATTN] = y_a.astype(BF16)

    lq1, lk1, lq2, lk2 = r.lam[0:1, :], r.lam[1:2, :], r.lam[2:3, :], r.lam[3:4, :]
    lam = (jnp.exp(jnp.sum(lq1 * lk1, axis=-1, keepdims=True))
           - jnp.exp(jnp.sum(lq2 * lk2, axis=-1, keepdims=True)) + lam_init)
    n_qb = seq // ROW_CHUNK

    lane_sgu = lax.broadcasted_iota(jnp.int32, (CHUNK, 2 * POOL_GROUP), 1)
    sgu_first = lane_sgu < (SGU_WIDTH // 4)
    out_cols = MIX_WIDTH // N_HEADS

    def out_proj(rows, part):
        cols = slice(part * out_cols, (part + 1) * out_cols)
        y = _dot(r.y_scr[rows, :], r.w_out[:, cols])
        r.out[rows, cols] = x_ref[rows, cols] + gate1[:, cols] * y

    def block(blk, with_prev):
        static = isinstance(blk, int)

        def row0(chunk):
            return chunk * ROW_CHUNK if static else pl.multiple_of(chunk * ROW_CHUNK, ROW_CHUNK)

        chunks = [blk * cpb + j for j in range(cpb)]

        for idx in chunks:
            for half in range(ROW_CHUNK // CHUNK):
                sub = pl.ds(row0(idx) + half * CHUNK, CHUNK)
                for j in range(2):
                    cols = slice(j * 128, (j + 1) * 128)
                    t = _dot(r.wsgu[j], r.vs_scr[sub, cols])
                    mixed = jnp.where(sgu_first, t[:CHUNK, :], t[CHUNK:, :]) + r.bsgu[:, cols]
                    y_c = r.u_scr[sub, cols] * mixed
                    r.y_scr[sub, Y_SGU + j * 128:Y_SGU + (j + 1) * 128] = y_c.astype(BF16)

        units = [(idx, hd) for idx in chunks for hd in range(N_HEADS)]
        prev = [(pl.ds(row0(idx - cpb), ROW_CHUNK), part) for idx in chunks for part in range(N_HEADS)]

        def scores(idx, hd):
            cols = slice(hd * HEAD_COLS, (hd + 1) * HEAD_COLS)
            s = _dot(r.q_scr[idx, :, cols], k_scr[idx // n_qb, cols, :])
            return s[:ROW_CHUNK], s[ROW_CHUNK:]

        pending = [scores(*u) for u in units[:ATTN_LOOKAHEAD]]
        for n, (idx, hd) in enumerate(units):
            cols = slice(hd * HEAD_COLS, (hd + 1) * HEAD_COLS)
            s1, s2 = pending.pop(0)
            if n + ATTN_LOOKAHEAD < len(units):
                pending.append(scores(*units[n + ATTN_LOOKAHEAD]))
            e1 = jnp.exp2(s1 - jnp.max(s1, axis=-1, keepdims=True))
            e2 = jnp.exp2(s2 - jnp.max(s2, axis=-1, keepdims=True))
            l1 = jnp.sum(e1, axis=-1, keepdims=True)
            l2 = jnp.sum(e2, axis=-1, keepdims=True)
            p = (e1 - e2 * (lam * l1 / l2)).astype(BF16)
            o = _dot(p, v_scr[idx // n_qb, :, cols]) * (1.0 / l1)
            if with_prev:
                out_proj(*prev[n])
            o = o * lax.rsqrt(jnp.mean(o * o, axis=-1, keepdims=True) + EPS) * r.subln[...]
            o = o * (1.0 - lam_init)
            r.y_scr[pl.ds(row0(idx), ROW_CHUNK), Y_ATTN + hd * V_DIM:Y_ATTN + (hd + 1) * V_DIM] = o.astype(BF16)

    n_blocks = ROWS // (ROW_CHUNK * cpb)
    block(0, False)
    if n_blocks == 2:
        block(1, True)
    else:
        def loop_block(blk, carry):
            block(blk, True)
            return carry

        lax.fori_loop(1, n_blocks, loop_block, 0)
    for idx in range((n_blocks - 1) * cpb, n_blocks * cpb):
        for part in range(N_HEADS):
            out_proj(pl.ds(idx * ROW_CHUNK, ROW_CHUNK), part)


def _mixer_kernel(*refs, nb, seq, has_ctx, lam_init, cache_slot, n_passthrough, n_cast):
    it = iter(refs)
    r = types.SimpleNamespace(cache_slot=cache_slot)
    x_ref, r.mod, r.g1, r.w_in = next(it), next(it), next(it), next(it)
    if has_ctx:
        r.rope_c, r.rope_s, r.ck, r.cv = next(it), next(it), next(it), next(it)
    (r.lam, r.subln, cnt_ref, r.wpool, r.pscale, r.sgu_g, r.wsgu, r.bsgu, r.w_out) = (next(it) for _ in range(9))
    for _ in range(n_passthrough):
        next(it)
    cast_in = [next(it) for _ in range(n_cast)]
    r.out = next(it)
    if not has_ctx:
        r.kc_out, r.vc_out = next(it), next(it)
    cast_out = [next(it) for _ in range(n_cast)]
    pool_scr, r.q_scr, k_scr, v_scr, r.u_scr, r.vs_scr, r.y_scr = (next(it) for _ in range(7))
    for src, dst in zip(cast_in, cast_out):
        dst[...] = src[...].astype(BF16)
    _mixer_body(r, x_ref, cnt_ref, pool_scr, k_scr, v_scr, nb=nb, seq=seq, has_ctx=has_ctx, lam_init=lam_init,
                cpb=CHUNKS_PER_BLOCK)


def _mixer_call(x2d, mod4, layer, seq, has_ctx, lam_init, params, big, ctx=None, new_cache=None, cast_f32=()):
    n_rows = x2d.shape[0]
    nb = ROWS // seq
    n_seq = n_rows // seq
    kv_len = seq + (PAST_LEN if has_ctx else 0)

    if has_ctx:
        mod_map = lambda i: (layer, 1 + i, 0, 0)
    else:
        mod_map = lambda i: (layer, 0, 0, 0)

    operands = [x2d, mod4, params["norm1_g"], big["w_in"]]
    in_specs = [
        pl.BlockSpec((ROWS, D_MODEL), lambda i: (i, 0)),
        pl.BlockSpec((None, None, N_MOD, D_MODEL), mod_map),
        _layer_spec((1, D_MODEL), layer),
        _const_spec((D_MODEL, IN_WIDTH)),
    ]
    if has_ctx:
        operands += list(ctx)
        in_specs += [
            _const_spec((seq, HEAD_COLS)),
            _const_spec((seq, HEAD_COLS)),
            pl.BlockSpec((None, None, PAST_LEN * N_HEADS, HEAD_COLS), lambda i: (i, layer, 0, 0)),
            pl.BlockSpec((None, None, PAST_LEN * N_HEADS, V_DIM), lambda i: (i, layer, 0, 0)),
        ]
    operands += [params["lam"], params["subln_g"], _pool_counts(seq), params["w_pool"], params["pool_scale"],
                 params["sgu_norm_g"], params["w_sgu"], params["b_sgu"], big["w_out"]]
    in_specs += [
        _layer_spec((8, 128), layer),
        _layer_spec((1, V_DIM), layer),
        _const_spec((seq, POOL_WIDTH)),
        _layer_spec((POOL_WIDTH, POOL_WIDTH), layer),
        _layer_spec((1, POOL_WIDTH), layer),
        _layer_spec((1, SGU_WIDTH), layer),
        _layer_spec((2, 2 * CHUNK, CHUNK), layer),
        _layer_spec((CHUNK, SGU_WIDTH), layer),
        _const_spec((MIX_WIDTH, D_MODEL)),
    ]

    out_shape = [jax.ShapeDtypeStruct((n_rows, D_MODEL), F32)]
    out_specs = [pl.BlockSpec((ROWS, D_MODEL), lambda i: (i, 0))]
    aliases = {}
    cache_slot = 0
    if not has_ctx:
        out_shape += [jax.ShapeDtypeStruct((n_seq, DEPTH, seq * N_HEADS, HEAD_COLS), F32),
                      jax.ShapeDtypeStruct((n_seq, DEPTH, seq * N_HEADS, V_DIM), F32)]
        if new_cache is None:
            cache_slot = layer
            out_specs += [pl.BlockSpec((nb, DEPTH, seq * N_HEADS, HEAD_COLS), lambda i: (i, 0, 0, 0)),
                          pl.BlockSpec((nb, DEPTH, seq * N_HEADS, V_DIM), lambda i: (i, 0, 0, 0))]
        else:
            out_specs += [pl.BlockSpec((nb, 1, seq * N_HEADS, HEAD_COLS), lambda i: (i, layer, 0, 0)),
                          pl.BlockSpec((nb, 1, seq * N_HEADS, V_DIM), lambda i: (i, layer, 0, 0))]
            aliases = {len(operands): 1, len(operands) + 1: 2}
            operands += list(new_cache)
            in_specs += [pl.BlockSpec(memory_space=pl.ANY)] * 2
    n_steps = n_rows // ROWS
    for w in cast_f32:
        _, rows, cols = w.shape
        slab = rows // n_steps
        assert slab * n_steps == rows and slab % 16 == 0
        operands.append(w)
        in_specs.append(pl.BlockSpec((None, slab, cols), lambda i: (layer, i, 0)))
        out_shape.append(jax.ShapeDtypeStruct((rows, cols), BF16))
        out_specs.append(pl.BlockSpec((slab, cols), lambda i: (i, 0)))

    scratch = [
        pltpu.VMEM((nb, seq + 2 * POOL_HALO, POOL_WIDTH), F32),
        pltpu.VMEM((ROWS // ROW_CHUNK, 2 * ROW_CHUNK, QK_WIDTH), BF16),
        pltpu.VMEM((nb, QK_WIDTH, kv_len), BF16),
        pltpu.VMEM((nb, kv_len, ATTN_WIDTH), BF16),
        pltpu.VMEM((ROWS, SGU_WIDTH), F32),
        pltpu.VMEM((ROWS, SGU_WIDTH), BF16),
        pltpu.VMEM((ROWS, MIX_WIDTH), BF16),
    ]
    kernel = functools.partial(_mixer_kernel, nb=nb, seq=seq, has_ctx=has_ctx, lam_init=lam_init,
                               cache_slot=cache_slot, n_passthrough=len(aliases), n_cast=len(cast_f32))
    return pl.pallas_call(
        kernel,
        out_shape=out_shape,
        grid=(n_rows // ROWS,),
        in_specs=in_specs,
        out_specs=out_specs,
        scratch_shapes=scratch,
        input_output_aliases=aliases,
        compiler_params=pltpu.CompilerParams(
            dimension_semantics=("arbitrary",), vmem_limit_bytes=VMEM_LIMIT),
        name="mixer_ctx" if has_ctx else "mixer_prompt",
    )(*operands)


def _ffn_block(x_ref, out_ref, mod_ref, g2_ref, w1_ref, w2_ref, gf_ref, act_scr, final_norm):
    shift2, scale2, gate2 = mod_ref[3:4, :], mod_ref[4:5, :], mod_ref[5:6, :]
    x = x_ref[...]
    h = _rms_mod(x, g2_ref[...], scale2, shift2).astype(BF16)
    for lo, hi in ((0, FF_SPLIT), (FF_SPLIT, D_FF)):
        gate = _dot(h, w1_ref[:, lo:hi])
        up = _dot(h, w1_ref[:, D_FF + lo:D_FF + hi])
        act_scr[:, lo:hi] = (gate * _sigmoid(gate) * up).astype(BF16)
    y = x + gate2 * _dot(act_scr[...], w2_ref[...])
    if final_norm:
        ms = jnp.mean(y * y, axis=-1, keepdims=True)
        y = y * lax.rsqrt(ms + EPS) * gf_ref[...]
    out_ref[...] = y


def _ffn_kernel(*refs, n_prompt_steps, final_norm, n_cast):
    xp_ref, xs_ref, mod_ref, g2_ref, w1_ref, w2_ref, gf_ref = refs[:7]
    cast_in = refs[7:7 + n_cast]
    yp_ref, ys_ref = refs[7 + n_cast:9 + n_cast]
    cast_out = refs[9 + n_cast:9 + 2 * n_cast]
    act_scr = refs[9 + 2 * n_cast]
    shared = (mod_ref, g2_ref, w1_ref, w2_ref, gf_ref, act_scr, final_norm)
    is_prompt = pl.program_id(0) < n_prompt_steps

    @pl.when(is_prompt)
    def _():
        _ffn_block(xp_ref, yp_ref, *shared)

    @pl.when(jnp.logical_not(is_prompt))
    def _():
        for src, dst in zip(cast_in, cast_out):
            dst[...] = src[...].astype(BF16)
        _ffn_block(xs_ref, ys_ref, *shared)


def _ffn_call(xp, xs, mod4, layer, sample_seq, params, big, final_norm, next_f32=()):
    n_p, n_s = xp.shape[0] // FFN_ROWS, xs.shape[0] // FFN_ROWS
    steps_per_seq = sample_seq // FFN_ROWS

    def prompt_step(i):
        return jnp.minimum(i, n_p - 1)

    def sample_step(i):
        return jnp.maximum(i - n_p, 0)

    def mod_map(i):
        return (layer, jnp.where(i < n_p, 0, 1 + sample_step(i) // steps_per_seq), 0, 0)

    in_specs = [
        pl.BlockSpec((FFN_ROWS, D_MODEL), lambda i: (prompt_step(i), 0)),
        pl.BlockSpec((FFN_ROWS, D_MODEL), lambda i: (sample_step(i), 0)),
        pl.BlockSpec((None, None, N_MOD, D_MODEL), mod_map),
        _layer_spec((1, D_MODEL), layer),
        _const_spec((D_MODEL, 2 * D_FF)),
        _const_spec((D_FF, D_MODEL)),
        _const_spec((1, D_MODEL)),
    ]
    out_shape = [jax.ShapeDtypeStruct(xp.shape, F32), jax.ShapeDtypeStruct(xs.shape, F32)]
    out_specs = [pl.BlockSpec((FFN_ROWS, D_MODEL), lambda i: (prompt_step(i), 0)),
                 pl.BlockSpec((FFN_ROWS, D_MODEL), lambda i: (sample_step(i), 0))]
    for w in next_f32:
        _, rows, cols = w.shape
        slab = rows // n_s
        assert slab * n_s == rows and slab % 16 == 0
        in_specs.append(pl.BlockSpec((None, slab, cols), lambda i: (layer + 1, sample_step(i), 0)))
        out_shape.append(jax.ShapeDtypeStruct((rows, cols), BF16))
        out_specs.append(pl.BlockSpec((slab, cols), lambda i: (sample_step(i), 0)))

    kernel = functools.partial(_ffn_kernel, n_prompt_steps=n_p, final_norm=final_norm, n_cast=len(next_f32))
    return pl.pallas_call(
        kernel,
        out_shape=out_shape,
        grid=(n_p + n_s,),
        in_specs=in_specs,
        out_specs=out_specs,
        scratch_shapes=[pltpu.VMEM((FFN_ROWS, D_FF), BF16)],
        compiler_params=pltpu.CompilerParams(
            dimension_semantics=("arbitrary",), vmem_limit_bytes=VMEM_LIMIT),
        name="ffn",
    )(xp, xs, mod4, params["norm2_g"], big["w_ffn_in"], big["w_ffn_out"], params["final_g"], *next_f32)


def _rope_tables(seq):
    n_rows = seq // GRID_W
    rows = np.repeat(np.arange(n_rows), GRID_W).astype(np.float32)
    cols = np.tile(np.arange(GRID_W), n_rows).astype(np.float32)
    inv = 1.0 / (ROPE_BASE ** (np.arange(0, ROPE_AXIS_DIM, 2, dtype=np.float32) / ROPE_AXIS_DIM))
    ar, ac = rows[:, None] * inv[None], cols[:, None] * inv[None]
    cos_parts, sin_parts = [], []
    for ang in (ar, ac):
        cos_parts += [np.cos(ang), np.cos(ang)]
        sin_parts += [-np.sin(ang), np.sin(ang)]
    cos64 = np.concatenate(cos_parts, axis=1)
    sin64 = np.concatenate(sin_parts, axis=1)
    cos = np.concatenate([cos64, cos64], axis=1).astype(np.float32)
    sin = np.concatenate([sin64, sin64], axis=1).astype(np.float32)
    return jnp.asarray(cos), jnp.asarray(sin)


def _pool_counts(seq):
    t = np.arange(seq)
    cols = []
    for w in POOL_WINDOWS:
        lo = np.clip(t - w // 2, 0, seq)
        hi = np.clip(t + w - w // 2, 0, seq)
        cols.append(np.repeat((hi - lo).astype(np.float32)[:, None], POOL_GROUP, axis=1))
    return jnp.asarray(np.concatenate(cols, axis=1))


def _small_params(norm1_g, w_pool, pool_scale, lam_q1, lam_k1, lam_q2, lam_k2, subln_g,
                  sgu_norm_g, w_sgu, b_sgu, norm2_g, final_g):
    n_groups = len(POOL_WINDOWS)
    eye = jnp.eye(n_groups, dtype=F32)
    w_pool_bd = (eye[None, :, None, :, None] * w_pool[:, :, :, None, :]).reshape(DEPTH, POOL_WIDTH, POOL_WIDTH)
    lam = jnp.zeros((DEPTH, 8, 128), F32)
    lam = lam.at[:, 0, :QK_DIM].set(lam_q1).at[:, 1, :QK_DIM].set(lam_k1)
    lam = lam.at[:, 2, :QK_DIM].set(lam_q2).at[:, 3, :QK_DIM].set(lam_k2)
    return {
        "norm1_g": norm1_g[:, None, :],
        "lam": lam,
        "subln_g": subln_g[:, None, :],
        "w_pool": w_pool_bd.astype(BF16),
        "pool_scale": pool_scale[:, None, :],
        "sgu_norm_g": sgu_norm_g[:, None, :],
        "w_sgu": w_sgu.reshape(DEPTH, 2, 2 * CHUNK, CHUNK).astype(BF16),
        "b_sgu": jnp.repeat(jnp.swapaxes(b_sgu, 1, 2), SGU_WIDTH // 4, axis=2),
        "norm2_g": norm2_g[:, None, :],
        "final_g": final_g[None, :],
    }


BIG_WEIGHTS = ("w_in", "w_out", "w_ffn_in", "w_ffn_out")


def kernel(x_prompt, x_sample, cache_k, cache_v, c, c_ctx, norm1_g, w_ada, b_ada, w_in, w_pool, pool_scale, lam_q1, lam_k1, lam_q2, lam_k2, subln_g, sgu_norm_g, w_sgu, b_sgu, w_out, norm2_g, w_ffn_in, w_ffn_out, final_g):
    batch, seq, _ = x_prompt.shape
    dec_batch, dec_seq, _ = x_sample.shape
    assert ROWS % seq == 0 and dec_seq == ROWS and 1 + dec_batch <= COND_ROWS

    cond = jnp.zeros((COND_ROWS, D_MODEL), F32).at[0].set(c_ctx).at[1:1 + dec_batch].set(c)
    mod4 = _ada_modulation(cond, w_ada, b_ada).reshape(DEPTH, COND_ROWS, N_MOD, D_MODEL)

    ctx = _rope_tables(dec_seq) + (cache_k.reshape(dec_batch, DEPTH, PAST_LEN * N_HEADS, HEAD_COLS),
                                   cache_v.reshape(dec_batch, DEPTH, PAST_LEN * N_HEADS, V_DIM))
    params = _small_params(norm1_g, w_pool, pool_scale, lam_q1, lam_k1, lam_q2, lam_k2, subln_g,
                           sgu_norm_g, w_sgu, b_sgu, norm2_g, final_g)

    big_f32 = (w_in, w_out, w_ffn_in, w_ffn_out)
    big = {"w_in": w_in[0].astype(BF16), "w_out": w_out[0].astype(BF16)}

    xp = x_prompt.reshape(batch * seq, D_MODEL)
    xs = x_sample.reshape(dec_batch * dec_seq, D_MODEL)
    new_cache = None
    for l in range(DEPTH):
        lam_init = 0.8 - 0.6 * math.exp(-0.3 * l)
        last = l == DEPTH - 1

        xp, *new_cache = _mixer_call(xp, mod4, l, seq, False, lam_init, params, big, new_cache=new_cache)
        if l == 0:
            xs, big["w_ffn_in"], big["w_ffn_out"] = _mixer_call(
                xs, mod4, l, dec_seq, True, lam_init, params, big, ctx=ctx, cast_f32=(w_ffn_in, w_ffn_out))
        else:
            xs, = _mixer_call(xs, mod4, l, dec_seq, True, lam_init, params, big, ctx=ctx)
        xp, xs, *next_big = _ffn_call(xp, xs, mod4, l, dec_seq, params, big, last,
                                      next_f32=() if last else big_f32)
        big = dict(zip(BIG_WEIGHTS, next_big))

    y_prompt = xp.reshape(batch, seq, D_MODEL)
    y_sample = xs.reshape(dec_batch, dec_seq, D_MODEL)
    new_cache_k = new_cache[0].reshape(batch, DEPTH, seq, N_HEADS, 2 * QK_DIM)
    new_cache_v = new_cache[1].reshape(batch, DEPTH, seq, N_HEADS, V_DIM)
    return (y_prompt, y_sample, new_cache_k, new_cache_v)
```

```python
import functools
import math
import types

import numpy as np
import jax
import jax.numpy as jnp
from jax import lax
from jax.experimental import pallas as pl
from jax.experimental.pallas import tpu as pltpu

D_MODEL = 1024
DEPTH = 2
GRID_W = 64
POOL_WINDOWS = (2, 4, 8, 16)
POOL_WIDTH = 256
POOL_GROUP = 64
POOL_HALO = 8
ATTN_WIDTH = 512
N_HEADS = 4
V_DIM = 128
QK_DIM = 64
HEAD_COLS = 2 * QK_DIM
ROPE_BASE = 10000.0
ROPE_AXIS_DIM = 32
ROPE_HALF = ROPE_AXIS_DIM // 2
CHUNK = 128
SGU_WIDTH = 256
QK_WIDTH = 512
IN_WIDTH = 2304
MIX_WIDTH = 1024
D_FF = 2816
N_MOD = 6
EPS = 1e-6
PAST_LEN = 256

C_POOL, C_Q, C_K, C_V, C_UV = 0, 256, 768, 1280, 1792
Y_POOL, Y_ATTN, Y_SGU = 0, 256, 768

ROWS = 1024
ROW_CHUNK = 256
ATTN_LOOKAHEAD = 2
CHUNKS_PER_BLOCK = 2
FFN_ROWS = 512
MXU_COLS = 256
FF_SPLIT = (D_FF // MXU_COLS + 1) // 2 * MXU_COLS
COND_ROWS = 16
ADA_COLS = 3072
VMEM_LIMIT = 58 * 1024 * 1024

BF16 = jnp.bfloat16
F32 = jnp.float32


def _dot(a, b):
    return jnp.dot(a, b, preferred_element_type=F32)


def _sigmoid(x):
    return 1.0 / (1.0 + jnp.exp(-x))


def _rms_mod(x, g, scale, shift):
    ms = jnp.mean(x * x, axis=-1, keepdims=True)
    return x * lax.rsqrt(ms + EPS) * (g * (1.0 + scale)) + shift


def _gelu_tanh(x):
    c = math.sqrt(2.0 / math.pi)
    return x * (0.5 * (1.0 + jnp.tanh(c * (x + 0.044715 * (x * x * x)))))


def _const_spec(shape):
    zeros = (0,) * len(shape)
    return pl.BlockSpec(shape, lambda *_: zeros, pipeline_mode=pl.Buffered(1))


def _layer_spec(shape, layer):
    index = (layer,) + (0,) * len(shape)
    return pl.BlockSpec((None,) + tuple(shape), lambda *_: index, pipeline_mode=pl.Buffered(1))


def _ada_kernel(cond_ref, w_ref, b_ref, out_ref):
    cond = cond_ref[...]
    s = (cond * _sigmoid(cond)).astype(BF16)
    out_ref[...] = _dot(s, w_ref[...].astype(BF16)) + b_ref[...]


def _ada_modulation(cond, w_ada, b_ada):
    n_cols = N_MOD * D_MODEL
    return pl.pallas_call(
        _ada_kernel,
        out_shape=jax.ShapeDtypeStruct((DEPTH, COND_ROWS, n_cols), F32),
        grid=(DEPTH, n_cols // ADA_COLS),
        in_specs=[
            pl.BlockSpec((COND_ROWS, D_MODEL), lambda l, j: (0, 0)),
            pl.BlockSpec((None, D_MODEL, ADA_COLS), lambda l, j: (l, 0, j)),
            pl.BlockSpec((None, 1, ADA_COLS), lambda l, j: (l, 0, j)),
        ],
        out_specs=pl.BlockSpec((None, COND_ROWS, ADA_COLS), lambda l, j: (l, 0, j)),
        compiler_params=pltpu.CompilerParams(
            dimension_semantics=("arbitrary", "arbitrary"), vmem_limit_bytes=VMEM_LIMIT),
        name="ada_modulation",
    )(cond, w_ada, b_ada.reshape(DEPTH, 1, n_cols))


def _mixer_body(r, x_ref, cnt_ref, pool_scr, k_scr, v_scr, *, nb, seq, has_ctx, lam_init, cpb):
    kv_base = PAST_LEN if has_ctx else 0
    shift1, scale1, gate1 = r.mod[0:1, :], r.mod[1:2, :], r.mod[2:3, :]

    lane_head = lax.broadcasted_iota(jnp.int32, (ROW_CHUNK, HEAD_COLS), 1)
    first_half = lane_head < QK_DIM

    for b in range(nb):
        pool_scr[b, 0:POOL_HALO, :] = jnp.zeros((POOL_HALO, POOL_WIDTH), F32)
        pool_scr[b, POOL_HALO + seq:POOL_HALO + seq + POOL_HALO, :] = jnp.zeros((POOL_HALO, POOL_WIDTH), F32)
    if not has_ctx:
        for slot in range(r.kc_out.shape[1]):
            if slot != r.cache_slot:
                r.kc_out[:, slot] = jnp.zeros((nb,) + r.kc_out.shape[2:], F32)
                r.vc_out[:, slot] = jnp.zeros((nb,) + r.vc_out.shape[2:], F32)
    if has_ctx:
        for hd in range(N_HEADS):
            cols = slice(hd * HEAD_COLS, (hd + 1) * HEAD_COLS)
            k_scr[0, cols, 0:PAST_LEN] = r.ck[pl.ds(hd, PAST_LEN, stride=N_HEADS), :].T.astype(BF16)
            v_scr[0, 0:PAST_LEN, cols] = r.cv[pl.ds(hd, PAST_LEN, stride=N_HEADS), :].astype(BF16)
        lane_q = lax.broadcasted_iota(jnp.int32, (ROW_CHUNK, QK_WIDTH), 1)
        rope_low = (lane_q % ROPE_AXIS_DIM) < ROPE_HALF

        def rope(t, pos0):
            partner = jnp.where(rope_low,
                                pltpu.roll(t, QK_WIDTH - ROPE_HALF, 1),
                                pltpu.roll(t, ROPE_HALF, 1))
            cos = r.rope_c[pos0:pos0 + ROW_CHUNK, :]
            sin = r.rope_s[pos0:pos0 + ROW_CHUNK, :]
            cos = jnp.concatenate([cos] * N_HEADS, axis=1)
            sin = jnp.concatenate([sin] * N_HEADS, axis=1)
            return t * cos + partner * sin

    lane_pool = lax.broadcasted_iota(jnp.int32, (ROW_CHUNK, 2 * POOL_GROUP), 1)
    narrow = lane_pool < POOL_GROUP

    def pool_chunk(c):
        r0 = c * ROW_CHUNK
        b, pos0 = r0 // seq, r0 % seq
        base = POOL_HALO + pos0
        pooled = []
        for j, (w_small, w_big) in enumerate(((POOL_WINDOWS[0], POOL_WINDOWS[1]),
                                              (POOL_WINDOWS[2], POOL_WINDOWS[3]))):
            cols = slice(j * 2 * POOL_GROUP, (j + 1) * 2 * POOL_GROUP)
            s_small = jnp.zeros((ROW_CHUNK, 2 * POOL_GROUP), F32)
            s_rest = jnp.zeros((ROW_CHUNK, 2 * POOL_GROUP), F32)
            for d in range(-(w_big // 2), w_big - w_big // 2):
                t = pool_scr[b, base + d:base + d + ROW_CHUNK, cols]
                if -(w_small // 2) <= d < w_small - w_small // 2:
                    s_small = s_small + t
                else:
                    s_rest = s_rest + t
            win_sum = jnp.where(narrow, s_small, s_small + s_rest)
            centre = pool_scr[b, base:base + ROW_CHUNK, cols]
            pooled.append(win_sum / cnt_ref[pos0:pos0 + ROW_CHUNK, cols] - centre)
        pooled = jnp.concatenate(pooled, axis=1).astype(BF16)
        y_a = _dot(pooled, r.wpool[...]) * r.pscale[...]
        r.y_scr[r0:r0 + ROW_CHUNK, Y_POOL:Y_ATTN] = y_a.astype(BF16)

    n_pooled = 0
    for c in range(ROWS // ROW_CHUNK):
        r0 = c * ROW_CHUNK
        b, pos0 = r0 // seq, r0 % seq
        rows = slice(r0, r0 + ROW_CHUNK)
        h = _rms_mod(x_ref[rows, :], r.g1[...], scale1, shift1).astype(BF16)

        pool_scr[b, POOL_HALO + pos0:POOL_HALO + pos0 + ROW_CHUNK, :] = _dot(h, r.w_in[:, C_POOL:C_Q])
        ready = c + 1 if pos0 + ROW_CHUNK == seq else c
        for pc in range(n_pooled, ready):
            pool_chunk(pc)
        n_pooled = ready

        q = _dot(h, r.w_in[:, C_Q:C_K])
        k = _dot(h, r.w_in[:, C_K:C_V])
        if has_ctx:
            q = rope(q, pos0)
            k = rope(k, pos0)
        else:
            for hd in range(N_HEADS):
                r.kc_out[b, r.cache_slot, pl.ds(N_HEADS * pos0 + hd, ROW_CHUNK, stride=N_HEADS), :] = (
                    k[:, hd * HEAD_COLS:(hd + 1) * HEAD_COLS])
        q = q * (QK_DIM ** -0.5 * math.log2(math.e))
        for hd in range(N_HEADS):
            cols = slice(hd * HEAD_COLS, (hd + 1) * HEAD_COLS)
            qh = q[:, cols]
            r.q_scr[c, 0:ROW_CHUNK, cols] = jnp.where(first_half, qh, 0.0).astype(BF16)
            r.q_scr[c, ROW_CHUNK:2 * ROW_CHUNK, cols] = jnp.where(first_half, 0.0, qh).astype(BF16)
        k_scr[b, :, kv_base + pos0:kv_base + pos0 + ROW_CHUNK] = k.T.astype(BF16)

        v = _dot(h, r.w_in[:, C_V:C_UV])
        if not has_ctx:
            for hd in range(N_HEADS):
                r.vc_out[b, r.cache_slot, pl.ds(N_HEADS * pos0 + hd, ROW_CHUNK, stride=N_HEADS), :] = (
                    v[:, hd * V_DIM:(hd + 1) * V_DIM])
        v_scr[b, kv_base + pos0:kv_base + pos0 + ROW_CHUNK, :] = v.astype(BF16)

        uv = _gelu_tanh(_dot(h, r.w_in[:, C_UV:IN_WIDTH]))
        r.u_scr[rows, :] = uv[:, :SGU_WIDTH]
        vv = uv[:, SGU_WIDTH:]
        mu = jnp.mean(vv, axis=-1, keepdims=True)
        vc = vv - mu
        vn = vc * lax.rsqrt(jnp.mean(vc * vc, axis=-1, keepdims=True) + EPS) * r.sgu_g[...]
        r.vs_scr[rows, :] = vn.astype(BF16)

    assert n_pooled == ROWS // ROW_CHUNK

    lq1, lk1, lq2, lk2 = r.lam[0:1, :], r.lam[1:2, :], r.lam[2:3, :], r.lam[3:4, :]
    lam = (jnp.exp(jnp.sum(lq1 * lk1, axis=-1, keepdims=True))
           - jnp.exp(jnp.sum(lq2 * lk2, axis=-1, keepdims=True)) + lam_init)
    n_qb = seq // ROW_CHUNK

    lane_sgu = lax.broadcasted_iota(jnp.int32, (CHUNK, 2 * POOL_GROUP), 1)
    sgu_first = lane_sgu < (SGU_WIDTH // 4)
    out_cols = MIX_WIDTH // N_HEADS

    def out_proj(rows, part):
        cols = slice(part * out_cols, (part + 1) * out_cols)
        y = _dot(r.y_scr[rows, :], r.w_out[:, cols])
        r.out[rows, cols] = x_ref[rows, cols] + gate1[:, cols] * y

    def block(blk, with_prev):
        static = isinstance(blk, int)

        def row0(chunk):
            return chunk * ROW_CHUNK if static else pl.multiple_of(chunk * ROW_CHUNK, ROW_CHUNK)

        chunks = [blk * cpb + j for j in range(cpb)]

        for idx in chunks:
            for half in range(ROW_CHUNK // CHUNK):
                sub = pl.ds(row0(idx) + half * CHUNK, CHUNK)
                for j in range(2):
                    cols = slice(j * 128, (j + 1) * 128)
                    t = _dot(r.wsgu[j], r.vs_scr[sub, cols])
                    mixed = jnp.where(sgu_first, t[:CHUNK, :], t[CHUNK:, :]) + r.bsgu[:, cols]
                    y_c = r.u_scr[sub, cols] * mixed
                    r.y_scr[sub, Y_SGU + j * 128:Y_SGU + (j + 1) * 128] = y_c.astype(BF16)

        units = [(idx, hd) for idx in chunks for hd in range(N_HEADS)]
        prev = [(pl.ds(row0(idx - cpb), ROW_CHUNK), part) for idx in chunks for part in range(N_HEADS)]

        def scores(idx, hd):
            cols = slice(hd * HEAD_COLS, (hd + 1) * HEAD_COLS)
            s = _dot(r.q_scr[idx, :, cols], k_scr[idx // n_qb, cols, :])
            return s[:ROW_CHUNK], s[ROW_CHUNK:]

        pending = [scores(*u) for u in units[:ATTN_LOOKAHEAD]]
        for n, (idx, hd) in enumerate(units):
            cols = slice(hd * HEAD_COLS, (hd + 1) * HEAD_COLS)
            s1, s2 = pending.pop(0)
            if n + ATTN_LOOKAHEAD < len(units):
                pending.append(scores(*units[n + ATTN_LOOKAHEAD]))
            e1 = jnp.exp2(s1 - jnp.max(s1, axis=-1, keepdims=True))
            e2 = jnp.exp2(s2 - jnp.max(s2, axis=-1, keepdims=True))
            l1 = jnp.sum(e1, axis=-1, keepdims=True)
            l2 = jnp.sum(e2, axis=-1, keepdims=True)
            p = (e1 - e2 * (lam * l1 / l2)).astype(BF16)
            o = _dot(p, v_scr[idx // n_qb, :, cols]) * (1.0 / l1)
            if with_prev:
                out_proj(*prev[n])
            o = o * lax.rsqrt(jnp.mean(o * o, axis=-1, keepdims=True) + EPS) * r.subln[...]
            o = o * (1.0 - lam_init)
            r.y_scr[pl.ds(row0(idx), ROW_CHUNK), Y_ATTN + hd * V_DIM:Y_ATTN + (hd + 1) * V_DIM] = o.astype(BF16)

    n_blocks = ROWS // (ROW_CHUNK * cpb)
    block(0, False)
    if n_blocks == 2:
        block(1, True)
    else:
        def loop_block(blk, carry):
            block(blk, True)
            return carry

        lax.fori_loop(1, n_blocks, loop_block, 0)
    for idx in range((n_blocks - 1) * cpb, n_blocks * cpb):
        for part in range(N_HEADS):
            out_proj(pl.ds(idx * ROW_CHUNK, ROW_CHUNK), part)


def _mixer_kernel(*refs, nb, seq, has_ctx, lam_init, cache_slot, n_passthrough, n_cast):
    it = iter(refs)
    r = types.SimpleNamespace(cache_slot=cache_slot)
    x_ref, r.mod, r.g1, r.w_in = next(it), next(it), next(it), next(it)
    if has_ctx:
        r.rope_c, r.rope_s, r.ck, r.cv = next(it), next(it), next(it), next(it)
    (r.lam, r.subln, cnt_ref, r.wpool, r.pscale, r.sgu_g, r.wsgu, r.bsgu, r.w_out) = (next(it) for _ in range(9))
    for _ in range(n_passthrough):
        next(it)
    cast_in = [next(it) for _ in range(n_cast)]
    r.out = next(it)
    if not has_ctx:
        r.kc_out, r.vc_out = next(it), next(it)
    cast_out = [next(it) for _ in range(n_cast)]
    pool_scr, r.q_scr, k_scr, v_scr, r.u_scr, r.vs_scr, r.y_scr = (next(it) for _ in range(7))
    for src, dst in zip(cast_in, cast_out):
        dst[...] = src[...].astype(BF16)
    _mixer_body(r, x_ref, cnt_ref, pool_scr, k_scr, v_scr, nb=nb, seq=seq, has_ctx=has_ctx, lam_init=lam_init,
                cpb=CHUNKS_PER_BLOCK)


def _mixer_call(x2d, mod4, layer, seq, has_ctx, lam_init, params, big, ctx=None, new_cache=None, cast_f32=()):
    n_rows = x2d.shape[0]
    nb = ROWS // seq
    n_seq = n_rows // seq
    kv_len = seq + (PAST_LEN if has_ctx else 0)

    if has_ctx:
        mod_map = lambda i: (layer, 1 + i, 0, 0)
    else:
        mod_map = lambda i: (layer, 0, 0, 0)

    operands = [x2d, mod4, params["norm1_g"], big["w_in"]]
    in_specs = [
        pl.BlockSpec((ROWS, D_MODEL), lambda i: (i, 0)),
        pl.BlockSpec((None, None, N_MOD, D_MODEL), mod_map),
        _layer_spec((1, D_MODEL), layer),
        _const_spec((D_MODEL, IN_WIDTH)),
    ]
    if has_ctx:
        operands += list(ctx)
        in_specs += [
            _const_spec((seq, HEAD_COLS)),
            _const_spec((seq, HEAD_COLS)),
            pl.BlockSpec((None, None, PAST_LEN * N_HEADS, HEAD_COLS), lambda i: (i, layer, 0, 0)),
            pl.BlockSpec((None, None, PAST_LEN * N_HEADS, V_DIM), lambda i: (i, layer, 0, 0)),
        ]
    operands += [params["lam"], params["subln_g"], _pool_counts(seq), params["w_pool"], params["pool_scale"],
                 params["sgu_norm_g"], params["w_sgu"], params["b_sgu"], big["w_out"]]
    in_specs += [
        _layer_spec((8, 128), layer),
        _layer_spec((1, V_DIM), layer),
        _const_spec((seq, POOL_WIDTH)),
        _layer_spec((POOL_WIDTH, POOL_WIDTH), layer),
        _layer_spec((1, POOL_WIDTH), layer),
        _layer_spec((1, SGU_WIDTH), layer),
        _layer_spec((2, 2 * CHUNK, CHUNK), layer),
        _layer_spec((CHUNK, SGU_WIDTH), layer),
        _const_spec((MIX_WIDTH, D_MODEL)),
    ]

    out_shape = [jax.ShapeDtypeStruct((n_rows, D_MODEL), F32)]
    out_specs = [pl.BlockSpec((ROWS, D_MODEL), lambda i: (i, 0))]
    aliases = {}
    cache_slot = 0
    if not has_ctx:
        out_shape += [jax.ShapeDtypeStruct((n_seq, DEPTH, seq * N_HEADS, HEAD_COLS), F32),
                      jax.ShapeDtypeStruct((n_seq, DEPTH, seq * N_HEADS, V_DIM), F32)]
        if new_cache is None:
            cache_slot = layer
            out_specs += [pl.BlockSpec((nb, DEPTH, seq * N_HEADS, HEAD_COLS), lambda i: (i, 0, 0, 0)),
                          pl.BlockSpec((nb, DEPTH, seq * N_HEADS, V_DIM), lambda i: (i, 0, 0, 0))]
        else:
            out_specs += [pl.BlockSpec((nb, 1, seq * N_HEADS, HEAD_COLS), lambda i: (i, layer, 0, 0)),
                          pl.BlockSpec((nb, 1, seq * N_HEADS, V_DIM), lambda i: (i, layer, 0, 0))]
            aliases = {len(operands): 1, len(operands) + 1: 2}
            operands += list(new_cache)
            in_specs += [pl.BlockSpec(memory_space=pl.ANY)] * 2
    n_steps = n_rows // ROWS
    for w in cast_f32:
        _, rows, cols = w.shape
        slab = rows // n_steps
        assert slab * n_steps == rows and slab % 16 == 0
        operands.append(w)
        in_specs.append(pl.BlockSpec((None, slab, cols), lambda i: (layer, i, 0)))
        out_shape.append(jax.ShapeDtypeStruct((rows, cols), BF16))
        out_specs.append(pl.BlockSpec((slab, cols), lambda i: (i, 0)))

    scratch = [
        pltpu.VMEM((nb, seq + 2 * POOL_HALO, POOL_WIDTH), F32),
        pltpu.VMEM((ROWS // ROW_CHUNK, 2 * ROW_CHUNK, QK_WIDTH), BF16),
        pltpu.VMEM((nb, QK_WIDTH, kv_len), BF16),
        pltpu.VMEM((nb, kv_len, ATTN_WIDTH), BF16),
        pltpu.VMEM((ROWS, SGU_WIDTH), F32),
        pltpu.VMEM((ROWS, SGU_WIDTH), BF16),
        pltpu.VMEM((ROWS, MIX_WIDTH), BF16),
    ]
    kernel = functools.partial(_mixer_kernel, nb=nb, seq=seq, has_ctx=has_ctx, lam_init=lam_init,
                               cache_slot=cache_slot, n_passthrough=len(aliases), n_cast=len(cast_f32))
    return pl.pallas_call(
        kernel,
        out_shape=out_shape,
        grid=(n_rows // ROWS,),
        in_specs=in_specs,
        out_specs=out_specs,
        scratch_shapes=scratch,
        input_output_aliases=aliases,
        compiler_params=pltpu.CompilerParams(
            dimension_semantics=("arbitrary",), vmem_limit_bytes=VMEM_LIMIT),
        name="mixer_ctx" if has_ctx else "mixer_prompt",
    )(*operands)


def _ffn_block(x_ref, out_ref, mod_ref, g2_ref, w1_ref, w2_ref, gf_ref, act_scr, final_norm):
    shift2, scale2, gate2 = mod_ref[3:4, :], mod_ref[4:5, :], mod_ref[5:6, :]
    x = x_ref[...]
    h = _rms_mod(x, g2_ref[...], scale2, shift2).astype(BF16)
    for lo, hi in ((0, FF_SPLIT), (FF_SPLIT, D_FF)):
        gate = _dot(h, w1_ref[:, lo:hi])
        up = _dot(h, w1_ref[:, D_FF + lo:D_FF + hi])
        act_scr[:, lo:hi] = (gate * _sigmoid(gate) * up).astype(BF16)
    y = x + gate2 * _dot(act_scr[...], w2_ref[...])
    if final_norm:
        ms = jnp.mean(y * y, axis=-1, keepdims=True)
        y = y * lax.rsqrt(ms + EPS) * gf_ref[...]
    out_ref[...] = y


def _ffn_kernel(*refs, n_prompt_steps, final_norm, n_cast):
    xp_ref, xs_ref, mod_ref, g2_ref, w1_ref, w2_ref, gf_ref = refs[:7]
    cast_in = refs[7:7 + n_cast]
    yp_ref, ys_ref = refs[7 + n_cast:9 + n_cast]
    cast_out = refs[9 + n_cast:9 + 2 * n_cast]
    act_scr = refs[9 + 2 * n_cast]
    shared = (mod_ref, g2_ref, w1_ref, w2_ref, gf_ref, act_scr, final_norm)
    is_prompt = pl.program_id(0) < n_prompt_steps

    @pl.when(is_prompt)
    def _():
        _ffn_block(xp_ref, yp_ref, *shared)

    @pl.when(jnp.logical_not(is_prompt))
    def _():
        for src, dst in zip(cast_in, cast_out):
            dst[...] = src[...].astype(BF16)
        _ffn_block(xs_ref, ys_ref, *shared)


def _ffn_call(xp, xs, mod4, layer, sample_seq, params, big, final_norm, next_f32=()):
    n_p, n_s = xp.shape[0] // FFN_ROWS, xs.shape[0] // FFN_ROWS
    steps_per_seq = sample_seq // FFN_ROWS

    def prompt_step(i):
        return jnp.minimum(i, n_p - 1)

    def sample_step(i):
        return jnp.maximum(i - n_p, 0)

    def mod_map(i):
        return (layer, jnp.where(i < n_p, 0, 1 + sample_step(i) // steps_per_seq), 0, 0)

    in_specs = [
        pl.BlockSpec((FFN_ROWS, D_MODEL), lambda i: (prompt_step(i), 0)),
        pl.BlockSpec((FFN_ROWS, D_MODEL), lambda i: (sample_step(i), 0)),
        pl.BlockSpec((None, None, N_MOD, D_MODEL), mod_map),
        _layer_spec((1, D_MODEL), layer),
        _const_spec((D_MODEL, 2 * D_FF)),
        _const_spec((D_FF, D_MODEL)),
        _const_spec((1, D_MODEL)),
    ]
    out_shape = [jax.ShapeDtypeStruct(xp.shape, F32), jax.ShapeDtypeStruct(xs.shape, F32)]
    out_specs = [pl.BlockSpec((FFN_ROWS, D_MODEL), lambda i: (prompt_step(i), 0)),
                 pl.BlockSpec((FFN_ROWS, D_MODEL), lambda i: (sample_step(i), 0))]
    for w in next_f32:
        _, rows, cols = w.shape
        slab = rows // n_s
        assert slab * n_s == rows and slab % 16 == 0
        in_specs.append(pl.BlockSpec((None, slab, cols), lambda i: (layer + 1, sample_step(i), 0)))
        out_shape.append(jax.ShapeDtypeStruct((rows, cols), BF16))
        out_specs.append(pl.BlockSpec((slab, cols), lambda i: (sample_step(i), 0)))

    kernel = functools.partial(_ffn_kernel, n_prompt_steps=n_p, final_norm=final_norm, n_cast=len(next_f32))
    return pl.pallas_call(
        kernel,
        out_shape=out_shape,
        grid=(n_p + n_s,),
        in_specs=in_specs,
        out_specs=out_specs,
        scratch_shapes=[pltpu.VMEM((FFN_ROWS, D_FF), BF16)],
        compiler_params=pltpu.CompilerParams(
            dimension_semantics=("arbitrary",), vmem_limit_bytes=VMEM_LIMIT),
        name="ffn",
    )(xp, xs, mod4, params["norm2_g"], big["w_ffn_in"], big["w_ffn_out"], params["final_g"], *next_f32)


def _rope_tables(seq):
    n_rows = seq // GRID_W
    rows = np.repeat(np.arange(n_rows), GRID_W).astype(np.float32)
    cols = np.tile(np.arange(GRID_W), n_rows).astype(np.float32)
    inv = 1.0 / (ROPE_BASE ** (np.arange(0, ROPE_AXIS_DIM, 2, dtype=np.float32) / ROPE_AXIS_DIM))
    ar, ac = rows[:, None] * inv[None], cols[:, None] * inv[None]
    cos_parts, sin_parts = [], []
    for ang in (ar, ac):
        cos_parts += [np.cos(ang), np.cos(ang)]
        sin_parts += [-np.sin(ang), np.sin(ang)]
    cos64 = np.concatenate(cos_parts, axis=1)
    sin64 = np.concatenate(sin_parts, axis=1)
    cos = np.concatenate([cos64, cos64], axis=1).astype(np.float32)
    sin = np.concatenate([sin64, sin64], axis=1).astype(np.float32)
    return jnp.asarray(cos), jnp.asarray(sin)


def _pool_counts(seq):
    t = np.arange(seq)
    cols = []
    for w in POOL_WINDOWS:
        lo = np.clip(t - w // 2, 0, seq)
        hi = np.clip(t + w - w // 2, 0, seq)
        cols.append(np.repeat((hi - lo).astype(np.float32)[:, None], POOL_GROUP, axis=1))
    return jnp.asarray(np.concatenate(cols, axis=1))


def _small_params(norm1_g, w_pool, pool_scale, lam_q1, lam_k1, lam_q2, lam_k2, subln_g,
                  sgu_norm_g, w_sgu, b_sgu, norm2_g, final_g):
    n_groups = len(POOL_WINDOWS)
    eye = jnp.eye(n_groups, dtype=F32)
    w_pool_bd = (eye[None, :, None, :, None] * w_pool[:, :, :, None, :]).reshape(DEPTH, POOL_WIDTH, POOL_WIDTH)
    lam = jnp.zeros((DEPTH, 8, 128), F32)
    lam = lam.at[:, 0, :QK_DIM].set(lam_q1).at[:, 1, :QK_DIM].set(lam_k1)
    lam = lam.at[:, 2, :QK_DIM].set(lam_q2).at[:, 3, :QK_DIM].set(lam_k2)
    return {
        "norm1_g": norm1_g[:, None, :],
        "lam": lam,
        "subln_g": subln_g[:, None, :],
        "w_pool": w_pool_bd.astype(BF16),
        "pool_scale": pool_scale[:, None, :],
        "sgu_norm_g": sgu_norm_g[:, None, :],
        "w_sgu": w_sgu.reshape(DEPTH, 2, 2 * CHUNK, CHUNK).astype(BF16),
        "b_sgu": jnp.repeat(jnp.swapaxes(b_sgu, 1, 2), SGU_WIDTH // 4, axis=2),
        "norm2_g": norm2_g[:, None, :],
        "final_g": final_g[None, :],
    }


BIG_WEIGHTS = ("w_in", "w_out", "w_ffn_in", "w_ffn_out")


def kernel(x_prompt, x_sample, cache_k, cache_v, c, c_ctx, norm1_g, w_ada, b_ada, w_in, w_pool, pool_scale, lam_q1, lam_k1, lam_q2, lam_k2, subln_g, sgu_norm_g, w_sgu, b_sgu, w_out, norm2_g, w_ffn_in, w_ffn_out, final_g):
    batch, seq, _ = x_prompt.shape
    dec_batch, dec_seq, _ = x_sample.shape
    assert ROWS % seq == 0 and dec_seq == ROWS and 1 + dec_batch <= COND_ROWS

    cond = jnp.zeros((COND_ROWS, D_MODEL), F32).at[0].set(c_ctx).at[1:1 + dec_batch].set(c)
    mod4 = _ada_modulation(cond, w_ada, b_ada).reshape(DEPTH, COND_ROWS, N_MOD, D_MODEL)

    ctx = _rope_tables(dec_seq) + (cache_k.reshape(dec_batch, DEPTH, PAST_LEN * N_HEADS, HEAD_COLS),
                                   cache_v.reshape(dec_batch, DEPTH, PAST_LEN * N_HEADS, V_DIM))
    params = _small_params(norm1_g, w_pool, pool_scale, lam_q1, lam_k1, lam_q2, lam_k2, subln_g,
                           sgu_norm_g, w_sgu, b_sgu, norm2_g, final_g)

    big_f32 = (w_in, w_out, w_ffn_in, w_ffn_out)
    big = {"w_in": w_in[0].astype(BF16), "w_out": w_out[0].astype(BF16)}

    xp = x_prompt.reshape(batch * seq, D_MODEL)
    xs = x_sample.reshape(dec_batch * dec_seq, D_MODEL)
    new_cache = None
    for l in range(DEPTH):
        lam_init = 0.8 - 0.6 * math.exp(-0.3 * l)
        last = l == DEPTH - 1

        xp, *new_cache = _mixer_call(xp, mod4, l, seq, False, lam_init, params, big, new_cache=new_cache)
        if l == 0:
            xs, big["w_ffn_in"], big["w_ffn_out"] = _mixer_call(
                xs, mod4, l, dec_seq, True, lam_init, params, big, ctx=ctx, cast_f32=(w_ffn_in, w_ffn_out))
        else:
            xs, = _mixer_call(xs, mod4, l, dec_seq, True, lam_init, params, big, ctx=ctx)
        xp, xs, *next_big = _ffn_call(xp, xs, mod4, l, dec_seq, params, big, last,
                                      next_f32=() if last else big_f32)
        big = dict(zip(BIG_WEIGHTS, next_big))

    y_prompt = xp.reshape(batch, seq, D_MODEL)
    y_sample = xs.reshape(dec_batch, dec_seq, D_MODEL)
    new_cache_k = new_cache[0].reshape(batch, DEPTH, seq, N_HEADS, 2 * QK_DIM)
    new_cache_v = new_cache[1].reshape(batch, DEPTH, seq, N_HEADS, V_DIM)
    return (y_prompt, y_sample, new_cache_k, new_cache_v)
```

```python
import functools
import math
import types

import numpy as np
import jax
import jax.numpy as jnp
from jax import lax
from jax.experimental import pallas as pl
from jax.experimental.pallas import tpu as pltpu

D_MODEL = 1024
DEPTH = 2
GRID_W = 64
POOL_WINDOWS = (2, 4, 8, 16)
POOL_WIDTH = 256
POOL_GROUP = 64
POOL_HALO = 8
ATTN_WIDTH = 512
N_HEADS = 4
V_DIM = 128
QK_DIM = 64
HEAD_COLS = 2 * QK_DIM
ROPE_BASE = 10000.0
ROPE_AXIS_DIM = 32
ROPE_HALF = ROPE_AXIS_DIM // 2
CHUNK = 128
SGU_WIDTH = 256
QK_WIDTH = 512
IN_WIDTH = 2304
MIX_WIDTH = 1024
D_FF = 2816
N_MOD = 6
EPS = 1e-6
PAST_LEN = 256

C_POOL, C_Q, C_K, C_V, C_UV = 0, 256, 768, 1280, 1792
Y_POOL, Y_ATTN, Y_SGU = 0, 256, 768

ROWS = 1024
ROW_CHUNK = 256
ATTN_LOOKAHEAD = 2
SOFTMAX_ROW_SPLIT = 2
CHUNKS_PER_BLOCK = 2
FFN_ROWS = 512
MXU_COLS = 256
FF_SPLIT = (D_FF // MXU_COLS + 1) // 2 * MXU_COLS
COND_ROWS = 16
ADA_COLS = 3072
VMEM_LIMIT = 60 * 1024 * 1024

BF16 = jnp.bfloat16
F32 = jnp.float32


def _dot(a, b):
    return jnp.dot(a, b, preferred_element_type=F32)


def _sigmoid(x):
    return 1.0 / (1.0 + jnp.exp(-x))


def _rms_mod(x, g, scale, shift):
    ms = jnp.mean(x * x, axis=-1, keepdims=True)
    return x * lax.rsqrt(ms + EPS) * (g * (1.0 + scale)) + shift


def _gelu_tanh(x):
    c = math.sqrt(2.0 / math.pi)
    return x * (0.5 * (1.0 + jnp.tanh(c * (x + 0.044715 * (x * x * x)))))


def _const_spec(shape):
    zeros = (0,) * len(shape)
    return pl.BlockSpec(shape, lambda *_: zeros, pipeline_mode=pl.Buffered(1))


def _layer_spec(shape, layer):
    index = (layer,) + (0,) * len(shape)
    return pl.BlockSpec((None,) + tuple(shape), lambda *_: index, pipeline_mode=pl.Buffered(1))


def _ada_kernel(cond_ref, w_ref, b_ref, out_ref):
    cond = cond_ref[...]
    s = (cond * _sigmoid(cond)).astype(BF16)
    out_ref[...] = _dot(s, w_ref[...].astype(BF16)) + b_ref[...]


def _ada_modulation(cond, w_ada, b_ada):
    n_cols = N_MOD * D_MODEL
    return pl.pallas_call(
        _ada_kernel,
        out_shape=jax.ShapeDtypeStruct((DEPTH, COND_ROWS, n_cols), F32),
        grid=(DEPTH, n_cols // ADA_COLS),
        in_specs=[
            pl.BlockSpec((COND_ROWS, D_MODEL), lambda l, j: (0, 0)),
            pl.BlockSpec((None, D_MODEL, ADA_COLS), lambda l, j: (l, 0, j)),
            pl.BlockSpec((None, 1, ADA_COLS), lambda l, j: (l, 0, j)),
        ],
        out_specs=pl.BlockSpec((None, COND_ROWS, ADA_COLS), lambda l, j: (l, 0, j)),
        compiler_params=pltpu.CompilerParams(
            dimension_semantics=("arbitrary", "arbitrary"), vmem_limit_bytes=VMEM_LIMIT),
        name="ada_modulation",
    )(cond, w_ada, b_ada.reshape(DEPTH, 1, n_cols))


def _mixer_body(r, x_ref, cnt_ref, pool_scr, k_scr, v_scr, *, nb, seq, has_ctx, lam_init, cpb):
    kv_base = PAST_LEN if has_ctx else 0
    shift1, scale1, gate1 = r.mod[0:1, :], r.mod[1:2, :], r.mod[2:3, :]

    lane_head = lax.broadcasted_iota(jnp.int32, (ROW_CHUNK, HEAD_COLS), 1)
    first_half = lane_head < QK_DIM

    for b in range(nb):
        pool_scr[b, 0:POOL_HALO, :] = jnp.zeros((POOL_HALO, POOL_WIDTH), F32)
        pool_scr[b, POOL_HALO + seq:POOL_HALO + seq + POOL_HALO, :] = jnp.zeros((POOL_HALO, POOL_WIDTH), F32)
    if not has_ctx:
        for slot in range(r.kc_out.shape[1]):
            if slot != r.cache_slot:
                r.kc_out[:, slot] = jnp.zeros((nb,) + r.kc_out.shape[2:], F32)
                r.vc_out[:, slot] = jnp.zeros((nb,) + r.vc_out.shape[2:], F32)
    if has_ctx:
        for hd in range(N_HEADS):
            cols = slice(hd * HEAD_COLS, (hd + 1) * HEAD_COLS)
            k_scr[0, cols, 0:PAST_LEN] = r.ck[pl.ds(hd, PAST_LEN, stride=N_HEADS), :].T.astype(BF16)
            v_scr[0, 0:PAST_LEN, cols] = r.cv[pl.ds(hd, PAST_LEN, stride=N_HEADS), :].astype(BF16)
        lane_q = lax.broadcasted_iota(jnp.int32, (ROW_CHUNK, QK_WIDTH), 1)
        rope_low = (lane_q % ROPE_AXIS_DIM) < ROPE_HALF

        def rope(t, pos0):
            partner = jnp.where(rope_low,
                                pltpu.roll(t, QK_WIDTH - ROPE_HALF, 1),
                                pltpu.roll(t, ROPE_HALF, 1))
            cos = r.rope_c[pos0:pos0 + ROW_CHUNK, :]
            sin = r.rope_s[pos0:pos0 + ROW_CHUNK, :]
            cos = jnp.concatenate([cos] * N_HEADS, axis=1)
            sin = jnp.concatenate([sin] * N_HEADS, axis=1)
            return t * cos + partner * sin

    for c in range(ROWS // ROW_CHUNK):
        r0 = c * ROW_CHUNK
        b, pos0 = r0 // seq, r0 % seq
        rows = slice(r0, r0 + ROW_CHUNK)
        h = _rms_mod(x_ref[rows, :], r.g1[...], scale1, shift1).astype(BF16)

        pool_scr[b, POOL_HALO + pos0:POOL_HALO + pos0 + ROW_CHUNK, :] = _dot(h, r.w_in[:, C_POOL:C_Q])

        q = _dot(h, r.w_in[:, C_Q:C_K])
        k = _dot(h, r.w_in[:, C_K:C_V])
        if has_ctx:
            q = rope(q, pos0)
            k = rope(k, pos0)
        else:
            for hd in range(N_HEADS):
                r.kc_out[b, r.cache_slot, pl.ds(N_HEADS * pos0 + hd, ROW_CHUNK, stride=N_HEADS), :] = (
                    k[:, hd * HEAD_COLS:(hd + 1) * HEAD_COLS])
        q = q * (QK_DIM ** -0.5 * math.log2(math.e))
        for hd in range(N_HEADS):
            cols = slice(hd * HEAD_COLS, (hd + 1) * HEAD_COLS)
            qh = q[:, cols]
            r.q_scr[c, 0:ROW_CHUNK, cols] = jnp.where(first_half, qh, 0.0).astype(BF16)
            r.q_scr[c, ROW_CHUNK:2 * ROW_CHUNK, cols] = jnp.where(first_half, 0.0, qh).astype(BF16)
        k_scr[b, :, kv_base + pos0:kv_base + pos0 + ROW_CHUNK] = k.T.astype(BF16)

        v = _dot(h, r.w_in[:, C_V:C_UV])
        if not has_ctx:
            for hd in range(N_HEADS):
                r.vc_out[b, r.cache_slot, pl.ds(N_HEADS * pos0 + hd, ROW_CHUNK, stride=N_HEADS), :] = (
                    v[:, hd * V_DIM:(hd + 1) * V_DIM])
        v_scr[b, kv_base + pos0:kv_base + pos0 + ROW_CHUNK, :] = v.astype(BF16)

        uv = _gelu_tanh(_dot(h, r.w_in[:, C_UV:IN_WIDTH]))
        r.u_scr[rows, :] = uv[:, :SGU_WIDTH]
        vv = uv[:, SGU_WIDTH:]
        mu = jnp.mean(vv, axis=-1, keepdims=True)
        vc = vv - mu
        vn = vc * lax.rsqrt(jnp.mean(vc * vc, axis=-1, keepdims=True) + EPS) * r.sgu_g[...]
        r.vs_scr[rows, :] = vn.astype(BF16)

    lane_pool = lax.broadcasted_iota(jnp.int32, (ROW_CHUNK, 2 * POOL_GROUP), 1)
    narrow = lane_pool < POOL_GROUP
    for c in range(ROWS // ROW_CHUNK):
        r0 = c * ROW_CHUNK
        b, pos0 = r0 // seq, r0 % seq
        base = POOL_HALO + pos0
        pooled = []
        for j, (w_small, w_big) in enumerate(((POOL_WINDOWS[0], POOL_WINDOWS[1]),
                                              (POOL_WINDOWS[2], POOL_WINDOWS[3]))):
            cols = slice(j * 2 * POOL_GROUP, (j + 1) * 2 * POOL_GROUP)
            s_small = jnp.zeros((ROW_CHUNK, 2 * POOL_GROUP), F32)
            s_rest = jnp.zeros((ROW_CHUNK, 2 * POOL_GROUP), F32)
            for d in range(-(w_big // 2), w_big - w_big // 2):
                t = pool_scr[b, base + d:base + d + ROW_CHUNK, cols]
                if -(w_small // 2) <= d < w_small - w_small // 2:
                    s_small = s_small + t
                else:
                    s_rest = s_rest + t
            win_sum = jnp.where(narrow, s_small, s_small + s_rest)
            centre = pool_scr[b, base:base + ROW_CHUNK, cols]
            pooled.append(win_sum / cnt_ref[pos0:pos0 + ROW_CHUNK, cols] - centre)
        pooled = jnp.concatenate(pooled, axis=1).astype(BF16)
        y_a = _dot(pooled, r.wpool[...]) * r.pscale[...]
        r.y_scr[r0:r0 + ROW_CHUNK, Y_POOL:Y_ATTN] = y_a.astype(BF16)

    lq1, lk1, lq2, lk2 = r.lam[0:1, :], r.lam[1:2, :], r.lam[2:3, :], r.lam[3:4, :]
    lam = (jnp.exp(jnp.sum(lq1 * lk1, axis=-1, keepdims=True))
           - jnp.exp(jnp.sum(lq2 * lk2, axis=-1, keepdims=True)) + lam_init)
    n_qb = seq // ROW_CHUNK

    lane_sgu = lax.broadcasted_iota(jnp.int32, (CHUNK, 2 * POOL_GROUP), 1)
    sgu_first = lane_sgu < (SGU_WIDTH // 4)
    out_cols = MIX_WIDTH // N_HEADS

    def out_proj(rows, part):
        cols = slice(part * out_cols, (part + 1) * out_cols)
        y = _dot(r.y_scr[rows, :], r.w_out[:, cols])
        r.out[rows, cols] = x_ref[rows, cols] + gate1[:, cols] * y

    def block(blk, with_prev):
        static = isinstance(blk, int)

        def row0(chunk):
            return chunk * ROW_CHUNK if static else pl.multiple_of(chunk * ROW_CHUNK, ROW_CHUNK)

        chunks = [blk * cpb + j for j in range(cpb)]

        for idx in chunks:
            for half in range(ROW_CHUNK // CHUNK):
                sub = pl.ds(row0(idx) + half * CHUNK, CHUNK)
                for j in range(2):
                    cols = slice(j * 128, (j + 1) * 128)
                    t = _dot(r.wsgu[j], r.vs_scr[sub, cols])
                    mixed = jnp.where(sgu_first, t[:CHUNK, :], t[CHUNK:, :]) + r.bsgu[:, cols]
                    y_c = r.u_scr[sub, cols] * mixed
                    r.y_scr[sub, Y_SGU + j * 128:Y_SGU + (j + 1) * 128] = y_c.astype(BF16)

        units = [(idx, hd) for idx in chunks for hd in range(N_HEADS)]
        prev = [(pl.ds(row0(idx - cpb), ROW_CHUNK), part) for idx in chunks for part in range(N_HEADS)]

        def scores(idx, hd):
            cols = slice(hd * HEAD_COLS, (hd + 1) * HEAD_COLS)
            s = _dot(r.q_scr[idx, :, cols], k_scr[idx // n_qb, cols, :])
            return s[:ROW_CHUNK], s[ROW_CHUNK:]

        pending = [scores(*u) for u in units[:ATTN_LOOKAHEAD]]
        for n, (idx, hd) in enumerate(units):
            cols = slice(hd * HEAD_COLS, (hd + 1) * HEAD_COLS)
            s1, s2 = pending.pop(0)
            if n + ATTN_LOOKAHEAD < len(units):
                pending.append(scores(*units[n + ATTN_LOOKAHEAD]))
            p_parts, l1_parts = [], []
            slab = ROW_CHUNK // SOFTMAX_ROW_SPLIT
            for part in range(SOFTMAX_ROW_SPLIT):
                t1 = s1[part * slab:(part + 1) * slab]
                t2 = s2[part * slab:(part + 1) * slab]
                e1 = jnp.exp2(t1 - jnp.max(t1, axis=-1, keepdims=True))
                e2 = jnp.exp2(t2 - jnp.max(t2, axis=-1, keepdims=True))
                l1 = jnp.sum(e1, axis=-1, keepdims=True)
                l2 = jnp.sum(e2, axis=-1, keepdims=True)
                p_parts.append((e1 - e2 * (lam * l1 / l2)).astype(BF16))
                l1_parts.append(l1)
            p = jnp.concatenate(p_parts, axis=0)
            l1 = jnp.concatenate(l1_parts, axis=0)
            o = _dot(p, v_scr[idx // n_qb, :, cols]) * (1.0 / l1)
            if with_prev:
                out_proj(*prev[n])
            o = o * lax.rsqrt(jnp.mean(o * o, axis=-1, keepdims=True) + EPS) * r.subln[...]
            o = o * (1.0 - lam_init)
            r.y_scr[pl.ds(row0(idx), ROW_CHUNK), Y_ATTN + hd * V_DIM:Y_ATTN + (hd + 1) * V_DIM] = o.astype(BF16)

    n_blocks = ROWS // (ROW_CHUNK * cpb)
    block(0, False)
    if n_blocks == 2:
        block(1, True)
    else:
        def loop_block(blk, carry):
            block(blk, True)
            return carry

        lax.fori_loop(1, n_blocks, loop_block, 0)
    for idx in range((n_blocks - 1) * cpb, n_blocks * cpb):
        for part in range(N_HEADS):
            out_proj(pl.ds(idx * ROW_CHUNK, ROW_CHUNK), part)


def _mixer_kernel(*refs, nb, seq, has_ctx, lam_init, cache_slot, n_passthrough, n_cast):
    it = iter(refs)
    r = types.SimpleNamespace(cache_slot=cache_slot)
    x_ref, r.mod, r.g1, r.w_in = next(it), next(it), next(it), next(it)
    if has_ctx:
        r.rope_c, r.rope_s, r.ck, r.cv = next(it), next(it), next(it), next(it)
    (r.lam, r.subln, cnt_ref, r.wpool, r.pscale, r.sgu_g, r.wsgu, r.bsgu, r.w_out) = (next(it) for _ in range(9))
    for _ in range(n_passthrough):
        next(it)
    cast_in = [next(it) for _ in range(n_cast)]
    r.out = next(it)
    if not has_ctx:
        r.kc_out, r.vc_out = next(it), next(it)
    cast_out = [next(it) for _ in range(n_cast)]
    pool_scr, r.q_scr, k_scr, v_scr, r.u_scr, r.vs_scr, r.y_scr = (next(it) for _ in range(7))
    for src, dst in zip(cast_in, cast_out):
        dst[...] = src[...].astype(BF16)
    _mixer_body(r, x_ref, cnt_ref, pool_scr, k_scr, v_scr, nb=nb, seq=seq, has_ctx=has_ctx, lam_init=lam_init,
                cpb=CHUNKS_PER_BLOCK)


def _mixer_call(x2d, mod4, layer, seq, has_ctx, lam_init, params, big, ctx=None, new_cache=None, cast_f32=()):
    n_rows = x2d.shape[0]
    nb = ROWS // seq
    n_seq = n_rows // seq
    kv_len = seq + (PAST_LEN if has_ctx else 0)

    if has_ctx:
        mod_map = lambda i: (layer, 1 + i, 0, 0)
    else:
        mod_map = lambda i: (layer, 0, 0, 0)

    operands = [x2d, mod4, params["norm1_g"], big["w_in"]]
    in_specs = [
        pl.BlockSpec((ROWS, D_MODEL), lambda i: (i, 0)),
        pl.BlockSpec((None, None, N_MOD, D_MODEL), mod_map),
        _layer_spec((1, D_MODEL), layer),
        _const_spec((D_MODEL, IN_WIDTH)),
    ]
    if has_ctx:
        operands += list(ctx)
        in_specs += [
            _const_spec((seq, HEAD_COLS)),
            _const_spec((seq, HEAD_COLS)),
            pl.BlockSpec((None, None, PAST_LEN * N_HEADS, HEAD_COLS), lambda i: (i, layer, 0, 0)),
            pl.BlockSpec((None, None, PAST_LEN * N_HEADS, V_DIM), lambda i: (i, layer, 0, 0)),
        ]
    operands += [params["lam"], params["subln_g"], _pool_counts(seq), params["w_pool"], params["pool_scale"],
                 params["sgu_norm_g"], params["w_sgu"], params["b_sgu"], big["w_out"]]
    in_specs += [
        _layer_spec((8, 128), layer),
        _layer_spec((1, V_DIM), layer),
        _const_spec((seq, POOL_WIDTH)),
        _layer_spec((POOL_WIDTH, POOL_WIDTH), layer),
        _layer_spec((1, POOL_WIDTH), layer),
        _layer_spec((1, SGU_WIDTH), layer),
        _layer_spec((2, 2 * CHUNK, CHUNK), layer),
        _layer_spec((CHUNK, SGU_WIDTH), layer),
        _const_spec((MIX_WIDTH, D_MODEL)),
    ]

    out_shape = [jax.ShapeDtypeStruct((n_rows, D_MODEL), F32)]
    out_specs = [pl.BlockSpec((ROWS, D_MODEL), lambda i: (i, 0))]
    aliases = {}
    cache_slot = 0
    if not has_ctx:
        out_shape += [jax.ShapeDtypeStruct((n_seq, DEPTH, seq * N_HEADS, HEAD_COLS), F32),
                      jax.ShapeDtypeStruct((n_seq, DEPTH, seq * N_HEADS, V_DIM), F32)]
        if new_cache is None:
            cache_slot = layer
            out_specs += [pl.BlockSpec((nb, DEPTH, seq * N_HEADS, HEAD_COLS), lambda i: (i, 0, 0, 0)),
                          pl.BlockSpec((nb, DEPTH, seq * N_HEADS, V_DIM), lambda i: (i, 0, 0, 0))]
        else:
            out_specs += [pl.BlockSpec((nb, 1, seq * N_HEADS, HEAD_COLS), lambda i: (i, layer, 0, 0)),
                          pl.BlockSpec((nb, 1, seq * N_HEADS, V_DIM), lambda i: (i, layer, 0, 0))]
            aliases = {len(operands): 1, len(operands) + 1: 2}
            operands += list(new_cache)
            in_specs += [pl.BlockSpec(memory_space=pl.ANY)] * 2
    n_steps = n_rows // ROWS
    for w in cast_f32:
        _, rows, cols = w.shape
        slab = rows // n_steps
        assert slab * n_steps == rows and slab % 16 == 0
        operands.append(w)
        in_specs.append(pl.BlockSpec((None, slab, cols), lambda i: (layer, i, 0)))
        out_shape.append(jax.ShapeDtypeStruct((rows, cols), BF16))
        out_specs.append(pl.BlockSpec((slab, cols), lambda i: (i, 0)))

    scratch = [
        pltpu.VMEM((nb, seq + 2 * POOL_HALO, POOL_WIDTH), F32),
        pltpu.VMEM((ROWS // ROW_CHUNK, 2 * ROW_CHUNK, QK_WIDTH), BF16),
        pltpu.VMEM((nb, QK_WIDTH, kv_len), BF16),
        pltpu.VMEM((nb, kv_len, ATTN_WIDTH), BF16),
        pltpu.VMEM((ROWS, SGU_WIDTH), F32),
        pltpu.VMEM((ROWS, SGU_WIDTH), BF16),
        pltpu.VMEM((ROWS, MIX_WIDTH), BF16),
    ]
    kernel = functools.partial(_mixer_kernel, nb=nb, seq=seq, has_ctx=has_ctx, lam_init=lam_init,
                               cache_slot=cache_slot, n_passthrough=len(aliases), n_cast=len(cast_f32))
    return pl.pallas_call(
        kernel,
        out_shape=out_shape,
        grid=(n_rows // ROWS,),
        in_specs=in_specs,
        out_specs=out_specs,
        scratch_shapes=scratch,
        input_output_aliases=aliases,
        compiler_params=pltpu.CompilerParams(
            dimension_semantics=("arbitrary",), vmem_limit_bytes=VMEM_LIMIT),
        name="mixer_ctx" if has_ctx else "mixer_prompt",
    )(*operands)


def _ffn_block(x_ref, out_ref, mod_ref, g2_ref, w1_ref, w2_ref, gf_ref, act_scr, final_norm):
    shift2, scale2, gate2 = mod_ref[3:4, :], mod_ref[4:5, :], mod_ref[5:6, :]
    x = x_ref[...]
    h = _rms_mod(x, g2_ref[...], scale2, shift2).astype(BF16)
    for lo, hi in ((0, FF_SPLIT), (FF_SPLIT, D_FF)):
        gate = _dot(h, w1_ref[:, lo:hi])
        up = _dot(h, w1_ref[:, D_FF + lo:D_FF + hi])
        act_scr[:, lo:hi] = (gate * _sigmoid(gate) * up).astype(BF16)
    y = x + gate2 * _dot(act_scr[...], w2_ref[...])
    if final_norm:
        ms = jnp.mean(y * y, axis=-1, keepdims=True)
        y = y * lax.rsqrt(ms + EPS) * gf_ref[...]
    out_ref[...] = y


def _ffn_kernel(*refs, n_prompt_steps, final_norm, n_cast):
    xp_ref, xs_ref, mod_ref, g2_ref, w1_ref, w2_ref, gf_ref = refs[:7]
    cast_in = refs[7:7 + n_cast]
    yp_ref, ys_ref = refs[7 + n_cast:9 + n_cast]
    cast_out = refs[9 + n_cast:9 + 2 * n_cast]
    act_scr = refs[9 + 2 * n_cast]
    shared = (mod_ref, g2_ref, w1_ref, w2_ref, gf_ref, act_scr, final_norm)
    is_prompt = pl.program_id(0) < n_prompt_steps

    @pl.when(is_prompt)
    def _():
        _ffn_block(xp_ref, yp_ref, *shared)

    @pl.when(jnp.logical_not(is_prompt))
    def _():
        for src, dst in zip(cast_in, cast_out):
            dst[...] = src[...].astype(BF16)
        _ffn_block(xs_ref, ys_ref, *shared)


def _ffn_call(xp, xs, mod4, layer, sample_seq, params, big, final_norm, next_f32=()):
    n_p, n_s = xp.shape[0] // FFN_ROWS, xs.shape[0] // FFN_ROWS
    steps_per_seq = sample_seq // FFN_ROWS

    def prompt_step(i):
        return jnp.minimum(i, n_p - 1)

    def sample_step(i):
        return jnp.maximum(i - n_p, 0)

    def mod_map(i):
        return (layer, jnp.where(i < n_p, 0, 1 + sample_step(i) // steps_per_seq), 0, 0)

    in_specs = [
        pl.BlockSpec((FFN_ROWS, D_MODEL), lambda i: (prompt_step(i), 0)),
        pl.BlockSpec((FFN_ROWS, D_MODEL), lambda i: (sample_step(i), 0)),
        pl.BlockSpec((None, None, N_MOD, D_MODEL), mod_map),
        _layer_spec((1, D_MODEL), layer),
        _const_spec((D_MODEL, 2 * D_FF)),
        _const_spec((D_FF, D_MODEL)),
        _const_spec((1, D_MODEL)),
    ]
    out_shape = [jax.ShapeDtypeStruct(xp.shape, F32), jax.ShapeDtypeStruct(xs.shape, F32)]
    out_specs = [pl.BlockSpec((FFN_ROWS, D_MODEL), lambda i: (prompt_step(i), 0)),
                 pl.BlockSpec((FFN_ROWS, D_MODEL), lambda i: (sample_step(i), 0))]
    for w in next_f32:
        _, rows, cols = w.shape
        slab = rows // n_s
        assert slab * n_s == rows and slab % 16 == 0
        in_specs.append(pl.BlockSpec((None, slab, cols), lambda i: (layer + 1, sample_step(i), 0)))
        out_shape.append(jax.ShapeDtypeStruct((rows, cols), BF16))
        out_specs.append(pl.BlockSpec((slab, cols), lambda i: (sample_step(i), 0)))

    kernel = functools.partial(_ffn_kernel, n_prompt_steps=n_p, final_norm=final_norm, n_cast=len(next_f32))
    return pl.pallas_call(
        kernel,
        out_shape=out_shape,
        grid=(n_p + n_s,),
        in_specs=in_specs,
        out_specs=out_specs,
        scratch_shapes=[pltpu.VMEM((FFN_ROWS, D_FF), BF16)],
        compiler_params=pltpu.CompilerParams(
            dimension_semantics=("arbitrary",), vmem_limit_bytes=VMEM_LIMIT),
        name="ffn",
    )(xp, xs, mod4, params["norm2_g"], big["w_ffn_in"], big["w_ffn_out"], params["final_g"], *next_f32)


def _rope_tables(seq):
    n_rows = seq // GRID_W
    rows = np.repeat(np.arange(n_rows), GRID_W).astype(np.float32)
    cols = np.tile(np.arange(GRID_W), n_rows).astype(np.float32)
    inv = 1.0 / (ROPE_BASE ** (np.arange(0, ROPE_AXIS_DIM, 2, dtype=np.float32) / ROPE_AXIS_DIM))
    ar, ac = rows[:, None] * inv[None], cols[:, None] * inv[None]
    cos_parts, sin_parts = [], []
    for ang in (ar, ac):
        cos_parts += [np.cos(ang), np.cos(ang)]
        sin_parts += [-np.sin(ang), np.sin(ang)]
    cos64 = np.concatenate(cos_parts, axis=1)
    sin64 = np.concatenate(sin_parts, axis=1)
    cos = np.concatenate([cos64, cos64], axis=1).astype(np.float32)
    sin = np.concatenate([sin64, sin64], axis=1).astype(np.float32)
    return jnp.asarray(cos), jnp.asarray(sin)


def _pool_counts(seq):
    t = np.arange(seq)
    cols = []
    for w in POOL_WINDOWS:
        lo = np.clip(t - w // 2, 0, seq)
        hi = np.clip(t + w - w // 2, 0, seq)
        cols.append(np.repeat((hi - lo).astype(np.float32)[:, None], POOL_GROUP, axis=1))
    return jnp.asarray(np.concatenate(cols, axis=1))


def _small_params(norm1_g, w_pool, pool_scale, lam_q1, lam_k1, lam_q2, lam_k2, subln_g,
                  sgu_norm_g, w_sgu, b_sgu, norm2_g, final_g):
    n_groups = len(POOL_WINDOWS)
    eye = jnp.eye(n_groups, dtype=F32)
    w_pool_bd = (eye[None, :, None, :, None] * w_pool[:, :, :, None, :]).reshape(DEPTH, POOL_WIDTH, POOL_WIDTH)
    lam = jnp.zeros((DEPTH, 8, 128), F32)
    lam = lam.at[:, 0, :QK_DIM].set(lam_q1).at[:, 1, :QK_DIM].set(lam_k1)
    lam = lam.at[:, 2, :QK_DIM].set(lam_q2).at[:, 3, :QK_DIM].set(lam_k2)
    return {
        "norm1_g": norm1_g[:, None, :],
        "lam": lam,
        "subln_g": subln_g[:, None, :],
        "w_pool": w_pool_bd.astype(BF16),
        "pool_scale": pool_scale[:, None, :],
        "sgu_norm_g": sgu_norm_g[:, None, :],
        "w_sgu": w_sgu.reshape(DEPTH, 2, 2 * CHUNK, CHUNK).astype(BF16),
        "b_sgu": jnp.repeat(jnp.swapaxes(b_sgu, 1, 2), SGU_WIDTH // 4, axis=2),
        "norm2_g": norm2_g[:, None, :],
        "final_g": final_g[None, :],
    }


BIG_WEIGHTS = ("w_in", "w_out", "w_ffn_in", "w_ffn_out")


def kernel(x_prompt, x_sample, cache_k, cache_v, c, c_ctx, norm1_g, w_ada, b_ada, w_in, w_pool, pool_scale, lam_q1, lam_k1, lam_q2, lam_k2, subln_g, sgu_norm_g, w_sgu, b_sgu, w_out, norm2_g, w_ffn_in, w_ffn_out, final_g):
    batch, seq, _ = x_prompt.shape
    dec_batch, dec_seq, _ = x_sample.shape
    assert ROWS % seq == 0 and dec_seq == ROWS and 1 + dec_batch <= COND_ROWS

    cond = jnp.zeros((COND_ROWS, D_MODEL), F32).at[0].set(c_ctx).at[1:1 + dec_batch].set(c)
    mod4 = _ada_modulation(cond, w_ada, b_ada).reshape(DEPTH, COND_ROWS, N_MOD, D_MODEL)

    ctx = _rope_tables(dec_seq) + (cache_k.reshape(dec_batch, DEPTH, PAST_LEN * N_HEADS, HEAD_COLS),
                                   cache_v.reshape(dec_batch, DEPTH, PAST_LEN * N_HEADS, V_DIM))
    params = _small_params(norm1_g, w_pool, pool_scale, lam_q1, lam_k1, lam_q2, lam_k2, subln_g,
                           sgu_norm_g, w_sgu, b_sgu, norm2_g, final_g)

    big_f32 = (w_in, w_out, w_ffn_in, w_ffn_out)
    big = {"w_in": w_in[0].astype(BF16), "w_out": w_out[0].astype(BF16)}

    xp = x_prompt.reshape(batch * seq, D_MODEL)
    xs = x_sample.reshape(dec_batch * dec_seq, D_MODEL)
    new_cache = None
    for l in range(DEPTH):
        lam_init = 0.8 - 0.6 * math.exp(-0.3 * l)
        last = l == DEPTH - 1

        xp, *new_cache = _mixer_call(xp, mod4, l, seq, False, lam_init, params, big, new_cache=new_cache)
        if l == 0:
            xs, big["w_ffn_in"], big["w_ffn_out"] = _mixer_call(
                xs, mod4, l, dec_seq, True, lam_init, params, big, ctx=ctx, cast_f32=(w_ffn_in, w_ffn_out))
        else:
            xs, = _mixer_call(xs, mod4, l, dec_seq, True, lam_init, params, big, ctx=ctx)
        xp, xs, *next_big = _ffn_call(xp, xs, mod4, l, dec_seq, params, big, last,
                                      next_f32=() if last else big_f32)
        big = dict(zip(BIG_WEIGHTS, next_big))

    y_prompt = xp.reshape(batch, seq, D_MODEL)
    y_sample = xs.reshape(dec_batch, dec_seq, D_MODEL)
    new_cache_k = new_cache[0].reshape(batch, DEPTH, seq, N_HEADS, 2 * QK_DIM)
    new_cache_v = new_cache[1].reshape(batch, DEPTH, seq, N_HEADS, V_DIM)
    return (y_prompt, y_sample, new_cache_k, new_cache_v)
```

```python
import functools
import math
import types

import numpy as np
import jax
import jax.numpy as jnp
from jax import lax
from jax.experimental import pallas as pl
from jax.experimental.pallas import tpu as pltpu

D_MODEL = 1024
DEPTH = 2
GRID_W = 64
POOL_WINDOWS = (2, 4, 8, 16)
POOL_WIDTH = 256
POOL_GROUP = 64
POOL_HALO = 8
ATTN_WIDTH = 512
N_HEADS = 4
V_DIM = 128
QK_DIM = 64
HEAD_COLS = 2 * QK_DIM
ROPE_BASE = 10000.0
ROPE_AXIS_DIM = 32
ROPE_HALF = ROPE_AXIS_DIM // 2
CHUNK = 128
SGU_WIDTH = 256
QK_WIDTH = 512
IN_WIDTH = 2304
MIX_WIDTH = 1024
D_FF = 2816
N_MOD = 6
EPS = 1e-6
PAST_LEN = 256

C_POOL, C_Q, C_K, C_V, C_UV = 0, 256, 768, 1280, 1792
Y_POOL, Y_ATTN, Y_SGU = 0, 256, 768

ROWS = 1024
ROW_CHUNK = 256
ATTN_LOOKAHEAD = 2
CHUNKS_PER_BLOCK = 2
FFN_ROWS = 512
MXU_COLS = 256
FF_SPLIT = (D_FF // MXU_COLS + 1) // 2 * MXU_COLS
COND_ROWS = 16
ADA_COLS = 3072
VMEM_LIMIT = 58 * 1024 * 1024

BF16 = jnp.bfloat16
F32 = jnp.float32


def _dot(a, b):
    return jnp.dot(a, b, preferred_element_type=F32)


def _sigmoid(x):
    return 1.0 / (1.0 + jnp.exp(-x))


def _rms_mod(x, g, scale, shift):
    ms = jnp.mean(x * x, axis=-1, keepdims=True)
    return x * lax.rsqrt(ms + EPS) * (g * (1.0 + scale)) + shift


def _gelu_tanh(x):
    c = math.sqrt(2.0 / math.pi)
    return x * (0.5 * (1.0 + jnp.tanh(c * (x + 0.044715 * (x * x * x)))))


def _const_spec(shape):
    zeros = (0,) * len(shape)
    return pl.BlockSpec(shape, lambda *_: zeros, pipeline_mode=pl.Buffered(1))


def _layer_spec(shape, layer):
    index = (layer,) + (0,) * len(shape)
    return pl.BlockSpec((None,) + tuple(shape), lambda *_: index, pipeline_mode=pl.Buffered(1))


def _ada_kernel(cond_ref, w_ref, b_ref, out_ref):
    cond = cond_ref[...]
    s = (cond * _sigmoid(cond)).astype(BF16)
    out_ref[...] = _dot(s, w_ref[...].astype(BF16)) + b_ref[...]


def _ada_modulation(cond, w_ada, b_ada):
    n_cols = N_MOD * D_MODEL
    return pl.pallas_call(
        _ada_kernel,
        out_shape=jax.ShapeDtypeStruct((DEPTH, COND_ROWS, n_cols), F32),
        grid=(DEPTH, n_cols // ADA_COLS),
        in_specs=[
            pl.BlockSpec((COND_ROWS, D_MODEL), lambda l, j: (0, 0)),
            pl.BlockSpec((None, D_MODEL, ADA_COLS), lambda l, j: (l, 0, j)),
            pl.BlockSpec((None, 1, ADA_COLS), lambda l, j: (l, 0, j)),
        ],
        out_specs=pl.BlockSpec((None, COND_ROWS, ADA_COLS), lambda l, j: (l, 0, j)),
        compiler_params=pltpu.CompilerParams(
            dimension_semantics=("arbitrary", "arbitrary"), vmem_limit_bytes=VMEM_LIMIT),
        name="ada_modulation",
    )(cond, w_ada, b_ada.reshape(DEPTH, 1, n_cols))


def _mixer_body(r, x_ref, cnt_ref, pool_scr, k_scr, v_scr, *, nb, seq, has_ctx, lam_init, cpb):
    kv_base = PAST_LEN if has_ctx else 0
    shift1, scale1, gate1 = r.mod[0:1, :], r.mod[1:2, :], r.mod[2:3, :]

    lane_head = lax.broadcasted_iota(jnp.int32, (ROW_CHUNK, HEAD_COLS), 1)
    first_half = lane_head < QK_DIM

    for b in range(nb):
        pool_scr[b, 0:POOL_HALO, :] = jnp.zeros((POOL_HALO, POOL_WIDTH), F32)
        pool_scr[b, POOL_HALO + seq:POOL_HALO + seq + POOL_HALO, :] = jnp.zeros((POOL_HALO, POOL_WIDTH), F32)
    if not has_ctx:
        for slot in range(r.kc_out.shape[1]):
            if slot != r.cache_slot:
                r.kc_out[:, slot] = jnp.zeros((nb,) + r.kc_out.shape[2:], F32)
                r.vc_out[:, slot] = jnp.zeros((nb,) + r.vc_out.shape[2:], F32)
    if has_ctx:
        for hd in range(N_HEADS):
            cols = slice(hd * HEAD_COLS, (hd + 1) * HEAD_COLS)
            k_scr[0, cols, 0:PAST_LEN] = r.ck[pl.ds(hd, PAST_LEN, stride=N_HEADS), :].T.astype(BF16)
            v_scr[0, 0:PAST_LEN, cols] = r.cv[pl.ds(hd, PAST_LEN, stride=N_HEADS), :].astype(BF16)
        lane_q = lax.broadcasted_iota(jnp.int32, (ROW_CHUNK, QK_WIDTH), 1)
        rope_low = (lane_q % ROPE_AXIS_DIM) < ROPE_HALF

        def rope(t, pos0):
            partner = jnp.where(rope_low,
                                pltpu.roll(t, QK_WIDTH - ROPE_HALF, 1),
                                pltpu.roll(t, ROPE_HALF, 1))
            cos = r.rope_c[pos0:pos0 + ROW_CHUNK, :]
            sin = r.rope_s[pos0:pos0 + ROW_CHUNK, :]
            cos = jnp.concatenate([cos] * N_HEADS, axis=1)
            sin = jnp.concatenate([sin] * N_HEADS, axis=1)
            return t * cos + partner * sin

    for c in range(ROWS // ROW_CHUNK):
        r0 = c * ROW_CHUNK
        b, pos0 = r0 // seq, r0 % seq
        rows = slice(r0, r0 + ROW_CHUNK)
        h = _rms_mod(x_ref[rows, :], r.g1[...], scale1, shift1).astype(BF16)

        pool_scr[b, POOL_HALO + pos0:POOL_HALO + pos0 + ROW_CHUNK, :] = _dot(h, r.w_in[:, C_POOL:C_Q])

        q = _dot(h, r.w_in[:, C_Q:C_K])
        k = _dot(h, r.w_in[:, C_K:C_V])
        if has_ctx:
            q = rope(q, pos0)
            k = rope(k, pos0)
        else:
            for hd in range(N_HEADS):
                r.kc_out[b, r.cache_slot, pl.ds(N_HEADS * pos0 + hd, ROW_CHUNK, stride=N_HEADS), :] = (
                    k[:, hd * HEAD_COLS:(hd + 1) * HEAD_COLS])
        q = q * (QK_DIM ** -0.5 * math.log2(math.e))
        for hd in range(N_HEADS):
            cols = slice(hd * HEAD_COLS, (hd + 1) * HEAD_COLS)
            qh = q[:, cols]
            r.q_scr[c, 0:ROW_CHUNK, cols] = jnp.where(first_half, qh, 0.0).astype(BF16)
            r.q_scr[c, ROW_CHUNK:2 * ROW_CHUNK, cols] = jnp.where(first_half, 0.0, qh).astype(BF16)
        k_scr[b, :, kv_base + pos0:kv_base + pos0 + ROW_CHUNK] = k.T.astype(BF16)

        v = _dot(h, r.w_in[:, C_V:C_UV])
        if not has_ctx:
            for hd in range(N_HEADS):
                r.vc_out[b, r.cache_slot, pl.ds(N_HEADS * pos0 + hd, ROW_CHUNK, stride=N_HEADS), :] = (
                    v[:, hd * V_DIM:(hd + 1) * V_DIM])
        v_scr[b, kv_base + pos0:kv_base + pos0 + ROW_CHUNK, :] = v.astype(BF16)

        uv = _gelu_tanh(_dot(h, r.w_in[:, C_UV:IN_WIDTH]))
        r.u_scr[rows, :] = uv[:, :SGU_WIDTH]
        vv = uv[:, SGU_WIDTH:]
        mu = jnp.mean(vv, axis=-1, keepdims=True)
        vc = vv - mu
        vn = vc * lax.rsqrt(jnp.mean(vc * vc, axis=-1, keepdims=True) + EPS) * r.sgu_g[...]
        r.vs_scr[rows, :] = vn.astype(BF16)

    lane_pool = lax.broadcasted_iota(jnp.int32, (ROW_CHUNK, 2 * POOL_GROUP), 1)
    narrow = lane_pool < POOL_GROUP
    for c in range(ROWS // ROW_CHUNK):
        r0 = c * ROW_CHUNK
        b, pos0 = r0 // seq, r0 % seq
        base = POOL_HALO + pos0
        pooled = []
        for j, (w_small, w_big) in enumerate(((POOL_WINDOWS[0], POOL_WINDOWS[1]),
                                              (POOL_WINDOWS[2], POOL_WINDOWS[3]))):
            cols = slice(j * 2 * POOL_GROUP, (j + 1) * 2 * POOL_GROUP)
            s_small = jnp.zeros((ROW_CHUNK, 2 * POOL_GROUP), F32)
            s_rest = jnp.zeros((ROW_CHUNK, 2 * POOL_GROUP), F32)
            for d in range(-(w_big // 2), w_big - w_big // 2):
                t = pool_scr[b, base + d:base + d + ROW_CHUNK, cols]
                if -(w_small // 2) <= d < w_small - w_small // 2:
                    s_small = s_small + t
                else:
                    s_rest = s_rest + t
            win_sum = jnp.where(narrow, s_small, s_small + s_rest)
            centre = pool_scr[b, base:base + ROW_CHUNK, cols]
            pooled.append(win_sum / cnt_ref[pos0:pos0 + ROW_CHUNK, cols] - centre)
        pooled = jnp.concatenate(pooled, axis=1).astype(BF16)
        y_a = _dot(pooled, r.wpool[...]) * r.pscale[...]
        r.y_scr[r0:r0 + ROW_CHUNK, Y_POOL:Y_ATTN] = y_a.astype(BF16)

    lq1, lk1, lq2, lk2 = r.lam[0:1, :], r.lam[1:2, :], r.lam[2:3, :], r.lam[3:4, :]
    lam = (jnp.exp(jnp.sum(lq1 * lk1, axis=-1, keepdims=True))
           - jnp.exp(jnp.sum(lq2 * lk2, axis=-1, keepdims=True)) + lam_init)
    n_qb = seq // ROW_CHUNK

    lane_sgu = lax.broadcasted_iota(jnp.int32, (CHUNK, 2 * POOL_GROUP), 1)
    sgu_first = lane_sgu < (SGU_WIDTH // 4)
    out_cols = MIX_WIDTH // N_HEADS

    def out_proj(rows, part):
        cols = slice(part * out_cols, (part + 1) * out_cols)
        y = _dot(r.y_scr[rows, :], r.w_out[:, cols])
        r.out[rows, cols] = x_ref[rows, cols] + gate1[:, cols] * y

    def block(blk, with_prev):
        static = isinstance(blk, int)

        def row0(chunk):
            return chunk * ROW_CHUNK if static else pl.multiple_of(chunk * ROW_CHUNK, ROW_CHUNK)

        chunks = [blk * cpb + j for j in range(cpb)]

        for idx in chunks:
            for half in range(ROW_CHUNK // CHUNK):
                sub = pl.ds(row0(idx) + half * CHUNK, CHUNK)
                for j in range(2):
                    cols = slice(j * 128, (j + 1) * 128)
                    t = _dot(r.wsgu[j], r.vs_scr[sub, cols])
                    mixed = jnp.where(sgu_first, t[:CHUNK, :], t[CHUNK:, :]) + r.bsgu[:, cols]
                    y_c = r.u_scr[sub, cols] * mixed
                    r.y_scr[sub, Y_SGU + j * 128:Y_SGU + (j + 1) * 128] = y_c.astype(BF16)

        units = [(idx, hd) for idx in chunks for hd in range(N_HEADS)]
        prev = [(pl.ds(row0(idx - cpb), ROW_CHUNK), part) for idx in chunks for part in range(N_HEADS)]

        def scores(idx, hd):
            cols = slice(hd * HEAD_COLS, (hd + 1) * HEAD_COLS)
            s = _dot(r.q_scr[idx, :, cols], k_scr[idx // n_qb, cols, :])
            return s[:ROW_CHUNK], s[ROW_CHUNK:]

        pending = [scores(*u) for u in units[:ATTN_LOOKAHEAD]]
        for n, (idx, hd) in enumerate(units):
            cols = slice(hd * HEAD_COLS, (hd + 1) * HEAD_COLS)
            s1, s2 = pending.pop(0)
            if n + ATTN_LOOKAHEAD < len(units):
                pending.append(scores(*units[n + ATTN_LOOKAHEAD]))
            e1 = jnp.exp2(s1 - jnp.max(s1, axis=-1, keepdims=True))
            e2 = jnp.exp2(s2 - jnp.max(s2, axis=-1, keepdims=True))
            l1 = jnp.sum(e1, axis=-1, keepdims=True)
            l2 = jnp.sum(e2, axis=-1, keepdims=True)
            p = (e1 - e2 * (lam * l1 / l2)).astype(BF16)
            o = _dot(p, v_scr[idx // n_qb, :, cols]) * (1.0 / l1)
            if with_prev:
                out_proj(*prev[n])
            o = o * lax.rsqrt(jnp.mean(o * o, axis=-1, keepdims=True) + EPS) * r.subln[...]
            o = o * (1.0 - lam_init)
            r.y_scr[pl.ds(row0(idx), ROW_CHUNK), Y_ATTN + hd * V_DIM:Y_ATTN + (hd + 1) * V_DIM] = o.astype(BF16)

    n_blocks = ROWS // (ROW_CHUNK * cpb)
    block(0, False)
    if n_blocks == 2:
        block(1, True)
    else:
        def loop_block(blk, carry):
            block(blk, True)
            return carry

        lax.fori_loop(1, n_blocks, loop_block, 0)
    for idx in range((n_blocks - 1) * cpb, n_blocks * cpb):
        for part in range(N_HEADS):
            out_proj(pl.ds(idx * ROW_CHUNK, ROW_CHUNK), part)


def _mixer_kernel(*refs, nb, seq, has_ctx, lam_init, cache_slot, n_passthrough, n_cast):
    it = iter(refs)
    r = types.SimpleNamespace(cache_slot=cache_slot)
    x_ref, r.mod, r.g1, r.w_in = next(it), next(it), next(it), next(it)
    if has_ctx:
        r.rope_c, r.rope_s, r.ck, r.cv = next(it), next(it), next(it), next(it)
    (r.lam, r.subln, cnt_ref, r.wpool, r.pscale, r.sgu_g, r.wsgu, r.bsgu, r.w_out) = (next(it) for _ in range(9))
    for _ in range(n_passthrough):
        next(it)
    cast_in = [next(it) for _ in range(n_cast)]
    r.out = next(it)
    if not has_ctx:
        r.kc_out, r.vc_out = next(it), next(it)
    cast_out = [next(it) for _ in range(n_cast)]
    pool_scr, r.q_scr, k_scr, v_scr, r.u_scr, r.vs_scr, r.y_scr = (next(it) for _ in range(7))
    for src, dst in zip(cast_in, cast_out):
        dst[...] = src[...].astype(BF16)
    _mixer_body(r, x_ref, cnt_ref, pool_scr, k_scr, v_scr, nb=nb, seq=seq, has_ctx=has_ctx, lam_init=lam_init,
                cpb=CHUNKS_PER_BLOCK)


def _mixer_call(x2d, mod4, layer, seq, has_ctx, lam_init, params, big, ctx=None, new_cache=None, cast_f32=()):
    n_rows = x2d.shape[0]
    nb = ROWS // seq
    n_seq = n_rows // seq
    kv_len = seq + (PAST_LEN if has_ctx else 0)

    if has_ctx:
        mod_map = lambda i: (layer, 1 + i, 0, 0)
    else:
        mod_map = lambda i: (layer, 0, 0, 0)

    operands = [x2d, mod4, params["norm1_g"], big["w_in"]]
    in_specs = [
        pl.BlockSpec((ROWS, D_MODEL), lambda i: (i, 0)),
        pl.BlockSpec((None, None, N_MOD, D_MODEL), mod_map),
        _layer_spec((1, D_MODEL), layer),
        _const_spec((D_MODEL, IN_WIDTH)),
    ]
    if has_ctx:
        operands += list(ctx)
        in_specs += [
            _const_spec((seq, HEAD_COLS)),
            _const_spec((seq, HEAD_COLS)),
            pl.BlockSpec((None, None, PAST_LEN * N_HEADS, HEAD_COLS), lambda i: (i, layer, 0, 0)),
            pl.BlockSpec((None, None, PAST_LEN * N_HEADS, V_DIM), lambda i: (i, layer, 0, 0)),
        ]
    operands += [params["lam"], params["subln_g"], _pool_counts(seq), params["w_pool"], params["pool_scale"],
                 params["sgu_norm_g"], params["w_sgu"], params["b_sgu"], big["w_out"]]
    in_specs += [
        _layer_spec((8, 128), layer),
        _layer_spec((1, V_DIM), layer),
        _const_spec((seq, POOL_WIDTH)),
        _layer_spec((POOL_WIDTH, POOL_WIDTH), layer),
        _layer_spec((1, POOL_WIDTH), layer),
        _layer_spec((1, SGU_WIDTH), layer),
        _layer_spec((2, 2 * CHUNK, CHUNK), layer),
        _layer_spec((CHUNK, SGU_WIDTH), layer),
        _const_spec((MIX_WIDTH, D_MODEL)),
    ]

    out_shape = [jax.ShapeDtypeStruct((n_rows, D_MODEL), F32)]
    out_specs = [pl.BlockSpec((ROWS, D_MODEL), lambda i: (i, 0))]
    aliases = {}
    cache_slot = 0
    if not has_ctx:
        out_shape += [jax.ShapeDtypeStruct((n_seq, DEPTH, seq * N_HEADS, HEAD_COLS), F32),
                      jax.ShapeDtypeStruct((n_seq, DEPTH, seq * N_HEADS, V_DIM), F32)]
        if new_cache is None:
            cache_slot = layer
            out_specs += [pl.BlockSpec((nb, DEPTH, seq * N_HEADS, HEAD_COLS), lambda i: (i, 0, 0, 0)),
                          pl.BlockSpec((nb, DEPTH, seq * N_HEADS, V_DIM), lambda i: (i, 0, 0, 0))]
        else:
            out_specs += [pl.BlockSpec((nb, 1, seq * N_HEADS, HEAD_COLS), lambda i: (i, layer, 0, 0)),
                          pl.BlockSpec((nb, 1, seq * N_HEADS, V_DIM), lambda i: (i, layer, 0, 0))]
            aliases = {len(operands): 1, len(operands) + 1: 2}
            operands += list(new_cache)
            in_specs += [pl.BlockSpec(memory_space=pl.ANY)] * 2
    n_steps = n_rows // ROWS
    for w in cast_f32:
        _, rows, cols = w.shape
        slab = rows // n_steps
        assert slab * n_steps == rows and slab % 16 == 0
        operands.append(w)
        in_specs.append(pl.BlockSpec((None, slab, cols), lambda i: (layer, i, 0)))
        out_shape.append(jax.ShapeDtypeStruct((rows, cols), BF16))
        out_specs.append(pl.BlockSpec((slab, cols), lambda i: (i, 0)))

    scratch = [
        pltpu.VMEM((nb, seq + 2 * POOL_HALO, POOL_WIDTH), F32),
        pltpu.VMEM((ROWS // ROW_CHUNK, 2 * ROW_CHUNK, QK_WIDTH), BF16),
        pltpu.VMEM((nb, QK_WIDTH, kv_len), BF16),
        pltpu.VMEM((nb, kv_len, ATTN_WIDTH), BF16),
        pltpu.VMEM((ROWS, SGU_WIDTH), F32),
        pltpu.VMEM((ROWS, SGU_WIDTH), BF16),
        pltpu.VMEM((ROWS, MIX_WIDTH), BF16),
    ]
    kernel = functools.partial(_mixer_kernel, nb=nb, seq=seq, has_ctx=has_ctx, lam_init=lam_init,
                               cache_slot=cache_slot, n_passthrough=len(aliases), n_cast=len(cast_f32))
    return pl.pallas_call(
        kernel,
        out_shape=out_shape,
        grid=(n_rows // ROWS,),
        in_specs=in_specs,
        out_specs=out_specs,
        scratch_shapes=scratch,
        input_output_aliases=aliases,
        compiler_params=pltpu.CompilerParams(
            dimension_semantics=("arbitrary",), vmem_limit_bytes=VMEM_LIMIT),
        name="mixer_ctx" if has_ctx else "mixer_prompt",
    )(*operands)


def _ffn_main(x_ref, mod_ref, g2_ref, w1_ref, act_scr):
    shift2, scale2 = mod_ref[3:4, :], mod_ref[4:5, :]
    h = _rms_mod(x_ref[...], g2_ref[...], scale2, shift2).astype(BF16)
    for lo, hi in ((0, FF_SPLIT), (FF_SPLIT, D_FF)):
        gate = _dot(h, w1_ref[:, lo:hi])
        up = _dot(h, w1_ref[:, D_FF + lo:D_FF + hi])
        act_scr[:, lo:hi] = (gate * _sigmoid(gate) * up).astype(BF16)


def _ffn_tail(x_ref, mod_ref, w2_ref, gf_ref, act_scr, out_ref, final_norm):
    y = x_ref[...] + mod_ref[5:6, :] * _dot(act_scr[...], w2_ref[...])
    if final_norm:
        ms = jnp.mean(y * y, axis=-1, keepdims=True)
        y = y * lax.rsqrt(ms + EPS) * gf_ref[...]
    out_ref[...] = y


def _ffn_kernel(*refs, n_p, n_s, final_norm, n_cast):
    (xp_ref, xs_ref, xp_prev_ref, xs_prev_ref, mod_ref, mod_prev_ref,
     g2_ref, w1_ref, w2_ref, gf_ref) = refs[:10]
    cast_in = refs[10:10 + n_cast]
    yp_ref, ys_ref = refs[10 + n_cast:12 + n_cast]
    cast_out = refs[12 + n_cast:12 + 2 * n_cast]
    act_scr = refs[12 + 2 * n_cast]
    i = pl.program_id(0)

    def tail(x_prev_ref, out_ref):
        _ffn_tail(x_prev_ref, mod_prev_ref, w2_ref, gf_ref, act_scr, out_ref, final_norm)

    def main(x_ref, with_casts):
        if with_casts:
            for src, dst in zip(cast_in, cast_out):
                dst[...] = src[...].astype(BF16)
        _ffn_main(x_ref, mod_ref, g2_ref, w1_ref, act_scr)

    @pl.when(i == 0)
    def _():
        main(xp_ref, False)

    @pl.when(jnp.logical_and(i >= 1, i < n_p))
    def _():
        tail(xp_prev_ref, yp_ref)
        main(xp_ref, False)

    @pl.when(i == n_p)
    def _():
        tail(xp_prev_ref, yp_ref)
        main(xs_ref, True)

    @pl.when(jnp.logical_and(i > n_p, i < n_p + n_s))
    def _():
        tail(xs_prev_ref, ys_ref)
        main(xs_ref, True)

    @pl.when(i == n_p + n_s)
    def _():
        tail(xs_prev_ref, ys_ref)


def _ffn_call(xp, xs, mod4, layer, sample_seq, params, big, final_norm, next_f32=()):
    n_p, n_s = xp.shape[0] // FFN_ROWS, xs.shape[0] // FFN_ROWS
    steps_per_seq = sample_seq // FFN_ROWS

    def prompt_block(i):
        return jnp.clip(i, 0, n_p - 1)

    def sample_block(i):
        return jnp.clip(i - n_p, 0, n_s - 1)

    def mod_map(i):
        blk = jnp.clip(i, 0, n_p + n_s - 1)
        return (layer, jnp.where(blk < n_p, 0, 1 + sample_block(blk) // steps_per_seq), 0, 0)

    row_block = (FFN_ROWS, D_MODEL)
    in_specs = [
        pl.BlockSpec(row_block, lambda i: (prompt_block(i), 0)),
        pl.BlockSpec(row_block, lambda i: (sample_block(i), 0)),
        pl.BlockSpec(row_block, lambda i: (prompt_block(i - 1), 0)),
        pl.BlockSpec(row_block, lambda i: (sample_block(i - 1), 0)),
        pl.BlockSpec((None, None, N_MOD, D_MODEL), mod_map),
        pl.BlockSpec((None, None, N_MOD, D_MODEL), lambda i: mod_map(i - 1)),
        _layer_spec((1, D_MODEL), layer),
        _const_spec((D_MODEL, 2 * D_FF)),
        _const_spec((D_FF, D_MODEL)),
        _const_spec((1, D_MODEL)),
    ]
    out_shape = [jax.ShapeDtypeStruct(xp.shape, F32), jax.ShapeDtypeStruct(xs.shape, F32)]
    out_specs = [pl.BlockSpec(row_block, lambda i: (prompt_block(i - 1), 0)),
                 pl.BlockSpec(row_block, lambda i: (sample_block(i - 1), 0))]
    for w in next_f32:
        _, rows, cols = w.shape
        slab = rows // n_s
        assert slab * n_s == rows and slab % 16 == 0
        in_specs.append(pl.BlockSpec((None, slab, cols), lambda i: (layer + 1, sample_block(i), 0)))
        out_shape.append(jax.ShapeDtypeStruct((rows, cols), BF16))
        out_specs.append(pl.BlockSpec((slab, cols), lambda i: (sample_block(i), 0)))

    kernel = functools.partial(_ffn_kernel, n_p=n_p, n_s=n_s, final_norm=final_norm, n_cast=len(next_f32))
    return pl.pallas_call(
        kernel,
        out_shape=out_shape,
        grid=(n_p + n_s + 1,),
        in_specs=in_specs,
        out_specs=out_specs,
        scratch_shapes=[pltpu.VMEM((FFN_ROWS, D_FF), BF16)],
        compiler_params=pltpu.CompilerParams(
            dimension_semantics=("arbitrary",), vmem_limit_bytes=VMEM_LIMIT),
        name="ffn",
    )(xp, xs, xp, xs, mod4, mod4, params["norm2_g"], big["w_ffn_in"], big["w_ffn_out"], params["final_g"],
      *next_f32)


def _rope_tables(seq):
    n_rows = seq // GRID_W
    rows = np.repeat(np.arange(n_rows), GRID_W).astype(np.float32)
    cols = np.tile(np.arange(GRID_W), n_rows).astype(np.float32)
    inv = 1.0 / (ROPE_BASE ** (np.arange(0, ROPE_AXIS_DIM, 2, dtype=np.float32) / ROPE_AXIS_DIM))
    ar, ac = rows[:, None] * inv[None], cols[:, None] * inv[None]
    cos_parts, sin_parts = [], []
    for ang in (ar, ac):
        cos_parts += [np.cos(ang), np.cos(ang)]
        sin_parts += [-np.sin(ang), np.sin(ang)]
    cos64 = np.concatenate(cos_parts, axis=1)
    sin64 = np.concatenate(sin_parts, axis=1)
    cos = np.concatenate([cos64, cos64], axis=1).astype(np.float32)
    sin = np.concatenate([sin64, sin64], axis=1).astype(np.float32)
    return jnp.asarray(cos), jnp.asarray(sin)


def _pool_counts(seq):
    t = np.arange(seq)
    cols = []
    for w in POOL_WINDOWS:
        lo = np.clip(t - w // 2, 0, seq)
        hi = np.clip(t + w - w // 2, 0, seq)
        cols.append(np.repeat((hi - lo).astype(np.float32)[:, None], POOL_GROUP, axis=1))
    return jnp.asarray(np.concatenate(cols, axis=1))


def _small_params(norm1_g, w_pool, pool_scale, lam_q1, lam_k1, lam_q2, lam_k2, subln_g,
                  sgu_norm_g, w_sgu, b_sgu, norm2_g, final_g):
    n_groups = len(POOL_WINDOWS)
    eye = jnp.eye(n_groups, dtype=F32)
    w_pool_bd = (eye[None, :, None, :, None] * w_pool[:, :, :, None, :]).reshape(DEPTH, POOL_WIDTH, POOL_WIDTH)
    lam = jnp.zeros((DEPTH, 8, 128), F32)
    lam = lam.at[:, 0, :QK_DIM].set(lam_q1).at[:, 1, :QK_DIM].set(lam_k1)
    lam = lam.at[:, 2, :QK_DIM].set(lam_q2).at[:, 3, :QK_DIM].set(lam_k2)
    return {
        "norm1_g": norm1_g[:, None, :],
        "lam": lam,
        "subln_g": subln_g[:, None, :],
        "w_pool": w_pool_bd.astype(BF16),
        "pool_scale": pool_scale[:, None, :],
        "sgu_norm_g": sgu_norm_g[:, None, :],
        "w_sgu": w_sgu.reshape(DEPTH, 2, 2 * CHUNK, CHUNK).astype(BF16),
        "b_sgu": jnp.repeat(jnp.swapaxes(b_sgu, 1, 2), SGU_WIDTH // 4, axis=2),
        "norm2_g": norm2_g[:, None, :],
        "final_g": final_g[None, :],
    }


BIG_WEIGHTS = ("w_in", "w_out", "w_ffn_in", "w_ffn_out")


def kernel(x_prompt, x_sample, cache_k, cache_v, c, c_ctx, norm1_g, w_ada, b_ada, w_in, w_pool, pool_scale, lam_q1, lam_k1, lam_q2, lam_k2, subln_g, sgu_norm_g, w_sgu, b_sgu, w_out, norm2_g, w_ffn_in, w_ffn_out, final_g):
    batch, seq, _ = x_prompt.shape
    dec_batch, dec_seq, _ = x_sample.shape
    assert ROWS % seq == 0 and dec_seq == ROWS and 1 + dec_batch <= COND_ROWS

    cond = jnp.zeros((COND_ROWS, D_MODEL), F32).at[0].set(c_ctx).at[1:1 + dec_batch].set(c)
    mod4 = _ada_modulation(cond, w_ada, b_ada).reshape(DEPTH, COND_ROWS, N_MOD, D_MODEL)

    ctx = _rope_tables(dec_seq) + (cache_k.reshape(dec_batch, DEPTH, PAST_LEN * N_HEADS, HEAD_COLS),
                                   cache_v.reshape(dec_batch, DEPTH, PAST_LEN * N_HEADS, V_DIM))
    params = _small_params(norm1_g, w_pool, pool_scale, lam_q1, lam_k1, lam_q2, lam_k2, subln_g,
                           sgu_norm_g, w_sgu, b_sgu, norm2_g, final_g)

    big_f32 = (w_in, w_out, w_ffn_in, w_ffn_out)
    big = {"w_in": w_in[0].astype(BF16), "w_out": w_out[0].astype(BF16)}

    xp = x_prompt.reshape(batch * seq, D_MODEL)
    xs = x_sample.reshape(dec_batch * dec_seq, D_MODEL)
    new_cache = None
    for l in range(DEPTH):
        lam_init = 0.8 - 0.6 * math.exp(-0.3 * l)
        last = l == DEPTH - 1

        xp, *new_cache = _mixer_call(xp, mod4, l, seq, False, lam_init, params, big, new_cache=new_cache)
        if l == 0:
            xs, big["w_ffn_in"], big["w_ffn_out"] = _mixer_call(
                xs, mod4, l, dec_seq, True, lam_init, params, big, ctx=ctx, cast_f32=(w_ffn_in, w_ffn_out))
        else:
            xs, = _mixer_call(xs, mod4, l, dec_seq, True, lam_init, params, big, ctx=ctx)
        xp, xs, *next_big = _ffn_call(xp, xs, mod4, l, dec_seq, params, big, last,
                                      next_f32=() if last else big_f32)
        big = dict(zip(BIG_WEIGHTS, next_big))

    y_prompt = xp.reshape(batch, seq, D_MODEL)
    y_sample = xs.reshape(dec_batch, dec_seq, D_MODEL)
    new_cache_k = new_cache[0].reshape(batch, DEPTH, seq, N_HEADS, 2 * QK_DIM)
    new_cache_v = new_cache[1].reshape(batch, DEPTH, seq, N_HEADS, V_DIM)
    return (y_prompt, y_sample, new_cache_k, new_cache_v)
```

```python
import functools
import math
import types

import numpy as np
import jax
import jax.numpy as jnp
from jax import lax
from jax.experimental import pallas as pl
from jax.experimental.pallas import tpu as pltpu

D_MODEL = 1024
DEPTH = 2
GRID_W = 64
POOL_WINDOWS = (2, 4, 8, 16)
POOL_WIDTH = 256
POOL_GROUP = 64
POOL_HALO = 8
ATTN_WIDTH = 512
N_HEADS = 4
V_DIM = 128
QK_DIM = 64
HEAD_COLS = 2 * QK_DIM
ROPE_BASE = 10000.0
ROPE_AXIS_DIM = 32
ROPE_HALF = ROPE_AXIS_DIM // 2
CHUNK = 128
SGU_WIDTH = 256
QK_WIDTH = 512
IN_WIDTH = 2304
MIX_WIDTH = 1024
D_FF = 2816
N_MOD = 6
EPS = 1e-6
PAST_LEN = 256

C_POOL, C_Q, C_K, C_V, C_UV = 0, 256, 768, 1280, 1792
Y_POOL, Y_ATTN, Y_SGU = 0, 256, 768

ROWS = 1024
ROW_CHUNK = 256
ATTN_LOOKAHEAD = 2
CHUNKS_PER_BLOCK = 2
FFN_ROWS = 512
MXU_COLS = 256
FF_SPLIT = (D_FF // MXU_COLS + 1) // 2 * MXU_COLS
COND_ROWS = 16
ADA_COLS = 3072
VMEM_LIMIT = 58 * 1024 * 1024

BF16 = jnp.bfloat16
F32 = jnp.float32


def _dot(a, b):
    return jnp.dot(a, b, preferred_element_type=F32)


def _sigmoid(x):
    return 1.0 / (1.0 + jnp.exp(-x))


def _rms_mod(x, g, scale, shift):
    ms = jnp.mean(x * x, axis=-1, keepdims=True)
    return x * lax.rsqrt(ms + EPS) * (g * (1.0 + scale)) + shift


def _gelu_tanh(x):
    c = math.sqrt(2.0 / math.pi)
    return x * (0.5 * (1.0 + jnp.tanh(c * (x + 0.044715 * (x * x * x)))))


def _const_spec(shape):
    zeros = (0,) * len(shape)
    return pl.BlockSpec(shape, lambda *_: zeros, pipeline_mode=pl.Buffered(1))


def _layer_spec(shape, layer):
    index = (layer,) + (0,) * len(shape)
    return pl.BlockSpec((None,) + tuple(shape), lambda *_: index, pipeline_mode=pl.Buffered(1))


def _ada_kernel(cond_ref, w_ref, b_ref, out_ref):
    cond = cond_ref[...]
    s = (cond * _sigmoid(cond)).astype(BF16)
    out_ref[...] = _dot(s, w_ref[...].astype(BF16)) + b_ref[...]


def _ada_modulation(cond, w_ada, b_ada):
    n_cols = N_MOD * D_MODEL
    return pl.pallas_call(
        _ada_kernel,
        out_shape=jax.ShapeDtypeStruct((DEPTH, COND_ROWS, n_cols), F32),
        grid=(DEPTH, n_cols // ADA_COLS),
        in_specs=[
            pl.BlockSpec((COND_ROWS, D_MODEL), lambda l, j: (0, 0)),
            pl.BlockSpec((None, D_MODEL, ADA_COLS), lambda l, j: (l, 0, j)),
            pl.BlockSpec((None, 1, ADA_COLS), lambda l, j: (l, 0, j)),
        ],
        out_specs=pl.BlockSpec((None, COND_ROWS, ADA_COLS), lambda l, j: (l, 0, j)),
        compiler_params=pltpu.CompilerParams(
            dimension_semantics=("arbitrary", "arbitrary"), vmem_limit_bytes=VMEM_LIMIT),
        name="ada_modulation",
    )(cond, w_ada, b_ada.reshape(DEPTH, 1, n_cols))


def _mixer_body(r, x_ref, cnt_ref, pool_scr, k_scr, v_scr, *, nb, seq, has_ctx, lam_init, cpb):
    kv_base = PAST_LEN if has_ctx else 0
    shift1, scale1, gate1 = r.mod[0:1, :], r.mod[1:2, :], r.mod[2:3, :]

    lane_head = lax.broadcasted_iota(jnp.int32, (ROW_CHUNK, HEAD_COLS), 1)
    first_half = lane_head < QK_DIM

    for b in range(nb):
        pool_scr[b, 0:POOL_HALO, :] = jnp.zeros((POOL_HALO, POOL_WIDTH), F32)
        pool_scr[b, POOL_HALO + seq:POOL_HALO + seq + POOL_HALO, :] = jnp.zeros((POOL_HALO, POOL_WIDTH), F32)
    if not has_ctx:
        for slot in range(r.kc_out.shape[1]):
            if slot != r.cache_slot:
                r.kc_out[:, slot] = jnp.zeros((nb,) + r.kc_out.shape[2:], F32)
                r.vc_out[:, slot] = jnp.zeros((nb,) + r.vc_out.shape[2:], F32)
    if has_ctx:
        for hd in range(N_HEADS):
            cols = slice(hd * HEAD_COLS, (hd + 1) * HEAD_COLS)
            k_scr[0, cols, 0:PAST_LEN] = r.ck[pl.ds(hd, PAST_LEN, stride=N_HEADS), :].T.astype(BF16)
            v_scr[0, 0:PAST_LEN, cols] = r.cv[pl.ds(hd, PAST_LEN, stride=N_HEADS), :].astype(BF16)
        lane_q = lax.broadcasted_iota(jnp.int32, (ROW_CHUNK, QK_WIDTH), 1)
        rope_low = (lane_q % ROPE_AXIS_DIM) < ROPE_HALF

        def rope(t, pos0):
            partner = jnp.where(rope_low,
                                pltpu.roll(t, QK_WIDTH - ROPE_HALF, 1),
                                pltpu.roll(t, ROPE_HALF, 1))
            cos = r.rope_c[pos0:pos0 + ROW_CHUNK, :]
            sin = r.rope_s[pos0:pos0 + ROW_CHUNK, :]
            cos = jnp.concatenate([cos] * N_HEADS, axis=1)
            sin = jnp.concatenate([sin] * N_HEADS, axis=1)
            return t * cos + partner * sin

    for c in range(ROWS // ROW_CHUNK):
        r0 = c * ROW_CHUNK
        b, pos0 = r0 // seq, r0 % seq
        rows = slice(r0, r0 + ROW_CHUNK)
        h = _rms_mod(x_ref[rows, :], r.g1[...], scale1, shift1).astype(BF16)

        pool_scr[b, POOL_HALO + pos0:POOL_HALO + pos0 + ROW_CHUNK, :] = _dot(h, r.w_in[:, C_POOL:C_Q])

        q = _dot(h, r.w_in[:, C_Q:C_K])
        k = _dot(h, r.w_in[:, C_K:C_V])
        if has_ctx:
            q = rope(q, pos0)
            k = rope(k, pos0)
        else:
            for hd in range(N_HEADS):
                r.kc_out[b, r.cache_slot, pl.ds(N_HEADS * pos0 + hd, ROW_CHUNK, stride=N_HEADS), :] = (
                    k[:, hd * HEAD_COLS:(hd + 1) * HEAD_COLS])
        q = q * (QK_DIM ** -0.5 * math.log2(math.e))
        for hd in range(N_HEADS):
            cols = slice(hd * HEAD_COLS, (hd + 1) * HEAD_COLS)
            qh = q[:, cols]
            r.q_scr[c, 0:ROW_CHUNK, cols] = jnp.where(first_half, qh, 0.0).astype(BF16)
            r.q_scr[c, ROW_CHUNK:2 * ROW_CHUNK, cols] = jnp.where(first_half, 0.0, qh).astype(BF16)
        k_scr[b, :, kv_base + pos0:kv_base + pos0 + ROW_CHUNK] = k.T.astype(BF16)

        v = _dot(h, r.w_in[:, C_V:C_UV])
        if not has_ctx:
            for hd in range(N_HEADS):
                r.vc_out[b, r.cache_slot, pl.ds(N_HEADS * pos0 + hd, ROW_CHUNK, stride=N_HEADS), :] = (
                    v[:, hd * V_DIM:(hd + 1) * V_DIM])
        v_scr[b, kv_base + pos0:kv_base + pos0 + ROW_CHUNK, :] = v.astype(BF16)

        uv = _gelu_tanh(_dot(h, r.w_in[:, C_UV:IN_WIDTH]))
        r.u_scr[rows, :] = uv[:, :SGU_WIDTH]
        vv = uv[:, SGU_WIDTH:]
        mu = jnp.mean(vv, axis=-1, keepdims=True)
        vc = vv - mu
        vn = vc * lax.rsqrt(jnp.mean(vc * vc, axis=-1, keepdims=True) + EPS) * r.sgu_g[...]
        r.vs_scr[rows, :] = vn.astype(BF16)

    lane_pool = lax.broadcasted_iota(jnp.int32, (ROW_CHUNK, 2 * POOL_GROUP), 1)
    narrow = lane_pool < POOL_GROUP
    for c in range(ROWS // ROW_CHUNK):
        r0 = c * ROW_CHUNK
        b, pos0 = r0 // seq, r0 % seq
        base = POOL_HALO + pos0
        pooled = []
        for j, (w_small, w_big) in enumerate(((POOL_WINDOWS[0], POOL_WINDOWS[1]),
                                              (POOL_WINDOWS[2], POOL_WINDOWS[3]))):
            cols = slice(j * 2 * POOL_GROUP, (j + 1) * 2 * POOL_GROUP)
            s_small = jnp.zeros((ROW_CHUNK, 2 * POOL_GROUP), F32)
            s_rest = jnp.zeros((ROW_CHUNK, 2 * POOL_GROUP), F32)
            for d in range(-(w_big // 2), w_big - w_big // 2):
                t = pool_scr[b, base + d:base + d + ROW_CHUNK, cols]
                if -(w_small // 2) <= d < w_small - w_small // 2:
                    s_small = s_small + t
                else:
                    s_rest = s_rest + t
            win_sum = jnp.where(narrow, s_small, s_small + s_rest)
            centre = pool_scr[b, base:base + ROW_CHUNK, cols]
            pooled.append(win_sum / cnt_ref[pos0:pos0 + ROW_CHUNK, cols] - centre)
        pooled = jnp.concatenate(pooled, axis=1).astype(BF16)
        y_a = _dot(pooled, r.wpool[...]) * r.pscale[...]
        r.y_scr[r0:r0 + ROW_CHUNK, Y_POOL:Y_ATTN] = y_a.astype(BF16)

    lq1, lk1, lq2, lk2 = r.lam[0:1, :], r.lam[1:2, :], r.lam[2:3, :], r.lam[3:4, :]
    lam = (jnp.exp(jnp.sum(lq1 * lk1, axis=-1, keepdims=True))
           - jnp.exp(jnp.sum(lq2 * lk2, axis=-1, keepdims=True)) + lam_init)
    n_qb = seq // ROW_CHUNK

    lane_sgu = lax.broadcasted_iota(jnp.int32, (CHUNK, 2 * POOL_GROUP), 1)
    sgu_first = lane_sgu < (SGU_WIDTH // 4)
    out_cols = MIX_WIDTH // N_HEADS

    def out_proj(rows, part):
        cols = slice(part * out_cols, (part + 1) * out_cols)
        y = _dot(r.y_scr[rows, :], r.w_out[:, cols])
        r.out[rows, cols] = x_ref[rows, cols] + gate1[:, cols] * y

    def block(blk, with_prev):
        static = isinstance(blk, int)

        def row0(chunk):
            return chunk * ROW_CHUNK if static else pl.multiple_of(chunk * ROW_CHUNK, ROW_CHUNK)

        chunks = [blk * cpb + j for j in range(cpb)]

        for idx in chunks:
            for half in range(ROW_CHUNK // CHUNK):
                sub = pl.ds(row0(idx) + half * CHUNK, CHUNK)
                for j in range(2):
                    cols = slice(j * 128, (j + 1) * 128)
                    t = _dot(r.wsgu[j], r.vs_scr[sub, cols])
                    mixed = jnp.where(sgu_first, t[:CHUNK, :], t[CHUNK:, :]) + r.bsgu[:, cols]
                    y_c = r.u_scr[sub, cols] * mixed
                    r.y_scr[sub, Y_SGU + j * 128:Y_SGU + (j + 1) * 128] = y_c.astype(BF16)

        units = [(idx, hd) for idx in chunks for hd in range(N_HEADS)]
        prev = [(pl.ds(row0(idx - cpb), ROW_CHUNK), part) for idx in chunks for part in range(N_HEADS)]

        def scores(idx, hd):
            cols = slice(hd * HEAD_COLS, (hd + 1) * HEAD_COLS)
            s = _dot(r.q_scr[idx, :, cols], k_scr[idx // n_qb, cols, :])
            return s[:ROW_CHUNK], s[ROW_CHUNK:]

        pending = [scores(*u) for u in units[:ATTN_LOOKAHEAD]]
        for n, (idx, hd) in enumerate(units):
            cols = slice(hd * HEAD_COLS, (hd + 1) * HEAD_COLS)
            s1, s2 = pending.pop(0)
            if n + ATTN_LOOKAHEAD < len(units):
                pending.append(scores(*units[n + ATTN_LOOKAHEAD]))
            e1 = jnp.exp2(s1 - jnp.max(s1, axis=-1, keepdims=True))
            e2 = jnp.exp2(s2 - jnp.max(s2, axis=-1, keepdims=True))
            l1 = jnp.sum(e1, axis=-1, keepdims=True)
            l2 = jnp.sum(e2, axis=-1, keepdims=True)
            p = (e1 - e2 * (lam * l1 / l2)).astype(BF16)
            o = _dot(p, v_scr[idx // n_qb, :, cols]) * (1.0 / l1)
            if with_prev:
                out_proj(*prev[n])
            o = o * lax.rsqrt(jnp.mean(o * o, axis=-1, keepdims=True) + EPS) * r.subln[...]
            o = o * (1.0 - lam_init)
            r.y_scr[pl.ds(row0(idx), ROW_CHUNK), Y_ATTN + hd * V_DIM:Y_ATTN + (hd + 1) * V_DIM] = o.astype(BF16)

    n_blocks = ROWS // (ROW_CHUNK * cpb)
    block(0, False)
    if n_blocks == 2:
        block(1, True)
    else:
        def loop_block(blk, carry):
            block(blk, True)
            return carry

        lax.fori_loop(1, n_blocks, loop_block, 0)
    for idx in range((n_blocks - 1) * cpb, n_blocks * cpb):
        for part in range(N_HEADS):
            out_proj(pl.ds(idx * ROW_CHUNK, ROW_CHUNK), part)


def _mixer_kernel(*refs, nb, seq, has_ctx, lam_init, cache_slot, n_passthrough, n_cast):
    it = iter(refs)
    r = types.SimpleNamespace(cache_slot=cache_slot)
    x_ref, r.mod, r.g1, r.w_in = next(it), next(it), next(it), next(it)
    if has_ctx:
        r.rope_c, r.rope_s, r.ck, r.cv = next(it), next(it), next(it), next(it)
    (r.lam, r.subln, cnt_ref, r.wpool, r.pscale, r.sgu_g, r.wsgu, r.bsgu, r.w_out) = (next(it) for _ in range(9))
    for _ in range(n_passthrough):
        next(it)
    cast_in = [next(it) for _ in range(n_cast)]
    r.out = next(it)
    if not has_ctx:
        r.kc_out, r.vc_out = next(it), next(it)
    cast_out = [next(it) for _ in range(n_cast)]
    pool_scr, r.q_scr, k_scr, v_scr, r.u_scr, r.vs_scr, r.y_scr = (next(it) for _ in range(7))
    for src, dst in zip(cast_in, cast_out):
        dst[...] = src[...].astype(BF16)
    _mixer_body(r, x_ref, cnt_ref, pool_scr, k_scr, v_scr, nb=nb, seq=seq, has_ctx=has_ctx, lam_init=lam_init,
                cpb=CHUNKS_PER_BLOCK)


def _mixer_call(x2d, mod4, layer, seq, has_ctx, lam_init, params, big, ctx=None, new_cache=None, cast_f32=()):
    n_rows = x2d.shape[0]
    nb = ROWS // seq
    n_seq = n_rows // seq
    kv_len = seq + (PAST_LEN if has_ctx else 0)

    if has_ctx:
        mod_map = lambda i: (layer, 1 + i, 0, 0)
    else:
        mod_map = lambda i: (layer, 0, 0, 0)

    operands = [x2d, mod4, params["norm1_g"], big["w_in"]]
    in_specs = [
        pl.BlockSpec((ROWS, D_MODEL), lambda i: (i, 0)),
        pl.BlockSpec((None, None, N_MOD, D_MODEL), mod_map),
        _layer_spec((1, D_MODEL), layer),
        _const_spec((D_MODEL, IN_WIDTH)),
    ]
    if has_ctx:
        operands += list(ctx)
        in_specs += [
            _const_spec((seq, HEAD_COLS)),
            _const_spec((seq, HEAD_COLS)),
            pl.BlockSpec((None, None, PAST_LEN * N_HEADS, HEAD_COLS), lambda i: (i, layer, 0, 0)),
            pl.BlockSpec((None, None, PAST_LEN * N_HEADS, V_DIM), lambda i: (i, layer, 0, 0)),
        ]
    operands += [params["lam"], params["subln_g"], _pool_counts(seq), params["w_pool"], params["pool_scale"],
                 params["sgu_norm_g"], params["w_sgu"], params["b_sgu"], big["w_out"]]
    in_specs += [
        _layer_spec((8, 128), layer),
        _layer_spec((1, V_DIM), layer),
        _const_spec((seq, POOL_WIDTH)),
        _layer_spec((POOL_WIDTH, POOL_WIDTH), layer),
        _layer_spec((1, POOL_WIDTH), layer),
        _layer_spec((1, SGU_WIDTH), layer),
        _layer_spec((2, 2 * CHUNK, CHUNK), layer),
        _layer_spec((CHUNK, SGU_WIDTH), layer),
        _const_spec((MIX_WIDTH, D_MODEL)),
    ]

    out_shape = [jax.ShapeDtypeStruct((n_rows, D_MODEL), F32)]
    out_specs = [pl.BlockSpec((ROWS, D_MODEL), lambda i: (i, 0))]
    aliases = {}
    cache_slot = 0
    if not has_ctx:
        out_shape += [jax.ShapeDtypeStruct((n_seq, DEPTH, seq * N_HEADS, HEAD_COLS), F32),
                      jax.ShapeDtypeStruct((n_seq, DEPTH, seq * N_HEADS, V_DIM), F32)]
        if new_cache is None:
            cache_slot = layer
            out_specs += [pl.BlockSpec((nb, DEPTH, seq * N_HEADS, HEAD_COLS), lambda i: (i, 0, 0, 0)),
                          pl.BlockSpec((nb, DEPTH, seq * N_HEADS, V_DIM), lambda i: (i, 0, 0, 0))]
        else:
            out_specs += [pl.BlockSpec((nb, 1, seq * N_HEADS, HEAD_COLS), lambda i: (i, layer, 0, 0)),
                          pl.BlockSpec((nb, 1, seq * N_HEADS, V_DIM), lambda i: (i, layer, 0, 0))]
            aliases = {len(operands): 1, len(operands) + 1: 2}
            operands += list(new_cache)
            in_specs += [pl.BlockSpec(memory_space=pl.ANY)] * 2
    n_steps = n_rows // ROWS
    for w in cast_f32:
        _, rows, cols = w.shape
        slab = rows // n_steps
        assert slab * n_steps == rows and slab % 16 == 0
        operands.append(w)
        in_specs.append(pl.BlockSpec((None, slab, cols), lambda i: (layer, i, 0)))
        out_shape.append(jax.ShapeDtypeStruct((rows, cols), BF16))
        out_specs.append(pl.BlockSpec((slab, cols), lambda i: (i, 0)))

    scratch = [
        pltpu.VMEM((nb, seq + 2 * POOL_HALO, POOL_WIDTH), F32),
        pltpu.VMEM((ROWS // ROW_CHUNK, 2 * ROW_CHUNK, QK_WIDTH), BF16),
        pltpu.VMEM((nb, QK_WIDTH, kv_len), BF16),
        pltpu.VMEM((nb, kv_len, ATTN_WIDTH), BF16),
        pltpu.VMEM((ROWS, SGU_WIDTH), F32),
        pltpu.VMEM((ROWS, SGU_WIDTH), BF16),
        pltpu.VMEM((ROWS, MIX_WIDTH), BF16),
    ]
    kernel = functools.partial(_mixer_kernel, nb=nb, seq=seq, has_ctx=has_ctx, lam_init=lam_init,
                               cache_slot=cache_slot, n_passthrough=len(aliases), n_cast=len(cast_f32))
    return pl.pallas_call(
        kernel,
        out_shape=out_shape,
        grid=(n_rows // ROWS,),
        in_specs=in_specs,
        out_specs=out_specs,
        scratch_shapes=scratch,
        input_output_aliases=aliases,
        compiler_params=pltpu.CompilerParams(
            dimension_semantics=("arbitrary",), vmem_limit_bytes=VMEM_LIMIT),
        name="mixer_ctx" if has_ctx else "mixer_prompt",
    )(*operands)


def _ffn_block(x_ref, out_ref, mod_ref, g2_ref, w1_ref, w2_ref, gf_ref, act_scr, final_norm):
    shift2, scale2, gate2 = mod_ref[3:4, :], mod_ref[4:5, :], mod_ref[5:6, :]
    x = x_ref[...]
    h = _rms_mod(x, g2_ref[...], scale2, shift2).astype(BF16)
    for lo, hi in ((0, FF_SPLIT), (FF_SPLIT, D_FF)):
        gate = _dot(h, w1_ref[:, lo:hi])
        up = _dot(h, w1_ref[:, D_FF + lo:D_FF + hi])
        act_scr[:, lo:hi] = (gate * _sigmoid(gate) * up).astype(BF16)
    y = x + gate2 * _dot(act_scr[...], w2_ref[...])
    if final_norm:
        ms = jnp.mean(y * y, axis=-1, keepdims=True)
        y = y * lax.rsqrt(ms + EPS) * gf_ref[...]
    out_ref[...] = y


def _ffn_kernel(*refs, n_prompt_steps, final_norm, n_cast):
    xp_ref, xs_ref, mod_ref, g2_ref, w1_ref, w2_ref, gf_ref = refs[:7]
    cast_in = refs[7:7 + n_cast]
    yp_ref, ys_ref = refs[7 + n_cast:9 + n_cast]
    cast_out = refs[9 + n_cast:9 + 2 * n_cast]
    act_scr = refs[9 + 2 * n_cast]
    shared = (mod_ref, g2_ref, w1_ref, w2_ref, gf_ref, act_scr, final_norm)
    is_prompt = pl.program_id(0) < n_prompt_steps

    @pl.when(is_prompt)
    def _():
        _ffn_block(xp_ref, yp_ref, *shared)

    @pl.when(jnp.logical_not(is_prompt))
    def _():
        for src, dst in zip(cast_in, cast_out):
            dst[...] = src[...].astype(BF16)
        _ffn_block(xs_ref, ys_ref, *shared)


def _ffn_call(xp, xs, mod4, layer, sample_seq, params, big, final_norm, next_f32=()):
    n_p, n_s = xp.shape[0] // FFN_ROWS, xs.shape[0] // FFN_ROWS
    steps_per_seq = sample_seq // FFN_ROWS

    def prompt_step(i):
        return jnp.minimum(i, n_p - 1)

    def sample_step(i):
        return jnp.maximum(i - n_p, 0)

    def mod_map(i):
        return (layer, jnp.where(i < n_p, 0, 1 + sample_step(i) // steps_per_seq), 0, 0)

    in_specs = [
        pl.BlockSpec((FFN_ROWS, D_MODEL), lambda i: (prompt_step(i), 0)),
        pl.BlockSpec((FFN_ROWS, D_MODEL), lambda i: (sample_step(i), 0)),
        pl.BlockSpec((None, None, N_MOD, D_MODEL), mod_map),
        _layer_spec((1, D_MODEL), layer),
        _const_spec((D_MODEL, 2 * D_FF)),
        _const_spec((D_FF, D_MODEL)),
        _const_spec((1, D_MODEL)),
    ]
    out_shape = [jax.ShapeDtypeStruct(xp.shape, F32), jax.ShapeDtypeStruct(xs.shape, F32)]
    out_specs = [pl.BlockSpec((FFN_ROWS, D_MODEL), lambda i: (prompt_step(i), 0)),
                 pl.BlockSpec((FFN_ROWS, D_MODEL), lambda i: (sample_step(i), 0))]
    for w in next_f32:
        _, rows, cols = w.shape
        slab = rows // n_s
        assert slab * n_s == rows and slab % 16 == 0
        in_specs.append(pl.BlockSpec((None, slab, cols), lambda i: (layer + 1, sample_step(i), 0)))
        out_shape.append(jax.ShapeDtypeStruct((rows, cols), BF16))
        out_specs.append(pl.BlockSpec((slab, cols), lambda i: (sample_step(i), 0)))

    kernel = functools.partial(_ffn_kernel, n_prompt_steps=n_p, final_norm=final_norm, n_cast=len(next_f32))
    return pl.pallas_call(
        kernel,
        out_shape=out_shape,
        grid=(n_p + n_s,),
        in_specs=in_specs,
        out_specs=out_specs,
        scratch_shapes=[pltpu.VMEM((FFN_ROWS, D_FF), BF16)],
        compiler_params=pltpu.CompilerParams(
            dimension_semantics=("arbitrary",), vmem_limit_bytes=VMEM_LIMIT),
        name="ffn",
    )(xp, xs, mod4, params["norm2_g"], big["w_ffn_in"], big["w_ffn_out"], params["final_g"], *next_f32)


def _rope_tables(seq):
    n_rows = seq // GRID_W
    rows = np.repeat(np.arange(n_rows), GRID_W).astype(np.float32)
    cols = np.tile(np.arange(GRID_W), n_rows).astype(np.float32)
    inv = 1.0 / (ROPE_BASE ** (np.arange(0, ROPE_AXIS_DIM, 2, dtype=np.float32) / ROPE_AXIS_DIM))
    ar, ac = rows[:, None] * inv[None], cols[:, None] * inv[None]
    cos_parts, sin_parts = [], []
    for ang in (ar, ac):
        cos_parts += [np.cos(ang), np.cos(ang)]
        sin_parts += [-np.sin(ang), np.sin(ang)]
    cos64 = np.concatenate(cos_parts, axis=1)
    sin64 = np.concatenate(sin_parts, axis=1)
    cos = np.concatenate([cos64, cos64], axis=1).astype(np.float32)
    sin = np.concatenate([sin64, sin64], axis=1).astype(np.float32)
    return jnp.asarray(cos), jnp.asarray(sin)


def _pool_counts(seq):
    t = np.arange(seq)
    cols = []
    for w in POOL_WINDOWS:
        lo = np.clip(t - w // 2, 0, seq)
        hi = np.clip(t + w - w // 2, 0, seq)
        cols.append(np.repeat((hi - lo).astype(np.float32)[:, None], POOL_GROUP, axis=1))
    return jnp.asarray(np.concatenate(cols, axis=1))


def _small_params(norm1_g, w_pool, pool_scale, lam_q1, lam_k1, lam_q2, lam_k2, subln_g,
                  sgu_norm_g, w_sgu, b_sgu, norm2_g, final_g):
    n_groups = len(POOL_WINDOWS)
    eye = jnp.eye(n_groups, dtype=F32)
    w_pool_bd = (eye[None, :, None, :, None] * w_pool[:, :, :, None, :]).reshape(DEPTH, POOL_WIDTH, POOL_WIDTH)
    lam = jnp.zeros((DEPTH, 8, 128), F32)
    lam = lam.at[:, 0, :QK_DIM].set(lam_q1).at[:, 1, :QK_DIM].set(lam_k1)
    lam = lam.at[:, 2, :QK_DIM].set(lam_q2).at[:, 3, :QK_DIM].set(lam_k2)
    return {
        "norm1_g": norm1_g[:, None, :],
        "lam": lam,
        "subln_g": subln_g[:, None, :],
        "w_pool": w_pool_bd.astype(BF16),
        "pool_scale": pool_scale[:, None, :],
        "sgu_norm_g": sgu_norm_g[:, None, :],
        "w_sgu": w_sgu.reshape(DEPTH, 2, 2 * CHUNK, CHUNK).astype(BF16),
        "b_sgu": jnp.repeat(jnp.swapaxes(b_sgu, 1, 2), SGU_WIDTH // 4, axis=2),
        "norm2_g": norm2_g[:, None, :],
        "final_g": final_g[None, :],
    }


BIG_WEIGHTS = ("w_in", "w_out", "w_ffn_in", "w_ffn_out")


def kernel(x_prompt, x_sample, cache_k, cache_v, c, c_ctx, norm1_g, w_ada, b_ada, w_in, w_pool, pool_scale, lam_q1, lam_k1, lam_q2, lam_k2, subln_g, sgu_norm_g, w_sgu, b_sgu, w_out, norm2_g, w_ffn_in, w_ffn_out, final_g):
    batch, seq, _ = x_prompt.shape
    dec_batch, dec_seq, _ = x_sample.shape
    assert ROWS % seq == 0 and dec_seq == ROWS and 1 + dec_batch <= COND_ROWS

    cond = jnp.zeros((COND_ROWS, D_MODEL), F32).at[0].set(c_ctx).at[1:1 + dec_batch].set(c)
    mod4 = _ada_modulation(cond, w_ada, b_ada).reshape(DEPTH, COND_ROWS, N_MOD, D_MODEL)

    ctx = _rope_tables(dec_seq) + (cache_k.reshape(dec_batch, DEPTH, PAST_LEN * N_HEADS, HEAD_COLS),
                                   cache_v.reshape(dec_batch, DEPTH, PAST_LEN * N_HEADS, V_DIM))
    params = _small_params(norm1_g, w_pool, pool_scale, lam_q1, lam_k1, lam_q2, lam_k2, subln_g,
                           sgu_norm_g, w_sgu, b_sgu, norm2_g, final_g)

    big_f32 = (w_in, w_out, w_ffn_in, w_ffn_out)
    big = {"w_in": w_in[0].astype(BF16), "w_out": w_out[0].astype(BF16)}

    xp = x_prompt.reshape(batch * seq, D_MODEL)
    xs = x_sample.reshape(dec_batch * dec_seq, D_MODEL)
    new_cache = None
    for l in range(DEPTH):
        lam_init = 0.8 - 0.6 * math.exp(-0.3 * l)
        last = l == DEPTH - 1

        if l == 0:
            xs, big["w_ffn_in"], big["w_ffn_out"] = _mixer_call(
                xs, mod4, l, dec_seq, True, lam_init, params, big, ctx=ctx, cast_f32=(w_ffn_in, w_ffn_out))
        else:
            xs, = _mixer_call(xs, mod4, l, dec_seq, True, lam_init, params, big, ctx=ctx)
        xp, *new_cache = _mixer_call(xp, mod4, l, seq, False, lam_init, params, big, new_cache=new_cache)
        xp, xs, *next_big = _ffn_call(xp, xs, mod4, l, dec_seq, params, big, last,
                                      next_f32=() if last else big_f32)
        big = dict(zip(BIG_WEIGHTS, next_big))

    y_prompt = xp.reshape(batch, seq, D_MODEL)
    y_sample = xs.reshape(dec_batch, dec_seq, D_MODEL)
    new_cache_k = new_cache[0].reshape(batch, DEPTH, seq, N_HEADS, 2 * QK_DIM)
    new_cache_v = new_cache[1].reshape(batch, DEPTH, seq, N_HEADS, V_DIM)
    return (y_prompt, y_sample, new_cache_k, new_cache_v)
```

```python
import functools
import math
import types

import numpy as np
import jax
import jax.numpy as jnp
from jax import lax
from jax.experimental import pallas as pl
from jax.experimental.pallas import tpu as pltpu

D_MODEL = 1024
DEPTH = 2
GRID_W = 64
POOL_WINDOWS = (2, 4, 8, 16)
POOL_WIDTH = 256
POOL_GROUP = 64
POOL_HALO = 8
ATTN_WIDTH = 512
N_HEADS = 4
V_DIM = 128
QK_DIM = 64
HEAD_COLS = 2 * QK_DIM
ROPE_BASE = 10000.0
ROPE_AXIS_DIM = 32
ROPE_HALF = ROPE_AXIS_DIM // 2
CHUNK = 128
SGU_WIDTH = 256
QK_WIDTH = 512
IN_WIDTH = 2304
MIX_WIDTH = 1024
D_FF = 2816
N_MOD = 6
EPS = 1e-6
PAST_LEN = 256

C_POOL, C_Q, C_K, C_V, C_UV = 0, 256, 768, 1280, 1792
Y_POOL, Y_ATTN, Y_SGU = 0, 256, 768

ROWS = 1024
ROW_CHUNK = 256
ATTN_LOOKAHEAD = 2
CHUNKS_PER_BLOCK = 2
FFN_ROWS = 512
MXU_COLS = 256
FF_SPLIT = (D_FF // MXU_COLS + 1) // 2 * MXU_COLS
COND_ROWS = 16
ADA_COLS = 3072
VMEM_LIMIT = 58 * 1024 * 1024

BF16 = jnp.bfloat16
F32 = jnp.float32


def _dot(a, b):
    return jnp.dot(a, b, preferred_element_type=F32)


def _sigmoid(x):
    return 1.0 / (1.0 + jnp.exp(-x))


def _rms_mod(x, g, scale, shift):
    ms = jnp.mean(x * x, axis=-1, keepdims=True)
    return x * lax.rsqrt(ms + EPS) * (g * (1.0 + scale)) + shift


def _gelu_tanh(x):
    c = math.sqrt(2.0 / math.pi)
    return x * (0.5 * (1.0 + jnp.tanh(c * (x + 0.044715 * (x * x * x)))))


def _const_spec(shape):
    zeros = (0,) * len(shape)
    return pl.BlockSpec(shape, lambda *_: zeros, pipeline_mode=pl.Buffered(1))


def _layer_spec(shape, layer):
    index = (layer,) + (0,) * len(shape)
    return pl.BlockSpec((None,) + tuple(shape), lambda *_: index, pipeline_mode=pl.Buffered(1))


def _ada_kernel(cond_ref, w_ref, b_ref, out_ref):
    cond = cond_ref[...]
    s = (cond * _sigmoid(cond)).astype(BF16)
    out_ref[...] = _dot(s, w_ref[...].astype(BF16)) + b_ref[...]


def _ada_modulation(cond, w_ada, b_ada):
    n_cols = N_MOD * D_MODEL
    return pl.pallas_call(
        _ada_kernel,
        out_shape=jax.ShapeDtypeStruct((DEPTH, COND_ROWS, n_cols), F32),
        grid=(DEPTH, n_cols // ADA_COLS),
        in_specs=[
            pl.BlockSpec((COND_ROWS, D_MODEL), lambda l, j: (0, 0)),
            pl.BlockSpec((None, D_MODEL, ADA_COLS), lambda l, j: (l, 0, j)),
            pl.BlockSpec((None, 1, ADA_COLS), lambda l, j: (l, 0, j)),
        ],
        out_specs=pl.BlockSpec((None, COND_ROWS, ADA_COLS), lambda l, j: (l, 0, j)),
        compiler_params=pltpu.CompilerParams(
            dimension_semantics=("arbitrary", "arbitrary"), vmem_limit_bytes=VMEM_LIMIT),
        name="ada_modulation",
    )(cond, w_ada, b_ada.reshape(DEPTH, 1, n_cols))


def _mixer_body(r, x_ref, cnt_ref, pool_scr, k_scr, v_scr, *, nb, seq, has_ctx, lam_init, cpb):
    kv_base = PAST_LEN if has_ctx else 0
    shift1, scale1, gate1 = r.mod[0:1, :], r.mod[1:2, :], r.mod[2:3, :]

    lane_head = lax.broadcasted_iota(jnp.int32, (ROW_CHUNK, HEAD_COLS), 1)
    first_half = lane_head < QK_DIM

    for b in range(nb):
        pool_scr[b, 0:POOL_HALO, :] = jnp.zeros((POOL_HALO, POOL_WIDTH), F32)
        pool_scr[b, POOL_HALO + seq:POOL_HALO + seq + POOL_HALO, :] = jnp.zeros((POOL_HALO, POOL_WIDTH), F32)
    if not has_ctx:
        for slot in range(r.kc_out.shape[1]):
            if slot != r.cache_slot:
                r.kc_out[:, slot] = jnp.zeros((nb,) + r.kc_out.shape[2:], F32)
                r.vc_out[:, slot] = jnp.zeros((nb,) + r.vc_out.shape[2:], F32)
    if has_ctx:
        for hd in range(N_HEADS):
            cols = slice(hd * HEAD_COLS, (hd + 1) * HEAD_COLS)
            k_scr[0, cols, 0:PAST_LEN] = r.ck[pl.ds(hd, PAST_LEN, stride=N_HEADS), :].T.astype(BF16)
            v_scr[0, 0:PAST_LEN, cols] = r.cv[pl.ds(hd, PAST_LEN, stride=N_HEADS), :].astype(BF16)
        lane_q = lax.broadcasted_iota(jnp.int32, (ROW_CHUNK, QK_WIDTH), 1)
        rope_low = (lane_q % ROPE_AXIS_DIM) < ROPE_HALF

        def rope(t, pos0):
            partner = jnp.where(rope_low,
                                pltpu.roll(t, QK_WIDTH - ROPE_HALF, 1),
                                pltpu.roll(t, ROPE_HALF, 1))
            cos = r.rope_c[pos0:pos0 + ROW_CHUNK, :]
            sin = r.rope_s[pos0:pos0 + ROW_CHUNK, :]
            cos = jnp.concatenate([cos] * N_HEADS, axis=1)
            sin = jnp.concatenate([sin] * N_HEADS, axis=1)
            return t * cos + partner * sin

    for c in range(ROWS // ROW_CHUNK):
        r0 = c * ROW_CHUNK
        b, pos0 = r0 // seq, r0 % seq
        rows = slice(r0, r0 + ROW_CHUNK)
        h = _rms_mod(x_ref[rows, :], r.g1[r.layer_row, :], scale1, shift1).astype(BF16)

        pool_scr[b, POOL_HALO + pos0:POOL_HALO + pos0 + ROW_CHUNK, :] = _dot(h, r.w_in[:, C_POOL:C_Q])

        q = _dot(h, r.w_in[:, C_Q:C_K])
        k = _dot(h, r.w_in[:, C_K:C_V])
        if has_ctx:
            q = rope(q, pos0)
            k = rope(k, pos0)
        else:
            for hd in range(N_HEADS):
                r.kc_out[b, r.cache_slot, pl.ds(N_HEADS * pos0 + hd, ROW_CHUNK, stride=N_HEADS), :] = (
                    k[:, hd * HEAD_COLS:(hd + 1) * HEAD_COLS])
        q = q * (QK_DIM ** -0.5 * math.log2(math.e))
        for hd in range(N_HEADS):
            cols = slice(hd * HEAD_COLS, (hd + 1) * HEAD_COLS)
            qh = q[:, cols]
            r.q_scr[c, 0:ROW_CHUNK, cols] = jnp.where(first_half, qh, 0.0).astype(BF16)
            r.q_scr[c, ROW_CHUNK:2 * ROW_CHUNK, cols] = jnp.where(first_half, 0.0, qh).astype(BF16)
        k_scr[b, :, kv_base + pos0:kv_base + pos0 + ROW_CHUNK] = k.T.astype(BF16)

        v = _dot(h, r.w_in[:, C_V:C_UV])
        if not has_ctx:
            for hd in range(N_HEADS):
                r.vc_out[b, r.cache_slot, pl.ds(N_HEADS * pos0 + hd, ROW_CHUNK, stride=N_HEADS), :] = (
                    v[:, hd * V_DIM:(hd + 1) * V_DIM])
        v_scr[b, kv_base + pos0:kv_base + pos0 + ROW_CHUNK, :] = v.astype(BF16)

        uv = _gelu_tanh(_dot(h, r.w_in[:, C_UV:IN_WIDTH]))
        r.u_scr[rows, :] = uv[:, :SGU_WIDTH]
        vv = uv[:, SGU_WIDTH:]
        mu = jnp.mean(vv, axis=-1, keepdims=True)
        vc = vv - mu
        vn = vc * lax.rsqrt(jnp.mean(vc * vc, axis=-1, keepdims=True) + EPS) * r.sgu_g[r.layer_row, :]
        r.vs_scr[rows, :] = vn.astype(BF16)

    lane_pool = lax.broadcasted_iota(jnp.int32, (ROW_CHUNK, 2 * POOL_GROUP), 1)
    narrow = lane_pool < POOL_GROUP
    for c in range(ROWS // ROW_CHUNK):
        r0 = c * ROW_CHUNK
        b, pos0 = r0 // seq, r0 % seq
        base = POOL_HALO + pos0
        pooled = []
        for j, (w_small, w_big) in enumerate(((POOL_WINDOWS[0], POOL_WINDOWS[1]),
                                              (POOL_WINDOWS[2], POOL_WINDOWS[3]))):
            cols = slice(j * 2 * POOL_GROUP, (j + 1) * 2 * POOL_GROUP)
            s_small = jnp.zeros((ROW_CHUNK, 2 * POOL_GROUP), F32)
            s_rest = jnp.zeros((ROW_CHUNK, 2 * POOL_GROUP), F32)
            for d in range(-(w_big // 2), w_big - w_big // 2):
                t = pool_scr[b, base + d:base + d + ROW_CHUNK, cols]
                if -(w_small // 2) <= d < w_small - w_small // 2:
                    s_small = s_small + t
                else:
                    s_rest = s_rest + t
            win_sum = jnp.where(narrow, s_small, s_small + s_rest)
            centre = pool_scr[b, base:base + ROW_CHUNK, cols]
            pooled.append(win_sum / cnt_ref[pos0:pos0 + ROW_CHUNK, cols] - centre)
        pooled = jnp.concatenate(pooled, axis=1).astype(BF16)
        y_a = _dot(pooled, r.wpool[...]) * r.pscale[r.layer_row, :]
        r.y_scr[r0:r0 + ROW_CHUNK, Y_POOL:Y_ATTN] = y_a.astype(BF16)

    lq1, lk1, lq2, lk2 = r.lam[0:1, :], r.lam[1:2, :], r.lam[2:3, :], r.lam[3:4, :]
    lam = (jnp.exp(jnp.sum(lq1 * lk1, axis=-1, keepdims=True))
           - jnp.exp(jnp.sum(lq2 * lk2, axis=-1, keepdims=True)) + lam_init)
    n_qb = seq // ROW_CHUNK

    lane_sgu = lax.broadcasted_iota(jnp.int32, (CHUNK, 2 * POOL_GROUP), 1)
    sgu_first = lane_sgu < (SGU_WIDTH // 4)
    out_cols = MIX_WIDTH // N_HEADS

    def out_proj(rows, part):
        cols = slice(part * out_cols, (part + 1) * out_cols)
        y = _dot(r.y_scr[rows, :], r.w_out[:, cols])
        r.out[rows, cols] = x_ref[rows, cols] + gate1[:, cols] * y

    def block(blk, with_prev):
        static = isinstance(blk, int)

        def row0(chunk):
            return chunk * ROW_CHUNK if static else pl.multiple_of(chunk * ROW_CHUNK, ROW_CHUNK)

        chunks = [blk * cpb + j for j in range(cpb)]

        for idx in chunks:
            for half in range(ROW_CHUNK // CHUNK):
                sub = pl.ds(row0(idx) + half * CHUNK, CHUNK)
                for j in range(2):
                    cols = slice(j * 128, (j + 1) * 128)
                    t = _dot(r.wsgu[j], r.vs_scr[sub, cols])
                    mixed = jnp.where(sgu_first, t[:CHUNK, :], t[CHUNK:, :]) + r.bsgu[:, cols]
                    y_c = r.u_scr[sub, cols] * mixed
                    r.y_scr[sub, Y_SGU + j * 128:Y_SGU + (j + 1) * 128] = y_c.astype(BF16)

        units = [(idx, hd) for idx in chunks for hd in range(N_HEADS)]
        prev = [(pl.ds(row0(idx - cpb), ROW_CHUNK), part) for idx in chunks for part in range(N_HEADS)]

        def scores(idx, hd):
            cols = slice(hd * HEAD_COLS, (hd + 1) * HEAD_COLS)
            s = _dot(r.q_scr[idx, :, cols], k_scr[idx // n_qb, cols, :])
            return s[:ROW_CHUNK], s[ROW_CHUNK:]

        pending = [scores(*u) for u in units[:ATTN_LOOKAHEAD]]
        for n, (idx, hd) in enumerate(units):
            cols = slice(hd * HEAD_COLS, (hd + 1) * HEAD_COLS)
            s1, s2 = pending.pop(0)
            if n + ATTN_LOOKAHEAD < len(units):
                pending.append(scores(*units[n + ATTN_LOOKAHEAD]))
            e1 = jnp.exp2(s1 - jnp.max(s1, axis=-1, keepdims=True))
            e2 = jnp.exp2(s2 - jnp.max(s2, axis=-1, keepdims=True))
            l1 = jnp.sum(e1, axis=-1, keepdims=True)
            l2 = jnp.sum(e2, axis=-1, keepdims=True)
            p = (e1 - e2 * (lam * l1 / l2)).astype(BF16)
            o = _dot(p, v_scr[idx // n_qb, :, cols]) * (1.0 / l1)
            if with_prev:
                out_proj(*prev[n])
            o = o * lax.rsqrt(jnp.mean(o * o, axis=-1, keepdims=True) + EPS) * r.subln[r.layer_row, :]
            o = o * (1.0 - lam_init)
            r.y_scr[pl.ds(row0(idx), ROW_CHUNK), Y_ATTN + hd * V_DIM:Y_ATTN + (hd + 1) * V_DIM] = o.astype(BF16)

    n_blocks = ROWS // (ROW_CHUNK * cpb)
    block(0, False)
    if n_blocks == 2:
        block(1, True)
    else:
        def loop_block(blk, carry):
            block(blk, True)
            return carry

        lax.fori_loop(1, n_blocks, loop_block, 0)
    for idx in range((n_blocks - 1) * cpb, n_blocks * cpb):
        for part in range(N_HEADS):
            out_proj(pl.ds(idx * ROW_CHUNK, ROW_CHUNK), part)


def _mixer_kernel(*refs, layer, nb, seq, has_ctx, lam_init, cache_slot, n_passthrough, n_cast):
    it = iter(refs)
    r = types.SimpleNamespace(cache_slot=cache_slot, layer_row=slice(layer, layer + 1))
    x_ref, r.mod, r.g1, r.w_in = next(it), next(it), next(it), next(it)
    if has_ctx:
        r.rope_c, r.rope_s, r.ck, r.cv = next(it), next(it), next(it), next(it)
    (r.lam, r.subln, cnt_ref, r.wpool, r.pscale, r.sgu_g, r.wsgu, r.bsgu, r.w_out) = (next(it) for _ in range(9))
    for _ in range(n_passthrough):
        next(it)
    cast_in = [next(it) for _ in range(n_cast)]
    r.out = next(it)
    if not has_ctx:
        r.kc_out, r.vc_out = next(it), next(it)
    cast_out = [next(it) for _ in range(n_cast)]
    pool_scr, r.q_scr, k_scr, v_scr, r.u_scr, r.vs_scr, r.y_scr = (next(it) for _ in range(7))
    for src, dst in zip(cast_in, cast_out):
        dst[...] = src[...].astype(BF16)
    _mixer_body(r, x_ref, cnt_ref, pool_scr, k_scr, v_scr, nb=nb, seq=seq, has_ctx=has_ctx, lam_init=lam_init,
                cpb=CHUNKS_PER_BLOCK)


def _mixer_call(x2d, mod4, layer, seq, has_ctx, lam_init, params, big, ctx=None, new_cache=None, cast_f32=()):
    n_rows = x2d.shape[0]
    nb = ROWS // seq
    n_seq = n_rows // seq
    kv_len = seq + (PAST_LEN if has_ctx else 0)

    if has_ctx:
        mod_map = lambda i: (layer, 1 + i, 0, 0)
    else:
        mod_map = lambda i: (layer, 0, 0, 0)

    operands = [x2d, mod4, params["norm1_g"], big["w_in"]]
    in_specs = [
        pl.BlockSpec((ROWS, D_MODEL), lambda i: (i, 0)),
        pl.BlockSpec((None, None, N_MOD, D_MODEL), mod_map),
        _const_spec((DEPTH, D_MODEL)),
        _const_spec((D_MODEL, IN_WIDTH)),
    ]
    if has_ctx:
        operands += list(ctx)
        in_specs += [
            _const_spec((seq, HEAD_COLS)),
            _const_spec((seq, HEAD_COLS)),
            pl.BlockSpec((None, None, PAST_LEN * N_HEADS, HEAD_COLS), lambda i: (i, layer, 0, 0)),
            pl.BlockSpec((None, None, PAST_LEN * N_HEADS, V_DIM), lambda i: (i, layer, 0, 0)),
        ]
    operands += [params["lam"], params["subln_g"], _pool_counts(seq), params["w_pool"], params["pool_scale"],
                 params["sgu_norm_g"], params["w_sgu"], params["b_sgu"], big["w_out"]]
    in_specs += [
        _layer_spec((8, 128), layer),
        _const_spec((DEPTH, V_DIM)),
        _const_spec((seq, POOL_WIDTH)),
        _layer_spec((POOL_WIDTH, POOL_WIDTH), layer),
        _const_spec((DEPTH, POOL_WIDTH)),
        _const_spec((DEPTH, SGU_WIDTH)),
        _layer_spec((2, 2 * CHUNK, CHUNK), layer),
        _layer_spec((CHUNK, SGU_WIDTH), layer),
        _const_spec((MIX_WIDTH, D_MODEL)),
    ]

    out_shape = [jax.ShapeDtypeStruct((n_rows, D_MODEL), F32)]
    out_specs = [pl.BlockSpec((ROWS, D_MODEL), lambda i: (i, 0))]
    aliases = {}
    cache_slot = 0
    if not has_ctx:
        out_shape += [jax.ShapeDtypeStruct((n_seq, DEPTH, seq * N_HEADS, HEAD_COLS), F32),
                      jax.ShapeDtypeStruct((n_seq, DEPTH, seq * N_HEADS, V_DIM), F32)]
        if new_cache is None:
            cache_slot = layer
            out_specs += [pl.BlockSpec((nb, DEPTH, seq * N_HEADS, HEAD_COLS), lambda i: (i, 0, 0, 0)),
                          pl.BlockSpec((nb, DEPTH, seq * N_HEADS, V_DIM), lambda i: (i, 0, 0, 0))]
        else:
            out_specs += [pl.BlockSpec((nb, 1, seq * N_HEADS, HEAD_COLS), lambda i: (i, layer, 0, 0)),
                          pl.BlockSpec((nb, 1, seq * N_HEADS, V_DIM), lambda i: (i, layer, 0, 0))]
            aliases = {len(operands): 1, len(operands) + 1: 2}
            operands += list(new_cache)
            in_specs += [pl.BlockSpec(memory_space=pl.ANY)] * 2
    n_steps = n_rows // ROWS
    for w in cast_f32:
        _, rows, cols = w.shape
        slab = rows // n_steps
        assert slab * n_steps == rows and slab % 16 == 0
        operands.append(w)
        in_specs.append(pl.BlockSpec((None, slab, cols), lambda i: (layer, i, 0)))
        out_shape.append(jax.ShapeDtypeStruct((rows, cols), BF16))
        out_specs.append(pl.BlockSpec((slab, cols), lambda i: (i, 0)))

    scratch = [
        pltpu.VMEM((nb, seq + 2 * POOL_HALO, POOL_WIDTH), F32),
        pltpu.VMEM((ROWS // ROW_CHUNK, 2 * ROW_CHUNK, QK_WIDTH), BF16),
        pltpu.VMEM((nb, QK_WIDTH, kv_len), BF16),
        pltpu.VMEM((nb, kv_len, ATTN_WIDTH), BF16),
        pltpu.VMEM((ROWS, SGU_WIDTH), F32),
        pltpu.VMEM((ROWS, SGU_WIDTH), BF16),
        pltpu.VMEM((ROWS, MIX_WIDTH), BF16),
    ]
    kernel = functools.partial(_mixer_kernel, layer=layer, nb=nb, seq=seq, has_ctx=has_ctx, lam_init=lam_init,
                               cache_slot=cache_slot, n_passthrough=len(aliases), n_cast=len(cast_f32))
    return pl.pallas_call(
        kernel,
        out_shape=out_shape,
        grid=(n_rows // ROWS,),
        in_specs=in_specs,
        out_specs=out_specs,
        scratch_shapes=scratch,
        input_output_aliases=aliases,
        compiler_params=pltpu.CompilerParams(
            dimension_semantics=("arbitrary",), vmem_limit_bytes=VMEM_LIMIT),
        name="mixer_ctx" if has_ctx else "mixer_prompt",
    )(*operands)


def _ffn_block(x_ref, out_ref, mod_ref, g2_ref, w1_ref, w2_ref, gf_ref, act_scr, final_norm, layer):
    shift2, scale2, gate2 = mod_ref[3:4, :], mod_ref[4:5, :], mod_ref[5:6, :]
    x = x_ref[...]
    h = _rms_mod(x, g2_ref[layer:layer + 1, :], scale2, shift2).astype(BF16)
    for lo, hi in ((0, FF_SPLIT), (FF_SPLIT, D_FF)):
        gate = _dot(h, w1_ref[:, lo:hi])
        up = _dot(h, w1_ref[:, D_FF + lo:D_FF + hi])
        act_scr[:, lo:hi] = (gate * _sigmoid(gate) * up).astype(BF16)
    y = x + gate2 * _dot(act_scr[...], w2_ref[...])
    if final_norm:
        ms = jnp.mean(y * y, axis=-1, keepdims=True)
        y = y * lax.rsqrt(ms + EPS) * gf_ref[...]
    out_ref[...] = y


def _ffn_kernel(*refs, layer, n_prompt_steps, final_norm, n_cast):
    xp_ref, xs_ref, mod_ref, g2_ref, w1_ref, w2_ref, gf_ref = refs[:7]
    cast_in = refs[7:7 + n_cast]
    yp_ref, ys_ref = refs[7 + n_cast:9 + n_cast]
    cast_out = refs[9 + n_cast:9 + 2 * n_cast]
    act_scr = refs[9 + 2 * n_cast]
    shared = (mod_ref, g2_ref, w1_ref, w2_ref, gf_ref, act_scr, final_norm, layer)
    is_prompt = pl.program_id(0) < n_prompt_steps

    @pl.when(is_prompt)
    def _():
        _ffn_block(xp_ref, yp_ref, *shared)

    @pl.when(jnp.logical_not(is_prompt))
    def _():
        for src, dst in zip(cast_in, cast_out):
            dst[...] = src[...].astype(BF16)
        _ffn_block(xs_ref, ys_ref, *shared)


def _ffn_call(xp, xs, mod4, layer, sample_seq, params, big, final_norm, next_f32=()):
    n_p, n_s = xp.shape[0] // FFN_ROWS, xs.shape[0] // FFN_ROWS
    steps_per_seq = sample_seq // FFN_ROWS

    def prompt_step(i):
        return jnp.minimum(i, n_p - 1)

    def sample_step(i):
        return jnp.maximum(i - n_p, 0)

    def mod_map(i):
        return (layer, jnp.where(i < n_p, 0, 1 + sample_step(i) // steps_per_seq), 0, 0)

    in_specs = [
        pl.BlockSpec((FFN_ROWS, D_MODEL), lambda i: (prompt_step(i), 0)),
        pl.BlockSpec((FFN_ROWS, D_MODEL), lambda i: (sample_step(i), 0)),
        pl.BlockSpec((None, None, N_MOD, D_MODEL), mod_map),
        _const_spec((DEPTH, D_MODEL)),
        _const_spec((D_MODEL, 2 * D_FF)),
        _const_spec((D_FF, D_MODEL)),
        _const_spec((1, D_MODEL)),
    ]
    out_shape = [jax.ShapeDtypeStruct(xp.shape, F32), jax.ShapeDtypeStruct(xs.shape, F32)]
    out_specs = [pl.BlockSpec((FFN_ROWS, D_MODEL), lambda i: (prompt_step(i), 0)),
                 pl.BlockSpec((FFN_ROWS, D_MODEL), lambda i: (sample_step(i), 0))]
    for w in next_f32:
        _, rows, cols = w.shape
        slab = rows // n_s
        assert slab * n_s == rows and slab % 16 == 0
        in_specs.append(pl.BlockSpec((None, slab, cols), lambda i: (layer + 1, sample_step(i), 0)))
        out_shape.append(jax.ShapeDtypeStruct((rows, cols), BF16))
        out_specs.append(pl.BlockSpec((slab, cols), lambda i: (sample_step(i), 0)))

    kernel = functools.partial(_ffn_kernel, layer=layer, n_prompt_steps=n_p, final_norm=final_norm, n_cast=len(next_f32))
    return pl.pallas_call(
        kernel,
        out_shape=out_shape,
        grid=(n_p + n_s,),
        in_specs=in_specs,
        out_specs=out_specs,
        scratch_shapes=[pltpu.VMEM((FFN_ROWS, D_FF), BF16)],
        compiler_params=pltpu.CompilerParams(
            dimension_semantics=("arbitrary",), vmem_limit_bytes=VMEM_LIMIT),
        name="ffn",
    )(xp, xs, mod4, params["norm2_g"], big["w_ffn_in"], big["w_ffn_out"], params["final_g"], *next_f32)


def _rope_tables(seq):
    n_rows = seq // GRID_W
    rows = np.repeat(np.arange(n_rows), GRID_W).astype(np.float32)
    cols = np.tile(np.arange(GRID_W), n_rows).astype(np.float32)
    inv = 1.0 / (ROPE_BASE ** (np.arange(0, ROPE_AXIS_DIM, 2, dtype=np.float32) / ROPE_AXIS_DIM))
    ar, ac = rows[:, None] * inv[None], cols[:, None] * inv[None]
    cos_parts, sin_parts = [], []
    for ang in (ar, ac):
        cos_parts += [np.cos(ang), np.cos(ang)]
        sin_parts += [-np.sin(ang), np.sin(ang)]
    cos64 = np.concatenate(cos_parts, axis=1)
    sin64 = np.concatenate(sin_parts, axis=1)
    cos = np.concatenate([cos64, cos64], axis=1).astype(np.float32)
    sin = np.concatenate([sin64, sin64], axis=1).astype(np.float32)
    return jnp.asarray(cos), jnp.asarray(sin)


def _pool_counts(seq):
    t = np.arange(seq)
    cols = []
    for w in POOL_WINDOWS:
        lo = np.clip(t - w // 2, 0, seq)
        hi = np.clip(t + w - w // 2, 0, seq)
        cols.append(np.repeat((hi - lo).astype(np.float32)[:, None], POOL_GROUP, axis=1))
    return jnp.asarray(np.concatenate(cols, axis=1))


def _small_params(norm1_g, w_pool, pool_scale, lam_q1, lam_k1, lam_q2, lam_k2, subln_g,
                  sgu_norm_g, w_sgu, b_sgu, norm2_g, final_g):
    n_groups = len(POOL_WINDOWS)
    eye = jnp.eye(n_groups, dtype=F32)
    w_pool_bd = (eye[None, :, None, :, None] * w_pool[:, :, :, None, :]).reshape(DEPTH, POOL_WIDTH, POOL_WIDTH)
    lam = jnp.pad(jnp.stack([lam_q1, lam_k1, lam_q2, lam_k2], axis=1), ((0, 0), (0, 4), (0, 128 - QK_DIM)))
    return {
        "norm1_g": norm1_g,
        "lam": lam,
        "subln_g": subln_g,
        "w_pool": w_pool_bd.astype(BF16),
        "pool_scale": pool_scale,
        "sgu_norm_g": sgu_norm_g,
        "w_sgu": w_sgu.reshape(DEPTH, 2, 2 * CHUNK, CHUNK).astype(BF16),
        "b_sgu": jnp.repeat(jnp.swapaxes(b_sgu, 1, 2), SGU_WIDTH // 4, axis=2),
        "norm2_g": norm2_g,
        "final_g": final_g[None, :],
    }


BIG_WEIGHTS = ("w_in", "w_out", "w_ffn_in", "w_ffn_out")


def kernel(x_prompt, x_sample, cache_k, cache_v, c, c_ctx, norm1_g, w_ada, b_ada, w_in, w_pool, pool_scale, lam_q1, lam_k1, lam_q2, lam_k2, subln_g, sgu_norm_g, w_sgu, b_sgu, w_out, norm2_g, w_ffn_in, w_ffn_out, final_g):
    batch, seq, _ = x_prompt.shape
    dec_batch, dec_seq, _ = x_sample.shape
    assert ROWS % seq == 0 and dec_seq == ROWS and 1 + dec_batch <= COND_ROWS

    cond = jnp.concatenate([c_ctx[None, :], c, jnp.zeros((COND_ROWS - 1 - dec_batch, D_MODEL), F32)], axis=0)
    mod4 = _ada_modulation(cond, w_ada, b_ada).reshape(DEPTH, COND_ROWS, N_MOD, D_MODEL)

    ctx = _rope_tables(dec_seq) + (cache_k.reshape(dec_batch, DEPTH, PAST_LEN * N_HEADS, HEAD_COLS),
                                   cache_v.reshape(dec_batch, DEPTH, PAST_LEN * N_HEADS, V_DIM))
    params = _small_params(norm1_g, w_pool, pool_scale, lam_q1, lam_k1, lam_q2, lam_k2, subln_g,
                           sgu_norm_g, w_sgu, b_sgu, norm2_g, final_g)

    big_f32 = (w_in, w_out, w_ffn_in, w_ffn_out)
    big = {"w_in": w_in[0].astype(BF16), "w_out": w_out[0].astype(BF16)}

    xp = x_prompt.reshape(batch * seq, D_MODEL)
    xs = x_sample.reshape(dec_batch * dec_seq, D_MODEL)
    new_cache = None
    for l in range(DEPTH):
        lam_init = 0.8 - 0.6 * math.exp(-0.3 * l)
        last = l == DEPTH - 1

        xp, *new_cache = _mixer_call(xp, mod4, l, seq, False, lam_init, params, big, new_cache=new_cache)
        if l == 0:
            xs, big["w_ffn_in"], big["w_ffn_out"] = _mixer_call(
                xs, mod4, l, dec_seq, True, lam_init, params, big, ctx=ctx, cast_f32=(w_ffn_in, w_ffn_out))
        else:
            xs, = _mixer_call(xs, mod4, l, dec_seq, True, lam_init, params, big, ctx=ctx)
        xp, xs, *next_big = _ffn_call(xp, xs, mod4, l, dec_seq, params, big, last,
                                      next_f32=() if last else big_f32)
        big = dict(zip(BIG_WEIGHTS, next_big))

    y_prompt = xp.reshape(batch, seq, D_MODEL)
    y_sample = xs.reshape(dec_batch, dec_seq, D_MODEL)
    new_cache_k = new_cache[0].reshape(batch, DEPTH, seq, N_HEADS, 2 * QK_DIM)
    new_cache_v = new_cache[1].reshape(batch, DEPTH, seq, N_HEADS, V_DIM)
    return (y_prompt, y_sample, new_cache_k, new_cache_v)
```

```python
import functools
import math
import types

import numpy as np
import jax
import jax.numpy as jnp
from jax import lax
from jax.experimental import pallas as pl
from jax.experimental.pallas import tpu as pltpu

D_MODEL = 1024
DEPTH = 2
GRID_W = 64
POOL_WINDOWS = (2, 4, 8, 16)
POOL_WIDTH = 256
POOL_GROUP = 64
POOL_HALO = 8
ATTN_WIDTH = 512
N_HEADS = 4
V_DIM = 128
QK_DIM = 64
HEAD_COLS = 2 * QK_DIM
ROPE_BASE = 10000.0
ROPE_AXIS_DIM = 32
ROPE_HALF = ROPE_AXIS_DIM // 2
CHUNK = 128
SGU_WIDTH = 256
QK_WIDTH = 512
IN_WIDTH = 2304
MIX_WIDTH = 1024
D_FF = 2816
N_MOD = 6
EPS = 1e-6
PAST_LEN = 256

C_POOL, C_Q, C_K, C_V, C_UV = 0, 256, 768, 1280, 1792
Y_POOL, Y_ATTN, Y_SGU = 0, 256, 768

ROWS = 1024
ROW_CHUNK = 256
ATTN_LOOKAHEAD = 2
ATTN_LOOKAHEAD_SHORT = 4
CHUNKS_PER_BLOCK = 2
FFN_ROWS = 512
MXU_COLS = 256
FF_SPLIT = (D_FF // MXU_COLS + 1) // 2 * MXU_COLS
COND_ROWS = 16
ADA_COLS = 3072
VMEM_LIMIT = 58 * 1024 * 1024

BF16 = jnp.bfloat16
F32 = jnp.float32


def _dot(a, b):
    return jnp.dot(a, b, preferred_element_type=F32)


def _sigmoid(x):
    return 1.0 / (1.0 + jnp.exp(-x))


def _rms_mod(x, g, scale, shift):
    ms = jnp.mean(x * x, axis=-1, keepdims=True)
    return x * lax.rsqrt(ms + EPS) * (g * (1.0 + scale)) + shift


def _gelu_tanh(x):
    c = math.sqrt(2.0 / math.pi)
    return x * (0.5 * (1.0 + jnp.tanh(c * (x + 0.044715 * (x * x * x)))))


def _const_spec(shape):
    zeros = (0,) * len(shape)
    return pl.BlockSpec(shape, lambda *_: zeros, pipeline_mode=pl.Buffered(1))


def _layer_spec(shape, layer):
    index = (layer,) + (0,) * len(shape)
    return pl.BlockSpec((None,) + tuple(shape), lambda *_: index, pipeline_mode=pl.Buffered(1))


def _ada_kernel(cond_ref, w_ref, b_ref, out_ref):
    cond = cond_ref[...]
    s = (cond * _sigmoid(cond)).astype(BF16)
    out_ref[...] = _dot(s, w_ref[...].astype(BF16)) + b_ref[...]


def _ada_modulation(cond, w_ada, b_ada):
    n_cols = N_MOD * D_MODEL
    return pl.pallas_call(
        _ada_kernel,
        out_shape=jax.ShapeDtypeStruct((DEPTH, COND_ROWS, n_cols), F32),
        grid=(DEPTH, n_cols // ADA_COLS),
        in_specs=[
            pl.BlockSpec((COND_ROWS, D_MODEL), lambda l, j: (0, 0)),
            pl.BlockSpec((None, D_MODEL, ADA_COLS), lambda l, j: (l, 0, j)),
            pl.BlockSpec((None, 1, ADA_COLS), lambda l, j: (l, 0, j)),
        ],
        out_specs=pl.BlockSpec((None, COND_ROWS, ADA_COLS), lambda l, j: (l, 0, j)),
        compiler_params=pltpu.CompilerParams(
            dimension_semantics=("arbitrary", "arbitrary"), vmem_limit_bytes=VMEM_LIMIT),
        name="ada_modulation",
    )(cond, w_ada, b_ada.reshape(DEPTH, 1, n_cols))


def _mixer_body(r, x_ref, cnt_ref, pool_scr, k_scr, v_scr, *, nb, seq, has_ctx, lam_init, cpb):
    kv_base = PAST_LEN if has_ctx else 0
    shift1, scale1, gate1 = r.mod[0:1, :], r.mod[1:2, :], r.mod[2:3, :]

    lane_head = lax.broadcasted_iota(jnp.int32, (ROW_CHUNK, HEAD_COLS), 1)
    first_half = lane_head < QK_DIM

    for b in range(nb):
        pool_scr[b, 0:POOL_HALO, :] = jnp.zeros((POOL_HALO, POOL_WIDTH), F32)
        pool_scr[b, POOL_HALO + seq:POOL_HALO + seq + POOL_HALO, :] = jnp.zeros((POOL_HALO, POOL_WIDTH), F32)
    if not has_ctx:
        for slot in range(r.kc_out.shape[1]):
            if slot != r.cache_slot:
                r.kc_out[:, slot] = jnp.zeros((nb,) + r.kc_out.shape[2:], F32)
                r.vc_out[:, slot] = jnp.zeros((nb,) + r.vc_out.shape[2:], F32)
    if has_ctx:
        for hd in range(N_HEADS):
            cols = slice(hd * HEAD_COLS, (hd + 1) * HEAD_COLS)
            k_scr[0, cols, 0:PAST_LEN] = r.ck[pl.ds(hd, PAST_LEN, stride=N_HEADS), :].T.astype(BF16)
            v_scr[0, 0:PAST_LEN, cols] = r.cv[pl.ds(hd, PAST_LEN, stride=N_HEADS), :].astype(BF16)
        lane_q = lax.broadcasted_iota(jnp.int32, (ROW_CHUNK, QK_WIDTH), 1)
        rope_low = (lane_q % ROPE_AXIS_DIM) < ROPE_HALF

        def rope(t, pos0):
            partner = jnp.where(rope_low,
                                pltpu.roll(t, QK_WIDTH - ROPE_HALF, 1),
                                pltpu.roll(t, ROPE_HALF, 1))
            cos = r.rope_c[pos0:pos0 + ROW_CHUNK, :]
            sin = r.rope_s[pos0:pos0 + ROW_CHUNK, :]
            cos = jnp.concatenate([cos] * N_HEADS, axis=1)
            sin = jnp.concatenate([sin] * N_HEADS, axis=1)
            return t * cos + partner * sin

    for c in range(ROWS // ROW_CHUNK):
        r0 = c * ROW_CHUNK
        b, pos0 = r0 // seq, r0 % seq
        rows = slice(r0, r0 + ROW_CHUNK)
        h = _rms_mod(x_ref[rows, :], r.g1[r.layer_row, :], scale1, shift1).astype(BF16)

        pool_scr[b, POOL_HALO + pos0:POOL_HALO + pos0 + ROW_CHUNK, :] = _dot(h, r.w_in[:, C_POOL:C_Q])

        q = _dot(h, r.w_in[:, C_Q:C_K])
        k = _dot(h, r.w_in[:, C_K:C_V])
        if has_ctx:
            q = rope(q, pos0)
            k = rope(k, pos0)
        else:
            for hd in range(N_HEADS):
                r.kc_out[b, r.cache_slot, pl.ds(N_HEADS * pos0 + hd, ROW_CHUNK, stride=N_HEADS), :] = (
                    k[:, hd * HEAD_COLS:(hd + 1) * HEAD_COLS])
        q = q * (QK_DIM ** -0.5 * math.log2(math.e))
        for hd in range(N_HEADS):
            cols = slice(hd * HEAD_COLS, (hd + 1) * HEAD_COLS)
            qh = q[:, cols]
            r.q_scr[c, 0:ROW_CHUNK, cols] = jnp.where(first_half, qh, 0.0).astype(BF16)
            r.q_scr[c, ROW_CHUNK:2 * ROW_CHUNK, cols] = jnp.where(first_half, 0.0, qh).astype(BF16)
        k_scr[b, :, kv_base + pos0:kv_base + pos0 + ROW_CHUNK] = k.T.astype(BF16)

        v = _dot(h, r.w_in[:, C_V:C_UV])
        if not has_ctx:
            for hd in range(N_HEADS):
                r.vc_out[b, r.cache_slot, pl.ds(N_HEADS * pos0 + hd, ROW_CHUNK, stride=N_HEADS), :] = (
                    v[:, hd * V_DIM:(hd + 1) * V_DIM])
        v_scr[b, kv_base + pos0:kv_base + pos0 + ROW_CHUNK, :] = v.astype(BF16)

        uv = _gelu_tanh(_dot(h, r.w_in[:, C_UV:IN_WIDTH]))
        r.u_scr[rows, :] = uv[:, :SGU_WIDTH]
        vv = uv[:, SGU_WIDTH:]
        mu = jnp.mean(vv, axis=-1, keepdims=True)
        vc = vv - mu
        vn = vc * lax.rsqrt(jnp.mean(vc * vc, axis=-1, keepdims=True) + EPS) * r.sgu_g[r.layer_row, :]
        r.vs_scr[rows, :] = vn.astype(BF16)

    lane_pool = lax.broadcasted_iota(jnp.int32, (ROW_CHUNK, 2 * POOL_GROUP), 1)
    narrow = lane_pool < POOL_GROUP
    for c in range(ROWS // ROW_CHUNK):
        r0 = c * ROW_CHUNK
        b, pos0 = r0 // seq, r0 % seq
        base = POOL_HALO + pos0
        pooled = []
        for j, (w_small, w_big) in enumerate(((POOL_WINDOWS[0], POOL_WINDOWS[1]),
                                              (POOL_WINDOWS[2], POOL_WINDOWS[3]))):
            cols = slice(j * 2 * POOL_GROUP, (j + 1) * 2 * POOL_GROUP)
            s_small = jnp.zeros((ROW_CHUNK, 2 * POOL_GROUP), F32)
            s_rest = jnp.zeros((ROW_CHUNK, 2 * POOL_GROUP), F32)
            for d in range(-(w_big // 2), w_big - w_big // 2):
                t = pool_scr[b, base + d:base + d + ROW_CHUNK, cols]
                if -(w_small // 2) <= d < w_small - w_small // 2:
                    s_small = s_small + t
                else:
                    s_rest = s_rest + t
            win_sum = jnp.where(narrow, s_small, s_small + s_rest)
            centre = pool_scr[b, base:base + ROW_CHUNK, cols]
            pooled.append(win_sum / cnt_ref[pos0:pos0 + ROW_CHUNK, cols] - centre)
        pooled = jnp.concatenate(pooled, axis=1).astype(BF16)
        y_a = _dot(pooled, r.wpool[...]) * r.pscale[r.layer_row, :]
        r.y_scr[r0:r0 + ROW_CHUNK, Y_POOL:Y_ATTN] = y_a.astype(BF16)

    lq1, lk1, lq2, lk2 = r.lam[0:1, :], r.lam[1:2, :], r.lam[2:3, :], r.lam[3:4, :]
    lam = (jnp.exp(jnp.sum(lq1 * lk1, axis=-1, keepdims=True))
           - jnp.exp(jnp.sum(lq2 * lk2, axis=-1, keepdims=True)) + lam_init)
    n_qb = seq // ROW_CHUNK

    lane_sgu = lax.broadcasted_iota(jnp.int32, (CHUNK, 2 * POOL_GROUP), 1)
    sgu_first = lane_sgu < (SGU_WIDTH // 4)
    out_cols = MIX_WIDTH // N_HEADS

    def out_proj(rows, part):
        cols = slice(part * out_cols, (part + 1) * out_cols)
        y = _dot(r.y_scr[rows, :], r.w_out[:, cols])
        r.out[rows, cols] = x_ref[rows, cols] + gate1[:, cols] * y

    def block(blk, with_prev):
        static = isinstance(blk, int)

        def row0(chunk):
            return chunk * ROW_CHUNK if static else pl.multiple_of(chunk * ROW_CHUNK, ROW_CHUNK)

        chunks = [blk * cpb + j for j in range(cpb)]

        for idx in chunks:
            for half in range(ROW_CHUNK // CHUNK):
                sub = pl.ds(row0(idx) + half * CHUNK, CHUNK)
                for j in range(2):
                    cols = slice(j * 128, (j + 1) * 128)
                    t = _dot(r.wsgu[j], r.vs_scr[sub, cols])
                    mixed = jnp.where(sgu_first, t[:CHUNK, :], t[CHUNK:, :]) + r.bsgu[:, cols]
                    y_c = r.u_scr[sub, cols] * mixed
                    r.y_scr[sub, Y_SGU + j * 128:Y_SGU + (j + 1) * 128] = y_c.astype(BF16)

        units = [(idx, hd) for idx in chunks for hd in range(N_HEADS)]
        prev = [(pl.ds(row0(idx - cpb), ROW_CHUNK), part) for idx in chunks for part in range(N_HEADS)]

        def scores(idx, hd):
            cols = slice(hd * HEAD_COLS, (hd + 1) * HEAD_COLS)
            s = _dot(r.q_scr[idx, :, cols], k_scr[idx // n_qb, cols, :])
            return s[:ROW_CHUNK], s[ROW_CHUNK:]

        ahead = ATTN_LOOKAHEAD if has_ctx else ATTN_LOOKAHEAD_SHORT
        pending = [scores(*u) for u in units[:ahead]]
        for n, (idx, hd) in enumerate(units):
            cols = slice(hd * HEAD_COLS, (hd + 1) * HEAD_COLS)
            s1, s2 = pending.pop(0)
            if n + ahead < len(units):
                pending.append(scores(*units[n + ahead]))
            e1 = jnp.exp2(s1 - jnp.max(s1, axis=-1, keepdims=True))
            e2 = jnp.exp2(s2 - jnp.max(s2, axis=-1, keepdims=True))
            l1 = jnp.sum(e1, axis=-1, keepdims=True)
            l2 = jnp.sum(e2, axis=-1, keepdims=True)
            p = (e1 - e2 * (lam * l1 / l2)).astype(BF16)
            o = _dot(p, v_scr[idx // n_qb, :, cols]) * (1.0 / l1)
            if with_prev:
                out_proj(*prev[n])
            o = o * lax.rsqrt(jnp.mean(o * o, axis=-1, keepdims=True) + EPS) * r.subln[r.layer_row, :]
            o = o * (1.0 - lam_init)
            r.y_scr[pl.ds(row0(idx), ROW_CHUNK), Y_ATTN + hd * V_DIM:Y_ATTN + (hd + 1) * V_DIM] = o.astype(BF16)

    n_blocks = ROWS // (ROW_CHUNK * cpb)
    block(0, False)
    if n_blocks == 2:
        block(1, True)
    else:
        def loop_block(blk, carry):
            block(blk, True)
            return carry

        lax.fori_loop(1, n_blocks, loop_block, 0)
    for idx in range((n_blocks - 1) * cpb, n_blocks * cpb):
        for part in range(N_HEADS):
            out_proj(pl.ds(idx * ROW_CHUNK, ROW_CHUNK), part)


def _mixer_kernel(*refs, layer, nb, seq, has_ctx, lam_init, cache_slot, n_passthrough, n_cast):
    it = iter(refs)
    r = types.SimpleNamespace(cache_slot=cache_slot, layer_row=slice(layer, layer + 1))
    x_ref, r.mod, r.g1, r.w_in = next(it), next(it), next(it), next(it)
    if has_ctx:
        r.rope_c, r.rope_s, r.ck, r.cv = next(it), next(it), next(it), next(it)
    (r.lam, r.subln, cnt_ref, r.wpool, r.pscale, r.sgu_g, r.wsgu, r.bsgu, r.w_out) = (next(it) for _ in range(9))
    for _ in range(n_passthrough):
        next(it)
    cast_in = [next(it) for _ in range(n_cast)]
    r.out = next(it)
    if not has_ctx:
        r.kc_out, r.vc_out = next(it), next(it)
    cast_out = [next(it) for _ in range(n_cast)]
    pool_scr, r.q_scr, k_scr, v_scr, r.u_scr, r.vs_scr, r.y_scr = (next(it) for _ in range(7))
    for src, dst in zip(cast_in, cast_out):
        dst[...] = src[...].astype(BF16)
    _mixer_body(r, x_ref, cnt_ref, pool_scr, k_scr, v_scr, nb=nb, seq=seq, has_ctx=has_ctx, lam_init=lam_init,
                cpb=CHUNKS_PER_BLOCK)


def _mixer_call(x2d, mod4, layer, seq, has_ctx, lam_init, params, big, ctx=None, new_cache=None, cast_f32=()):
    n_rows = x2d.shape[0]
    nb = ROWS // seq
    n_seq = n_rows // seq
    kv_len = seq + (PAST_LEN if has_ctx else 0)

    if has_ctx:
        mod_map = lambda i: (layer, 1 + i, 0, 0)
    else:
        mod_map = lambda i: (layer, 0, 0, 0)

    operands = [x2d, mod4, params["norm1_g"], big["w_in"]]
    in_specs = [
        pl.BlockSpec((ROWS, D_MODEL), lambda i: (i, 0)),
        pl.BlockSpec((None, None, N_MOD, D_MODEL), mod_map),
        _const_spec((DEPTH, D_MODEL)),
        _const_spec((D_MODEL, IN_WIDTH)),
    ]
    if has_ctx:
        operands += list(ctx)
        in_specs += [
            _const_spec((seq, HEAD_COLS)),
            _const_spec((seq, HEAD_COLS)),
            pl.BlockSpec((None, None, PAST_LEN * N_HEADS, HEAD_COLS), lambda i: (i, layer, 0, 0)),
            pl.BlockSpec((None, None, PAST_LEN * N_HEADS, V_DIM), lambda i: (i, layer, 0, 0)),
        ]
    operands += [params["lam"], params["subln_g"], _pool_counts(seq), params["w_pool"], params["pool_scale"],
                 params["sgu_norm_g"], params["w_sgu"], params["b_sgu"], big["w_out"]]
    in_specs += [
        _layer_spec((8, 128), layer),
        _const_spec((DEPTH, V_DIM)),
        _const_spec((seq, POOL_WIDTH)),
        _layer_spec((POOL_WIDTH, POOL_WIDTH), layer),
        _const_spec((DEPTH, POOL_WIDTH)),
        _const_spec((DEPTH, SGU_WIDTH)),
        _layer_spec((2, 2 * CHUNK, CHUNK), layer),
        _layer_spec((CHUNK, SGU_WIDTH), layer),
        _const_spec((MIX_WIDTH, D_MODEL)),
    ]

    out_shape = [jax.ShapeDtypeStruct((n_rows, D_MODEL), F32)]
    out_specs = [pl.BlockSpec((ROWS, D_MODEL), lambda i: (i, 0))]
    aliases = {}
    cache_slot = 0
    if not has_ctx:
        out_shape += [jax.ShapeDtypeStruct((n_seq, DEPTH, seq * N_HEADS, HEAD_COLS), F32),
                      jax.ShapeDtypeStruct((n_seq, DEPTH, seq * N_HEADS, V_DIM), F32)]
        if new_cache is None:
            cache_slot = layer
            out_specs += [pl.BlockSpec((nb, DEPTH, seq * N_HEADS, HEAD_COLS), lambda i: (i, 0, 0, 0)),
                          pl.BlockSpec((nb, DEPTH, seq * N_HEADS, V_DIM), lambda i: (i, 0, 0, 0))]
        else:
            out_specs += [pl.BlockSpec((nb, 1, seq * N_HEADS, HEAD_COLS), lambda i: (i, layer, 0, 0)),
                          pl.BlockSpec((nb, 1, seq * N_HEADS, V_DIM), lambda i: (i, layer, 0, 0))]
            aliases = {len(operands): 1, len(operands) + 1: 2}
            operands += list(new_cache)
            in_specs += [pl.BlockSpec(memory_space=pl.ANY)] * 2
    n_steps = n_rows // ROWS
    for w in cast_f32:
        _, rows, cols = w.shape
        slab = rows // n_steps
        assert slab * n_steps == rows and slab % 16 == 0
        operands.append(w)
        in_specs.append(pl.BlockSpec((None, slab, cols), lambda i: (layer, i, 0)))
        out_shape.append(jax.ShapeDtypeStruct((rows, cols), BF16))
        out_specs.append(pl.BlockSpec((slab, cols), lambda i: (i, 0)))

    scratch = [
        pltpu.VMEM((nb, seq + 2 * POOL_HALO, POOL_WIDTH), F32),
        pltpu.VMEM((ROWS // ROW_CHUNK, 2 * ROW_CHUNK, QK_WIDTH), BF16),
        pltpu.VMEM((nb, QK_WIDTH, kv_len), BF16),
        pltpu.VMEM((nb, kv_len, ATTN_WIDTH), BF16),
        pltpu.VMEM((ROWS, SGU_WIDTH), F32),
        pltpu.VMEM((ROWS, SGU_WIDTH), BF16),
        pltpu.VMEM((ROWS, MIX_WIDTH), BF16),
    ]
    kernel = functools.partial(_mixer_kernel, layer=layer, nb=nb, seq=seq, has_ctx=has_ctx, lam_init=lam_init,
                               cache_slot=cache_slot, n_passthrough=len(aliases), n_cast=len(cast_f32))
    return pl.pallas_call(
        kernel,
        out_shape=out_shape,
        grid=(n_rows // ROWS,),
        in_specs=in_specs,
        out_specs=out_specs,
        scratch_shapes=scratch,
        input_output_aliases=aliases,
        compiler_params=pltpu.CompilerParams(
            dimension_semantics=("arbitrary",), vmem_limit_bytes=VMEM_LIMIT),
        name="mixer_ctx" if has_ctx else "mixer_prompt",
    )(*operands)


def _ffn_block(x_ref, out_ref, mod_ref, g2_ref, w1_ref, w2_ref, gf_ref, act_scr, final_norm, layer):
    shift2, scale2, gate2 = mod_ref[3:4, :], mod_ref[4:5, :], mod_ref[5:6, :]
    x = x_ref[...]
    h = _rms_mod(x, g2_ref[layer:layer + 1, :], scale2, shift2).astype(BF16)
    for lo, hi in ((0, FF_SPLIT), (FF_SPLIT, D_FF)):
        gate = _dot(h, w1_ref[:, lo:hi])
        up = _dot(h, w1_ref[:, D_FF + lo:D_FF + hi])
        act_scr[:, lo:hi] = (gate * _sigmoid(gate) * up).astype(BF16)
    y = x + gate2 * _dot(act_scr[...], w2_ref[...])
    if final_norm:
        ms = jnp.mean(y * y, axis=-1, keepdims=True)
        y = y * lax.rsqrt(ms + EPS) * gf_ref[...]
    out_ref[...] = y


def _ffn_kernel(*refs, layer, n_prompt_steps, final_norm, n_cast):
    xp_ref, xs_ref, mod_ref, g2_ref, w1_ref, w2_ref, gf_ref = refs[:7]
    cast_in = refs[7:7 + n_cast]
    yp_ref, ys_ref = refs[7 + n_cast:9 + n_cast]
    cast_out = refs[9 + n_cast:9 + 2 * n_cast]
    act_scr = refs[9 + 2 * n_cast]
    shared = (mod_ref, g2_ref, w1_ref, w2_ref, gf_ref, act_scr, final_norm, layer)
    is_prompt = pl.program_id(0) < n_prompt_steps

    @pl.when(is_prompt)
    def _():
        _ffn_block(xp_ref, yp_ref, *shared)

    @pl.when(jnp.logical_not(is_prompt))
    def _():
        for src, dst in zip(cast_in, cast_out):
            dst[...] = src[...].astype(BF16)
        _ffn_block(xs_ref, ys_ref, *shared)


def _ffn_call(xp, xs, mod4, layer, sample_seq, params, big, final_norm, next_f32=()):
    n_p, n_s = xp.shape[0] // FFN_ROWS, xs.shape[0] // FFN_ROWS
    steps_per_seq = sample_seq // FFN_ROWS

    def prompt_step(i):
        return jnp.minimum(i, n_p - 1)

    def sample_step(i):
        return jnp.maximum(i - n_p, 0)

    def mod_map(i):
        return (layer, jnp.where(i < n_p, 0, 1 + sample_step(i) // steps_per_seq), 0, 0)

    in_specs = [
        pl.BlockSpec((FFN_ROWS, D_MODEL), lambda i: (prompt_step(i), 0)),
        pl.BlockSpec((FFN_ROWS, D_MODEL), lambda i: (sample_step(i), 0)),
        pl.BlockSpec((None, None, N_MOD, D_MODEL), mod_map),
        _const_spec((DEPTH, D_MODEL)),
        _const_spec((D_MODEL, 2 * D_FF)),
        _const_spec((D_FF, D_MODEL)),
        _const_spec((1, D_MODEL)),
    ]
    out_shape = [jax.ShapeDtypeStruct(xp.shape, F32), jax.ShapeDtypeStruct(xs.shape, F32)]
    out_specs = [pl.BlockSpec((FFN_ROWS, D_MODEL), lambda i: (prompt_step(i), 0)),
                 pl.BlockSpec((FFN_ROWS, D_MODEL), lambda i: (sample_step(i), 0))]
    for w in next_f32:
        _, rows, cols = w.shape
        slab = rows // n_s
        assert slab * n_s == rows and slab % 16 == 0
        in_specs.append(pl.BlockSpec((None, slab, cols), lambda i: (layer + 1, sample_step(i), 0)))
        out_shape.append(jax.ShapeDtypeStruct((rows, cols), BF16))
        out_specs.append(pl.BlockSpec((slab, cols), lambda i: (sample_step(i), 0)))

    kernel = functools.partial(_ffn_kernel, layer=layer, n_prompt_steps=n_p, final_norm=final_norm, n_cast=len(next_f32))
    return pl.pallas_call(
        kernel,
        out_shape=out_shape,
        grid=(n_p + n_s,),
        in_specs=in_specs,
        out_specs=out_specs,
        scratch_shapes=[pltpu.VMEM((FFN_ROWS, D_FF), BF16)],
        compiler_params=pltpu.CompilerParams(
            dimension_semantics=("arbitrary",), vmem_limit_bytes=VMEM_LIMIT),
        name="ffn",
    )(xp, xs, mod4, params["norm2_g"], big["w_ffn_in"], big["w_ffn_out"], params["final_g"], *next_f32)


def _rope_tables(seq):
    n_rows = seq // GRID_W
    rows = np.repeat(np.arange(n_rows), GRID_W).astype(np.float32)
    cols = np.tile(np.arange(GRID_W), n_rows).astype(np.float32)
    inv = 1.0 / (ROPE_BASE ** (np.arange(0, ROPE_AXIS_DIM, 2, dtype=np.float32) / ROPE_AXIS_DIM))
    ar, ac = rows[:, None] * inv[None], cols[:, None] * inv[None]
    cos_parts, sin_parts = [], []
    for ang in (ar, ac):
        cos_parts += [np.cos(ang), np.cos(ang)]
        sin_parts += [-np.sin(ang), np.sin(ang)]
    cos64 = np.concatenate(cos_parts, axis=1)
    sin64 = np.concatenate(sin_parts, axis=1)
    cos = np.concatenate([cos64, cos64], axis=1).astype(np.float32)
    sin = np.concatenate([sin64, sin64], axis=1).astype(np.float32)
    return jnp.asarray(cos), jnp.asarray(sin)


def _pool_counts(seq):
    t = np.arange(seq)
    cols = []
    for w in POOL_WINDOWS:
        lo = np.clip(t - w // 2, 0, seq)
        hi = np.clip(t + w - w // 2, 0, seq)
        cols.append(np.repeat((hi - lo).astype(np.float32)[:, None], POOL_GROUP, axis=1))
    return jnp.asarray(np.concatenate(cols, axis=1))


def _small_params(norm1_g, w_pool, pool_scale, lam_q1, lam_k1, lam_q2, lam_k2, subln_g,
                  sgu_norm_g, w_sgu, b_sgu, norm2_g, final_g):
    n_groups = len(POOL_WINDOWS)
    eye = jnp.eye(n_groups, dtype=F32)
    w_pool_bd = (eye[None, :, None, :, None] * w_pool[:, :, :, None, :]).reshape(DEPTH, POOL_WIDTH, POOL_WIDTH)
    lam = jnp.pad(jnp.stack([lam_q1, lam_k1, lam_q2, lam_k2], axis=1), ((0, 0), (0, 4), (0, 128 - QK_DIM)))
    return {
        "norm1_g": norm1_g,
        "lam": lam,
        "subln_g": subln_g,
        "w_pool": w_pool_bd.astype(BF16),
        "pool_scale": pool_scale,
        "sgu_norm_g": sgu_norm_g,
        "w_sgu": w_sgu.reshape(DEPTH, 2, 2 * CHUNK, CHUNK).astype(BF16),
        "b_sgu": jnp.repeat(jnp.swapaxes(b_sgu, 1, 2), SGU_WIDTH // 4, axis=2),
        "norm2_g": norm2_g,
        "final_g": final_g[None, :],
    }


BIG_WEIGHTS = ("w_in", "w_out", "w_ffn_in", "w_ffn_out")


def kernel(x_prompt, x_sample, cache_k, cache_v, c, c_ctx, norm1_g, w_ada, b_ada, w_in, w_pool, pool_scale, lam_q1, lam_k1, lam_q2, lam_k2, subln_g, sgu_norm_g, w_sgu, b_sgu, w_out, norm2_g, w_ffn_in, w_ffn_out, final_g):
    batch, seq, _ = x_prompt.shape
    dec_batch, dec_seq, _ = x_sample.shape
    assert ROWS % seq == 0 and dec_seq == ROWS and 1 + dec_batch <= COND_ROWS

    cond = jnp.concatenate([c_ctx[None, :], c, jnp.zeros((COND_ROWS - 1 - dec_batch, D_MODEL), F32)], axis=0)
    mod4 = _ada_modulation(cond, w_ada, b_ada).reshape(DEPTH, COND_ROWS, N_MOD, D_MODEL)

    ctx = _rope_tables(dec_seq) + (cache_k.reshape(dec_batch, DEPTH, PAST_LEN * N_HEADS, HEAD_COLS),
                                   cache_v.reshape(dec_batch, DEPTH, PAST_LEN * N_HEADS, V_DIM))
    params = _small_params(norm1_g, w_pool, pool_scale, lam_q1, lam_k1, lam_q2, lam_k2, subln_g,
                           sgu_norm_g, w_sgu, b_sgu, norm2_g, final_g)

    big_f32 = (w_in, w_out, w_ffn_in, w_ffn_out)
    big = {"w_in": w_in[0].astype(BF16), "w_out": w_out[0].astype(BF16)}

    xp = x_prompt.reshape(batch * seq, D_MODEL)
    xs = x_sample.reshape(dec_batch * dec_seq, D_MODEL)
    new_cache = None
    for l in range(DEPTH):
        lam_init = 0.8 - 0.6 * math.exp(-0.3 * l)
        last = l == DEPTH - 1

        xp, *new_cache = _mixer_call(xp, mod4, l, seq, False, lam_init, params, big, new_cache=new_cache)
        if l == 0:
            xs, big["w_ffn_in"], big["w_ffn_out"] = _mixer_call(
                xs, mod4, l, dec_seq, True, lam_init, params, big, ctx=ctx, cast_f32=(w_ffn_in, w_ffn_out))
        else:
            xs, = _mixer_call(xs, mod4, l, dec_seq, True, lam_init, params, big, ctx=ctx)
        xp, xs, *next_big = _ffn_call(xp, xs, mod4, l, dec_seq, params, big, last,
                                      next_f32=() if last else big_f32)
        big = dict(zip(BIG_WEIGHTS, next_big))

    y_prompt = xp.reshape(batch, seq, D_MODEL)
    y_sample = xs.reshape(dec_batch, dec_seq, D_MODEL)
    new_cache_k = new_cache[0].reshape(batch, DEPTH, seq, N_HEADS, 2 * QK_DIM)
    new_cache_v = new_cache[1].reshape(batch, DEPTH, seq, N_HEADS, V_DIM)
    return (y_prompt, y_sample, new_cache_k, new_cache_v)
```

```python
import functools
import math
import types

import numpy as np
import jax
import jax.numpy as jnp
from jax import lax
from jax.experimental import pallas as pl
from jax.experimental.pallas import tpu as pltpu

D_MODEL = 1024
DEPTH = 2
GRID_W = 64
POOL_WINDOWS = (2, 4, 8, 16)
POOL_WIDTH = 256
POOL_GROUP = 64
POOL_HALO = 8
ATTN_WIDTH = 512
N_HEADS = 4
V_DIM = 128
QK_DIM = 64
HEAD_COLS = 2 * QK_DIM
ROPE_BASE = 10000.0
ROPE_AXIS_DIM = 32
ROPE_HALF = ROPE_AXIS_DIM // 2
CHUNK = 128
SGU_WIDTH = 256
QK_WIDTH = 512
IN_WIDTH = 2304
MIX_WIDTH = 1024
D_FF = 2816
N_MOD = 6
EPS = 1e-6
PAST_LEN = 256

C_POOL, C_Q, C_K, C_V, C_UV = 0, 256, 768, 1280, 1792
Y_POOL, Y_ATTN, Y_SGU = 0, 256, 768

ROWS = 1024
ROW_CHUNK = 256
ATTN_LOOKAHEAD = 2
CHUNKS_PER_BLOCK = 2
FFN_ROWS = 512
MXU_COLS = 256
FF_SPLIT = (D_FF // MXU_COLS + 1) // 2 * MXU_COLS
COND_ROWS = 16
ADA_COLS = 3072
VMEM_LIMIT = 58 * 1024 * 1024

BF16 = jnp.bfloat16
F32 = jnp.float32


def _dot(a, b):
    return jnp.dot(a, b, preferred_element_type=F32)


def _sigmoid(x):
    return 1.0 / (1.0 + jnp.exp(-x))


def _rms_mod(x, g, scale, shift):
    ms = jnp.mean(x * x, axis=-1, keepdims=True)
    return x * lax.rsqrt(ms + EPS) * (g * (1.0 + scale)) + shift


def _gelu_tanh(x):
    c = math.sqrt(2.0 / math.pi)
    return x * (0.5 * (1.0 + jnp.tanh(c * (x + 0.044715 * (x * x * x)))))


def _const_spec(shape):
    zeros = (0,) * len(shape)
    return pl.BlockSpec(shape, lambda *_: zeros, pipeline_mode=pl.Buffered(1))


def _layer_spec(shape, layer):
    index = (layer,) + (0,) * len(shape)
    return pl.BlockSpec((None,) + tuple(shape), lambda *_: index, pipeline_mode=pl.Buffered(1))


def _ada_kernel(cond_ref, w_ref, b_ref, out_ref):
    cond = cond_ref[...]
    s = (cond * _sigmoid(cond)).astype(BF16)
    out_ref[...] = _dot(s, w_ref[...].astype(BF16)) + b_ref[...]


def _ada_modulation(cond, w_ada, b_ada):
    n_cols = N_MOD * D_MODEL
    return pl.pallas_call(
        _ada_kernel,
        out_shape=jax.ShapeDtypeStruct((DEPTH, COND_ROWS, n_cols), F32),
        grid=(DEPTH, n_cols // ADA_COLS),
        in_specs=[
            pl.BlockSpec((COND_ROWS, D_MODEL), lambda l, j: (0, 0)),
            pl.BlockSpec((None, D_MODEL, ADA_COLS), lambda l, j: (l, 0, j)),
            pl.BlockSpec((None, 1, ADA_COLS), lambda l, j: (l, 0, j)),
        ],
        out_specs=pl.BlockSpec((None, COND_ROWS, ADA_COLS), lambda l, j: (l, 0, j)),
        compiler_params=pltpu.CompilerParams(
            dimension_semantics=("arbitrary", "arbitrary"), vmem_limit_bytes=VMEM_LIMIT),
        name="ada_modulation",
    )(cond, w_ada, b_ada.reshape(DEPTH, 1, n_cols))


def _mixer_body(r, x_ref, cnt_ref, pool_scr, k_scr, v_scr, *, nb, seq, has_ctx, lam_init, cpb):
    kv_base = PAST_LEN if has_ctx else 0
    shift1, scale1, gate1 = r.mod[0:1, :], r.mod[1:2, :], r.mod[2:3, :]

    lane_head = lax.broadcasted_iota(jnp.int32, (ROW_CHUNK, HEAD_COLS), 1)
    first_half = lane_head < QK_DIM

    for b in range(nb):
        pool_scr[b, 0:POOL_HALO, :] = jnp.zeros((POOL_HALO, POOL_WIDTH), F32)
        pool_scr[b, POOL_HALO + seq:POOL_HALO + seq + POOL_HALO, :] = jnp.zeros((POOL_HALO, POOL_WIDTH), F32)
    if not has_ctx:
        for slot in range(r.kc_out.shape[1]):
            if slot != r.cache_slot:
                r.kc_out[:, slot] = jnp.zeros((nb,) + r.kc_out.shape[2:], F32)
                r.vc_out[:, slot] = jnp.zeros((nb,) + r.vc_out.shape[2:], F32)
    if has_ctx:
        for hd in range(N_HEADS):
            cols = slice(hd * HEAD_COLS, (hd + 1) * HEAD_COLS)
            k_scr[0, cols, 0:PAST_LEN] = r.ck[pl.ds(hd, PAST_LEN, stride=N_HEADS), :].T.astype(BF16)
            v_scr[0, 0:PAST_LEN, cols] = r.cv[pl.ds(hd, PAST_LEN, stride=N_HEADS), :].astype(BF16)
        lane_q = lax.broadcasted_iota(jnp.int32, (ROW_CHUNK, QK_WIDTH), 1)
        rope_low = (lane_q % ROPE_AXIS_DIM) < ROPE_HALF

        def rope(t, pos0):
            partner = jnp.where(rope_low,
                                pltpu.roll(t, QK_WIDTH - ROPE_HALF, 1),
                                pltpu.roll(t, ROPE_HALF, 1))
            cos = r.rope_c[pos0:pos0 + ROW_CHUNK, :]
            sin = r.rope_s[pos0:pos0 + ROW_CHUNK, :]
            cos = jnp.concatenate([cos] * N_HEADS, axis=1)
            sin = jnp.concatenate([sin] * N_HEADS, axis=1)
            return t * cos + partner * sin

    for c in range(ROWS // ROW_CHUNK):
        r0 = c * ROW_CHUNK
        b, pos0 = r0 // seq, r0 % seq
        rows = slice(r0, r0 + ROW_CHUNK)
        h = _rms_mod(x_ref[rows, :], r.g1[r.layer_row, :], scale1, shift1).astype(BF16)

        pool_scr[b, POOL_HALO + pos0:POOL_HALO + pos0 + ROW_CHUNK, :] = _dot(h, r.w_in[:, C_POOL:C_Q])

        q = _dot(h, r.w_in[:, C_Q:C_K])
        k = _dot(h, r.w_in[:, C_K:C_V])
        if has_ctx:
            q = rope(q, pos0)
            k = rope(k, pos0)
        else:
            for hd in range(N_HEADS):
                r.kc_out[b, r.cache_slot, pl.ds(N_HEADS * pos0 + hd, ROW_CHUNK, stride=N_HEADS), :] = (
                    k[:, hd * HEAD_COLS:(hd + 1) * HEAD_COLS])
        q = q * (QK_DIM ** -0.5 * math.log2(math.e))
        for hd in range(N_HEADS):
            cols = slice(hd * HEAD_COLS, (hd + 1) * HEAD_COLS)
            qh = q[:, cols]
            r.q_scr[c, 0:ROW_CHUNK, cols] = jnp.where(first_half, qh, 0.0).astype(BF16)
            r.q_scr[c, ROW_CHUNK:2 * ROW_CHUNK, cols] = jnp.where(first_half, 0.0, qh).astype(BF16)
        k_scr[b, :, kv_base + pos0:kv_base + pos0 + ROW_CHUNK] = k.T.astype(BF16)

        v = _dot(h, r.w_in[:, C_V:C_UV])
        if not has_ctx:
            for hd in range(N_HEADS):
                r.vc_out[b, r.cache_slot, pl.ds(N_HEADS * pos0 + hd, ROW_CHUNK, stride=N_HEADS), :] = (
                    v[:, hd * V_DIM:(hd + 1) * V_DIM])
        v_scr[b, kv_base + pos0:kv_base + pos0 + ROW_CHUNK, :] = v.astype(BF16)

        uv = _gelu_tanh(_dot(h, r.w_in[:, C_UV:IN_WIDTH]))
        r.u_scr[rows, :] = uv[:, :SGU_WIDTH]
        vv = uv[:, SGU_WIDTH:]
        mu = jnp.mean(vv, axis=-1, keepdims=True)
        vc = vv - mu
        vn = vc * lax.rsqrt(jnp.mean(vc * vc, axis=-1, keepdims=True) + EPS) * r.sgu_g[r.layer_row, :]
        r.vs_scr[rows, :] = vn.astype(BF16)

    lane_pool = lax.broadcasted_iota(jnp.int32, (ROW_CHUNK, 2 * POOL_GROUP), 1)
    narrow = lane_pool < POOL_GROUP
    for c in range(ROWS // ROW_CHUNK):
        r0 = c * ROW_CHUNK
        b, pos0 = r0 // seq, r0 % seq
        base = POOL_HALO + pos0
        pooled = []
        for j, (w_small, w_big) in enumerate(((POOL_WINDOWS[0], POOL_WINDOWS[1]),
                                              (POOL_WINDOWS[2], POOL_WINDOWS[3]))):
            cols = slice(j * 2 * POOL_GROUP, (j + 1) * 2 * POOL_GROUP)
            s_small = jnp.zeros((ROW_CHUNK, 2 * POOL_GROUP), F32)
            s_rest = jnp.zeros((ROW_CHUNK, 2 * POOL_GROUP), F32)
            for d in range(-(w_big // 2), w_big - w_big // 2):
                t = pool_scr[b, base + d:base + d + ROW_CHUNK, cols]
                if -(w_small // 2) <= d < w_small - w_small // 2:
                    s_small = s_small + t
                else:
                    s_rest = s_rest + t
            win_sum = jnp.where(narrow, s_small, s_small + s_rest)
            centre = pool_scr[b, base:base + ROW_CHUNK, cols]
            pooled.append(win_sum / cnt_ref[pos0:pos0 + ROW_CHUNK, cols] - centre)
        pooled = jnp.concatenate(pooled, axis=1).astype(BF16)
        y_a = _dot(pooled, r.wpool[...]) * r.pscale[r.layer_row, :]
        r.y_scr[r0:r0 + ROW_CHUNK, Y_POOL:Y_ATTN] = y_a.astype(BF16)

    lq1, lk1, lq2, lk2 = r.lam[0:1, :], r.lam[1:2, :], r.lam[2:3, :], r.lam[3:4, :]
    lam = (jnp.exp(jnp.sum(lq1 * lk1, axis=-1, keepdims=True))
           - jnp.exp(jnp.sum(lq2 * lk2, axis=-1, keepdims=True)) + lam_init)
    n_qb = seq // ROW_CHUNK

    lane_sgu = lax.broadcasted_iota(jnp.int32, (CHUNK, 2 * POOL_GROUP), 1)
    sgu_first = lane_sgu < (SGU_WIDTH // 4)
    out_cols = MIX_WIDTH // N_HEADS

    def out_proj(rows, part):
        cols = slice(part * out_cols, (part + 1) * out_cols)
        y = _dot(r.y_scr[rows, :], r.w_out[:, cols])
        r.out[rows, cols] = x_ref[rows, cols] + gate1[:, cols] * y

    def block(blk, with_prev):
        static = isinstance(blk, int)

        def row0(chunk):
            return chunk * ROW_CHUNK if static else pl.multiple_of(chunk * ROW_CHUNK, ROW_CHUNK)

        chunks = [blk * cpb + j for j in range(cpb)]

        for idx in chunks:
            for half in range(ROW_CHUNK // CHUNK):
                sub = pl.ds(row0(idx) + half * CHUNK, CHUNK)
                for j in range(2):
                    cols = slice(j * 128, (j + 1) * 128)
                    t = _dot(r.wsgu[j], r.vs_scr[sub, cols])
                    mixed = jnp.where(sgu_first, t[:CHUNK, :], t[CHUNK:, :]) + r.bsgu[:, cols]
                    y_c = r.u_scr[sub, cols] * mixed
                    r.y_scr[sub, Y_SGU + j * 128:Y_SGU + (j + 1) * 128] = y_c.astype(BF16)

        units = [(idx, hd) for idx in chunks for hd in range(N_HEADS)]
        prev = [(pl.ds(row0(idx - cpb), ROW_CHUNK), part) for idx in chunks for part in range(N_HEADS)]

        def scores(idx, hd):
            cols = slice(hd * HEAD_COLS, (hd + 1) * HEAD_COLS)
            s = _dot(r.q_scr[idx, :, cols], k_scr[idx // n_qb, cols, :])
            return s[:ROW_CHUNK], s[ROW_CHUNK:]

        pending = [scores(*u) for u in units[:ATTN_LOOKAHEAD]]
        for n, (idx, hd) in enumerate(units):
            cols = slice(hd * HEAD_COLS, (hd + 1) * HEAD_COLS)
            s1, s2 = pending.pop(0)
            if n + ATTN_LOOKAHEAD < len(units):
                pending.append(scores(*units[n + ATTN_LOOKAHEAD]))
            e1 = jnp.exp2(s1 - jnp.max(s1, axis=-1, keepdims=True))
            e2 = jnp.exp2(s2 - jnp.max(s2, axis=-1, keepdims=True))
            l1 = jnp.sum(e1, axis=-1, keepdims=True)
            l2 = jnp.sum(e2, axis=-1, keepdims=True)
            p = (e1 - e2 * (lam * l1 / l2)).astype(BF16)
            o = _dot(p, v_scr[idx // n_qb, :, cols]) * (1.0 / l1)
            if with_prev:
                out_proj(*prev[n])
            o = o * lax.rsqrt(jnp.mean(o * o, axis=-1, keepdims=True) + EPS) * r.subln[r.layer_row, :]
            o = o * (1.0 - lam_init)
            r.y_scr[pl.ds(row0(idx), ROW_CHUNK), Y_ATTN + hd * V_DIM:Y_ATTN + (hd + 1) * V_DIM] = o.astype(BF16)

    n_blocks = ROWS // (ROW_CHUNK * cpb)
    block(0, False)
    if n_blocks == 2:
        block(1, True)
    else:
        def loop_block(blk, carry):
            block(blk, True)
            return carry

        lax.fori_loop(1, n_blocks, loop_block, 0)
    for idx in range((n_blocks - 1) * cpb, n_blocks * cpb):
        for part in range(N_HEADS):
            out_proj(pl.ds(idx * ROW_CHUNK, ROW_CHUNK), part)


def _mixer_kernel(*refs, layer, nb, seq, has_ctx, lam_init, cache_slot, n_passthrough, n_cast):
    it = iter(refs)
    r = types.SimpleNamespace(cache_slot=cache_slot, layer_row=slice(layer, layer + 1))
    x_ref, r.mod, r.g1, r.w_in = next(it), next(it), next(it), next(it)
    if has_ctx:
        r.rope_c, r.rope_s, r.ck, r.cv = next(it), next(it), next(it), next(it)
    (r.lam, r.subln, cnt_ref, r.wpool, r.pscale, r.sgu_g, r.wsgu, r.bsgu, r.w_out) = (next(it) for _ in range(9))
    for _ in range(n_passthrough):
        next(it)
    cast_in = [next(it) for _ in range(n_cast)]
    r.out = next(it)
    if not has_ctx:
        r.kc_out, r.vc_out = next(it), next(it)
    cast_out = [next(it) for _ in range(n_cast)]
    pool_scr, r.q_scr, k_scr, v_scr, r.u_scr, r.vs_scr, r.y_scr = (next(it) for _ in range(7))
    for src, dst in zip(cast_in, cast_out):
        dst[...] = src[...].astype(BF16)
    _mixer_body(r, x_ref, cnt_ref, pool_scr, k_scr, v_scr, nb=nb, seq=seq, has_ctx=has_ctx, lam_init=lam_init,
                cpb=CHUNKS_PER_BLOCK if has_ctx else ROWS // ROW_CHUNK)


def _mixer_call(x2d, mod4, layer, seq, has_ctx, lam_init, params, big, ctx=None, new_cache=None, cast_f32=()):
    n_rows = x2d.shape[0]
    nb = ROWS // seq
    n_seq = n_rows // seq
    kv_len = seq + (PAST_LEN if has_ctx else 0)

    if has_ctx:
        mod_map = lambda i: (layer, 1 + i, 0, 0)
    else:
        mod_map = lambda i: (layer, 0, 0, 0)

    operands = [x2d, mod4, params["norm1_g"], big["w_in"]]
    in_specs = [
        pl.BlockSpec((ROWS, D_MODEL), lambda i: (i, 0)),
        pl.BlockSpec((None, None, N_MOD, D_MODEL), mod_map),
        _const_spec((DEPTH, D_MODEL)),
        _const_spec((D_MODEL, IN_WIDTH)),
    ]
    if has_ctx:
        operands += list(ctx)
        in_specs += [
            _const_spec((seq, HEAD_COLS)),
            _const_spec((seq, HEAD_COLS)),
            pl.BlockSpec((None, None, PAST_LEN * N_HEADS, HEAD_COLS), lambda i: (i, layer, 0, 0)),
            pl.BlockSpec((None, None, PAST_LEN * N_HEADS, V_DIM), lambda i: (i, layer, 0, 0)),
        ]
    operands += [params["lam"], params["subln_g"], _pool_counts(seq), params["w_pool"], params["pool_scale"],
                 params["sgu_norm_g"], params["w_sgu"], params["b_sgu"], big["w_out"]]
    in_specs += [
        _layer_spec((8, 128), layer),
        _const_spec((DEPTH, V_DIM)),
        _const_spec((seq, POOL_WIDTH)),
        _layer_spec((POOL_WIDTH, POOL_WIDTH), layer),
        _const_spec((DEPTH, POOL_WIDTH)),
        _const_spec((DEPTH, SGU_WIDTH)),
        _layer_spec((2, 2 * CHUNK, CHUNK), layer),
        _layer_spec((CHUNK, SGU_WIDTH), layer),
        _const_spec((MIX_WIDTH, D_MODEL)),
    ]

    out_shape = [jax.ShapeDtypeStruct((n_rows, D_MODEL), F32)]
    out_specs = [pl.BlockSpec((ROWS, D_MODEL), lambda i: (i, 0))]
    aliases = {}
    cache_slot = 0
    if not has_ctx:
        out_shape += [jax.ShapeDtypeStruct((n_seq, DEPTH, seq * N_HEADS, HEAD_COLS), F32),
                      jax.ShapeDtypeStruct((n_seq, DEPTH, seq * N_HEADS, V_DIM), F32)]
        if new_cache is None:
            cache_slot = layer
            out_specs += [pl.BlockSpec((nb, DEPTH, seq * N_HEADS, HEAD_COLS), lambda i: (i, 0, 0, 0)),
                          pl.BlockSpec((nb, DEPTH, seq * N_HEADS, V_DIM), lambda i: (i, 0, 0, 0))]
        else:
            out_specs += [pl.BlockSpec((nb, 1, seq * N_HEADS, HEAD_COLS), lambda i: (i, layer, 0, 0)),
                          pl.BlockSpec((nb, 1, seq * N_HEADS, V_DIM), lambda i: (i, layer, 0, 0))]
            aliases = {len(operands): 1, len(operands) + 1: 2}
            operands += list(new_cache)
            in_specs += [pl.BlockSpec(memory_space=pl.ANY)] * 2
    n_steps = n_rows // ROWS
    for w in cast_f32:
        _, rows, cols = w.shape
        slab = rows // n_steps
        assert slab * n_steps == rows and slab % 16 == 0
        operands.append(w)
        in_specs.append(pl.BlockSpec((None, slab, cols), lambda i: (layer, i, 0)))
        out_shape.append(jax.ShapeDtypeStruct((rows, cols), BF16))
        out_specs.append(pl.BlockSpec((slab, cols), lambda i: (i, 0)))

    scratch = [
        pltpu.VMEM((nb, seq + 2 * POOL_HALO, POOL_WIDTH), F32),
        pltpu.VMEM((ROWS // ROW_CHUNK, 2 * ROW_CHUNK, QK_WIDTH), BF16),
        pltpu.VMEM((nb, QK_WIDTH, kv_len), BF16),
        pltpu.VMEM((nb, kv_len, ATTN_WIDTH), BF16),
        pltpu.VMEM((ROWS, SGU_WIDTH), F32),
        pltpu.VMEM((ROWS, SGU_WIDTH), BF16),
        pltpu.VMEM((ROWS, MIX_WIDTH), BF16),
    ]
    kernel = functools.partial(_mixer_kernel, layer=layer, nb=nb, seq=seq, has_ctx=has_ctx, lam_init=lam_init,
                               cache_slot=cache_slot, n_passthrough=len(aliases), n_cast=len(cast_f32))
    return pl.pallas_call(
        kernel,
        out_shape=out_shape,
        grid=(n_rows // ROWS,),
        in_specs=in_specs,
        out_specs=out_specs,
        scratch_shapes=scratch,
        input_output_aliases=aliases,
        compiler_params=pltpu.CompilerParams(
            dimension_semantics=("arbitrary",), vmem_limit_bytes=VMEM_LIMIT),
        name="mixer_ctx" if has_ctx else "mixer_prompt",
    )(*operands)


def _ffn_block(x_ref, out_ref, mod_ref, g2_ref, w1_ref, w2_ref, gf_ref, act_scr, final_norm, layer):
    shift2, scale2, gate2 = mod_ref[3:4, :], mod_ref[4:5, :], mod_ref[5:6, :]
    x = x_ref[...]
    h = _rms_mod(x, g2_ref[layer:layer + 1, :], scale2, shift2).astype(BF16)
    for lo, hi in ((0, FF_SPLIT), (FF_SPLIT, D_FF)):
        gate = _dot(h, w1_ref[:, lo:hi])
        up = _dot(h, w1_ref[:, D_FF + lo:D_FF + hi])
        act_scr[:, lo:hi] = (gate * _sigmoid(gate) * up).astype(BF16)
    y = x + gate2 * _dot(act_scr[...], w2_ref[...])
    if final_norm:
        ms = jnp.mean(y * y, axis=-1, keepdims=True)
        y = y * lax.rsqrt(ms + EPS) * gf_ref[...]
    out_ref[...] = y


def _ffn_kernel(*refs, layer, n_prompt_steps, final_norm, n_cast):
    xp_ref, xs_ref, mod_ref, g2_ref, w1_ref, w2_ref, gf_ref = refs[:7]
    cast_in = refs[7:7 + n_cast]
    yp_ref, ys_ref = refs[7 + n_cast:9 + n_cast]
    cast_out = refs[9 + n_cast:9 + 2 * n_cast]
    act_scr = refs[9 + 2 * n_cast]
    shared = (mod_ref, g2_ref, w1_ref, w2_ref, gf_ref, act_scr, final_norm, layer)
    is_prompt = pl.program_id(0) < n_prompt_steps

    @pl.when(is_prompt)
    def _():
        _ffn_block(xp_ref, yp_ref, *shared)

    @pl.when(jnp.logical_not(is_prompt))
    def _():
        for src, dst in zip(cast_in, cast_out):
            dst[...] = src[...].astype(BF16)
        _ffn_block(xs_ref, ys_ref, *shared)


def _ffn_call(xp, xs, mod4, layer, sample_seq, params, big, final_norm, next_f32=()):
    n_p, n_s = xp.shape[0] // FFN_ROWS, xs.shape[0] // FFN_ROWS
    steps_per_seq = sample_seq // FFN_ROWS

    def prompt_step(i):
        return jnp.minimum(i, n_p - 1)

    def sample_step(i):
        return jnp.maximum(i - n_p, 0)

    def mod_map(i):
        return (layer, jnp.where(i < n_p, 0, 1 + sample_step(i) // steps_per_seq), 0, 0)

    in_specs = [
        pl.BlockSpec((FFN_ROWS, D_MODEL), lambda i: (prompt_step(i), 0)),
        pl.BlockSpec((FFN_ROWS, D_MODEL), lambda i: (sample_step(i), 0)),
        pl.BlockSpec((None, None, N_MOD, D_MODEL), mod_map),
        _const_spec((DEPTH, D_MODEL)),
        _const_spec((D_MODEL, 2 * D_FF)),
        _const_spec((D_FF, D_MODEL)),
        _const_spec((1, D_MODEL)),
    ]
    out_shape = [jax.ShapeDtypeStruct(xp.shape, F32), jax.ShapeDtypeStruct(xs.shape, F32)]
    out_specs = [pl.BlockSpec((FFN_ROWS, D_MODEL), lambda i: (prompt_step(i), 0)),
                 pl.BlockSpec((FFN_ROWS, D_MODEL), lambda i: (sample_step(i), 0))]
    for w in next_f32:
        _, rows, cols = w.shape
        slab = rows // n_s
        assert slab * n_s == rows and slab % 16 == 0
        in_specs.append(pl.BlockSpec((None, slab, cols), lambda i: (layer + 1, sample_step(i), 0)))
        out_shape.append(jax.ShapeDtypeStruct((rows, cols), BF16))
        out_specs.append(pl.BlockSpec((slab, cols), lambda i: (sample_step(i), 0)))

    kernel = functools.partial(_ffn_kernel, layer=layer, n_prompt_steps=n_p, final_norm=final_norm, n_cast=len(next_f32))
    return pl.pallas_call(
        kernel,
        out_shape=out_shape,
        grid=(n_p + n_s,),
        in_specs=in_specs,
        out_specs=out_specs,
        scratch_shapes=[pltpu.VMEM((FFN_ROWS, D_FF), BF16)],
        compiler_params=pltpu.CompilerParams(
            dimension_semantics=("arbitrary",), vmem_limit_bytes=VMEM_LIMIT),
        name="ffn",
    )(xp, xs, mod4, params["norm2_g"], big["w_ffn_in"], big["w_ffn_out"], params["final_g"], *next_f32)


def _rope_tables(seq):
    n_rows = seq // GRID_W
    rows = np.repeat(np.arange(n_rows), GRID_W).astype(np.float32)
    cols = np.tile(np.arange(GRID_W), n_rows).astype(np.float32)
    inv = 1.0 / (ROPE_BASE ** (np.arange(0, ROPE_AXIS_DIM, 2, dtype=np.float32) / ROPE_AXIS_DIM))
    ar, ac = rows[:, None] * inv[None], cols[:, None] * inv[None]
    cos_parts, sin_parts = [], []
    for ang in (ar, ac):
        cos_parts += [np.cos(ang), np.cos(ang)]
        sin_parts += [-np.sin(ang), np.sin(ang)]
    cos64 = np.concatenate(cos_parts, axis=1)
    sin64 = np.concatenate(sin_parts, axis=1)
    cos = np.concatenate([cos64, cos64], axis=1).astype(np.float32)
    sin = np.concatenate([sin64, sin64], axis=1).astype(np.float32)
    return jnp.asarray(cos), jnp.asarray(sin)


def _pool_counts(seq):
    t = np.arange(seq)
    cols = []
    for w in POOL_WINDOWS:
        lo = np.clip(t - w // 2, 0, seq)
        hi = np.clip(t + w - w // 2, 0, seq)
        cols.append(np.repeat((hi - lo).astype(np.float32)[:, None], POOL_GROUP, axis=1))
    return jnp.asarray(np.concatenate(cols, axis=1))


def _small_params(norm1_g, w_pool, pool_scale, lam_q1, lam_k1, lam_q2, lam_k2, subln_g,
                  sgu_norm_g, w_sgu, b_sgu, norm2_g, final_g):
    n_groups = len(POOL_WINDOWS)
    eye = jnp.eye(n_groups, dtype=F32)
    w_pool_bd = (eye[None, :, None, :, None] * w_pool[:, :, :, None, :]).reshape(DEPTH, POOL_WIDTH, POOL_WIDTH)
    lam = jnp.pad(jnp.stack([lam_q1, lam_k1, lam_q2, lam_k2], axis=1), ((0, 0), (0, 4), (0, 128 - QK_DIM)))
    return {
        "norm1_g": norm1_g,
        "lam": lam,
        "subln_g": subln_g,
        "w_pool": w_pool_bd.astype(BF16),
        "pool_scale": pool_scale,
        "sgu_norm_g": sgu_norm_g,
        "w_sgu": w_sgu.reshape(DEPTH, 2, 2 * CHUNK, CHUNK).astype(BF16),
        "b_sgu": jnp.repeat(jnp.swapaxes(b_sgu, 1, 2), SGU_WIDTH // 4, axis=2),
        "norm2_g": norm2_g,
        "final_g": final_g[None, :],
    }


BIG_WEIGHTS = ("w_in", "w_out", "w_ffn_in", "w_ffn_out")


def kernel(x_prompt, x_sample, cache_k, cache_v, c, c_ctx, norm1_g, w_ada, b_ada, w_in, w_pool, pool_scale, lam_q1, lam_k1, lam_q2, lam_k2, subln_g, sgu_norm_g, w_sgu, b_sgu, w_out, norm2_g, w_ffn_in, w_ffn_out, final_g):
    batch, seq, _ = x_prompt.shape
    dec_batch, dec_seq, _ = x_sample.shape
    assert ROWS % seq == 0 and dec_seq == ROWS and 1 + dec_batch <= COND_ROWS

    cond = jnp.concatenate([c_ctx[None, :], c, jnp.zeros((COND_ROWS - 1 - dec_batch, D_MODEL), F32)], axis=0)
    mod4 = _ada_modulation(cond, w_ada, b_ada).reshape(DEPTH, COND_ROWS, N_MOD, D_MODEL)

    ctx = _rope_tables(dec_seq) + (cache_k.reshape(dec_batch, DEPTH, PAST_LEN * N_HEADS, HEAD_COLS),
                                   cache_v.reshape(dec_batch, DEPTH, PAST_LEN * N_HEADS, V_DIM))
    params = _small_params(norm1_g, w_pool, pool_scale, lam_q1, lam_k1, lam_q2, lam_k2, subln_g,
                           sgu_norm_g, w_sgu, b_sgu, norm2_g, final_g)

    big_f32 = (w_in, w_out, w_ffn_in, w_ffn_out)
    big = {"w_in": w_in[0].astype(BF16), "w_out": w_out[0].astype(BF16)}

    xp = x_prompt.reshape(batch * seq, D_MODEL)
    xs = x_sample.reshape(dec_batch * dec_seq, D_MODEL)
    new_cache = None
    for l in range(DEPTH):
        lam_init = 0.8 - 0.6 * math.exp(-0.3 * l)
        last = l == DEPTH - 1

        xp, *new_cache = _mixer_call(xp, mod4, l, seq, False, lam_init, params, big, new_cache=new_cache)
        if l == 0:
            xs, big["w_ffn_in"], big["w_ffn_out"] = _mixer_call(
                xs, mod4, l, dec_seq, True, lam_init, params, big, ctx=ctx, cast_f32=(w_ffn_in, w_ffn_out))
        else:
            xs, = _mixer_call(xs, mod4, l, dec_seq, True, lam_init, params, big, ctx=ctx)
        xp, xs, *next_big = _ffn_call(xp, xs, mod4, l, dec_seq, params, big, last,
                                      next_f32=() if last else big_f32)
        big = dict(zip(BIG_WEIGHTS, next_big))

    y_prompt = xp.reshape(batch, seq, D_MODEL)
    y_sample = xs.reshape(dec_batch, dec_seq, D_MODEL)
    new_cache_k = new_cache[0].reshape(batch, DEPTH, seq, N_HEADS, 2 * QK_DIM)
    new_cache_v = new_cache[1].reshape(batch, DEPTH, seq, N_HEADS, V_DIM)
    return (y_prompt, y_sample, new_cache_k, new_cache_v)
```

```python
import functools
import math
import types

import numpy as np
import jax
import jax.numpy as jnp
from jax import lax
from jax.experimental import pallas as pl
from jax.experimental.pallas import tpu as pltpu

D_MODEL = 1024
DEPTH = 2
GRID_W = 64
POOL_WINDOWS = (2, 4, 8, 16)
POOL_WIDTH = 256
POOL_GROUP = 64
POOL_HALO = 8
ATTN_WIDTH = 512
N_HEADS = 4
V_DIM = 128
QK_DIM = 64
HEAD_COLS = 2 * QK_DIM
ROPE_BASE = 10000.0
ROPE_AXIS_DIM = 32
ROPE_HALF = ROPE_AXIS_DIM // 2
CHUNK = 128
SGU_WIDTH = 256
QK_WIDTH = 512
IN_WIDTH = 2304
MIX_WIDTH = 1024
D_FF = 2816
N_MOD = 6
EPS = 1e-6
PAST_LEN = 256

C_POOL, C_Q, C_K, C_V, C_UV = 0, 256, 768, 1280, 1792
Y_POOL, Y_ATTN, Y_SGU = 0, 256, 768

ROWS = 1024
ROW_CHUNK = 256
ATTN_LOOKAHEAD = 2
CHUNKS_PER_BLOCK = 2
FFN_ROWS = 512
MXU_COLS = 256
FF_SPLIT = (D_FF // MXU_COLS + 1) // 2 * MXU_COLS
COND_ROWS = 16
ADA_COLS = 3072
VMEM_LIMIT = 58 * 1024 * 1024

BF16 = jnp.bfloat16
F32 = jnp.float32


def _dot(a, b):
    return jnp.dot(a, b, preferred_element_type=F32)


def _sigmoid(x):
    return 1.0 / (1.0 + jnp.exp(-x))


def _rms_mod(x, g, scale, shift):
    ms = jnp.mean(x * x, axis=-1, keepdims=True)
    return x * lax.rsqrt(ms + EPS) * (g * (1.0 + scale)) + shift


def _gelu_tanh(x):
    c = math.sqrt(2.0 / math.pi)
    return x * (0.5 * (1.0 + jnp.tanh(c * (x + 0.044715 * (x * x * x)))))


def _const_spec(shape):
    zeros = (0,) * len(shape)
    return pl.BlockSpec(shape, lambda *_: zeros, pipeline_mode=pl.Buffered(1))


def _layer_spec(shape, layer):
    index = (layer,) + (0,) * len(shape)
    return pl.BlockSpec((None,) + tuple(shape), lambda *_: index, pipeline_mode=pl.Buffered(1))


def _ada_kernel(cond_ref, w_ref, b_ref, out_ref):
    cond = cond_ref[...]
    s = (cond * _sigmoid(cond)).astype(BF16)
    out_ref[...] = _dot(s, w_ref[...].astype(BF16)) + b_ref[...]


def _ada_modulation(cond, w_ada, b_ada, n_layers):
    n_cols = N_MOD * D_MODEL
    return pl.pallas_call(
        _ada_kernel,
        out_shape=jax.ShapeDtypeStruct((n_layers, COND_ROWS, n_cols), F32),
        grid=(n_layers, n_cols // ADA_COLS),
        in_specs=[
            pl.BlockSpec((COND_ROWS, D_MODEL), lambda l, j: (0, 0)),
            pl.BlockSpec((None, D_MODEL, ADA_COLS), lambda l, j: (l, 0, j)),
            pl.BlockSpec((None, 1, ADA_COLS), lambda l, j: (l, 0, j)),
        ],
        out_specs=pl.BlockSpec((None, COND_ROWS, ADA_COLS), lambda l, j: (l, 0, j)),
        compiler_params=pltpu.CompilerParams(
            dimension_semantics=("arbitrary", "arbitrary"), vmem_limit_bytes=VMEM_LIMIT),
        name="ada_modulation",
    )(cond, w_ada, b_ada.reshape(DEPTH, 1, n_cols))


def _mixer_body(r, x_ref, cnt_ref, pool_scr, k_scr, v_scr, *, nb, seq, has_ctx, lam_init, cpb):
    kv_base = PAST_LEN if has_ctx else 0
    shift1, scale1, gate1 = r.mod[0:1, :], r.mod[1:2, :], r.mod[2:3, :]

    lane_head = lax.broadcasted_iota(jnp.int32, (ROW_CHUNK, HEAD_COLS), 1)
    first_half = lane_head < QK_DIM

    for b in range(nb):
        pool_scr[b, 0:POOL_HALO, :] = jnp.zeros((POOL_HALO, POOL_WIDTH), F32)
        pool_scr[b, POOL_HALO + seq:POOL_HALO + seq + POOL_HALO, :] = jnp.zeros((POOL_HALO, POOL_WIDTH), F32)
    if not has_ctx:
        for slot in range(r.kc_out.shape[1]):
            if slot != r.cache_slot:
                r.kc_out[:, slot] = jnp.zeros((nb,) + r.kc_out.shape[2:], F32)
                r.vc_out[:, slot] = jnp.zeros((nb,) + r.vc_out.shape[2:], F32)
    if has_ctx:
        for hd in range(N_HEADS):
            cols = slice(hd * HEAD_COLS, (hd + 1) * HEAD_COLS)
            k_scr[0, cols, 0:PAST_LEN] = r.ck[pl.ds(hd, PAST_LEN, stride=N_HEADS), :].T.astype(BF16)
            v_scr[0, 0:PAST_LEN, cols] = r.cv[pl.ds(hd, PAST_LEN, stride=N_HEADS), :].astype(BF16)
        lane_q = lax.broadcasted_iota(jnp.int32, (ROW_CHUNK, QK_WIDTH), 1)
        rope_low = (lane_q % ROPE_AXIS_DIM) < ROPE_HALF

        def rope(t, pos0):
            partner = jnp.where(rope_low,
                                pltpu.roll(t, QK_WIDTH - ROPE_HALF, 1),
                                pltpu.roll(t, ROPE_HALF, 1))
            cos = r.rope_c[pos0:pos0 + ROW_CHUNK, :]
            sin = r.rope_s[pos0:pos0 + ROW_CHUNK, :]
            cos = jnp.concatenate([cos] * N_HEADS, axis=1)
            sin = jnp.concatenate([sin] * N_HEADS, axis=1)
            return t * cos + partner * sin

    for c in range(ROWS // ROW_CHUNK):
        r0 = c * ROW_CHUNK
        b, pos0 = r0 // seq, r0 % seq
        rows = slice(r0, r0 + ROW_CHUNK)
        h = _rms_mod(x_ref[rows, :], r.g1[r.layer_row, :], scale1, shift1).astype(BF16)

        pool_scr[b, POOL_HALO + pos0:POOL_HALO + pos0 + ROW_CHUNK, :] = _dot(h, r.w_in[:, C_POOL:C_Q])

        q = _dot(h, r.w_in[:, C_Q:C_K])
        k = _dot(h, r.w_in[:, C_K:C_V])
        if has_ctx:
            q = rope(q, pos0)
            k = rope(k, pos0)
        else:
            for hd in range(N_HEADS):
                r.kc_out[b, r.cache_slot, pl.ds(N_HEADS * pos0 + hd, ROW_CHUNK, stride=N_HEADS), :] = (
                    k[:, hd * HEAD_COLS:(hd + 1) * HEAD_COLS])
        q = q * (QK_DIM ** -0.5 * math.log2(math.e))
        for hd in range(N_HEADS):
            cols = slice(hd * HEAD_COLS, (hd + 1) * HEAD_COLS)
            qh = q[:, cols]
            r.q_scr[c, 0:ROW_CHUNK, cols] = jnp.where(first_half, qh, 0.0).astype(BF16)
            r.q_scr[c, ROW_CHUNK:2 * ROW_CHUNK, cols] = jnp.where(first_half, 0.0, qh).astype(BF16)
        k_scr[b, :, kv_base + pos0:kv_base + pos0 + ROW_CHUNK] = k.T.astype(BF16)

        v = _dot(h, r.w_in[:, C_V:C_UV])
        if not has_ctx:
            for hd in range(N_HEADS):
                r.vc_out[b, r.cache_slot, pl.ds(N_HEADS * pos0 + hd, ROW_CHUNK, stride=N_HEADS), :] = (
                    v[:, hd * V_DIM:(hd + 1) * V_DIM])
        v_scr[b, kv_base + pos0:kv_base + pos0 + ROW_CHUNK, :] = v.astype(BF16)

        uv = _gelu_tanh(_dot(h, r.w_in[:, C_UV:IN_WIDTH]))
        r.u_scr[rows, :] = uv[:, :SGU_WIDTH]
        vv = uv[:, SGU_WIDTH:]
        mu = jnp.mean(vv, axis=-1, keepdims=True)
        vc = vv - mu
        vn = vc * lax.rsqrt(jnp.mean(vc * vc, axis=-1, keepdims=True) + EPS) * r.sgu_g[r.layer_row, :]
        r.vs_scr[rows, :] = vn.astype(BF16)

    lane_pool = lax.broadcasted_iota(jnp.int32, (ROW_CHUNK, 2 * POOL_GROUP), 1)
    narrow = lane_pool < POOL_GROUP
    for c in range(ROWS // ROW_CHUNK):
        r0 = c * ROW_CHUNK
        b, pos0 = r0 // seq, r0 % seq
        base = POOL_HALO + pos0
        pooled = []
        for j, (w_small, w_big) in enumerate(((POOL_WINDOWS[0], POOL_WINDOWS[1]),
                                              (POOL_WINDOWS[2], POOL_WINDOWS[3]))):
            cols = slice(j * 2 * POOL_GROUP, (j + 1) * 2 * POOL_GROUP)
            s_small = jnp.zeros((ROW_CHUNK, 2 * POOL_GROUP), F32)
            s_rest = jnp.zeros((ROW_CHUNK, 2 * POOL_GROUP), F32)
            for d in range(-(w_big // 2), w_big - w_big // 2):
                t = pool_scr[b, base + d:base + d + ROW_CHUNK, cols]
                if -(w_small // 2) <= d < w_small - w_small // 2:
                    s_small = s_small + t
                else:
                    s_rest = s_rest + t
            win_sum = jnp.where(narrow, s_small, s_small + s_rest)
            centre = pool_scr[b, base:base + ROW_CHUNK, cols]
            pooled.append(win_sum / cnt_ref[pos0:pos0 + ROW_CHUNK, cols] - centre)
        pooled = jnp.concatenate(pooled, axis=1).astype(BF16)
        y_a = _dot(pooled, r.wpool[...]) * r.pscale[r.layer_row, :]
        r.y_scr[r0:r0 + ROW_CHUNK, Y_POOL:Y_ATTN] = y_a.astype(BF16)

    lq1, lk1, lq2, lk2 = r.lam[0:1, :], r.lam[1:2, :], r.lam[2:3, :], r.lam[3:4, :]
    lam = (jnp.exp(jnp.sum(lq1 * lk1, axis=-1, keepdims=True))
           - jnp.exp(jnp.sum(lq2 * lk2, axis=-1, keepdims=True)) + lam_init)
    n_qb = seq // ROW_CHUNK

    lane_sgu = lax.broadcasted_iota(jnp.int32, (CHUNK, 2 * POOL_GROUP), 1)
    sgu_first = lane_sgu < (SGU_WIDTH // 4)
    out_cols = MIX_WIDTH // N_HEADS

    def out_proj(rows, part):
        cols = slice(part * out_cols, (part + 1) * out_cols)
        y = _dot(r.y_scr[rows, :], r.w_out[:, cols])
        r.out[rows, cols] = x_ref[rows, cols] + gate1[:, cols] * y

    def block(blk, with_prev):
        static = isinstance(blk, int)

        def row0(chunk):
            return chunk * ROW_CHUNK if static else pl.multiple_of(chunk * ROW_CHUNK, ROW_CHUNK)

        chunks = [blk * cpb + j for j in range(cpb)]

        for idx in chunks:
            for half in range(ROW_CHUNK // CHUNK):
                sub = pl.ds(row0(idx) + half * CHUNK, CHUNK)
                for j in range(2):
                    cols = slice(j * 128, (j + 1) * 128)
                    t = _dot(r.wsgu[j], r.vs_scr[sub, cols])
                    mixed = jnp.where(sgu_first, t[:CHUNK, :], t[CHUNK:, :]) + r.bsgu[:, cols]
                    y_c = r.u_scr[sub, cols] * mixed
                    r.y_scr[sub, Y_SGU + j * 128:Y_SGU + (j + 1) * 128] = y_c.astype(BF16)

        units = [(idx, hd) for idx in chunks for hd in range(N_HEADS)]
        prev = [(pl.ds(row0(idx - cpb), ROW_CHUNK), part) for idx in chunks for part in range(N_HEADS)]

        def scores(idx, hd):
            cols = slice(hd * HEAD_COLS, (hd + 1) * HEAD_COLS)
            s = _dot(r.q_scr[idx, :, cols], k_scr[idx // n_qb, cols, :])
            return s[:ROW_CHUNK], s[ROW_CHUNK:]

        pending = [scores(*u) for u in units[:ATTN_LOOKAHEAD]]
        for n, (idx, hd) in enumerate(units):
            cols = slice(hd * HEAD_COLS, (hd + 1) * HEAD_COLS)
            s1, s2 = pending.pop(0)
            if n + ATTN_LOOKAHEAD < len(units):
                pending.append(scores(*units[n + ATTN_LOOKAHEAD]))
            e1 = jnp.exp2(s1 - jnp.max(s1, axis=-1, keepdims=True))
            e2 = jnp.exp2(s2 - jnp.max(s2, axis=-1, keepdims=True))
            l1 = jnp.sum(e1, axis=-1, keepdims=True)
            l2 = jnp.sum(e2, axis=-1, keepdims=True)
            p = (e1 - e2 * (lam * l1 / l2)).astype(BF16)
            o = _dot(p, v_scr[idx // n_qb, :, cols]) * (1.0 / l1)
            if with_prev:
                out_proj(*prev[n])
            o = o * lax.rsqrt(jnp.mean(o * o, axis=-1, keepdims=True) + EPS) * r.subln[r.layer_row, :]
            o = o * (1.0 - lam_init)
            r.y_scr[pl.ds(row0(idx), ROW_CHUNK), Y_ATTN + hd * V_DIM:Y_ATTN + (hd + 1) * V_DIM] = o.astype(BF16)

    n_blocks = ROWS // (ROW_CHUNK * cpb)
    block(0, False)
    if n_blocks == 2:
        block(1, True)
    else:
        def loop_block(blk, carry):
            block(blk, True)
            return carry

        lax.fori_loop(1, n_blocks, loop_block, 0)
    for idx in range((n_blocks - 1) * cpb, n_blocks * cpb):
        for part in range(N_HEADS):
            out_proj(pl.ds(idx * ROW_CHUNK, ROW_CHUNK), part)


def _mixer_kernel(*refs, layer, nb, seq, has_ctx, lam_init, cache_slot, n_passthrough, n_cast):
    it = iter(refs)
    r = types.SimpleNamespace(cache_slot=cache_slot, layer_row=slice(layer, layer + 1))
    x_ref, r.mod, r.g1, r.w_in = next(it), next(it), next(it), next(it)
    if has_ctx:
        r.rope_c, r.rope_s, r.ck, r.cv = next(it), next(it), next(it), next(it)
    (r.lam, r.subln, cnt_ref, r.wpool, r.pscale, r.sgu_g, r.wsgu, r.bsgu, r.w_out) = (next(it) for _ in range(9))
    for _ in range(n_passthrough):
        next(it)
    cast_in = [next(it) for _ in range(n_cast)]
    r.out = next(it)
    if not has_ctx:
        r.kc_out, r.vc_out = next(it), next(it)
    cast_out = [next(it) for _ in range(n_cast)]
    pool_scr, r.q_scr, k_scr, v_scr, r.u_scr, r.vs_scr, r.y_scr = (next(it) for _ in range(7))
    for src, dst in zip(cast_in, cast_out):
        dst[...] = src[...].astype(BF16)
    _mixer_body(r, x_ref, cnt_ref, pool_scr, k_scr, v_scr, nb=nb, seq=seq, has_ctx=has_ctx, lam_init=lam_init,
                cpb=CHUNKS_PER_BLOCK)


def _mixer_call(x2d, mod4, layer, seq, has_ctx, lam_init, params, big, ctx=None, new_cache=None, cast_f32=()):
    n_rows = x2d.shape[0]
    nb = ROWS // seq
    n_seq = n_rows // seq
    kv_len = seq + (PAST_LEN if has_ctx else 0)

    if has_ctx:
        mod_map = lambda i: (0, 1 + i, 0, 0)
    else:
        mod_map = lambda i: (0, 0, 0, 0)

    operands = [x2d, mod4, params["norm1_g"], big["w_in"]]
    in_specs = [
        pl.BlockSpec((ROWS, D_MODEL), lambda i: (i, 0)),
        pl.BlockSpec((None, None, N_MOD, D_MODEL), mod_map),
        _const_spec((DEPTH, D_MODEL)),
        _const_spec((D_MODEL, IN_WIDTH)),
    ]
    if has_ctx:
        operands += list(ctx)
        in_specs += [
            _const_spec((seq, HEAD_COLS)),
            _const_spec((seq, HEAD_COLS)),
            pl.BlockSpec((None, None, PAST_LEN * N_HEADS, HEAD_COLS), lambda i: (i, layer, 0, 0)),
            pl.BlockSpec((None, None, PAST_LEN * N_HEADS, V_DIM), lambda i: (i, layer, 0, 0)),
        ]
    operands += [params["lam"], params["subln_g"], _pool_counts(seq), params["w_pool"], params["pool_scale"],
                 params["sgu_norm_g"], params["w_sgu"], params["b_sgu"], big["w_out"]]
    in_specs += [
        _layer_spec((8, 128), layer),
        _const_spec((DEPTH, V_DIM)),
        _const_spec((seq, POOL_WIDTH)),
        _layer_spec((POOL_WIDTH, POOL_WIDTH), layer),
        _const_spec((DEPTH, POOL_WIDTH)),
        _const_spec((DEPTH, SGU_WIDTH)),
        _layer_spec((2, 2 * CHUNK, CHUNK), layer),
        _layer_spec((CHUNK, SGU_WIDTH), layer),
        _const_spec((MIX_WIDTH, D_MODEL)),
    ]

    out_shape = [jax.ShapeDtypeStruct((n_rows, D_MODEL), F32)]
    out_specs = [pl.BlockSpec((ROWS, D_MODEL), lambda i: (i, 0))]
    aliases = {}
    cache_slot = 0
    if not has_ctx:
        out_shape += [jax.ShapeDtypeStruct((n_seq, DEPTH, seq * N_HEADS, HEAD_COLS), F32),
                      jax.ShapeDtypeStruct((n_seq, DEPTH, seq * N_HEADS, V_DIM), F32)]
        if new_cache is None:
            cache_slot = layer
            out_specs += [pl.BlockSpec((nb, DEPTH, seq * N_HEADS, HEAD_COLS), lambda i: (i, 0, 0, 0)),
                          pl.BlockSpec((nb, DEPTH, seq * N_HEADS, V_DIM), lambda i: (i, 0, 0, 0))]
        else:
            out_specs += [pl.BlockSpec((nb, 1, seq * N_HEADS, HEAD_COLS), lambda i: (i, layer, 0, 0)),
                          pl.BlockSpec((nb, 1, seq * N_HEADS, V_DIM), lambda i: (i, layer, 0, 0))]
            aliases = {len(operands): 1, len(operands) + 1: 2}
            operands += list(new_cache)
            in_specs += [pl.BlockSpec(memory_space=pl.ANY)] * 2
    n_steps = n_rows // ROWS
    for w in cast_f32:
        _, rows, cols = w.shape
        slab = rows // n_steps
        assert slab * n_steps == rows and slab % 16 == 0
        operands.append(w)
        in_specs.append(pl.BlockSpec((None, slab, cols), lambda i: (layer, i, 0)))
        out_shape.append(jax.ShapeDtypeStruct((rows, cols), BF16))
        out_specs.append(pl.BlockSpec((slab, cols), lambda i: (i, 0)))

    scratch = [
        pltpu.VMEM((nb, seq + 2 * POOL_HALO, POOL_WIDTH), F32),
        pltpu.VMEM((ROWS // ROW_CHUNK, 2 * ROW_CHUNK, QK_WIDTH), BF16),
        pltpu.VMEM((nb, QK_WIDTH, kv_len), BF16),
        pltpu.VMEM((nb, kv_len, ATTN_WIDTH), BF16),
        pltpu.VMEM((ROWS, SGU_WIDTH), F32),
        pltpu.VMEM((ROWS, SGU_WIDTH), BF16),
        pltpu.VMEM((ROWS, MIX_WIDTH), BF16),
    ]
    kernel = functools.partial(_mixer_kernel, layer=layer, nb=nb, seq=seq, has_ctx=has_ctx, lam_init=lam_init,
                               cache_slot=cache_slot, n_passthrough=len(aliases), n_cast=len(cast_f32))
    return pl.pallas_call(
        kernel,
        out_shape=out_shape,
        grid=(n_rows // ROWS,),
        in_specs=in_specs,
        out_specs=out_specs,
        scratch_shapes=scratch,
        input_output_aliases=aliases,
        compiler_params=pltpu.CompilerParams(
            dimension_semantics=("arbitrary",), vmem_limit_bytes=VMEM_LIMIT),
        name="mixer_ctx" if has_ctx else "mixer_prompt",
    )(*operands)


def _ffn_block(x_ref, out_ref, mod_ref, g2_ref, w1_ref, w2_ref, gf_ref, act_scr, final_norm, layer):
    shift2, scale2, gate2 = mod_ref[3:4, :], mod_ref[4:5, :], mod_ref[5:6, :]
    x = x_ref[...]
    h = _rms_mod(x, g2_ref[layer:layer + 1, :], scale2, shift2).astype(BF16)
    for lo, hi in ((0, FF_SPLIT), (FF_SPLIT, D_FF)):
        gate = _dot(h, w1_ref[:, lo:hi])
        up = _dot(h, w1_ref[:, D_FF + lo:D_FF + hi])
        act_scr[:, lo:hi] = (gate * _sigmoid(gate) * up).astype(BF16)
    y = x + gate2 * _dot(act_scr[...], w2_ref[...])
    if final_norm:
        ms = jnp.mean(y * y, axis=-1, keepdims=True)
        y = y * lax.rsqrt(ms + EPS) * gf_ref[...]
    out_ref[...] = y


def _ffn_kernel(*refs, layer, n_prompt_steps, final_norm, n_cast, next_mod):
    xp_ref, xs_ref, mod_ref, g2_ref, w1_ref, w2_ref, gf_ref = refs[:7]
    n_in = 7 + n_cast + (3 if next_mod else 0)
    cast_in = refs[7:7 + n_cast]
    yp_ref, ys_ref = refs[n_in:n_in + 2]
    cast_out = refs[n_in + 2:n_in + 2 + n_cast]
    act_scr = refs[-1]
    shared = (mod_ref, g2_ref, w1_ref, w2_ref, gf_ref, act_scr, final_norm, layer)
    is_prompt = pl.program_id(0) < n_prompt_steps

    def next_modulation():
        if next_mod:
            cond_ref, wa_ref, ba_ref = refs[7 + n_cast:n_in]
            cond = cond_ref[...]
            s = (cond * _sigmoid(cond)).astype(BF16)
            refs[n_in + 2 + n_cast][...] = _dot(s, wa_ref[...].astype(BF16)) + ba_ref[...]

    @pl.when(is_prompt)
    def _():
        next_modulation()
        _ffn_block(xp_ref, yp_ref, *shared)

    @pl.when(jnp.logical_not(is_prompt))
    def _():
        next_modulation()
        for src, dst in zip(cast_in, cast_out):
            dst[...] = src[...].astype(BF16)
        _ffn_block(xs_ref, ys_ref, *shared)


def _ffn_call(xp, xs, mod4, layer, sample_seq, params, big, final_norm, next_f32=(), ada=None):
    n_p, n_s = xp.shape[0] // FFN_ROWS, xs.shape[0] // FFN_ROWS
    steps_per_seq = sample_seq // FFN_ROWS

    def prompt_step(i):
        return jnp.minimum(i, n_p - 1)

    def sample_step(i):
        return jnp.maximum(i - n_p, 0)

    def mod_map(i):
        return (0, jnp.where(i < n_p, 0, 1 + sample_step(i) // steps_per_seq), 0, 0)

    in_specs = [
        pl.BlockSpec((FFN_ROWS, D_MODEL), lambda i: (prompt_step(i), 0)),
        pl.BlockSpec((FFN_ROWS, D_MODEL), lambda i: (sample_step(i), 0)),
        pl.BlockSpec((None, None, N_MOD, D_MODEL), mod_map),
        _const_spec((DEPTH, D_MODEL)),
        _const_spec((D_MODEL, 2 * D_FF)),
        _const_spec((D_FF, D_MODEL)),
        _const_spec((1, D_MODEL)),
    ]
    out_shape = [jax.ShapeDtypeStruct(xp.shape, F32), jax.ShapeDtypeStruct(xs.shape, F32)]
    out_specs = [pl.BlockSpec((FFN_ROWS, D_MODEL), lambda i: (prompt_step(i), 0)),
                 pl.BlockSpec((FFN_ROWS, D_MODEL), lambda i: (sample_step(i), 0))]
    for w in next_f32:
        _, rows, cols = w.shape
        slab = rows // n_s
        assert slab * n_s == rows and slab % 16 == 0
        in_specs.append(pl.BlockSpec((None, slab, cols), lambda i: (layer + 1, sample_step(i), 0)))
        out_shape.append(jax.ShapeDtypeStruct((rows, cols), BF16))
        out_specs.append(pl.BlockSpec((slab, cols), lambda i: (sample_step(i), 0)))

    operands = [xp, xs, mod4, params["norm2_g"], big["w_ffn_in"], big["w_ffn_out"], params["final_g"], *next_f32]
    if ada is not None:
        cond, w_ada, b_ada = ada
        n_cols = N_MOD * D_MODEL
        slab = n_cols // (n_p + n_s)
        assert slab * (n_p + n_s) == n_cols and slab % 128 == 0
        operands += [cond, w_ada, b_ada.reshape(DEPTH, 1, n_cols)]
        in_specs += [_const_spec((COND_ROWS, D_MODEL)),
                     pl.BlockSpec((None, D_MODEL, slab), lambda i: (layer + 1, 0, i)),
                     pl.BlockSpec((None, 1, slab), lambda i: (layer + 1, 0, i))]
        out_shape.append(jax.ShapeDtypeStruct((COND_ROWS, n_cols), F32))
        out_specs.append(pl.BlockSpec((COND_ROWS, slab), lambda i: (0, i)))

    kernel = functools.partial(_ffn_kernel, layer=layer, n_prompt_steps=n_p, final_norm=final_norm,
                               n_cast=len(next_f32), next_mod=ada is not None)
    return pl.pallas_call(
        kernel,
        out_shape=out_shape,
        grid=(n_p + n_s,),
        in_specs=in_specs,
        out_specs=out_specs,
        scratch_shapes=[pltpu.VMEM((FFN_ROWS, D_FF), BF16)],
        compiler_params=pltpu.CompilerParams(
            dimension_semantics=("arbitrary",), vmem_limit_bytes=VMEM_LIMIT),
        name="ffn",
    )(*operands)


def _rope_tables(seq):
    n_rows = seq // GRID_W
    rows = np.repeat(np.arange(n_rows), GRID_W).astype(np.float32)
    cols = np.tile(np.arange(GRID_W), n_rows).astype(np.float32)
    inv = 1.0 / (ROPE_BASE ** (np.arange(0, ROPE_AXIS_DIM, 2, dtype=np.float32) / ROPE_AXIS_DIM))
    ar, ac = rows[:, None] * inv[None], cols[:, None] * inv[None]
    cos_parts, sin_parts = [], []
    for ang in (ar, ac):
        cos_parts += [np.cos(ang), np.cos(ang)]
        sin_parts += [-np.sin(ang), np.sin(ang)]
    cos64 = np.concatenate(cos_parts, axis=1)
    sin64 = np.concatenate(sin_parts, axis=1)
    cos = np.concatenate([cos64, cos64], axis=1).astype(np.float32)
    sin = np.concatenate([sin64, sin64], axis=1).astype(np.float32)
    return jnp.asarray(cos), jnp.asarray(sin)


def _pool_counts(seq):
    t = np.arange(seq)
    cols = []
    for w in POOL_WINDOWS:
        lo = np.clip(t - w // 2, 0, seq)
        hi = np.clip(t + w - w // 2, 0, seq)
        cols.append(np.repeat((hi - lo).astype(np.float32)[:, None], POOL_GROUP, axis=1))
    return jnp.asarray(np.concatenate(cols, axis=1))


def _small_params(norm1_g, w_pool, pool_scale, lam_q1, lam_k1, lam_q2, lam_k2, subln_g,
                  sgu_norm_g, w_sgu, b_sgu, norm2_g, final_g):
    n_groups = len(POOL_WINDOWS)
    eye = jnp.eye(n_groups, dtype=F32)
    w_pool_bd = (eye[None, :, None, :, None] * w_pool[:, :, :, None, :]).reshape(DEPTH, POOL_WIDTH, POOL_WIDTH)
    lam = jnp.pad(jnp.stack([lam_q1, lam_k1, lam_q2, lam_k2], axis=1), ((0, 0), (0, 4), (0, 128 - QK_DIM)))
    return {
        "norm1_g": norm1_g,
        "lam": lam,
        "subln_g": subln_g,
        "w_pool": w_pool_bd.astype(BF16),
        "pool_scale": pool_scale,
        "sgu_norm_g": sgu_norm_g,
        "w_sgu": w_sgu.reshape(DEPTH, 2, 2 * CHUNK, CHUNK).astype(BF16),
        "b_sgu": jnp.repeat(jnp.swapaxes(b_sgu, 1, 2), SGU_WIDTH // 4, axis=2),
        "norm2_g": norm2_g,
        "final_g": final_g[None, :],
    }


BIG_WEIGHTS = ("w_in", "w_out", "w_ffn_in", "w_ffn_out")


def kernel(x_prompt, x_sample, cache_k, cache_v, c, c_ctx, norm1_g, w_ada, b_ada, w_in, w_pool, pool_scale, lam_q1, lam_k1, lam_q2, lam_k2, subln_g, sgu_norm_g, w_sgu, b_sgu, w_out, norm2_g, w_ffn_in, w_ffn_out, final_g):
    batch, seq, _ = x_prompt.shape
    dec_batch, dec_seq, _ = x_sample.shape
    assert ROWS % seq == 0 and dec_seq == ROWS and 1 + dec_batch <= COND_ROWS

    cond = jnp.concatenate([c_ctx[None, :], c, jnp.zeros((COND_ROWS - 1 - dec_batch, D_MODEL), F32)], axis=0)
    mod4 = _ada_modulation(cond, w_ada, b_ada, 1).reshape(1, COND_ROWS, N_MOD, D_MODEL)

    ctx = _rope_tables(dec_seq) + (cache_k.reshape(dec_batch, DEPTH, PAST_LEN * N_HEADS, HEAD_COLS),
                                   cache_v.reshape(dec_batch, DEPTH, PAST_LEN * N_HEADS, V_DIM))
    params = _small_params(norm1_g, w_pool, pool_scale, lam_q1, lam_k1, lam_q2, lam_k2, subln_g,
                           sgu_norm_g, w_sgu, b_sgu, norm2_g, final_g)

    big_f32 = (w_in, w_out, w_ffn_in, w_ffn_out)
    big = {"w_in": w_in[0].astype(BF16), "w_out": w_out[0].astype(BF16)}

    xp = x_prompt.reshape(batch * seq, D_MODEL)
    xs = x_sample.reshape(dec_batch * dec_seq, D_MODEL)
    new_cache = None
    for l in range(DEPTH):
        lam_init = 0.8 - 0.6 * math.exp(-0.3 * l)
        last = l == DEPTH - 1

        xp, *new_cache = _mixer_call(xp, mod4, l, seq, False, lam_init, params, big, new_cache=new_cache)
        if l == 0:
            xs, big["w_ffn_in"], big["w_ffn_out"] = _mixer_call(
                xs, mod4, l, dec_seq, True, lam_init, params, big, ctx=ctx, cast_f32=(w_ffn_in, w_ffn_out))
        else:
            xs, = _mixer_call(xs, mod4, l, dec_seq, True, lam_init, params, big, ctx=ctx)
        xp, xs, *rest = _ffn_call(xp, xs, mod4, l, dec_seq, params, big, last,
                                  next_f32=() if last else big_f32, ada=None if last else (cond, w_ada, b_ada))
        if not last:
            big = dict(zip(BIG_WEIGHTS, rest[:-1]))
            mod4 = rest[-1].reshape(1, COND_ROWS, N_MOD, D_MODEL)

    y_prompt = xp.reshape(batch, seq, D_MODEL)
    y_sample = xs.reshape(dec_batch, dec_seq, D_MODEL)
    new_cache_k = new_cache[0].reshape(batch, DEPTH, seq, N_HEADS, 2 * QK_DIM)
    new_cache_v = new_cache[1].reshape(batch, DEPTH, seq, N_HEADS, V_DIM)
    return (y_prompt, y_sample, new_cache_k, new_cache_v)
```

```python
import functools
import math
import types

import numpy as np
import jax
import jax.numpy as jnp
from jax import lax
from jax.experimental import pallas as pl
from jax.experimental.pallas import tpu as pltpu

D_MODEL = 1024
DEPTH = 2
GRID_W = 64
POOL_WINDOWS = (2, 4, 8, 16)
POOL_WIDTH = 256
POOL_GROUP = 64
POOL_HALO = 8
ATTN_WIDTH = 512
N_HEADS = 4
V_DIM = 128
QK_DIM = 64
HEAD_COLS = 2 * QK_DIM
ROPE_BASE = 10000.0
ROPE_AXIS_DIM = 32
ROPE_HALF = ROPE_AXIS_DIM // 2
CHUNK = 128
SGU_WIDTH = 256
QK_WIDTH = 512
IN_WIDTH = 2304
MIX_WIDTH = 1024
D_FF = 2816
N_MOD = 6
EPS = 1e-6
PAST_LEN = 256

C_POOL, C_Q, C_K, C_V, C_UV = 0, 256, 768, 1280, 1792
Y_POOL, Y_ATTN, Y_SGU = 0, 256, 768

ROWS = 1024
ROW_CHUNK = 256
ATTN_LOOKAHEAD = 2
CHUNKS_PER_BLOCK = 2
FFN_ROWS = 512
MXU_COLS = 256
FF_SPLIT = (D_FF // MXU_COLS + 1) // 2 * MXU_COLS
COND_ROWS = 16
ADA_COLS = 3072
VMEM_LIMIT = 58 * 1024 * 1024

BF16 = jnp.bfloat16
F32 = jnp.float32


def _dot(a, b):
    return jnp.dot(a, b, preferred_element_type=F32)


def _sigmoid(x):
    return 1.0 / (1.0 + jnp.exp(-x))


def _rms_mod(x, g, scale, shift):
    ms = jnp.mean(x * x, axis=-1, keepdims=True)
    return x * lax.rsqrt(ms + EPS) * (g * (1.0 + scale)) + shift


def _gelu_tanh(x):
    c = math.sqrt(2.0 / math.pi)
    return x * (0.5 * (1.0 + jnp.tanh(c * (x + 0.044715 * (x * x * x)))))


def _const_spec(shape):
    zeros = (0,) * len(shape)
    return pl.BlockSpec(shape, lambda *_: zeros, pipeline_mode=pl.Buffered(1))


def _layer_spec(shape, layer):
    index = (layer,) + (0,) * len(shape)
    return pl.BlockSpec((None,) + tuple(shape), lambda *_: index, pipeline_mode=pl.Buffered(1))


def _ada_kernel(c_ctx_ref, c_ref, w_ref, b_ref, out_ref, cond_scr):
    n = c_ref.shape[0]
    cond_scr[...] = jnp.zeros(cond_scr.shape, F32)
    cond_scr[0:1, :] = c_ctx_ref[...]
    cond_scr[1:1 + n, :] = c_ref[...]
    cond = cond_scr[...]
    s = (cond * _sigmoid(cond)).astype(BF16)
    out_ref[...] = _dot(s, w_ref[...].astype(BF16)) + b_ref[...]


def _ada_modulation(c_ctx, c, w_ada, b_ada):
    n_cols = N_MOD * D_MODEL
    return pl.pallas_call(
        _ada_kernel,
        out_shape=jax.ShapeDtypeStruct((DEPTH, COND_ROWS, n_cols), F32),
        grid=(DEPTH, n_cols // ADA_COLS),
        in_specs=[
            pl.BlockSpec((1, D_MODEL), lambda l, j: (0, 0)),
            pl.BlockSpec(c.shape, lambda l, j: (0, 0)),
            pl.BlockSpec((None, D_MODEL, ADA_COLS), lambda l, j: (l, 0, j)),
            pl.BlockSpec((None, 1, ADA_COLS), lambda l, j: (l, 0, j)),
        ],
        out_specs=pl.BlockSpec((None, COND_ROWS, ADA_COLS), lambda l, j: (l, 0, j)),
        scratch_shapes=[pltpu.VMEM((COND_ROWS, D_MODEL), F32)],
        compiler_params=pltpu.CompilerParams(
            dimension_semantics=("arbitrary", "arbitrary"), vmem_limit_bytes=VMEM_LIMIT),
        name="ada_modulation",
    )(c_ctx.reshape(1, D_MODEL), c, w_ada, b_ada.reshape(DEPTH, 1, n_cols))


def _mixer_body(r, x_ref, cnt_ref, pool_scr, k_scr, v_scr, *, nb, seq, has_ctx, lam_init, cpb):
    kv_base = PAST_LEN if has_ctx else 0
    shift1, scale1, gate1 = r.mod[0:1, :], r.mod[1:2, :], r.mod[2:3, :]

    lane_head = lax.broadcasted_iota(jnp.int32, (ROW_CHUNK, HEAD_COLS), 1)
    first_half = lane_head < QK_DIM

    for b in range(nb):
        pool_scr[b, 0:POOL_HALO, :] = jnp.zeros((POOL_HALO, POOL_WIDTH), F32)
        pool_scr[b, POOL_HALO + seq:POOL_HALO + seq + POOL_HALO, :] = jnp.zeros((POOL_HALO, POOL_WIDTH), F32)
    if not has_ctx:
        for slot in range(r.kc_out.shape[1]):
            if slot != r.cache_slot:
                r.kc_out[:, slot] = jnp.zeros((nb,) + r.kc_out.shape[2:], F32)
                r.vc_out[:, slot] = jnp.zeros((nb,) + r.vc_out.shape[2:], F32)
    if has_ctx:
        for hd in range(N_HEADS):
            cols = slice(hd * HEAD_COLS, (hd + 1) * HEAD_COLS)
            k_scr[0, cols, 0:PAST_LEN] = r.ck[pl.ds(hd, PAST_LEN, stride=N_HEADS), :].T.astype(BF16)
            v_scr[0, 0:PAST_LEN, cols] = r.cv[pl.ds(hd, PAST_LEN, stride=N_HEADS), :].astype(BF16)
        lane_q = lax.broadcasted_iota(jnp.int32, (ROW_CHUNK, QK_WIDTH), 1)
        rope_low = (lane_q % ROPE_AXIS_DIM) < ROPE_HALF

        def rope(t, pos0):
            partner = jnp.where(rope_low,
                                pltpu.roll(t, QK_WIDTH - ROPE_HALF, 1),
                                pltpu.roll(t, ROPE_HALF, 1))
            cos = r.rope_c[pos0:pos0 + ROW_CHUNK, :]
            sin = r.rope_s[pos0:pos0 + ROW_CHUNK, :]
            cos = jnp.concatenate([cos] * N_HEADS, axis=1)
            sin = jnp.concatenate([sin] * N_HEADS, axis=1)
            return t * cos + partner * sin

    for c in range(ROWS // ROW_CHUNK):
        r0 = c * ROW_CHUNK
        b, pos0 = r0 // seq, r0 % seq
        rows = slice(r0, r0 + ROW_CHUNK)
        h = _rms_mod(x_ref[rows, :], r.g1[r.layer_row, :], scale1, shift1).astype(BF16)

        pool_scr[b, POOL_HALO + pos0:POOL_HALO + pos0 + ROW_CHUNK, :] = _dot(h, r.w_in[:, C_POOL:C_Q])

        q = _dot(h, r.w_in[:, C_Q:C_K])
        k = _dot(h, r.w_in[:, C_K:C_V])
        if has_ctx:
            q = rope(q, pos0)
            k = rope(k, pos0)
        else:
            for hd in range(N_HEADS):
                r.kc_out[b, r.cache_slot, pl.ds(N_HEADS * pos0 + hd, ROW_CHUNK, stride=N_HEADS), :] = (
                    k[:, hd * HEAD_COLS:(hd + 1) * HEAD_COLS])
        q = q * (QK_DIM ** -0.5 * math.log2(math.e))
        for hd in range(N_HEADS):
            cols = slice(hd * HEAD_COLS, (hd + 1) * HEAD_COLS)
            qh = q[:, cols]
            r.q_scr[c, 0:ROW_CHUNK, cols] = jnp.where(first_half, qh, 0.0).astype(BF16)
            r.q_scr[c, ROW_CHUNK:2 * ROW_CHUNK, cols] = jnp.where(first_half, 0.0, qh).astype(BF16)
        k_scr[b, :, kv_base + pos0:kv_base + pos0 + ROW_CHUNK] = k.T.astype(BF16)

        v = _dot(h, r.w_in[:, C_V:C_UV])
        if not has_ctx:
            for hd in range(N_HEADS):
                r.vc_out[b, r.cache_slot, pl.ds(N_HEADS * pos0 + hd, ROW_CHUNK, stride=N_HEADS), :] = (
                    v[:, hd * V_DIM:(hd + 1) * V_DIM])
        v_scr[b, kv_base + pos0:kv_base + pos0 + ROW_CHUNK, :] = v.astype(BF16)

        uv = _gelu_tanh(_dot(h, r.w_in[:, C_UV:IN_WIDTH]))
        r.u_scr[rows, :] = uv[:, :SGU_WIDTH]
        vv = uv[:, SGU_WIDTH:]
        mu = jnp.mean(vv, axis=-1, keepdims=True)
        vc = vv - mu
        vn = vc * lax.rsqrt(jnp.mean(vc * vc, axis=-1, keepdims=True) + EPS) * r.sgu_g[r.layer_row, :]
        r.vs_scr[rows, :] = vn.astype(BF16)

    lane_pool = lax.broadcasted_iota(jnp.int32, (ROW_CHUNK, 2 * POOL_GROUP), 1)
    narrow = lane_pool < POOL_GROUP
    for c in range(ROWS // ROW_CHUNK):
        r0 = c * ROW_CHUNK
        b, pos0 = r0 // seq, r0 % seq
        base = POOL_HALO + pos0
        pooled = []
        for j, (w_small, w_big) in enumerate(((POOL_WINDOWS[0], POOL_WINDOWS[1]),
                                              (POOL_WINDOWS[2], POOL_WINDOWS[3]))):
            cols = slice(j * 2 * POOL_GROUP, (j + 1) * 2 * POOL_GROUP)
            s_small = jnp.zeros((ROW_CHUNK, 2 * POOL_GROUP), F32)
            s_rest = jnp.zeros((ROW_CHUNK, 2 * POOL_GROUP), F32)
            for d in range(-(w_big // 2), w_big - w_big // 2):
                t = pool_scr[b, base + d:base + d + ROW_CHUNK, cols]
                if -(w_small // 2) <= d < w_small - w_small // 2:
                    s_small = s_small + t
                else:
                    s_rest = s_rest + t
            win_sum = jnp.where(narrow, s_small, s_small + s_rest)
            centre = pool_scr[b, base:base + ROW_CHUNK, cols]
            pooled.append(win_sum / cnt_ref[pos0:pos0 + ROW_CHUNK, cols] - centre)
        pooled = jnp.concatenate(pooled, axis=1).astype(BF16)
        y_a = _dot(pooled, r.wpool[...]) * r.pscale[r.layer_row, :]
        r.y_scr[r0:r0 + ROW_CHUNK, Y_POOL:Y_ATTN] = y_a.astype(BF16)

    lq1, lk1, lq2, lk2 = r.lam[0:1, :], r.lam[1:2, :], r.lam[2:3, :], r.lam[3:4, :]
    lam = (jnp.exp(jnp.sum(lq1 * lk1, axis=-1, keepdims=True))
           - jnp.exp(jnp.sum(lq2 * lk2, axis=-1, keepdims=True)) + lam_init)
    n_qb = seq // ROW_CHUNK

    lane_sgu = lax.broadcasted_iota(jnp.int32, (CHUNK, 2 * POOL_GROUP), 1)
    sgu_first = lane_sgu < (SGU_WIDTH // 4)
    out_cols = MIX_WIDTH // N_HEADS

    def out_proj(rows, part):
        cols = slice(part * out_cols, (part + 1) * out_cols)
        y = _dot(r.y_scr[rows, :], r.w_out[:, cols])
        r.out[rows, cols] = x_ref[rows, cols] + gate1[:, cols] * y

    def block(blk, with_prev):
        static = isinstance(blk, int)

        def row0(chunk):
            return chunk * ROW_CHUNK if static else pl.multiple_of(chunk * ROW_CHUNK, ROW_CHUNK)

        chunks = [blk * cpb + j for j in range(cpb)]

        for idx in chunks:
            for half in range(ROW_CHUNK // CHUNK):
                sub = pl.ds(row0(idx) + half * CHUNK, CHUNK)
                for j in range(2):
                    cols = slice(j * 128, (j + 1) * 128)
                    t = _dot(r.wsgu[j], r.vs_scr[sub, cols])
                    mixed = jnp.where(sgu_first, t[:CHUNK, :], t[CHUNK:, :]) + r.bsgu[:, cols]
                    y_c = r.u_scr[sub, cols] * mixed
                    r.y_scr[sub, Y_SGU + j * 128:Y_SGU + (j + 1) * 128] = y_c.astype(BF16)

        units = [(idx, hd) for idx in chunks for hd in range(N_HEADS)]
        prev = [(pl.ds(row0(idx - cpb), ROW_CHUNK), part) for idx in chunks for part in range(N_HEADS)]

        def scores(idx, hd):
            cols = slice(hd * HEAD_COLS, (hd + 1) * HEAD_COLS)
            s = _dot(r.q_scr[idx, :, cols], k_scr[idx // n_qb, cols, :])
            return s[:ROW_CHUNK], s[ROW_CHUNK:]

        pending = [scores(*u) for u in units[:ATTN_LOOKAHEAD]]
        for n, (idx, hd) in enumerate(units):
            cols = slice(hd * HEAD_COLS, (hd + 1) * HEAD_COLS)
            s1, s2 = pending.pop(0)
            if n + ATTN_LOOKAHEAD < len(units):
                pending.append(scores(*units[n + ATTN_LOOKAHEAD]))
            e1 = jnp.exp2(s1 - jnp.max(s1, axis=-1, keepdims=True))
            e2 = jnp.exp2(s2 - jnp.max(s2, axis=-1, keepdims=True))
            l1 = jnp.sum(e1, axis=-1, keepdims=True)
            l2 = jnp.sum(e2, axis=-1, keepdims=True)
            p = (e1 - e2 * (lam * l1 / l2)).astype(BF16)
            o = _dot(p, v_scr[idx // n_qb, :, cols]) * (1.0 / l1)
            if with_prev:
                out_proj(*prev[n])
            o = o * lax.rsqrt(jnp.mean(o * o, axis=-1, keepdims=True) + EPS) * r.subln[r.layer_row, :]
            o = o * (1.0 - lam_init)
            r.y_scr[pl.ds(row0(idx), ROW_CHUNK), Y_ATTN + hd * V_DIM:Y_ATTN + (hd + 1) * V_DIM] = o.astype(BF16)

    n_blocks = ROWS // (ROW_CHUNK * cpb)
    block(0, False)
    if n_blocks == 2:
        block(1, True)
    else:
        def loop_block(blk, carry):
            block(blk, True)
            return carry

        lax.fori_loop(1, n_blocks, loop_block, 0)
    for idx in range((n_blocks - 1) * cpb, n_blocks * cpb):
        for part in range(N_HEADS):
            out_proj(pl.ds(idx * ROW_CHUNK, ROW_CHUNK), part)


def _mixer_kernel(*refs, layer, nb, seq, has_ctx, lam_init, cache_slot, n_passthrough, n_cast):
    it = iter(refs)
    r = types.SimpleNamespace(cache_slot=cache_slot, layer_row=slice(layer, layer + 1))
    x_ref, r.mod, r.g1, r.w_in = next(it), next(it), next(it), next(it)
    if has_ctx:
        r.rope_c, r.rope_s, r.ck, r.cv = next(it), next(it), next(it), next(it)
    (r.lam, r.subln, cnt_ref, r.wpool, r.pscale, r.sgu_g, r.wsgu, r.bsgu, r.w_out) = (next(it) for _ in range(9))
    for _ in range(n_passthrough):
        next(it)
    cast_in = [next(it) for _ in range(n_cast)]
    r.out = next(it)
    if not has_ctx:
        r.kc_out, r.vc_out = next(it), next(it)
    cast_out = [next(it) for _ in range(n_cast)]
    pool_scr, r.q_scr, k_scr, v_scr, r.u_scr, r.vs_scr, r.y_scr = (next(it) for _ in range(7))
    for src, dst in zip(cast_in, cast_out):
        dst[...] = src[...].astype(BF16)
    _mixer_body(r, x_ref, cnt_ref, pool_scr, k_scr, v_scr, nb=nb, seq=seq, has_ctx=has_ctx, lam_init=lam_init,
                cpb=CHUNKS_PER_BLOCK)


def _mixer_call(x2d, mod4, layer, seq, has_ctx, lam_init, params, big, ctx=None, new_cache=None, cast_f32=()):
    n_rows = x2d.shape[0]
    nb = ROWS // seq
    n_seq = n_rows // seq
    kv_len = seq + (PAST_LEN if has_ctx else 0)

    if has_ctx:
        mod_map = lambda i: (layer, 1 + i, 0, 0)
    else:
        mod_map = lambda i: (layer, 0, 0, 0)

    operands = [x2d, mod4, params["norm1_g"], big["w_in"]]
    in_specs = [
        pl.BlockSpec((ROWS, D_MODEL), lambda i: (i, 0)),
        pl.BlockSpec((None, None, N_MOD, D_MODEL), mod_map),
        _const_spec((DEPTH, D_MODEL)),
        _const_spec((D_MODEL, IN_WIDTH)),
    ]
    if has_ctx:
        operands += list(ctx)
        in_specs += [
            _const_spec((seq, HEAD_COLS)),
            _const_spec((seq, HEAD_COLS)),
            pl.BlockSpec((None, None, PAST_LEN * N_HEADS, HEAD_COLS), lambda i: (i, layer, 0, 0)),
            pl.BlockSpec((None, None, PAST_LEN * N_HEADS, V_DIM), lambda i: (i, layer, 0, 0)),
        ]
    operands += [params["lam"], params["subln_g"], _pool_counts(seq), params["w_pool"], params["pool_scale"],
                 params["sgu_norm_g"], params["w_sgu"], params["b_sgu"], big["w_out"]]
    in_specs += [
        _layer_spec((8, 128), layer),
        _const_spec((DEPTH, V_DIM)),
        _const_spec((seq, POOL_WIDTH)),
        _layer_spec((POOL_WIDTH, POOL_WIDTH), layer),
        _const_spec((DEPTH, POOL_WIDTH)),
        _const_spec((DEPTH, SGU_WIDTH)),
        _layer_spec((2, 2 * CHUNK, CHUNK), layer),
        _layer_spec((CHUNK, SGU_WIDTH), layer),
        _const_spec((MIX_WIDTH, D_MODEL)),
    ]

    out_shape = [jax.ShapeDtypeStruct((n_rows, D_MODEL), F32)]
    out_specs = [pl.BlockSpec((ROWS, D_MODEL), lambda i: (i, 0))]
    aliases = {}
    cache_slot = 0
    if not has_ctx:
        out_shape += [jax.ShapeDtypeStruct((n_seq, DEPTH, seq * N_HEADS, HEAD_COLS), F32),
                      jax.ShapeDtypeStruct((n_seq, DEPTH, seq * N_HEADS, V_DIM), F32)]
        if new_cache is None:
            cache_slot = layer
            out_specs += [pl.BlockSpec((nb, DEPTH, seq * N_HEADS, HEAD_COLS), lambda i: (i, 0, 0, 0)),
                          pl.BlockSpec((nb, DEPTH, seq * N_HEADS, V_DIM), lambda i: (i, 0, 0, 0))]
        else:
            out_specs += [pl.BlockSpec((nb, 1, seq * N_HEADS, HEAD_COLS), lambda i: (i, layer, 0, 0)),
                          pl.BlockSpec((nb, 1, seq * N_HEADS, V_DIM), lambda i: (i, layer, 0, 0))]
            aliases = {len(operands): 1, len(operands) + 1: 2}
            operands += list(new_cache)
            in_specs += [pl.BlockSpec(memory_space=pl.ANY)] * 2
    n_steps = n_rows // ROWS
    for w in cast_f32:
        _, rows, cols = w.shape
        slab = rows // n_steps
        assert slab * n_steps == rows and slab % 16 == 0
        operands.append(w)
        in_specs.append(pl.BlockSpec((None, slab, cols), lambda i: (layer, i, 0)))
        out_shape.append(jax.ShapeDtypeStruct((rows, cols), BF16))
        out_specs.append(pl.BlockSpec((slab, cols), lambda i: (i, 0)))

    scratch = [
        pltpu.VMEM((nb, seq + 2 * POOL_HALO, POOL_WIDTH), F32),
        pltpu.VMEM((ROWS // ROW_CHUNK, 2 * ROW_CHUNK, QK_WIDTH), BF16),
        pltpu.VMEM((nb, QK_WIDTH, kv_len), BF16),
        pltpu.VMEM((nb, kv_len, ATTN_WIDTH), BF16),
        pltpu.VMEM((ROWS, SGU_WIDTH), F32),
        pltpu.VMEM((ROWS, SGU_WIDTH), BF16),
        pltpu.VMEM((ROWS, MIX_WIDTH), BF16),
    ]
    kernel = functools.partial(_mixer_kernel, layer=layer, nb=nb, seq=seq, has_ctx=has_ctx, lam_init=lam_init,
                               cache_slot=cache_slot, n_passthrough=len(aliases), n_cast=len(cast_f32))
    return pl.pallas_call(
        kernel,
        out_shape=out_shape,
        grid=(n_rows // ROWS,),
        in_specs=in_specs,
        out_specs=out_specs,
        scratch_shapes=scratch,
        input_output_aliases=aliases,
        compiler_params=pltpu.CompilerParams(
            dimension_semantics=("arbitrary",), vmem_limit_bytes=VMEM_LIMIT),
        name="mixer_ctx" if has_ctx else "mixer_prompt",
    )(*operands)


def _ffn_block(x_ref, out_ref, mod_ref, g2_ref, w1_ref, w2_ref, gf_ref, act_scr, final_norm, layer):
    shift2, scale2, gate2 = mod_ref[3:4, :], mod_ref[4:5, :], mod_ref[5:6, :]
    x = x_ref[...]
    h = _rms_mod(x, g2_ref[layer:layer + 1, :], scale2, shift2).astype(BF16)
    for lo, hi in ((0, FF_SPLIT), (FF_SPLIT, D_FF)):
        gate = _dot(h, w1_ref[:, lo:hi])
        up = _dot(h, w1_ref[:, D_FF + lo:D_FF + hi])
        act_scr[:, lo:hi] = (gate * _sigmoid(gate) * up).astype(BF16)
    y = x + gate2 * _dot(act_scr[...], w2_ref[...])
    if final_norm:
        ms = jnp.mean(y * y, axis=-1, keepdims=True)
        y = y * lax.rsqrt(ms + EPS) * gf_ref[...]
    out_ref[...] = y


def _ffn_kernel(*refs, layer, n_prompt_steps, final_norm, n_cast):
    xp_ref, xs_ref, mod_ref, g2_ref, w1_ref, w2_ref, gf_ref = refs[:7]
    cast_in = refs[7:7 + n_cast]
    yp_ref, ys_ref = refs[7 + n_cast:9 + n_cast]
    cast_out = refs[9 + n_cast:9 + 2 * n_cast]
    act_scr = refs[9 + 2 * n_cast]
    shared = (mod_ref, g2_ref, w1_ref, w2_ref, gf_ref, act_scr, final_norm, layer)
    is_prompt = pl.program_id(0) < n_prompt_steps

    @pl.when(is_prompt)
    def _():
        _ffn_block(xp_ref, yp_ref, *shared)

    @pl.when(jnp.logical_not(is_prompt))
    def _():
        for src, dst in zip(cast_in, cast_out):
            dst[...] = src[...].astype(BF16)
        _ffn_block(xs_ref, ys_ref, *shared)


def _ffn_call(xp, xs, mod4, layer, sample_seq, params, big, final_norm, next_f32=()):
    n_p, n_s = xp.shape[0] // FFN_ROWS, xs.shape[0] // FFN_ROWS
    steps_per_seq = sample_seq // FFN_ROWS

    def prompt_step(i):
        return jnp.minimum(i, n_p - 1)

    def sample_step(i):
        return jnp.maximum(i - n_p, 0)

    def mod_map(i):
        return (layer, jnp.where(i < n_p, 0, 1 + sample_step(i) // steps_per_seq), 0, 0)

    in_specs = [
        pl.BlockSpec((FFN_ROWS, D_MODEL), lambda i: (prompt_step(i), 0)),
        pl.BlockSpec((FFN_ROWS, D_MODEL), lambda i: (sample_step(i), 0)),
        pl.BlockSpec((None, None, N_MOD, D_MODEL), mod_map),
        _const_spec((DEPTH, D_MODEL)),
        _const_spec((D_MODEL, 2 * D_FF)),
        _const_spec((D_FF, D_MODEL)),
        _const_spec((1, D_MODEL)),
    ]
    out_shape = [jax.ShapeDtypeStruct(xp.shape, F32), jax.ShapeDtypeStruct(xs.shape, F32)]
    out_specs = [pl.BlockSpec((FFN_ROWS, D_MODEL), lambda i: (prompt_step(i), 0)),
                 pl.BlockSpec((FFN_ROWS, D_MODEL), lambda i: (sample_step(i), 0))]
    for w in next_f32:
        _, rows, cols = w.shape
        slab = rows // n_s
        assert slab * n_s == rows and slab % 16 == 0
        in_specs.append(pl.BlockSpec((None, slab, cols), lambda i: (layer + 1, sample_step(i), 0)))
        out_shape.append(jax.ShapeDtypeStruct((rows, cols), BF16))
        out_specs.append(pl.BlockSpec((slab, cols), lambda i: (sample_step(i), 0)))

    kernel = functools.partial(_ffn_kernel, layer=layer, n_prompt_steps=n_p, final_norm=final_norm, n_cast=len(next_f32))
    return pl.pallas_call(
        kernel,
        out_shape=out_shape,
        grid=(n_p + n_s,),
        in_specs=in_specs,
        out_specs=out_specs,
        scratch_shapes=[pltpu.VMEM((FFN_ROWS, D_FF), BF16)],
        compiler_params=pltpu.CompilerParams(
            dimension_semantics=("arbitrary",), vmem_limit_bytes=VMEM_LIMIT),
        name="ffn",
    )(xp, xs, mod4, params["norm2_g"], big["w_ffn_in"], big["w_ffn_out"], params["final_g"], *next_f32)


def _rope_tables(seq):
    n_rows = seq // GRID_W
    rows = np.repeat(np.arange(n_rows), GRID_W).astype(np.float32)
    cols = np.tile(np.arange(GRID_W), n_rows).astype(np.float32)
    inv = 1.0 / (ROPE_BASE ** (np.arange(0, ROPE_AXIS_DIM, 2, dtype=np.float32) / ROPE_AXIS_DIM))
    ar, ac = rows[:, None] * inv[None], cols[:, None] * inv[None]
    cos_parts, sin_parts = [], []
    for ang in (ar, ac):
        cos_parts += [np.cos(ang), np.cos(ang)]
        sin_parts += [-np.sin(ang), np.sin(ang)]
    cos64 = np.concatenate(cos_parts, axis=1)
    sin64 = np.concatenate(sin_parts, axis=1)
    cos = np.concatenate([cos64, cos64], axis=1).astype(np.float32)
    sin = np.concatenate([sin64, sin64], axis=1).astype(np.float32)
    return jnp.asarray(cos), jnp.asarray(sin)


def _pool_counts(seq):
    t = np.arange(seq)
    cols = []
    for w in POOL_WINDOWS:
        lo = np.clip(t - w // 2, 0, seq)
        hi = np.clip(t + w - w // 2, 0, seq)
        cols.append(np.repeat((hi - lo).astype(np.float32)[:, None], POOL_GROUP, axis=1))
    return jnp.asarray(np.concatenate(cols, axis=1))


def _small_params(norm1_g, w_pool, pool_scale, lam_q1, lam_k1, lam_q2, lam_k2, subln_g,
                  sgu_norm_g, w_sgu, b_sgu, norm2_g, final_g):
    n_groups = len(POOL_WINDOWS)
    eye = jnp.eye(n_groups, dtype=F32)
    w_pool_bd = (eye[None, :, None, :, None] * w_pool[:, :, :, None, :]).reshape(DEPTH, POOL_WIDTH, POOL_WIDTH)
    lam = jnp.pad(jnp.stack([lam_q1, lam_k1, lam_q2, lam_k2], axis=1), ((0, 0), (0, 4), (0, 128 - QK_DIM)))
    return {
        "norm1_g": norm1_g,
        "lam": lam,
        "subln_g": subln_g,
        "w_pool": w_pool_bd.astype(BF16),
        "pool_scale": pool_scale,
        "sgu_norm_g": sgu_norm_g,
        "w_sgu": w_sgu.reshape(DEPTH, 2, 2 * CHUNK, CHUNK).astype(BF16),
        "b_sgu": jnp.repeat(jnp.swapaxes(b_sgu, 1, 2), SGU_WIDTH // 4, axis=2),
        "norm2_g": norm2_g,
        "final_g": final_g[None, :],
    }


BIG_WEIGHTS = ("w_in", "w_out", "w_ffn_in", "w_ffn_out")


def kernel(x_prompt, x_sample, cache_k, cache_v, c, c_ctx, norm1_g, w_ada, b_ada, w_in, w_pool, pool_scale, lam_q1, lam_k1, lam_q2, lam_k2, subln_g, sgu_norm_g, w_sgu, b_sgu, w_out, norm2_g, w_ffn_in, w_ffn_out, final_g):
    batch, seq, _ = x_prompt.shape
    dec_batch, dec_seq, _ = x_sample.shape
    assert ROWS % seq == 0 and dec_seq == ROWS and 1 + dec_batch <= COND_ROWS

    mod4 = _ada_modulation(c_ctx, c, w_ada, b_ada).reshape(DEPTH, COND_ROWS, N_MOD, D_MODEL)

    ctx = _rope_tables(dec_seq) + (cache_k.reshape(dec_batch, DEPTH, PAST_LEN * N_HEADS, HEAD_COLS),
                                   cache_v.reshape(dec_batch, DEPTH, PAST_LEN * N_HEADS, V_DIM))
    params = _small_params(norm1_g, w_pool, pool_scale, lam_q1, lam_k1, lam_q2, lam_k2, subln_g,
                           sgu_norm_g, w_sgu, b_sgu, norm2_g, final_g)

    big_f32 = (w_in, w_out, w_ffn_in, w_ffn_out)
    big = {"w_in": w_in[0].astype(BF16), "w_out": w_out[0].astype(BF16)}

    xp = x_prompt.reshape(batch * seq, D_MODEL)
    xs = x_sample.reshape(dec_batch * dec_seq, D_MODEL)
    new_cache = None
    for l in range(DEPTH):
        lam_init = 0.8 - 0.6 * math.exp(-0.3 * l)
        last = l == DEPTH - 1

        xp, *new_cache = _mixer_call(xp, mod4, l, seq, False, lam_init, params, big, new_cache=new_cache)
        if l == 0:
            xs, big["w_ffn_in"], big["w_ffn_out"] = _mixer_call(
                xs, mod4, l, dec_seq, True, lam_init, params, big, ctx=ctx, cast_f32=(w_ffn_in, w_ffn_out))
        else:
            xs, = _mixer_call(xs, mod4, l, dec_seq, True, lam_init, params, big, ctx=ctx)
        xp, xs, *next_big = _ffn_call(xp, xs, mod4, l, dec_seq, params, big, last,
                                      next_f32=() if last else big_f32)
        big = dict(zip(BIG_WEIGHTS, next_big))

    y_prompt = xp.reshape(batch, seq, D_MODEL)
    y_sample = xs.reshape(dec_batch, dec_seq, D_MODEL)
    new_cache_k = new_cache[0].reshape(batch, DEPTH, seq, N_HEADS, 2 * QK_DIM)
    new_cache_v = new_cache[1].reshape(batch, DEPTH, seq, N_HEADS, V_DIM)
    return (y_prompt, y_sample, new_cache_k, new_cache_v)
```

```python
import functools
import math
import types

import numpy as np
import jax
import jax.numpy as jnp
from jax import lax
from jax.experimental import pallas as pl
from jax.experimental.pallas import tpu as pltpu

D_MODEL = 1024
DEPTH = 2
GRID_W = 64
POOL_WINDOWS = (2, 4, 8, 16)
POOL_WIDTH = 256
POOL_GROUP = 64
POOL_HALO = 8
ATTN_WIDTH = 512
N_HEADS = 4
V_DIM = 128
QK_DIM = 64
HEAD_COLS = 2 * QK_DIM
ROPE_BASE = 10000.0
ROPE_AXIS_DIM = 32
ROPE_HALF = ROPE_AXIS_DIM // 2
CHUNK = 128
SGU_WIDTH = 256
QK_WIDTH = 512
IN_WIDTH = 2304
MIX_WIDTH = 1024
D_FF = 2816
N_MOD = 6
EPS = 1e-6
PAST_LEN = 256

C_POOL, C_Q, C_K, C_V, C_UV = 0, 256, 768, 1280, 1792
Y_POOL, Y_ATTN, Y_SGU = 0, 256, 768

ROWS = 1024
ROW_CHUNK = 256
ATTN_LOOKAHEAD = 2
CHUNKS_PER_BLOCK = 2
FFN_ROWS = 512
MXU_COLS = 256
FF_SPLIT = (D_FF // MXU_COLS + 1) // 2 * MXU_COLS
COND_ROWS = 16
ADA_COLS = 3072
VMEM_LIMIT = 58 * 1024 * 1024

BF16 = jnp.bfloat16
F32 = jnp.float32


def _dot(a, b):
    return jnp.dot(a, b, preferred_element_type=F32)


def _sigmoid(x):
    return 1.0 / (1.0 + jnp.exp(-x))


def _rms_mod(x, g, scale, shift):
    ms = jnp.mean(x * x, axis=-1, keepdims=True)
    return x * lax.rsqrt(ms + EPS) * (g * (1.0 + scale)) + shift


def _gelu_tanh(x):
    c = math.sqrt(2.0 / math.pi)
    return x * (0.5 * (1.0 + jnp.tanh(c * (x + 0.044715 * (x * x * x)))))


def _const_spec(shape):
    zeros = (0,) * len(shape)
    return pl.BlockSpec(shape, lambda *_: zeros, pipeline_mode=pl.Buffered(1))


def _layer_spec(shape, layer):
    index = (layer,) + (0,) * len(shape)
    return pl.BlockSpec((None,) + tuple(shape), lambda *_: index, pipeline_mode=pl.Buffered(1))


def _ada_kernel(cond_ref, w_ref, b_ref, out_ref):
    cond = cond_ref[...]
    s = (cond * _sigmoid(cond)).astype(BF16)
    out_ref[...] = _dot(s, w_ref[...].astype(BF16)) + b_ref[...]


def _ada_modulation(cond, w_ada, b_ada):
    n_cols = N_MOD * D_MODEL
    return pl.pallas_call(
        _ada_kernel,
        out_shape=jax.ShapeDtypeStruct((DEPTH, COND_ROWS, n_cols), F32),
        grid=(DEPTH, n_cols // ADA_COLS),
        in_specs=[
            pl.BlockSpec((COND_ROWS, D_MODEL), lambda l, j: (0, 0)),
            pl.BlockSpec((None, D_MODEL, ADA_COLS), lambda l, j: (l, 0, j)),
            pl.BlockSpec((None, 1, ADA_COLS), lambda l, j: (l, 0, j)),
        ],
        out_specs=pl.BlockSpec((None, COND_ROWS, ADA_COLS), lambda l, j: (l, 0, j)),
        compiler_params=pltpu.CompilerParams(
            dimension_semantics=("arbitrary", "arbitrary"), vmem_limit_bytes=VMEM_LIMIT),
        name="ada_modulation",
    )(cond, w_ada, b_ada.reshape(DEPTH, 1, n_cols))


def _mixer_body(r, x_ref, cnt_ref, pool_scr, k_scr, v_scr, *, nb, seq, has_ctx, lam_init, cpb):
    kv_base = PAST_LEN if has_ctx else 0
    shift1, scale1, gate1 = r.mod[0:1, :], r.mod[1:2, :], r.mod[2:3, :]

    lane_head = lax.broadcasted_iota(jnp.int32, (ROW_CHUNK, HEAD_COLS), 1)
    first_half = lane_head < QK_DIM

    for b in range(nb):
        pool_scr[b, 0:POOL_HALO, :] = jnp.zeros((POOL_HALO, POOL_WIDTH), F32)
        pool_scr[b, POOL_HALO + seq:POOL_HALO + seq + POOL_HALO, :] = jnp.zeros((POOL_HALO, POOL_WIDTH), F32)
    if not has_ctx:
        for slot in range(r.kc_out.shape[1]):
            if slot != r.cache_slot:
                r.kc_out[:, slot] = jnp.zeros((nb,) + r.kc_out.shape[2:], F32)
                r.vc_out[:, slot] = jnp.zeros((nb,) + r.vc_out.shape[2:], F32)
    if has_ctx:
        for hd in range(N_HEADS):
            cols = slice(hd * HEAD_COLS, (hd + 1) * HEAD_COLS)
            k_scr[0, cols, 0:PAST_LEN] = r.ck[pl.ds(hd, PAST_LEN, stride=N_HEADS), :].T.astype(BF16)
            v_scr[0, 0:PAST_LEN, cols] = r.cv[pl.ds(hd, PAST_LEN, stride=N_HEADS), :].astype(BF16)
        lane_q = lax.broadcasted_iota(jnp.int32, (ROW_CHUNK, QK_WIDTH), 1)
        rope_low = (lane_q % ROPE_AXIS_DIM) < ROPE_HALF

        def rope(t, pos0):
            partner = jnp.where(rope_low,
                                pltpu.roll(t, QK_WIDTH - ROPE_HALF, 1),
                                pltpu.roll(t, ROPE_HALF, 1))
            cos = r.rope_c[pos0:pos0 + ROW_CHUNK, :]
            sin = r.rope_s[pos0:pos0 + ROW_CHUNK, :]
            cos = jnp.concatenate([cos] * N_HEADS, axis=1)
            sin = jnp.concatenate([sin] * N_HEADS, axis=1)
            return t * cos + partner * sin

    for c in range(ROWS // ROW_CHUNK):
        r0 = c * ROW_CHUNK
        b, pos0 = r0 // seq, r0 % seq
        rows = slice(r0, r0 + ROW_CHUNK)
        if r.h is None:
            h = _rms_mod(x_ref[rows, :], r.g1[r.layer_row, :], scale1, shift1).astype(BF16)
        else:
            h = r.h[rows, :]

        pool_scr[b, POOL_HALO + pos0:POOL_HALO + pos0 + ROW_CHUNK, :] = _dot(h, r.w_in[:, C_POOL:C_Q])

        q = _dot(h, r.w_in[:, C_Q:C_K])
        k = _dot(h, r.w_in[:, C_K:C_V])
        if has_ctx:
            q = rope(q, pos0)
            k = rope(k, pos0)
        else:
            for hd in range(N_HEADS):
                r.kc_out[b, r.cache_slot, pl.ds(N_HEADS * pos0 + hd, ROW_CHUNK, stride=N_HEADS), :] = (
                    k[:, hd * HEAD_COLS:(hd + 1) * HEAD_COLS])
        q = q * (QK_DIM ** -0.5 * math.log2(math.e))
        for hd in range(N_HEADS):
            cols = slice(hd * HEAD_COLS, (hd + 1) * HEAD_COLS)
            qh = q[:, cols]
            r.q_scr[c, 0:ROW_CHUNK, cols] = jnp.where(first_half, qh, 0.0).astype(BF16)
            r.q_scr[c, ROW_CHUNK:2 * ROW_CHUNK, cols] = jnp.where(first_half, 0.0, qh).astype(BF16)
        k_scr[b, :, kv_base + pos0:kv_base + pos0 + ROW_CHUNK] = k.T.astype(BF16)

        v = _dot(h, r.w_in[:, C_V:C_UV])
        if not has_ctx:
            for hd in range(N_HEADS):
                r.vc_out[b, r.cache_slot, pl.ds(N_HEADS * pos0 + hd, ROW_CHUNK, stride=N_HEADS), :] = (
                    v[:, hd * V_DIM:(hd + 1) * V_DIM])
        v_scr[b, kv_base + pos0:kv_base + pos0 + ROW_CHUNK, :] = v.astype(BF16)

        uv = _gelu_tanh(_dot(h, r.w_in[:, C_UV:IN_WIDTH]))
        r.u_scr[rows, :] = uv[:, :SGU_WIDTH]
        vv = uv[:, SGU_WIDTH:]
        mu = jnp.mean(vv, axis=-1, keepdims=True)
        vc = vv - mu
        vn = vc * lax.rsqrt(jnp.mean(vc * vc, axis=-1, keepdims=True) + EPS) * r.sgu_g[r.layer_row, :]
        r.vs_scr[rows, :] = vn.astype(BF16)

    lane_pool = lax.broadcasted_iota(jnp.int32, (ROW_CHUNK, 2 * POOL_GROUP), 1)
    narrow = lane_pool < POOL_GROUP
    for c in range(ROWS // ROW_CHUNK):
        r0 = c * ROW_CHUNK
        b, pos0 = r0 // seq, r0 % seq
        base = POOL_HALO + pos0
        pooled = []
        for j, (w_small, w_big) in enumerate(((POOL_WINDOWS[0], POOL_WINDOWS[1]),
                                              (POOL_WINDOWS[2], POOL_WINDOWS[3]))):
            cols = slice(j * 2 * POOL_GROUP, (j + 1) * 2 * POOL_GROUP)
            s_small = jnp.zeros((ROW_CHUNK, 2 * POOL_GROUP), F32)
            s_rest = jnp.zeros((ROW_CHUNK, 2 * POOL_GROUP), F32)
            for d in range(-(w_big // 2), w_big - w_big // 2):
                t = pool_scr[b, base + d:base + d + ROW_CHUNK, cols]
                if -(w_small // 2) <= d < w_small - w_small // 2:
                    s_small = s_small + t
                else:
                    s_rest = s_rest + t
            win_sum = jnp.where(narrow, s_small, s_small + s_rest)
            centre = pool_scr[b, base:base + ROW_CHUNK, cols]
            pooled.append(win_sum / cnt_ref[pos0:pos0 + ROW_CHUNK, cols] - centre)
        pooled = jnp.concatenate(pooled, axis=1).astype(BF16)
        y_a = _dot(pooled, r.wpool[...]) * r.pscale[r.layer_row, :]
        r.y_scr[r0:r0 + ROW_CHUNK, Y_POOL:Y_ATTN] = y_a.astype(BF16)

    lq1, lk1, lq2, lk2 = r.lam[0:1, :], r.lam[1:2, :], r.lam[2:3, :], r.lam[3:4, :]
    lam = (jnp.exp(jnp.sum(lq1 * lk1, axis=-1, keepdims=True))
           - jnp.exp(jnp.sum(lq2 * lk2, axis=-1, keepdims=True)) + lam_init)
    n_qb = seq // ROW_CHUNK

    lane_sgu = lax.broadcasted_iota(jnp.int32, (CHUNK, 2 * POOL_GROUP), 1)
    sgu_first = lane_sgu < (SGU_WIDTH // 4)
    out_cols = MIX_WIDTH // N_HEADS

    def out_proj(rows, part):
        cols = slice(part * out_cols, (part + 1) * out_cols)
        y = _dot(r.y_scr[rows, :], r.w_out[:, cols])
        r.out[rows, cols] = x_ref[rows, cols] + gate1[:, cols] * y

    def block(blk, with_prev):
        static = isinstance(blk, int)

        def row0(chunk):
            return chunk * ROW_CHUNK if static else pl.multiple_of(chunk * ROW_CHUNK, ROW_CHUNK)

        chunks = [blk * cpb + j for j in range(cpb)]

        for idx in chunks:
            for half in range(ROW_CHUNK // CHUNK):
                sub = pl.ds(row0(idx) + half * CHUNK, CHUNK)
                for j in range(2):
                    cols = slice(j * 128, (j + 1) * 128)
                    t = _dot(r.wsgu[j], r.vs_scr[sub, cols])
                    mixed = jnp.where(sgu_first, t[:CHUNK, :], t[CHUNK:, :]) + r.bsgu[:, cols]
                    y_c = r.u_scr[sub, cols] * mixed
                    r.y_scr[sub, Y_SGU + j * 128:Y_SGU + (j + 1) * 128] = y_c.astype(BF16)

        units = [(idx, hd) for idx in chunks for hd in range(N_HEADS)]
        prev = [(pl.ds(row0(idx - cpb), ROW_CHUNK), part) for idx in chunks for part in range(N_HEADS)]

        def scores(idx, hd):
            cols = slice(hd * HEAD_COLS, (hd + 1) * HEAD_COLS)
            s = _dot(r.q_scr[idx, :, cols], k_scr[idx // n_qb, cols, :])
            return s[:ROW_CHUNK], s[ROW_CHUNK:]

        pending = [scores(*u) for u in units[:ATTN_LOOKAHEAD]]
        for n, (idx, hd) in enumerate(units):
            cols = slice(hd * HEAD_COLS, (hd + 1) * HEAD_COLS)
            s1, s2 = pending.pop(0)
            if n + ATTN_LOOKAHEAD < len(units):
                pending.append(scores(*units[n + ATTN_LOOKAHEAD]))
            e1 = jnp.exp2(s1 - jnp.max(s1, axis=-1, keepdims=True))
            e2 = jnp.exp2(s2 - jnp.max(s2, axis=-1, keepdims=True))
            l1 = jnp.sum(e1, axis=-1, keepdims=True)
            l2 = jnp.sum(e2, axis=-1, keepdims=True)
            p = (e1 - e2 * (lam * l1 / l2)).astype(BF16)
            o = _dot(p, v_scr[idx // n_qb, :, cols]) * (1.0 / l1)
            if with_prev:
                out_proj(*prev[n])
            o = o * lax.rsqrt(jnp.mean(o * o, axis=-1, keepdims=True) + EPS) * r.subln[r.layer_row, :]
            o = o * (1.0 - lam_init)
            r.y_scr[pl.ds(row0(idx), ROW_CHUNK), Y_ATTN + hd * V_DIM:Y_ATTN + (hd + 1) * V_DIM] = o.astype(BF16)

    n_blocks = ROWS // (ROW_CHUNK * cpb)
    block(0, False)
    if n_blocks == 2:
        block(1, True)
    else:
        def loop_block(blk, carry):
            block(blk, True)
            return carry

        lax.fori_loop(1, n_blocks, loop_block, 0)
    for idx in range((n_blocks - 1) * cpb, n_blocks * cpb):
        for part in range(N_HEADS):
            out_proj(pl.ds(idx * ROW_CHUNK, ROW_CHUNK), part)


def _mixer_kernel(*refs, layer, nb, seq, has_ctx, lam_init, cache_slot, n_passthrough, n_cast, has_h):
    it = iter(refs)
    r = types.SimpleNamespace(cache_slot=cache_slot, layer_row=slice(layer, layer + 1))
    x_ref, r.mod, r.g1, r.w_in = next(it), next(it), next(it), next(it)
    r.h = next(it) if has_h else None
    if has_ctx:
        r.rope_c, r.rope_s, r.ck, r.cv = next(it), next(it), next(it), next(it)
    (r.lam, r.subln, cnt_ref, r.wpool, r.pscale, r.sgu_g, r.wsgu, r.bsgu, r.w_out) = (next(it) for _ in range(9))
    for _ in range(n_passthrough):
        next(it)
    cast_in = [next(it) for _ in range(n_cast)]
    r.out = next(it)
    if not has_ctx:
        r.kc_out, r.vc_out = next(it), next(it)
    cast_out = [next(it) for _ in range(n_cast)]
    pool_scr, r.q_scr, k_scr, v_scr, r.u_scr, r.vs_scr, r.y_scr = (next(it) for _ in range(7))
    for src, dst in zip(cast_in, cast_out):
        dst[...] = src[...].astype(BF16)
    _mixer_body(r, x_ref, cnt_ref, pool_scr, k_scr, v_scr, nb=nb, seq=seq, has_ctx=has_ctx, lam_init=lam_init,
                cpb=CHUNKS_PER_BLOCK)


def _mixer_call(x2d, mod4, layer, seq, has_ctx, lam_init, params, big, ctx=None, new_cache=None, cast_f32=(),
                h_pre=None):
    n_rows = x2d.shape[0]
    nb = ROWS // seq
    n_seq = n_rows // seq
    kv_len = seq + (PAST_LEN if has_ctx else 0)

    if has_ctx:
        mod_map = lambda i: (layer, 1 + i, 0, 0)
    else:
        mod_map = lambda i: (layer, 0, 0, 0)

    operands = [x2d, mod4, params["norm1_g"], big["w_in"]]
    in_specs = [
        pl.BlockSpec((ROWS, D_MODEL), lambda i: (i, 0)),
        pl.BlockSpec((None, None, N_MOD, D_MODEL), mod_map),
        _const_spec((DEPTH, D_MODEL)),
        _const_spec((D_MODEL, IN_WIDTH)),
    ]
    if h_pre is not None:
        operands.append(h_pre)
        in_specs.append(pl.BlockSpec((ROWS, D_MODEL), lambda i: (i, 0)))
    if has_ctx:
        operands += list(ctx)
        in_specs += [
            _const_spec((seq, HEAD_COLS)),
            _const_spec((seq, HEAD_COLS)),
            pl.BlockSpec((None, None, PAST_LEN * N_HEADS, HEAD_COLS), lambda i: (i, layer, 0, 0)),
            pl.BlockSpec((None, None, PAST_LEN * N_HEADS, V_DIM), lambda i: (i, layer, 0, 0)),
        ]
    operands += [params["lam"], params["subln_g"], _pool_counts(seq), params["w_pool"], params["pool_scale"],
                 params["sgu_norm_g"], params["w_sgu"], params["b_sgu"], big["w_out"]]
    in_specs += [
        _layer_spec((8, 128), layer),
        _const_spec((DEPTH, V_DIM)),
        _const_spec((seq, POOL_WIDTH)),
        _layer_spec((POOL_WIDTH, POOL_WIDTH), layer),
        _const_spec((DEPTH, POOL_WIDTH)),
        _const_spec((DEPTH, SGU_WIDTH)),
        _layer_spec((2, 2 * CHUNK, CHUNK), layer),
        _layer_spec((CHUNK, SGU_WIDTH), layer),
        _const_spec((MIX_WIDTH, D_MODEL)),
    ]

    out_shape = [jax.ShapeDtypeStruct((n_rows, D_MODEL), F32)]
    out_specs = [pl.BlockSpec((ROWS, D_MODEL), lambda i: (i, 0))]
    aliases = {}
    cache_slot = 0
    if not has_ctx:
        out_shape += [jax.ShapeDtypeStruct((n_seq, DEPTH, seq * N_HEADS, HEAD_COLS), F32),
                      jax.ShapeDtypeStruct((n_seq, DEPTH, seq * N_HEADS, V_DIM), F32)]
        if new_cache is None:
            cache_slot = layer
            out_specs += [pl.BlockSpec((nb, DEPTH, seq * N_HEADS, HEAD_COLS), lambda i: (i, 0, 0, 0)),
                          pl.BlockSpec((nb, DEPTH, seq * N_HEADS, V_DIM), lambda i: (i, 0, 0, 0))]
        else:
            out_specs += [pl.BlockSpec((nb, 1, seq * N_HEADS, HEAD_COLS), lambda i: (i, layer, 0, 0)),
                          pl.BlockSpec((nb, 1, seq * N_HEADS, V_DIM), lambda i: (i, layer, 0, 0))]
            aliases = {len(operands): 1, len(operands) + 1: 2}
            operands += list(new_cache)
            in_specs += [pl.BlockSpec(memory_space=pl.ANY)] * 2
    n_steps = n_rows // ROWS
    for w in cast_f32:
        _, rows, cols = w.shape
        slab = rows // n_steps
        assert slab * n_steps == rows and slab % 16 == 0
        operands.append(w)
        in_specs.append(pl.BlockSpec((None, slab, cols), lambda i: (layer, i, 0)))
        out_shape.append(jax.ShapeDtypeStruct((rows, cols), BF16))
        out_specs.append(pl.BlockSpec((slab, cols), lambda i: (i, 0)))

    scratch = [
        pltpu.VMEM((nb, seq + 2 * POOL_HALO, POOL_WIDTH), F32),
        pltpu.VMEM((ROWS // ROW_CHUNK, 2 * ROW_CHUNK, QK_WIDTH), BF16),
        pltpu.VMEM((nb, QK_WIDTH, kv_len), BF16),
        pltpu.VMEM((nb, kv_len, ATTN_WIDTH), BF16),
        pltpu.VMEM((ROWS, SGU_WIDTH), F32),
        pltpu.VMEM((ROWS, SGU_WIDTH), BF16),
        pltpu.VMEM((ROWS, MIX_WIDTH), BF16),
    ]
    kernel = functools.partial(_mixer_kernel, layer=layer, nb=nb, seq=seq, has_ctx=has_ctx, lam_init=lam_init,
                               cache_slot=cache_slot, n_passthrough=len(aliases), n_cast=len(cast_f32),
                               has_h=h_pre is not None)
    return pl.pallas_call(
        kernel,
        out_shape=out_shape,
        grid=(n_rows // ROWS,),
        in_specs=in_specs,
        out_specs=out_specs,
        scratch_shapes=scratch,
        input_output_aliases=aliases,
        compiler_params=pltpu.CompilerParams(
            dimension_semantics=("arbitrary",), vmem_limit_bytes=VMEM_LIMIT),
        name="mixer_ctx" if has_ctx else "mixer_prompt",
    )(*operands)


def _ffn_block(x_ref, out_ref, h_ref, mod_ref, g2_ref, w1_ref, w2_ref, gf_ref, act_scr, final_norm, layer, nxt):
    shift2, scale2, gate2 = mod_ref[3:4, :], mod_ref[4:5, :], mod_ref[5:6, :]
    x = x_ref[...]
    h = _rms_mod(x, g2_ref[layer:layer + 1, :], scale2, shift2).astype(BF16)
    for lo, hi in ((0, FF_SPLIT), (FF_SPLIT, D_FF)):
        gate = _dot(h, w1_ref[:, lo:hi])
        up = _dot(h, w1_ref[:, D_FF + lo:D_FF + hi])
        act_scr[:, lo:hi] = (gate * _sigmoid(gate) * up).astype(BF16)
    y = x + gate2 * _dot(act_scr[...], w2_ref[...])
    if final_norm:
        ms = jnp.mean(y * y, axis=-1, keepdims=True)
        y = y * lax.rsqrt(ms + EPS) * gf_ref[...]
    out_ref[...] = y
    if nxt is not None:
        modn_ref, g1_ref = nxt
        h_ref[...] = _rms_mod(y, g1_ref[layer + 1:layer + 2, :], modn_ref[1:2, :], modn_ref[0:1, :]).astype(BF16)


def _ffn_kernel(*refs, layer, n_prompt_steps, final_norm, n_cast, next_norm):
    it = iter(refs)
    xp_ref, xs_ref, mod_ref, g2_ref, w1_ref, w2_ref, gf_ref = (next(it) for _ in range(7))
    nxt = (next(it), next(it)) if next_norm else None
    cast_in = [next(it) for _ in range(n_cast)]
    yp_ref, ys_ref = next(it), next(it)
    hp_ref, hs_ref = (next(it), next(it)) if next_norm else (None, None)
    cast_out = [next(it) for _ in range(n_cast)]
    act_scr = next(it)
    shared = (mod_ref, g2_ref, w1_ref, w2_ref, gf_ref, act_scr, final_norm, layer, nxt)
    is_prompt = pl.program_id(0) < n_prompt_steps

    @pl.when(is_prompt)
    def _():
        _ffn_block(xp_ref, yp_ref, hp_ref, *shared)

    @pl.when(jnp.logical_not(is_prompt))
    def _():
        for src, dst in zip(cast_in, cast_out):
            dst[...] = src[...].astype(BF16)
        _ffn_block(xs_ref, ys_ref, hs_ref, *shared)


def _ffn_call(xp, xs, mod4, layer, sample_seq, params, big, final_norm, next_f32=(), next_norm=False):
    n_p, n_s = xp.shape[0] // FFN_ROWS, xs.shape[0] // FFN_ROWS
    steps_per_seq = sample_seq // FFN_ROWS

    def prompt_step(i):
        return jnp.minimum(i, n_p - 1)

    def sample_step(i):
        return jnp.maximum(i - n_p, 0)

    def mod_map(i):
        return (layer, jnp.where(i < n_p, 0, 1 + sample_step(i) // steps_per_seq), 0, 0)

    in_specs = [
        pl.BlockSpec((FFN_ROWS, D_MODEL), lambda i: (prompt_step(i), 0)),
        pl.BlockSpec((FFN_ROWS, D_MODEL), lambda i: (sample_step(i), 0)),
        pl.BlockSpec((None, None, N_MOD, D_MODEL), mod_map),
        _const_spec((DEPTH, D_MODEL)),
        _const_spec((D_MODEL, 2 * D_FF)),
        _const_spec((D_FF, D_MODEL)),
        _const_spec((1, D_MODEL)),
    ]
    operands = [xp, xs, mod4, params["norm2_g"], big["w_ffn_in"], big["w_ffn_out"], params["final_g"]]
    out_shape = [jax.ShapeDtypeStruct(xp.shape, F32), jax.ShapeDtypeStruct(xs.shape, F32)]
    out_specs = [pl.BlockSpec((FFN_ROWS, D_MODEL), lambda i: (prompt_step(i), 0)),
                 pl.BlockSpec((FFN_ROWS, D_MODEL), lambda i: (sample_step(i), 0))]
    if next_norm:
        operands += [mod4, params["norm1_g"]]
        in_specs += [pl.BlockSpec((None, None, N_MOD, D_MODEL), lambda i: (layer + 1,) + mod_map(i)[1:]),
                     _const_spec((DEPTH, D_MODEL))]
        out_shape += [jax.ShapeDtypeStruct(xp.shape, BF16), jax.ShapeDtypeStruct(xs.shape, BF16)]
        out_specs += [pl.BlockSpec((FFN_ROWS, D_MODEL), lambda i: (prompt_step(i), 0)),
                      pl.BlockSpec((FFN_ROWS, D_MODEL), lambda i: (sample_step(i), 0))]
    operands += list(next_f32)
    for w in next_f32:
        _, rows, cols = w.shape
        slab = rows // n_s
        assert slab * n_s == rows and slab % 16 == 0
        in_specs.append(pl.BlockSpec((None, slab, cols), lambda i: (layer + 1, sample_step(i), 0)))
        out_shape.append(jax.ShapeDtypeStruct((rows, cols), BF16))
        out_specs.append(pl.BlockSpec((slab, cols), lambda i: (sample_step(i), 0)))

    kernel = functools.partial(_ffn_kernel, layer=layer, n_prompt_steps=n_p, final_norm=final_norm,
                               n_cast=len(next_f32), next_norm=next_norm)
    return pl.pallas_call(
        kernel,
        out_shape=out_shape,
        grid=(n_p + n_s,),
        in_specs=in_specs,
        out_specs=out_specs,
        scratch_shapes=[pltpu.VMEM((FFN_ROWS, D_FF), BF16)],
        compiler_params=pltpu.CompilerParams(
            dimension_semantics=("arbitrary",), vmem_limit_bytes=VMEM_LIMIT),
        name="ffn",
    )(*operands)


def _rope_tables(seq):
    n_rows = seq // GRID_W
    rows = np.repeat(np.arange(n_rows), GRID_W).astype(np.float32)
    cols = np.tile(np.arange(GRID_W), n_rows).astype(np.float32)
    inv = 1.0 / (ROPE_BASE ** (np.arange(0, ROPE_AXIS_DIM, 2, dtype=np.float32) / ROPE_AXIS_DIM))
    ar, ac = rows[:, None] * inv[None], cols[:, None] * inv[None]
    cos_parts, sin_parts = [], []
    for ang in (ar, ac):
        cos_parts += [np.cos(ang), np.cos(ang)]
        sin_parts += [-np.sin(ang), np.sin(ang)]
    cos64 = np.concatenate(cos_parts, axis=1)
    sin64 = np.concatenate(sin_parts, axis=1)
    cos = np.concatenate([cos64, cos64], axis=1).astype(np.float32)
    sin = np.concatenate([sin64, sin64], axis=1).astype(np.float32)
    return jnp.asarray(cos), jnp.asarray(sin)


def _pool_counts(seq):
    t = np.arange(seq)
    cols = []
    for w in POOL_WINDOWS:
        lo = np.clip(t - w // 2, 0, seq)
        hi = np.clip(t + w - w // 2, 0, seq)
        cols.append(np.repeat((hi - lo).astype(np.float32)[:, None], POOL_GROUP, axis=1))
    return jnp.asarray(np.concatenate(cols, axis=1))


def _small_params(norm1_g, w_pool, pool_scale, lam_q1, lam_k1, lam_q2, lam_k2, subln_g,
                  sgu_norm_g, w_sgu, b_sgu, norm2_g, final_g):
    n_groups = len(POOL_WINDOWS)
    eye = jnp.eye(n_groups, dtype=F32)
    w_pool_bd = (eye[None, :, None, :, None] * w_pool[:, :, :, None, :]).reshape(DEPTH, POOL_WIDTH, POOL_WIDTH)
    lam = jnp.pad(jnp.stack([lam_q1, lam_k1, lam_q2, lam_k2], axis=1), ((0, 0), (0, 4), (0, 128 - QK_DIM)))
    return {
        "norm1_g": norm1_g,
        "lam": lam,
        "subln_g": subln_g,
        "w_pool": w_pool_bd.astype(BF16),
        "pool_scale": pool_scale,
        "sgu_norm_g": sgu_norm_g,
        "w_sgu": w_sgu.reshape(DEPTH, 2, 2 * CHUNK, CHUNK).astype(BF16),
        "b_sgu": jnp.repeat(jnp.swapaxes(b_sgu, 1, 2), SGU_WIDTH // 4, axis=2),
        "norm2_g": norm2_g,
        "final_g": final_g[None, :],
    }


BIG_WEIGHTS = ("w_in", "w_out", "w_ffn_in", "w_ffn_out")


def kernel(x_prompt, x_sample, cache_k, cache_v, c, c_ctx, norm1_g, w_ada, b_ada, w_in, w_pool, pool_scale, lam_q1, lam_k1, lam_q2, lam_k2, subln_g, sgu_norm_g, w_sgu, b_sgu, w_out, norm2_g, w_ffn_in, w_ffn_out, final_g):
    batch, seq, _ = x_prompt.shape
    dec_batch, dec_seq, _ = x_sample.shape
    assert ROWS % seq == 0 and dec_seq == ROWS and 1 + dec_batch <= COND_ROWS

    cond = jnp.concatenate([c_ctx[None, :], c, jnp.zeros((COND_ROWS - 1 - dec_batch, D_MODEL), F32)], axis=0)
    mod4 = _ada_modulation(cond, w_ada, b_ada).reshape(DEPTH, COND_ROWS, N_MOD, D_MODEL)

    ctx = _rope_tables(dec_seq) + (cache_k.reshape(dec_batch, DEPTH, PAST_LEN * N_HEADS, HEAD_COLS),
                                   cache_v.reshape(dec_batch, DEPTH, PAST_LEN * N_HEADS, V_DIM))
    params = _small_params(norm1_g, w_pool, pool_scale, lam_q1, lam_k1, lam_q2, lam_k2, subln_g,
                           sgu_norm_g, w_sgu, b_sgu, norm2_g, final_g)

    big_f32 = (w_in, w_out, w_ffn_in, w_ffn_out)
    big = {"w_in": w_in[0].astype(BF16), "w_out": w_out[0].astype(BF16)}

    xp = x_prompt.reshape(batch * seq, D_MODEL)
    xs = x_sample.reshape(dec_batch * dec_seq, D_MODEL)
    new_cache = None
    hp = hs = None
    for l in range(DEPTH):
        lam_init = 0.8 - 0.6 * math.exp(-0.3 * l)
        last = l == DEPTH - 1

        xp, *new_cache = _mixer_call(xp, mod4, l, seq, False, lam_init, params, big, new_cache=new_cache, h_pre=hp)
        if l == 0:
            xs, big["w_ffn_in"], big["w_ffn_out"] = _mixer_call(
                xs, mod4, l, dec_seq, True, lam_init, params, big, ctx=ctx, cast_f32=(w_ffn_in, w_ffn_out))
        else:
            xs, = _mixer_call(xs, mod4, l, dec_seq, True, lam_init, params, big, ctx=ctx, h_pre=hs)
        if last:
            xp, xs = _ffn_call(xp, xs, mod4, l, dec_seq, params, big, True)
        else:
            xp, xs, hp, hs, *next_big = _ffn_call(xp, xs, mod4, l, dec_seq, params, big, False,
                                                  next_f32=big_f32, next_norm=True)
            big = dict(zip(BIG_WEIGHTS, next_big))

    y_prompt = xp.reshape(batch, seq, D_MODEL)
    y_sample = xs.reshape(dec_batch, dec_seq, D_MODEL)
    new_cache_k = new_cache[0].reshape(batch, DEPTH, seq, N_HEADS, 2 * QK_DIM)
    new_cache_v = new_cache[1].reshape(batch, DEPTH, seq, N_HEADS, V_DIM)
    return (y_prompt, y_sample, new_cache_k, new_cache_v)
```

```python
import functools
import math
import types

import numpy as np
import jax
import jax.numpy as jnp
from jax import lax
from jax.experimental import pallas as pl
from jax.experimental.pallas import tpu as pltpu

D_MODEL = 1024
DEPTH = 2
GRID_W = 64
POOL_WINDOWS = (2, 4, 8, 16)
POOL_WIDTH = 256
POOL_GROUP = 64
POOL_HALO = 8
ATTN_WIDTH = 512
N_HEADS = 4
V_DIM = 128
QK_DIM = 64
HEAD_COLS = 2 * QK_DIM
ROPE_BASE = 10000.0
ROPE_AXIS_DIM = 32
ROPE_HALF = ROPE_AXIS_DIM // 2
CHUNK = 128
SGU_WIDTH = 256
QK_WIDTH = 512
IN_WIDTH = 2304
MIX_WIDTH = 1024
D_FF = 2816
N_MOD = 6
EPS = 1e-6
PAST_LEN = 256

C_POOL, C_Q, C_K, C_V, C_UV = 0, 256, 768, 1280, 1792
Y_POOL, Y_ATTN, Y_SGU = 0, 256, 768

ROWS = 1024
ROW_CHUNK = 256
ATTN_LOOKAHEAD = 2
CHUNKS_PER_BLOCK = 2
FFN_ROWS = 512
MXU_COLS = 256
FF_SPLIT = (D_FF // MXU_COLS + 1) // 2 * MXU_COLS
COND_ROWS = 16
ADA_COLS = 3072
VMEM_LIMIT = 58 * 1024 * 1024

BF16 = jnp.bfloat16
F32 = jnp.float32


def _dot(a, b):
    return jnp.dot(a, b, preferred_element_type=F32)


def _sigmoid(x):
    return 1.0 / (1.0 + jnp.exp(-x))


def _rms_mod(x, g, scale, shift):
    ms = jnp.mean(x * x, axis=-1, keepdims=True)
    return x * lax.rsqrt(ms + EPS) * (g * (1.0 + scale)) + shift


def _gelu_tanh(x):
    c = math.sqrt(2.0 / math.pi)
    return x * (0.5 * (1.0 + jnp.tanh(c * (x + 0.044715 * (x * x * x)))))


def _const_spec(shape):
    zeros = (0,) * len(shape)
    return pl.BlockSpec(shape, lambda *_: zeros, pipeline_mode=pl.Buffered(1))


def _layer_spec(shape, layer):
    index = (layer,) + (0,) * len(shape)
    return pl.BlockSpec((None,) + tuple(shape), lambda *_: index, pipeline_mode=pl.Buffered(1))


def _ada_kernel(cond_ref, w_ref, b_ref, out_ref):
    cond = cond_ref[...]
    s = (cond * _sigmoid(cond)).astype(BF16)
    out_ref[...] = _dot(s, w_ref[...].astype(BF16)) + b_ref[...]


def _ada_modulation(cond, w_ada, b_ada):
    n_cols = N_MOD * D_MODEL
    return pl.pallas_call(
        _ada_kernel,
        out_shape=jax.ShapeDtypeStruct((DEPTH, COND_ROWS, n_cols), F32),
        grid=(DEPTH, n_cols // ADA_COLS),
        in_specs=[
            pl.BlockSpec((COND_ROWS, D_MODEL), lambda l, j: (0, 0)),
            pl.BlockSpec((None, D_MODEL, ADA_COLS), lambda l, j: (l, 0, j)),
            pl.BlockSpec((None, 1, ADA_COLS), lambda l, j: (l, 0, j)),
        ],
        out_specs=pl.BlockSpec((None, COND_ROWS, ADA_COLS), lambda l, j: (l, 0, j)),
        compiler_params=pltpu.CompilerParams(
            dimension_semantics=("arbitrary", "arbitrary"), vmem_limit_bytes=VMEM_LIMIT),
        name="ada_modulation",
    )(cond, w_ada, b_ada.reshape(DEPTH, 1, n_cols))


def _mixer_body(r, x_ref, cnt_ref, pool_scr, k_scr, v_scr, *, nb, seq, has_ctx, lam_init, cpb):
    kv_base = PAST_LEN if has_ctx else 0
    shift1, scale1, gate1 = r.mod[0:1, :], r.mod[1:2, :], r.mod[2:3, :]

    lane_head = lax.broadcasted_iota(jnp.int32, (ROW_CHUNK, HEAD_COLS), 1)
    first_half = lane_head < QK_DIM

    for b in range(nb):
        pool_scr[b, 0:POOL_HALO, :] = jnp.zeros((POOL_HALO, POOL_WIDTH), F32)
        pool_scr[b, POOL_HALO + seq:POOL_HALO + seq + POOL_HALO, :] = jnp.zeros((POOL_HALO, POOL_WIDTH), F32)
    if not has_ctx:
        for slot in range(r.kc_out.shape[1]):
            if slot != r.cache_slot:
                r.kc_out[:, slot] = jnp.zeros((nb,) + r.kc_out.shape[2:], F32)
                r.vc_out[:, slot] = jnp.zeros((nb,) + r.vc_out.shape[2:], F32)
    if has_ctx:
        for hd in range(N_HEADS):
            cols = slice(hd * HEAD_COLS, (hd + 1) * HEAD_COLS)
            k_scr[0, cols, 0:PAST_LEN] = r.ck[pl.ds(hd, PAST_LEN, stride=N_HEADS), :].T.astype(BF16)
            v_scr[0, 0:PAST_LEN, cols] = r.cv[pl.ds(hd, PAST_LEN, stride=N_HEADS), :].astype(BF16)
        lane_q = lax.broadcasted_iota(jnp.int32, (ROW_CHUNK, QK_WIDTH), 1)
        rope_low = (lane_q % ROPE_AXIS_DIM) < ROPE_HALF

        def rope(t, pos0):
            partner = jnp.where(rope_low,
                                pltpu.roll(t, QK_WIDTH - ROPE_HALF, 1),
                                pltpu.roll(t, ROPE_HALF, 1))
            cos = r.rope_c[pos0:pos0 + ROW_CHUNK, :]
            sin = r.rope_s[pos0:pos0 + ROW_CHUNK, :]
            cos = jnp.concatenate([cos] * N_HEADS, axis=1)
            sin = jnp.concatenate([sin] * N_HEADS, axis=1)
            return t * cos + partner * sin

    for c in range(ROWS // ROW_CHUNK):
        r0 = c * ROW_CHUNK
        b, pos0 = r0 // seq, r0 % seq
        rows = slice(r0, r0 + ROW_CHUNK)
        if r.h is None:
            h = _rms_mod(x_ref[rows, :], r.g1[r.layer_row, :], scale1, shift1).astype(BF16)
        else:
            h = r.h[rows, :]

        pool_scr[b, POOL_HALO + pos0:POOL_HALO + pos0 + ROW_CHUNK, :] = _dot(h, r.w_in[:, C_POOL:C_Q])

        q = _dot(h, r.w_in[:, C_Q:C_K])
        k = _dot(h, r.w_in[:, C_K:C_V])
        if has_ctx:
            q = rope(q, pos0)
            k = rope(k, pos0)
        else:
            for hd in range(N_HEADS):
                r.kc_out[b, r.cache_slot, pl.ds(N_HEADS * pos0 + hd, ROW_CHUNK, stride=N_HEADS), :] = (
                    k[:, hd * HEAD_COLS:(hd + 1) * HEAD_COLS])
        q = q * (QK_DIM ** -0.5 * math.log2(math.e))
        for hd in range(N_HEADS):
            cols = slice(hd * HEAD_COLS, (hd + 1) * HEAD_COLS)
            qh = q[:, cols]
            r.q_scr[c, 0:ROW_CHUNK, cols] = jnp.where(first_half, qh, 0.0).astype(BF16)
            r.q_scr[c, ROW_CHUNK:2 * ROW_CHUNK, cols] = jnp.where(first_half, 0.0, qh).astype(BF16)
        k_scr[b, :, kv_base + pos0:kv_base + pos0 + ROW_CHUNK] = k.T.astype(BF16)

        v = _dot(h, r.w_in[:, C_V:C_UV])
        if not has_ctx:
            for hd in range(N_HEADS):
                r.vc_out[b, r.cache_slot, pl.ds(N_HEADS * pos0 + hd, ROW_CHUNK, stride=N_HEADS), :] = (
                    v[:, hd * V_DIM:(hd + 1) * V_DIM])
        v_scr[b, kv_base + pos0:kv_base + pos0 + ROW_CHUNK, :] = v.astype(BF16)

        uv = _gelu_tanh(_dot(h, r.w_in[:, C_UV:IN_WIDTH]))
        r.u_scr[rows, :] = uv[:, :SGU_WIDTH]
        vv = uv[:, SGU_WIDTH:]
        mu = jnp.mean(vv, axis=-1, keepdims=True)
        vc = vv - mu
        vn = vc * lax.rsqrt(jnp.mean(vc * vc, axis=-1, keepdims=True) + EPS) * r.sgu_g[r.layer_row, :]
        r.vs_scr[rows, :] = vn.astype(BF16)

    lane_pool = lax.broadcasted_iota(jnp.int32, (ROW_CHUNK, 2 * POOL_GROUP), 1)
    narrow = lane_pool < POOL_GROUP
    for c in range(ROWS // ROW_CHUNK):
        r0 = c * ROW_CHUNK
        b, pos0 = r0 // seq, r0 % seq
        base = POOL_HALO + pos0
        pooled = []
        for j, (w_small, w_big) in enumerate(((POOL_WINDOWS[0], POOL_WINDOWS[1]),
                                              (POOL_WINDOWS[2], POOL_WINDOWS[3]))):
            cols = slice(j * 2 * POOL_GROUP, (j + 1) * 2 * POOL_GROUP)
            s_small = jnp.zeros((ROW_CHUNK, 2 * POOL_GROUP), F32)
            s_rest = jnp.zeros((ROW_CHUNK, 2 * POOL_GROUP), F32)
            for d in range(-(w_big // 2), w_big - w_big // 2):
                t = pool_scr[b, base + d:base + d + ROW_CHUNK, cols]
                if -(w_small // 2) <= d < w_small - w_small // 2:
                    s_small = s_small + t
                else:
                    s_rest = s_rest + t
            win_sum = jnp.where(narrow, s_small, s_small + s_rest)
            centre = pool_scr[b, base:base + ROW_CHUNK, cols]
            pooled.append(win_sum / cnt_ref[pos0:pos0 + ROW_CHUNK, cols] - centre)
        pooled = jnp.concatenate(pooled, axis=1).astype(BF16)
        y_a = _dot(pooled, r.wpool[...]) * r.pscale[r.layer_row, :]
        r.y_scr[r0:r0 + ROW_CHUNK, Y_POOL:Y_ATTN] = y_a.astype(BF16)

    lq1, lk1, lq2, lk2 = r.lam[0:1, :], r.lam[1:2, :], r.lam[2:3, :], r.lam[3:4, :]
    lam = (jnp.exp(jnp.sum(lq1 * lk1, axis=-1, keepdims=True))
           - jnp.exp(jnp.sum(lq2 * lk2, axis=-1, keepdims=True)) + lam_init)
    n_qb = seq // ROW_CHUNK

    lane_sgu = lax.broadcasted_iota(jnp.int32, (CHUNK, 2 * POOL_GROUP), 1)
    sgu_first = lane_sgu < (SGU_WIDTH // 4)
    out_cols = MIX_WIDTH // N_HEADS

    def out_proj(rows, part):
        cols = slice(part * out_cols, (part + 1) * out_cols)
        y = _dot(r.y_scr[rows, :], r.w_out[:, cols])
        r.out[rows, cols] = x_ref[rows, cols] + gate1[:, cols] * y

    def block(blk, with_prev):
        static = isinstance(blk, int)

        def row0(chunk):
            return chunk * ROW_CHUNK if static else pl.multiple_of(chunk * ROW_CHUNK, ROW_CHUNK)

        chunks = [blk * cpb + j for j in range(cpb)]

        for idx in chunks:
            for half in range(ROW_CHUNK // CHUNK):
                sub = pl.ds(row0(idx) + half * CHUNK, CHUNK)
                for j in range(2):
                    cols = slice(j * 128, (j + 1) * 128)
                    t = _dot(r.wsgu[j], r.vs_scr[sub, cols])
                    mixed = jnp.where(sgu_first, t[:CHUNK, :], t[CHUNK:, :]) + r.bsgu[:, cols]
                    y_c = r.u_scr[sub, cols] * mixed
                    r.y_scr[sub, Y_SGU + j * 128:Y_SGU + (j + 1) * 128] = y_c.astype(BF16)

        units = [(idx, hd) for idx in chunks for hd in range(N_HEADS)]
        prev = [(pl.ds(row0(idx - cpb), ROW_CHUNK), part) for idx in chunks for part in range(N_HEADS)]

        def scores(idx, hd):
            cols = slice(hd * HEAD_COLS, (hd + 1) * HEAD_COLS)
            s = _dot(r.q_scr[idx, :, cols], k_scr[idx // n_qb, cols, :])
            return s[:ROW_CHUNK], s[ROW_CHUNK:]

        pending = [scores(*u) for u in units[:ATTN_LOOKAHEAD]]
        for n, (idx, hd) in enumerate(units):
            cols = slice(hd * HEAD_COLS, (hd + 1) * HEAD_COLS)
            s1, s2 = pending.pop(0)
            if n + ATTN_LOOKAHEAD < len(units):
                pending.append(scores(*units[n + ATTN_LOOKAHEAD]))
            e1 = jnp.exp2(s1 - jnp.max(s1, axis=-1, keepdims=True))
            e2 = jnp.exp2(s2 - jnp.max(s2, axis=-1, keepdims=True))
            l1 = jnp.sum(e1, axis=-1, keepdims=True)
            l2 = jnp.sum(e2, axis=-1, keepdims=True)
            p = (e1 - e2 * (lam * l1 / l2)).astype(BF16)
            o = _dot(p, v_scr[idx // n_qb, :, cols]) * (1.0 / l1)
            if with_prev:
                out_proj(*prev[n])
            o = o * lax.rsqrt(jnp.mean(o * o, axis=-1, keepdims=True) + EPS) * r.subln[r.layer_row, :]
            o = o * (1.0 - lam_init)
            r.y_scr[pl.ds(row0(idx), ROW_CHUNK), Y_ATTN + hd * V_DIM:Y_ATTN + (hd + 1) * V_DIM] = o.astype(BF16)

    n_blocks = ROWS // (ROW_CHUNK * cpb)
    block(0, False)
    if n_blocks == 2:
        block(1, True)
    else:
        def loop_block(blk, carry):
            block(blk, True)
            return carry

        lax.fori_loop(1, n_blocks, loop_block, 0)
    for idx in range((n_blocks - 1) * cpb, n_blocks * cpb):
        for part in range(N_HEADS):
            out_proj(pl.ds(idx * ROW_CHUNK, ROW_CHUNK), part)


def _mixer_kernel(*refs, layer, nb, seq, has_ctx, lam_init, cache_slot, n_passthrough, n_cast, has_h):
    it = iter(refs)
    r = types.SimpleNamespace(cache_slot=cache_slot, layer_row=slice(layer, layer + 1))
    x_ref, r.mod, r.g1, r.w_in = next(it), next(it), next(it), next(it)
    r.h = next(it) if has_h else None
    if has_ctx:
        r.rope_c, r.rope_s, r.ck, r.cv = next(it), next(it), next(it), next(it)
    (r.lam, r.subln, cnt_ref, r.wpool, r.pscale, r.sgu_g, r.wsgu, r.bsgu, r.w_out) = (next(it) for _ in range(9))
    for _ in range(n_passthrough):
        next(it)
    cast_in = [next(it) for _ in range(n_cast)]
    r.out = next(it)
    if not has_ctx:
        r.kc_out, r.vc_out = next(it), next(it)
    cast_out = [next(it) for _ in range(n_cast)]
    pool_scr, r.q_scr, k_scr, v_scr, r.u_scr, r.vs_scr, r.y_scr = (next(it) for _ in range(7))
    for src, dst in zip(cast_in, cast_out):
        dst[...] = src[...].astype(BF16)
    _mixer_body(r, x_ref, cnt_ref, pool_scr, k_scr, v_scr, nb=nb, seq=seq, has_ctx=has_ctx, lam_init=lam_init,
                cpb=CHUNKS_PER_BLOCK)


def _mixer_call(x2d, mod4, layer, seq, has_ctx, lam_init, params, big, ctx=None, new_cache=None, cast_f32=(),
                h_pre=None):
    n_rows = x2d.shape[0]
    nb = ROWS // seq
    n_seq = n_rows // seq
    kv_len = seq + (PAST_LEN if has_ctx else 0)

    if has_ctx:
        mod_map = lambda i: (layer, 1 + i, 0, 0)
    else:
        mod_map = lambda i: (layer, 0, 0, 0)

    operands = [x2d, mod4, params["norm1_g"], big["w_in"]]
    in_specs = [
        pl.BlockSpec((ROWS, D_MODEL), lambda i: (i, 0)),
        pl.BlockSpec((None, None, N_MOD, D_MODEL), mod_map),
        _const_spec((DEPTH, D_MODEL)),
        _const_spec((D_MODEL, IN_WIDTH)),
    ]
    if h_pre is not None:
        operands.append(h_pre)
        in_specs.append(pl.BlockSpec((ROWS, D_MODEL), lambda i: (i, 0)))
    if has_ctx:
        operands += list(ctx)
        in_specs += [
            _const_spec((seq, HEAD_COLS)),
            _const_spec((seq, HEAD_COLS)),
            pl.BlockSpec((None, None, PAST_LEN * N_HEADS, HEAD_COLS), lambda i: (i, layer, 0, 0)),
            pl.BlockSpec((None, None, PAST_LEN * N_HEADS, V_DIM), lambda i: (i, layer, 0, 0)),
        ]
    operands += [params["lam"], params["subln_g"], _pool_counts(seq), params["w_pool"], params["pool_scale"],
                 params["sgu_norm_g"], params["w_sgu"], params["b_sgu"], big["w_out"]]
    in_specs += [
        _layer_spec((8, 128), layer),
        _const_spec((DEPTH, V_DIM)),
        _const_spec((seq, POOL_WIDTH)),
        _layer_spec((POOL_WIDTH, POOL_WIDTH), layer),
        _const_spec((DEPTH, POOL_WIDTH)),
        _const_spec((DEPTH, SGU_WIDTH)),
        _layer_spec((2, 2 * CHUNK, CHUNK), layer),
        _layer_spec((CHUNK, SGU_WIDTH), layer),
        _const_spec((MIX_WIDTH, D_MODEL)),
    ]

    out_shape = [jax.ShapeDtypeStruct((n_rows, D_MODEL), F32)]
    out_specs = [pl.BlockSpec((ROWS, D_MODEL), lambda i: (i, 0))]
    aliases = {}
    cache_slot = 0
    if not has_ctx:
        out_shape += [jax.ShapeDtypeStruct((n_seq, DEPTH, seq * N_HEADS, HEAD_COLS), F32),
                      jax.ShapeDtypeStruct((n_seq, DEPTH, seq * N_HEADS, V_DIM), F32)]
        if new_cache is None:
            cache_slot = layer
            out_specs += [pl.BlockSpec((nb, DEPTH, seq * N_HEADS, HEAD_COLS), lambda i: (i, 0, 0, 0)),
                          pl.BlockSpec((nb, DEPTH, seq * N_HEADS, V_DIM), lambda i: (i, 0, 0, 0))]
        else:
            out_specs += [pl.BlockSpec((nb, 1, seq * N_HEADS, HEAD_COLS), lambda i: (i, layer, 0, 0)),
                          pl.BlockSpec((nb, 1, seq * N_HEADS, V_DIM), lambda i: (i, layer, 0, 0))]
            aliases = {len(operands): 1, len(operands) + 1: 2}
            operands += list(new_cache)
            in_specs += [pl.BlockSpec(memory_space=pl.ANY)] * 2
    n_steps = n_rows // ROWS
    for w in cast_f32:
        _, rows, cols = w.shape
        slab = rows // n_steps
        assert slab * n_steps == rows and slab % 16 == 0
        operands.append(w)
        in_specs.append(pl.BlockSpec((None, slab, cols), lambda i: (layer, i, 0)))
        out_shape.append(jax.ShapeDtypeStruct((rows, cols), BF16))
        out_specs.append(pl.BlockSpec((slab, cols), lambda i: (i, 0)))

    scratch = [
        pltpu.VMEM((nb, seq + 2 * POOL_HALO, POOL_WIDTH), F32),
        pltpu.VMEM((ROWS // ROW_CHUNK, 2 * ROW_CHUNK, QK_WIDTH), BF16),
        pltpu.VMEM((nb, QK_WIDTH, kv_len), BF16),
        pltpu.VMEM((nb, kv_len, ATTN_WIDTH), BF16),
        pltpu.VMEM((ROWS, SGU_WIDTH), F32),
        pltpu.VMEM((ROWS, SGU_WIDTH), BF16),
        pltpu.VMEM((ROWS, MIX_WIDTH), BF16),
    ]
    kernel = functools.partial(_mixer_kernel, layer=layer, nb=nb, seq=seq, has_ctx=has_ctx, lam_init=lam_init,
                               cache_slot=cache_slot, n_passthrough=len(aliases), n_cast=len(cast_f32),
                               has_h=h_pre is not None)
    return pl.pallas_call(
        kernel,
        out_shape=out_shape,
        grid=(n_rows // ROWS,),
        in_specs=in_specs,
        out_specs=out_specs,
        scratch_shapes=scratch,
        input_output_aliases=aliases,
        compiler_params=pltpu.CompilerParams(
            dimension_semantics=("arbitrary",), vmem_limit_bytes=VMEM_LIMIT),
        name="mixer_ctx" if has_ctx else "mixer_prompt",
    )(*operands)


def _ffn_block(x_ref, out_ref, h_ref, mod_ref, g2_ref, w1_ref, w2_ref, gf_ref, act_scr, final_norm, layer, nxt):
    shift2, scale2, gate2 = mod_ref[3:4, :], mod_ref[4:5, :], mod_ref[5:6, :]
    x = x_ref[...]
    h = _rms_mod(x, g2_ref[layer:layer + 1, :], scale2, shift2).astype(BF16)
    for lo, hi in ((0, FF_SPLIT), (FF_SPLIT, D_FF)):
        gate = _dot(h, w1_ref[:, lo:hi])
        up = _dot(h, w1_ref[:, D_FF + lo:D_FF + hi])
        act_scr[:, lo:hi] = (gate * _sigmoid(gate) * up).astype(BF16)
    y = x + gate2 * _dot(act_scr[...], w2_ref[...])
    if final_norm:
        ms = jnp.mean(y * y, axis=-1, keepdims=True)
        y = y * lax.rsqrt(ms + EPS) * gf_ref[...]
    out_ref[...] = y
    if nxt is not None and h_ref is not None:
        modn_ref, g1_ref = nxt
        h_ref[...] = _rms_mod(y, g1_ref[layer + 1:layer + 2, :], modn_ref[1:2, :], modn_ref[0:1, :]).astype(BF16)


def _ffn_kernel(*refs, layer, n_prompt_steps, final_norm, n_cast, next_norm):
    it = iter(refs)
    xp_ref, xs_ref, mod_ref, g2_ref, w1_ref, w2_ref, gf_ref = (next(it) for _ in range(7))
    nxt = (next(it), next(it)) if next_norm else None
    cast_in = [next(it) for _ in range(n_cast)]
    yp_ref, ys_ref = next(it), next(it)
    hp_ref, hs_ref = None, (next(it) if next_norm else None)
    cast_out = [next(it) for _ in range(n_cast)]
    act_scr = next(it)
    shared = (mod_ref, g2_ref, w1_ref, w2_ref, gf_ref, act_scr, final_norm, layer, nxt)
    is_prompt = pl.program_id(0) < n_prompt_steps

    @pl.when(is_prompt)
    def _():
        _ffn_block(xp_ref, yp_ref, hp_ref, *shared)

    @pl.when(jnp.logical_not(is_prompt))
    def _():
        for src, dst in zip(cast_in, cast_out):
            dst[...] = src[...].astype(BF16)
        _ffn_block(xs_ref, ys_ref, hs_ref, *shared)


def _ffn_call(xp, xs, mod4, layer, sample_seq, params, big, final_norm, next_f32=(), next_norm=False):
    n_p, n_s = xp.shape[0] // FFN_ROWS, xs.shape[0] // FFN_ROWS
    steps_per_seq = sample_seq // FFN_ROWS

    def prompt_step(i):
        return jnp.minimum(i, n_p - 1)

    def sample_step(i):
        return jnp.maximum(i - n_p, 0)

    def mod_map(i):
        return (layer, jnp.where(i < n_p, 0, 1 + sample_step(i) // steps_per_seq), 0, 0)

    in_specs = [
        pl.BlockSpec((FFN_ROWS, D_MODEL), lambda i: (prompt_step(i), 0)),
        pl.BlockSpec((FFN_ROWS, D_MODEL), lambda i: (sample_step(i), 0)),
        pl.BlockSpec((None, None, N_MOD, D_MODEL), mod_map),
        _const_spec((DEPTH, D_MODEL)),
        _const_spec((D_MODEL, 2 * D_FF)),
        _const_spec((D_FF, D_MODEL)),
        _const_spec((1, D_MODEL)),
    ]
    operands = [xp, xs, mod4, params["norm2_g"], big["w_ffn_in"], big["w_ffn_out"], params["final_g"]]
    out_shape = [jax.ShapeDtypeStruct(xp.shape, F32), jax.ShapeDtypeStruct(xs.shape, F32)]
    out_specs = [pl.BlockSpec((FFN_ROWS, D_MODEL), lambda i: (prompt_step(i), 0)),
                 pl.BlockSpec((FFN_ROWS, D_MODEL), lambda i: (sample_step(i), 0))]
    if next_norm:
        operands += [mod4, params["norm1_g"]]
        in_specs += [pl.BlockSpec((None, None, N_MOD, D_MODEL), lambda i: (layer + 1,) + mod_map(i)[1:]),
                     _const_spec((DEPTH, D_MODEL))]
        out_shape += [jax.ShapeDtypeStruct(xs.shape, BF16)]
        out_specs += [pl.BlockSpec((FFN_ROWS, D_MODEL), lambda i: (sample_step(i), 0))]
    operands += list(next_f32)
    for w in next_f32:
        _, rows, cols = w.shape
        slab = rows // n_s
        assert slab * n_s == rows and slab % 16 == 0
        in_specs.append(pl.BlockSpec((None, slab, cols), lambda i: (layer + 1, sample_step(i), 0)))
        out_shape.append(jax.ShapeDtypeStruct((rows, cols), BF16))
        out_specs.append(pl.BlockSpec((slab, cols), lambda i: (sample_step(i), 0)))

    kernel = functools.partial(_ffn_kernel, layer=layer, n_prompt_steps=n_p, final_norm=final_norm,
                               n_cast=len(next_f32), next_norm=next_norm)
    return pl.pallas_call(
        kernel,
        out_shape=out_shape,
        grid=(n_p + n_s,),
        in_specs=in_specs,
        out_specs=out_specs,
        scratch_shapes=[pltpu.VMEM((FFN_ROWS, D_FF), BF16)],
        compiler_params=pltpu.CompilerParams(
            dimension_semantics=("arbitrary",), vmem_limit_bytes=VMEM_LIMIT),
        name="ffn",
    )(*operands)


def _rope_tables(seq):
    n_rows = seq // GRID_W
    rows = np.repeat(np.arange(n_rows), GRID_W).astype(np.float32)
    cols = np.tile(np.arange(GRID_W), n_rows).astype(np.float32)
    inv = 1.0 / (ROPE_BASE ** (np.arange(0, ROPE_AXIS_DIM, 2, dtype=np.float32) / ROPE_AXIS_DIM))
    ar, ac = rows[:, None] * inv[None], cols[:, None] * inv[None]
    cos_parts, sin_parts = [], []
    for ang in (ar, ac):
        cos_parts += [np.cos(ang), np.cos(ang)]
        sin_parts += [-np.sin(ang), np.sin(ang)]
    cos64 = np.concatenate(cos_parts, axis=1)
    sin64 = np.concatenate(sin_parts, axis=1)
    cos = np.concatenate([cos64, cos64], axis=1).astype(np.float32)
    sin = np.concatenate([sin64, sin64], axis=1).astype(np.float32)
    return jnp.asarray(cos), jnp.asarray(sin)


def _pool_counts(seq):
    t = np.arange(seq)
    cols = []
    for w in POOL_WINDOWS:
        lo = np.clip(t - w // 2, 0, seq)
        hi = np.clip(t + w - w // 2, 0, seq)
        cols.append(np.repeat((hi - lo).astype(np.float32)[:, None], POOL_GROUP, axis=1))
    return jnp.asarray(np.concatenate(cols, axis=1))


def _small_params(norm1_g, w_pool, pool_scale, lam_q1, lam_k1, lam_q2, lam_k2, subln_g,
                  sgu_norm_g, w_sgu, b_sgu, norm2_g, final_g):
    n_groups = len(POOL_WINDOWS)
    eye = jnp.eye(n_groups, dtype=F32)
    w_pool_bd = (eye[None, :, None, :, None] * w_pool[:, :, :, None, :]).reshape(DEPTH, POOL_WIDTH, POOL_WIDTH)
    lam = jnp.pad(jnp.stack([lam_q1, lam_k1, lam_q2, lam_k2], axis=1), ((0, 0), (0, 4), (0, 128 - QK_DIM)))
    return {
        "norm1_g": norm1_g,
        "lam": lam,
        "subln_g": subln_g,
        "w_pool": w_pool_bd.astype(BF16),
        "pool_scale": pool_scale,
        "sgu_norm_g": sgu_norm_g,
        "w_sgu": w_sgu.reshape(DEPTH, 2, 2 * CHUNK, CHUNK).astype(BF16),
        "b_sgu": jnp.repeat(jnp.swapaxes(b_sgu, 1, 2), SGU_WIDTH // 4, axis=2),
        "norm2_g": norm2_g,
        "final_g": final_g[None, :],
    }


BIG_WEIGHTS = ("w_in", "w_out", "w_ffn_in", "w_ffn_out")


def kernel(x_prompt, x_sample, cache_k, cache_v, c, c_ctx, norm1_g, w_ada, b_ada, w_in, w_pool, pool_scale, lam_q1, lam_k1, lam_q2, lam_k2, subln_g, sgu_norm_g, w_sgu, b_sgu, w_out, norm2_g, w_ffn_in, w_ffn_out, final_g):
    batch, seq, _ = x_prompt.shape
    dec_batch, dec_seq, _ = x_sample.shape
    assert ROWS % seq == 0 and dec_seq == ROWS and 1 + dec_batch <= COND_ROWS

    cond = jnp.concatenate([c_ctx[None, :], c, jnp.zeros((COND_ROWS - 1 - dec_batch, D_MODEL), F32)], axis=0)
    mod4 = _ada_modulation(cond, w_ada, b_ada).reshape(DEPTH, COND_ROWS, N_MOD, D_MODEL)

    ctx = _rope_tables(dec_seq) + (cache_k.reshape(dec_batch, DEPTH, PAST_LEN * N_HEADS, HEAD_COLS),
                                   cache_v.reshape(dec_batch, DEPTH, PAST_LEN * N_HEADS, V_DIM))
    params = _small_params(norm1_g, w_pool, pool_scale, lam_q1, lam_k1, lam_q2, lam_k2, subln_g,
                           sgu_norm_g, w_sgu, b_sgu, norm2_g, final_g)

    big_f32 = (w_in, w_out, w_ffn_in, w_ffn_out)
    big = {"w_in": w_in[0].astype(BF16), "w_out": w_out[0].astype(BF16)}

    xp = x_prompt.reshape(batch * seq, D_MODEL)
    xs = x_sample.reshape(dec_batch * dec_seq, D_MODEL)
    new_cache = None
    hp = hs = None
    for l in range(DEPTH):
        lam_init = 0.8 - 0.6 * math.exp(-0.3 * l)
        last = l == DEPTH - 1

        xp, *new_cache = _mixer_call(xp, mod4, l, seq, False, lam_init, params, big, new_cache=new_cache, h_pre=hp)
        if l == 0:
            xs, big["w_ffn_in"], big["w_ffn_out"] = _mixer_call(
                xs, mod4, l, dec_seq, True, lam_init, params, big, ctx=ctx, cast_f32=(w_ffn_in, w_ffn_out))
        else:
            xs, = _mixer_call(xs, mod4, l, dec_seq, True, lam_init, params, big, ctx=ctx, h_pre=hs)
        if last:
            xp, xs = _ffn_call(xp, xs, mod4, l, dec_seq, params, big, True)
        else:
            xp, xs, hs, *next_big = _ffn_call(xp, xs, mod4, l, dec_seq, params, big, False,
                                                  next_f32=big_f32, next_norm=True)
            big = dict(zip(BIG_WEIGHTS, next_big))

    y_prompt = xp.reshape(batch, seq, D_MODEL)
    y_sample = xs.reshape(dec_batch, dec_seq, D_MODEL)
    new_cache_k = new_cache[0].reshape(batch, DEPTH, seq, N_HEADS, 2 * QK_DIM)
    new_cache_v = new_cache[1].reshape(batch, DEPTH, seq, N_HEADS, V_DIM)
    return (y_prompt, y_sample, new_cache_k, new_cache_v)
```

```python
import functools
import math
import types

import numpy as np
import jax
import jax.numpy as jnp
from jax import lax
from jax.experimental import pallas as pl
from jax.experimental.pallas import tpu as pltpu

D_MODEL = 1024
DEPTH = 2
GRID_W = 64
POOL_WINDOWS = (2, 4, 8, 16)
POOL_WIDTH = 256
POOL_GROUP = 64
POOL_HALO = 8
ATTN_WIDTH = 512
N_HEADS = 4
V_DIM = 128
QK_DIM = 64
HEAD_COLS = 2 * QK_DIM
ROPE_BASE = 10000.0
ROPE_AXIS_DIM = 32
ROPE_HALF = ROPE_AXIS_DIM // 2
CHUNK = 128
SGU_WIDTH = 256
QK_WIDTH = 512
IN_WIDTH = 2304
MIX_WIDTH = 1024
D_FF = 2816
N_MOD = 6
EPS = 1e-6
PAST_LEN = 256

C_POOL, C_Q, C_K, C_V, C_UV = 0, 256, 768, 1280, 1792
Y_POOL, Y_ATTN, Y_SGU = 0, 256, 768

ROWS = 1024
ROW_CHUNK = 256
ATTN_LOOKAHEAD = 2
CHUNKS_PER_BLOCK = 2
FFN_ROWS = 512
MXU_COLS = 256
FF_SPLIT = (D_FF // MXU_COLS + 1) // 2 * MXU_COLS
COND_ROWS = 16
ADA_COLS = 3072
VMEM_LIMIT = 58 * 1024 * 1024

BF16 = jnp.bfloat16
F32 = jnp.float32


def _dot(a, b):
    return jnp.dot(a, b, preferred_element_type=F32)


def _sigmoid(x):
    return 1.0 / (1.0 + jnp.exp(-x))


def _rms_mod(x, g, scale, shift):
    ms = jnp.mean(x * x, axis=-1, keepdims=True)
    return x * lax.rsqrt(ms + EPS) * (g * (1.0 + scale)) + shift


def _gelu_tanh(x):
    c = math.sqrt(2.0 / math.pi)
    return x * (0.5 * (1.0 + jnp.tanh(c * (x + 0.044715 * (x * x * x)))))


def _const_spec(shape):
    zeros = (0,) * len(shape)
    return pl.BlockSpec(shape, lambda *_: zeros, pipeline_mode=pl.Buffered(1))


def _layer_spec(shape, layer):
    index = (layer,) + (0,) * len(shape)
    return pl.BlockSpec((None,) + tuple(shape), lambda *_: index, pipeline_mode=pl.Buffered(1))


def _ada_kernel(cond_ref, w_ref, b_ref, out_ref):
    cond = cond_ref[...]
    s = (cond * _sigmoid(cond)).astype(BF16)
    out_ref[...] = _dot(s, w_ref[...].astype(BF16)) + b_ref[...]


def _ada_modulation(cond, w_ada, b_ada):
    n_cols = N_MOD * D_MODEL
    return pl.pallas_call(
        _ada_kernel,
        out_shape=jax.ShapeDtypeStruct((DEPTH, COND_ROWS, n_cols), F32),
        grid=(DEPTH, n_cols // ADA_COLS),
        in_specs=[
            pl.BlockSpec((COND_ROWS, D_MODEL), lambda l, j: (0, 0)),
            pl.BlockSpec((None, D_MODEL, ADA_COLS), lambda l, j: (l, 0, j)),
            pl.BlockSpec((None, 1, ADA_COLS), lambda l, j: (l, 0, j)),
        ],
        out_specs=pl.BlockSpec((None, COND_ROWS, ADA_COLS), lambda l, j: (l, 0, j)),
        compiler_params=pltpu.CompilerParams(
            dimension_semantics=("arbitrary", "arbitrary"), vmem_limit_bytes=VMEM_LIMIT),
        name="ada_modulation",
    )(cond, w_ada, b_ada.reshape(DEPTH, 1, n_cols))


def _mixer_body(r, x_ref, cnt_ref, pool_scr, k_scr, v_scr, *, nb, seq, has_ctx, lam_init, cpb):
    kv_base = PAST_LEN if has_ctx else 0
    shift1, scale1, gate1 = r.mod[0:1, :], r.mod[1:2, :], r.mod[2:3, :]

    lane_head = lax.broadcasted_iota(jnp.int32, (ROW_CHUNK, HEAD_COLS), 1)
    first_half = lane_head < QK_DIM

    for b in range(nb):
        pool_scr[b, 0:POOL_HALO, :] = jnp.zeros((POOL_HALO, POOL_WIDTH), F32)
        pool_scr[b, POOL_HALO + seq:POOL_HALO + seq + POOL_HALO, :] = jnp.zeros((POOL_HALO, POOL_WIDTH), F32)
    if not has_ctx:
        for slot in range(r.kc_out.shape[1]):
            if slot != r.cache_slot:
                r.kc_out[:, slot] = jnp.zeros((nb,) + r.kc_out.shape[2:], F32)
                r.vc_out[:, slot] = jnp.zeros((nb,) + r.vc_out.shape[2:], F32)
    if has_ctx:
        for hd in range(N_HEADS):
            cols = slice(hd * HEAD_COLS, (hd + 1) * HEAD_COLS)
            k_scr[0, cols, 0:PAST_LEN] = r.ck[pl.ds(hd, PAST_LEN, stride=N_HEADS), :].T.astype(BF16)
            v_scr[0, 0:PAST_LEN, cols] = r.cv[pl.ds(hd, PAST_LEN, stride=N_HEADS), :].astype(BF16)
        lane_q = lax.broadcasted_iota(jnp.int32, (ROW_CHUNK, QK_WIDTH), 1)
        rope_low = (lane_q % ROPE_AXIS_DIM) < ROPE_HALF

        def rope(t, pos0):
            partner = jnp.where(rope_low,
                                pltpu.roll(t, QK_WIDTH - ROPE_HALF, 1),
                                pltpu.roll(t, ROPE_HALF, 1))
            cos = r.rope_c[pos0:pos0 + ROW_CHUNK, :]
            sin = r.rope_s[pos0:pos0 + ROW_CHUNK, :]
            cos = jnp.concatenate([cos] * N_HEADS, axis=1)
            sin = jnp.concatenate([sin] * N_HEADS, axis=1)
            return t * cos + partner * sin

    for c in range(ROWS // ROW_CHUNK):
        r0 = c * ROW_CHUNK
        b, pos0 = r0 // seq, r0 % seq
        rows = slice(r0, r0 + ROW_CHUNK)
        h = _rms_mod(x_ref[rows, :], r.g1[r.layer_row, :], scale1, shift1).astype(BF16)

        pool_scr[b, POOL_HALO + pos0:POOL_HALO + pos0 + ROW_CHUNK, :] = _dot(h, r.w_in[:, C_POOL:C_Q])

        q = _dot(h, r.w_in[:, C_Q:C_K])
        k = _dot(h, r.w_in[:, C_K:C_V])
        if has_ctx:
            q = rope(q, pos0)
            k = rope(k, pos0)
        else:
            for hd in range(N_HEADS):
                r.kc_out[b, r.cache_slot, pl.ds(N_HEADS * pos0 + hd, ROW_CHUNK, stride=N_HEADS), :] = (
                    k[:, hd * HEAD_COLS:(hd + 1) * HEAD_COLS])
        q = q * (QK_DIM ** -0.5 * math.log2(math.e))
        for hd in range(N_HEADS):
            cols = slice(hd * HEAD_COLS, (hd + 1) * HEAD_COLS)
            qh = q[:, cols]
            r.q_scr[c, 0:ROW_CHUNK, cols] = jnp.where(first_half, qh, 0.0).astype(BF16)
            r.q_scr[c, ROW_CHUNK:2 * ROW_CHUNK, cols] = jnp.where(first_half, 0.0, qh).astype(BF16)
        k_scr[b, :, kv_base + pos0:kv_base + pos0 + ROW_CHUNK] = k.T.astype(BF16)

        v = _dot(h, r.w_in[:, C_V:C_UV])
        if not has_ctx:
            for hd in range(N_HEADS):
                r.vc_out[b, r.cache_slot, pl.ds(N_HEADS * pos0 + hd, ROW_CHUNK, stride=N_HEADS), :] = (
                    v[:, hd * V_DIM:(hd + 1) * V_DIM])
        v_scr[b, kv_base + pos0:kv_base + pos0 + ROW_CHUNK, :] = v.astype(BF16)

        uv = _gelu_tanh(_dot(h, r.w_in[:, C_UV:IN_WIDTH]))
        r.u_scr[rows, :] = uv[:, :SGU_WIDTH]
        vv = uv[:, SGU_WIDTH:]
        mu = jnp.mean(vv, axis=-1, keepdims=True)
        vc = vv - mu
        vn = vc * lax.rsqrt(jnp.mean(vc * vc, axis=-1, keepdims=True) + EPS) * r.sgu_g[r.layer_row, :]
        r.vs_scr[rows, :] = vn.astype(BF16)

    lane_pool = lax.broadcasted_iota(jnp.int32, (ROW_CHUNK, 2 * POOL_GROUP), 1)
    narrow = lane_pool < POOL_GROUP
    for c in range(ROWS // ROW_CHUNK):
        r0 = c * ROW_CHUNK
        b, pos0 = r0 // seq, r0 % seq
        base = POOL_HALO + pos0
        pooled = []
        for j, (w_small, w_big) in enumerate(((POOL_WINDOWS[0], POOL_WINDOWS[1]),
                                              (POOL_WINDOWS[2], POOL_WINDOWS[3]))):
            cols = slice(j * 2 * POOL_GROUP, (j + 1) * 2 * POOL_GROUP)
            s_small = jnp.zeros((ROW_CHUNK, 2 * POOL_GROUP), F32)
            s_rest = jnp.zeros((ROW_CHUNK, 2 * POOL_GROUP), F32)
            for d in range(-(w_big // 2), w_big - w_big // 2):
                t = pool_scr[b, base + d:base + d + ROW_CHUNK, cols]
                if -(w_small // 2) <= d < w_small - w_small // 2:
                    s_small = s_small + t
                else:
                    s_rest = s_rest + t
            win_sum = jnp.where(narrow, s_small, s_small + s_rest)
            centre = pool_scr[b, base:base + ROW_CHUNK, cols]
            pooled.append(win_sum / cnt_ref[pos0:pos0 + ROW_CHUNK, cols] - centre)
        pooled = jnp.concatenate(pooled, axis=1).astype(BF16)
        y_a = _dot(pooled, r.wpool[...]) * r.pscale[r.layer_row, :]
        r.y_scr[r0:r0 + ROW_CHUNK, Y_POOL:Y_ATTN] = y_a.astype(BF16)

    lq1, lk1, lq2, lk2 = r.lam[0:1, :], r.lam[1:2, :], r.lam[2:3, :], r.lam[3:4, :]
    lam = (jnp.exp(jnp.sum(lq1 * lk1, axis=-1, keepdims=True))
           - jnp.exp(jnp.sum(lq2 * lk2, axis=-1, keepdims=True)) + lam_init)
    n_qb = seq // ROW_CHUNK

    lane_sgu = lax.broadcasted_iota(jnp.int32, (CHUNK, 2 * POOL_GROUP), 1)
    sgu_first = lane_sgu < (SGU_WIDTH // 4)
    out_cols = MIX_WIDTH // N_HEADS

    def out_proj(rows, part):
        cols = slice(part * out_cols, (part + 1) * out_cols)
        y = _dot(r.y_scr[rows, :], r.w_out[:, cols])
        r.out[rows, cols] = x_ref[rows, cols] + gate1[:, cols] * y

    def block(blk, with_prev):
        static = isinstance(blk, int)

        def row0(chunk):
            return chunk * ROW_CHUNK if static else pl.multiple_of(chunk * ROW_CHUNK, ROW_CHUNK)

        chunks = [blk * cpb + j for j in range(cpb)]

        for idx in chunks:
            for half in range(ROW_CHUNK // CHUNK):
                sub = pl.ds(row0(idx) + half * CHUNK, CHUNK)
                for j in range(2):
                    cols = slice(j * 128, (j + 1) * 128)
                    t = _dot(r.wsgu[j], r.vs_scr[sub, cols])
                    mixed = jnp.where(sgu_first, t[:CHUNK, :], t[CHUNK:, :]) + r.bsgu[:, cols]
                    y_c = r.u_scr[sub, cols] * mixed
                    r.y_scr[sub, Y_SGU + j * 128:Y_SGU + (j + 1) * 128] = y_c.astype(BF16)

        units = [(idx, hd) for idx in chunks for hd in range(N_HEADS)]
        prev = [(pl.ds(row0(idx - cpb), ROW_CHUNK), part) for idx in chunks for part in range(N_HEADS)]

        def scores(idx, hd):
            cols = slice(hd * HEAD_COLS, (hd + 1) * HEAD_COLS)
            s = _dot(r.q_scr[idx, :, cols], k_scr[idx // n_qb, cols, :])
            return s[:ROW_CHUNK], s[ROW_CHUNK:]

        pending = [scores(*u) for u in units[:ATTN_LOOKAHEAD]]
        for n, (idx, hd) in enumerate(units):
            cols = slice(hd * HEAD_COLS, (hd + 1) * HEAD_COLS)
            s1, s2 = pending.pop(0)
            if n + ATTN_LOOKAHEAD < len(units):
                pending.append(scores(*units[n + ATTN_LOOKAHEAD]))
            e1 = jnp.exp2(s1 - jnp.max(s1, axis=-1, keepdims=True))
            e2 = jnp.exp2(s2 - jnp.max(s2, axis=-1, keepdims=True))
            l1 = jnp.sum(e1, axis=-1, keepdims=True)
            l2 = jnp.sum(e2, axis=-1, keepdims=True)
            p = (e1 - e2 * (lam * l1 / l2)).astype(BF16)
            o = _dot(p, v_scr[idx // n_qb, :, cols]) * (1.0 / l1)
            if with_prev:
                out_proj(*prev[n])
            o = o * lax.rsqrt(jnp.mean(o * o, axis=-1, keepdims=True) + EPS) * r.subln[r.layer_row, :]
            o = o * (1.0 - lam_init)
            r.y_scr[pl.ds(row0(idx), ROW_CHUNK), Y_ATTN + hd * V_DIM:Y_ATTN + (hd + 1) * V_DIM] = o.astype(BF16)

    n_blocks = ROWS // (ROW_CHUNK * cpb)
    block(0, False)
    if n_blocks == 2:
        block(1, True)
    else:
        def loop_block(blk, carry):
            block(blk, True)
            return carry

        lax.fori_loop(1, n_blocks, loop_block, 0)
    for idx in range((n_blocks - 1) * cpb, n_blocks * cpb):
        for part in range(N_HEADS):
            out_proj(pl.ds(idx * ROW_CHUNK, ROW_CHUNK), part)


def _mixer_kernel(*refs, layer, nb, seq, has_ctx, lam_init, cache_slot, n_passthrough, n_cast):
    it = iter(refs)
    r = types.SimpleNamespace(cache_slot=cache_slot, layer_row=slice(layer, layer + 1))
    x_ref, r.mod, r.g1, r.w_in = next(it), next(it), next(it), next(it)
    if has_ctx:
        r.rope_c, r.rope_s, r.ck, r.cv = next(it), next(it), next(it), next(it)
    (r.lam, r.subln, cnt_ref, r.wpool, r.pscale, r.sgu_g, r.wsgu, r.bsgu, r.w_out) = (next(it) for _ in range(9))
    for _ in range(n_passthrough):
        next(it)
    cast_in = [next(it) for _ in range(n_cast)]
    r.out = next(it)
    if not has_ctx:
        r.kc_out, r.vc_out = next(it), next(it)
    cast_out = [next(it) for _ in range(n_cast)]
    pool_scr, r.q_scr, k_scr, v_scr, r.u_scr, r.vs_scr, r.y_scr = (next(it) for _ in range(7))
    for src, dst in zip(cast_in, cast_out):
        dst[...] = src[...].astype(BF16)
    _mixer_body(r, x_ref, cnt_ref, pool_scr, k_scr, v_scr, nb=nb, seq=seq, has_ctx=has_ctx, lam_init=lam_init,
                cpb=CHUNKS_PER_BLOCK)


def _mixer_call(x2d, mod4, layer, seq, has_ctx, lam_init, params, big, ctx=None, new_cache=None, cast_f32=()):
    n_rows = x2d.shape[0]
    nb = ROWS // seq
    n_seq = n_rows // seq
    kv_len = seq + (PAST_LEN if has_ctx else 0)

    if has_ctx:
        mod_map = lambda i: (layer, 1 + i, 0, 0)
    else:
        mod_map = lambda i: (layer, 0, 0, 0)

    operands = [x2d, mod4, params["norm1_g"], big["w_in"]]
    in_specs = [
        pl.BlockSpec((ROWS, D_MODEL), lambda i: (i, 0)),
        pl.BlockSpec((None, None, N_MOD, D_MODEL), mod_map),
        _const_spec((DEPTH, D_MODEL)),
        _const_spec((D_MODEL, IN_WIDTH)),
    ]
    if has_ctx:
        operands += list(ctx)
        in_specs += [
            _const_spec((seq, HEAD_COLS)),
            _const_spec((seq, HEAD_COLS)),
            pl.BlockSpec((None, None, PAST_LEN * N_HEADS, HEAD_COLS), lambda i: (i, layer, 0, 0)),
            pl.BlockSpec((None, None, PAST_LEN * N_HEADS, V_DIM), lambda i: (i, layer, 0, 0)),
        ]
    operands += [params["lam"], params["subln_g"], _pool_counts(seq), params["w_pool"], params["pool_scale"],
                 params["sgu_norm_g"], params["w_sgu"], params["b_sgu"], big["w_out"]]
    in_specs += [
        _layer_spec((8, 128), layer),
        _const_spec((DEPTH, V_DIM)),
        _const_spec((seq, POOL_WIDTH)),
        _layer_spec((POOL_WIDTH, POOL_WIDTH), layer),
        _const_spec((DEPTH, POOL_WIDTH)),
        _const_spec((DEPTH, SGU_WIDTH)),
        _layer_spec((2, 2 * CHUNK, CHUNK), layer),
        _layer_spec((CHUNK, SGU_WIDTH), layer),
        _const_spec((MIX_WIDTH, D_MODEL)),
    ]

    out_shape = [jax.ShapeDtypeStruct((n_rows, D_MODEL), F32)]
    out_specs = [pl.BlockSpec((ROWS, D_MODEL), lambda i: (i, 0))]
    aliases = {}
    cache_slot = 0
    if not has_ctx:
        out_shape += [jax.ShapeDtypeStruct((n_seq, DEPTH, seq * N_HEADS, HEAD_COLS), F32),
                      jax.ShapeDtypeStruct((n_seq, DEPTH, seq * N_HEADS, V_DIM), F32)]
        if new_cache is None:
            cache_slot = layer
            out_specs += [pl.BlockSpec((nb, DEPTH, seq * N_HEADS, HEAD_COLS), lambda i: (i, 0, 0, 0)),
                          pl.BlockSpec((nb, DEPTH, seq * N_HEADS, V_DIM), lambda i: (i, 0, 0, 0))]
        else:
            out_specs += [pl.BlockSpec((nb, 1, seq * N_HEADS, HEAD_COLS), lambda i: (i, layer, 0, 0)),
                          pl.BlockSpec((nb, 1, seq * N_HEADS, V_DIM), lambda i: (i, layer, 0, 0))]
            aliases = {len(operands): 1, len(operands) + 1: 2}
            operands += list(new_cache)
            in_specs += [pl.BlockSpec(memory_space=pl.ANY)] * 2
    n_steps = n_rows // ROWS
    for w in cast_f32:
        _, rows, cols = w.shape
        slab = rows // n_steps
        assert slab * n_steps == rows and slab % 16 == 0
        operands.append(w)
        in_specs.append(pl.BlockSpec((None, slab, cols), lambda i: (layer, i, 0)))
        out_shape.append(jax.ShapeDtypeStruct((rows, cols), BF16))
        out_specs.append(pl.BlockSpec((slab, cols), lambda i: (i, 0)))

    scratch = [
        pltpu.VMEM((nb, seq + 2 * POOL_HALO, POOL_WIDTH), F32),
        pltpu.VMEM((ROWS // ROW_CHUNK, 2 * ROW_CHUNK, QK_WIDTH), BF16),
        pltpu.VMEM((nb, QK_WIDTH, kv_len), BF16),
        pltpu.VMEM((nb, kv_len, ATTN_WIDTH), BF16),
        pltpu.VMEM((ROWS, SGU_WIDTH), F32),
        pltpu.VMEM((ROWS, SGU_WIDTH), BF16),
        pltpu.VMEM((ROWS, MIX_WIDTH), BF16),
    ]
    kernel = functools.partial(_mixer_kernel, layer=layer, nb=nb, seq=seq, has_ctx=has_ctx, lam_init=lam_init,
                               cache_slot=cache_slot, n_passthrough=len(aliases), n_cast=len(cast_f32))
    return pl.pallas_call(
        kernel,
        out_shape=out_shape,
        grid=(n_rows // ROWS,),
        in_specs=in_specs,
        out_specs=out_specs,
        scratch_shapes=scratch,
        input_output_aliases=aliases,
        compiler_params=pltpu.CompilerParams(
            dimension_semantics=("arbitrary",), vmem_limit_bytes=VMEM_LIMIT),
        name="mixer_ctx" if has_ctx else "mixer_prompt",
    )(*operands)


def _ffn_block(x_ref, out_ref, mod_ref, g2_ref, w1_ref, w2_ref, gf_ref, act_scr, final_norm, layer):
    shift2, scale2, gate2 = mod_ref[3:4, :], mod_ref[4:5, :], mod_ref[5:6, :]
    h = _rms_mod(x_ref[...], g2_ref[layer:layer + 1, :], scale2, shift2).astype(BF16)
    for lo, hi in ((0, FF_SPLIT), (FF_SPLIT, D_FF)):
        gate = _dot(h, w1_ref[:, lo:hi])
        up = _dot(h, w1_ref[:, D_FF + lo:D_FF + hi])
        act_scr[:, lo:hi] = (gate * _sigmoid(gate) * up).astype(BF16)
    if final_norm:
        y = x_ref[...] + gate2 * _dot(act_scr[...], w2_ref[...])
        ms = jnp.mean(y * y, axis=-1, keepdims=True)
        out_ref[...] = y * lax.rsqrt(ms + EPS) * gf_ref[...]
    else:
        for lo in range(0, D_MODEL, MXU_COLS):
            cols = slice(lo, lo + MXU_COLS)
            out_ref[:, cols] = x_ref[:, cols] + gate2[:, cols] * _dot(act_scr[...], w2_ref[:, cols])


def _ffn_kernel(*refs, layer, n_prompt_steps, final_norm, n_cast):
    xp_ref, xs_ref, mod_ref, g2_ref, w1_ref, w2_ref, gf_ref = refs[:7]
    cast_in = refs[7:7 + n_cast]
    yp_ref, ys_ref = refs[7 + n_cast:9 + n_cast]
    cast_out = refs[9 + n_cast:9 + 2 * n_cast]
    act_scr = refs[9 + 2 * n_cast]
    shared = (mod_ref, g2_ref, w1_ref, w2_ref, gf_ref, act_scr, final_norm, layer)
    is_prompt = pl.program_id(0) < n_prompt_steps

    @pl.when(is_prompt)
    def _():
        _ffn_block(xp_ref, yp_ref, *shared)

    @pl.when(jnp.logical_not(is_prompt))
    def _():
        for src, dst in zip(cast_in, cast_out):
            dst[...] = src[...].astype(BF16)
        _ffn_block(xs_ref, ys_ref, *shared)


def _ffn_call(xp, xs, mod4, layer, sample_seq, params, big, final_norm, next_f32=()):
    n_p, n_s = xp.shape[0] // FFN_ROWS, xs.shape[0] // FFN_ROWS
    steps_per_seq = sample_seq // FFN_ROWS

    def prompt_step(i):
        return jnp.minimum(i, n_p - 1)

    def sample_step(i):
        return jnp.maximum(i - n_p, 0)

    def mod_map(i):
        return (layer, jnp.where(i < n_p, 0, 1 + sample_step(i) // steps_per_seq), 0, 0)

    in_specs = [
        pl.BlockSpec((FFN_ROWS, D_MODEL), lambda i: (prompt_step(i), 0)),
        pl.BlockSpec((FFN_ROWS, D_MODEL), lambda i: (sample_step(i), 0)),
        pl.BlockSpec((None, None, N_MOD, D_MODEL), mod_map),
        _const_spec((DEPTH, D_MODEL)),
        _const_spec((D_MODEL, 2 * D_FF)),
        _const_spec((D_FF, D_MODEL)),
        _const_spec((1, D_MODEL)),
    ]
    out_shape = [jax.ShapeDtypeStruct(xp.shape, F32), jax.ShapeDtypeStruct(xs.shape, F32)]
    out_specs = [pl.BlockSpec((FFN_ROWS, D_MODEL), lambda i: (prompt_step(i), 0)),
                 pl.BlockSpec((FFN_ROWS, D_MODEL), lambda i: (sample_step(i), 0))]
    for w in next_f32:
        _, rows, cols = w.shape
        slab = rows // n_s
        assert slab * n_s == rows and slab % 16 == 0
        in_specs.append(pl.BlockSpec((None, slab, cols), lambda i: (layer + 1, sample_step(i), 0)))
        out_shape.append(jax.ShapeDtypeStruct((rows, cols), BF16))
        out_specs.append(pl.BlockSpec((slab, cols), lambda i: (sample_step(i), 0)))

    kernel = functools.partial(_ffn_kernel, layer=layer, n_prompt_steps=n_p, final_norm=final_norm, n_cast=len(next_f32))
    return pl.pallas_call(
        kernel,
        out_shape=out_shape,
        grid=(n_p + n_s,),
        in_specs=in_specs,
        out_specs=out_specs,
        scratch_shapes=[pltpu.VMEM((FFN_ROWS, D_FF), BF16)],
        compiler_params=pltpu.CompilerParams(
            dimension_semantics=("arbitrary",), vmem_limit_bytes=VMEM_LIMIT),
        name="ffn",
    )(xp, xs, mod4, params["norm2_g"], big["w_ffn_in"], big["w_ffn_out"], params["final_g"], *next_f32)


def _rope_tables(seq):
    n_rows = seq // GRID_W
    rows = np.repeat(np.arange(n_rows), GRID_W).astype(np.float32)
    cols = np.tile(np.arange(GRID_W), n_rows).astype(np.float32)
    inv = 1.0 / (ROPE_BASE ** (np.arange(0, ROPE_AXIS_DIM, 2, dtype=np.float32) / ROPE_AXIS_DIM))
    ar, ac = rows[:, None] * inv[None], cols[:, None] * inv[None]
    cos_parts, sin_parts = [], []
    for ang in (ar, ac):
        cos_parts += [np.cos(ang), np.cos(ang)]
        sin_parts += [-np.sin(ang), np.sin(ang)]
    cos64 = np.concatenate(cos_parts, axis=1)
    sin64 = np.concatenate(sin_parts, axis=1)
    cos = np.concatenate([cos64, cos64], axis=1).astype(np.float32)
    sin = np.concatenate([sin64, sin64], axis=1).astype(np.float32)
    return jnp.asarray(cos), jnp.asarray(sin)


def _pool_counts(seq):
    t = np.arange(seq)
    cols = []
    for w in POOL_WINDOWS:
        lo = np.clip(t - w // 2, 0, seq)
        hi = np.clip(t + w - w // 2, 0, seq)
        cols.append(np.repeat((hi - lo).astype(np.float32)[:, None], POOL_GROUP, axis=1))
    return jnp.asarray(np.concatenate(cols, axis=1))


def _small_params(norm1_g, w_pool, pool_scale, lam_q1, lam_k1, lam_q2, lam_k2, subln_g,
                  sgu_norm_g, w_sgu, b_sgu, norm2_g, final_g):
    n_groups = len(POOL_WINDOWS)
    eye = jnp.eye(n_groups, dtype=F32)
    w_pool_bd = (eye[None, :, None, :, None] * w_pool[:, :, :, None, :]).reshape(DEPTH, POOL_WIDTH, POOL_WIDTH)
    lam = jnp.pad(jnp.stack([lam_q1, lam_k1, lam_q2, lam_k2], axis=1), ((0, 0), (0, 4), (0, 128 - QK_DIM)))
    return {
        "norm1_g": norm1_g,
        "lam": lam,
        "subln_g": subln_g,
        "w_pool": w_pool_bd.astype(BF16),
        "pool_scale": pool_scale,
        "sgu_norm_g": sgu_norm_g,
        "w_sgu": w_sgu.reshape(DEPTH, 2, 2 * CHUNK, CHUNK).astype(BF16),
        "b_sgu": jnp.repeat(jnp.swapaxes(b_sgu, 1, 2), SGU_WIDTH // 4, axis=2),
        "norm2_g": norm2_g,
        "final_g": final_g[None, :],
    }


BIG_WEIGHTS = ("w_in", "w_out", "w_ffn_in", "w_ffn_out")


def kernel(x_prompt, x_sample, cache_k, cache_v, c, c_ctx, norm1_g, w_ada, b_ada, w_in, w_pool, pool_scale, lam_q1, lam_k1, lam_q2, lam_k2, subln_g, sgu_norm_g, w_sgu, b_sgu, w_out, norm2_g, w_ffn_in, w_ffn_out, final_g):
    batch, seq, _ = x_prompt.shape
    dec_batch, dec_seq, _ = x_sample.shape
    assert ROWS % seq == 0 and dec_seq == ROWS and 1 + dec_batch <= COND_ROWS

    cond = jnp.concatenate([c_ctx[None, :], c, jnp.zeros((COND_ROWS - 1 - dec_batch, D_MODEL), F32)], axis=0)
    mod4 = _ada_modulation(cond, w_ada, b_ada).reshape(DEPTH, COND_ROWS, N_MOD, D_MODEL)

    ctx = _rope_tables(dec_seq) + (cache_k.reshape(dec_batch, DEPTH, PAST_LEN * N_HEADS, HEAD_COLS),
                                   cache_v.reshape(dec_batch, DEPTH, PAST_LEN * N_HEADS, V_DIM))
    params = _small_params(norm1_g, w_pool, pool_scale, lam_q1, lam_k1, lam_q2, lam_k2, subln_g,
                           sgu_norm_g, w_sgu, b_sgu, norm2_g, final_g)

    big_f32 = (w_in, w_out, w_ffn_in, w_ffn_out)
    big = {"w_in": w_in[0].astype(BF16), "w_out": w_out[0].astype(BF16)}

    xp = x_prompt.reshape(batch * seq, D_MODEL)
    xs = x_sample.reshape(dec_batch * dec_seq, D_MODEL)
    new_cache = None
    for l in range(DEPTH):
        lam_init = 0.8 - 0.6 * math.exp(-0.3 * l)
        last = l == DEPTH - 1

        xp, *new_cache = _mixer_call(xp, mod4, l, seq, False, lam_init, params, big, new_cache=new_cache)
        if l == 0:
            xs, big["w_ffn_in"], big["w_ffn_out"] = _mixer_call(
                xs, mod4, l, dec_seq, True, lam_init, params, big, ctx=ctx, cast_f32=(w_ffn_in, w_ffn_out))
        else:
            xs, = _mixer_call(xs, mod4, l, dec_seq, True, lam_init, params, big, ctx=ctx)
        xp, xs, *next_big = _ffn_call(xp, xs, mod4, l, dec_seq, params, big, last,
                                      next_f32=() if last else big_f32)
        big = dict(zip(BIG_WEIGHTS, next_big))

    y_prompt = xp.reshape(batch, seq, D_MODEL)
    y_sample = xs.reshape(dec_batch, dec_seq, D_MODEL)
    new_cache_k = new_cache[0].reshape(batch, DEPTH, seq, N_HEADS, 2 * QK_DIM)
    new_cache_v = new_cache[1].reshape(batch, DEPTH, seq, N_HEADS, V_DIM)
    return (y_prompt, y_sample, new_cache_k, new_cache_v)
```
